```python
import jax, jax.numpy as jnp
from jax import lax
import numpy as np

D_MODEL = 2048
BATCH = 4
SEQ = 2048
DEPTH = 4
DEC_BATCH = 128
DEC_SEQ = 8
PAST_LEN = 16384
PAGE_SIZE = 128

N_AB_LAYERS = (DEPTH + 1) // 2
N_LRU_LAYERS = DEPTH // 2
CONV_W = 4
H_A = 8
DK_A = 128
DV_A = 128
H_B = 4
DK_B = 128
DV_B = 256
GLA_LOWRANK = 16
GLA_NORMALIZER = 16.0
W_LRU = D_MODEL
LRU_BLOCKS = 16
LRU_BW = W_LRU // LRU_BLOCKS
LRU_C = 8.0
CHUNK_A = 64
CHUNK_B = 16
KA = H_A * DK_A
VA = H_A * DV_A
KB = H_B * DK_B
VB = H_B * DV_B
AB_IN = 2 * KA + VA + 2 * H_A + VA + 2 * KB + VB + GLA_LOWRANK + VB
AB_OUT = VA + VB
EPS = 1e-6

kernel_name = 'hybrid_gdn_gla_rglru_step'


def _rmsnorm(x, g):
    xf = x.astype(jnp.float32)
    y = xf * lax.rsqrt(jnp.mean(xf * xf, axis=-1, keepdims=True) + EPS)
    return (y * g.astype(jnp.float32)).astype(x.dtype)


def _l2norm(x):
    xf = x.astype(jnp.float32)
    return xf * lax.rsqrt(jnp.sum(xf * xf, axis=-1, keepdims=True) + EPS)


def _causal_conv(x, buf, w, b=None):
    K = w.shape[0]
    T = x.shape[1]
    xp = jnp.concatenate([buf.astype(x.dtype), x], axis=1)
    y = xp[:, 0:T] * w[0]
    for j in range(1, K):
        y = y + xp[:, j:j + T] * w[j]
    if b is not None:
        y = y + b
    return y, xp[:, T:]


def _to_chunks(a, C):
    B, T = a.shape[0], a.shape[1]
    N = -(-T // C)
    a = jnp.pad(a, [(0, 0), (0, N * C - T)] + [(0, 0)] * (a.ndim - 2))
    a = a.reshape((B, N, C) + a.shape[2:])
    return jnp.transpose(a, (1, 0, 3, 2) + tuple(range(4, a.ndim)))


def _from_chunks(o, T):
    N, B, H, C, D = o.shape
    return jnp.transpose(o, (1, 0, 3, 2, 4)).reshape(B, N * C, H, D)[:, :T]


def _gated_delta_rule(q, k, v, g, beta, S0):
    T = q.shape[1]
    DV = v.shape[-1]
    C = min(CHUNK_A, T)
    q, k, v, g, beta = [_to_chunks(a.astype(jnp.float32), C) for a in (q, k, v, g, beta)]
    G = jnp.cumsum(g, axis=-1)
    incl = jnp.tril(jnp.ones((C, C), bool))
    strict = jnp.tril(jnp.ones((C, C), bool), -1)
    diff = G[..., :, None] - G[..., None, :]
    decay = jnp.where(incl, jnp.exp(jnp.where(incl, diff, 0.0)), 0.0)
    kb = k * beta[..., None]
    A = jnp.where(strict, jnp.einsum('nbhid,nbhjd->nbhij', kb, k) * decay, 0.0)
    rhs = jnp.concatenate([v * beta[..., None], kb * jnp.exp(G)[..., None]], axis=-1)
    sol = lax.linalg.triangular_solve(A + jnp.eye(C, dtype=jnp.float32), rhs,
                                      left_side=True, lower=True, unit_diagonal=True)
    u, w = sol[..., :DV], sol[..., DV:]
    qk = jnp.einsum('nbhid,nbhjd->nbhij', q, k) * decay
    qg = q * jnp.exp(G)[..., None]

    def step(S, inp):
        qg_c, k_c, u_c, w_c, G_c, qk_c = inp
        v_new = u_c - jnp.einsum('bhck,bhkv->bhcv', w_c, S)
        o = jnp.einsum('bhck,bhkv->bhcv', qg_c, S) + jnp.einsum('bhij,bhjv->bhiv', qk_c, v_new)
        g_last = G_c[..., -1]
        k_dec = k_c * jnp.exp(g_last[..., None] - G_c)[..., None]
        S = S * jnp.exp(g_last)[..., None, None] + jnp.einsum('bhck,bhcv->bhkv', k_dec, v_new)
        return S, o

    S, o = lax.scan(step, S0.astype(jnp.float32), (qg, k, u, w, G, qk))
    return _from_chunks(o, T), S


def _gla(q, k, v, gk, S0):
    T = q.shape[1]
    C = min(CHUNK_B, T)
    q, k, v, gk = [_to_chunks(a.astype(jnp.float32), C) for a in (q, k, v, gk)]
    Bc = jnp.cumsum(gk, axis=-2)
    incl = jnp.tril(jnp.ones((C, C), bool))
    qg = q * jnp.exp(Bc)
    A = jnp.where(incl, jnp.einsum('nbhik,nbhjk->nbhij', qg, k * jnp.exp(-Bc)), 0.0)
    intra = jnp.einsum('nbhij,nbhjv->nbhiv', A, v)

    def step(S, inp):
        qg_c, k_c, v_c, B_c, intra_c = inp
        o = jnp.einsum('bhck,bhkv->bhcv', qg_c, S) + intra_c
        b_last = B_c[..., -1, :]
        k_dec = k_c * jnp.exp(b_last[..., None, :] - B_c)
        S = S * jnp.exp(b_last)[..., None] + jnp.einsum('bhck,bhcv->bhkv', k_dec, v_c)
        return S, o

    S, o = lax.scan(step, S0.astype(jnp.float32), (qg, k, v, Bc, intra))
    return _from_chunks(o, T), S


def _ab_split_points():
    sizes = [2 * KA + VA, H_A, H_A, VA, KB, KB, VB, GLA_LOWRANK, VB]
    return [int(s) for s in np.cumsum(sizes)[:-1]]


def _ab_mixer(x, S_a, conv_a, S_b, w_in, conv_w, a_log, dt_bias, norm_a, w_lr, b_lr, norm_b, w_out):
    B, T, _ = x.shape
    f32 = jnp.float32
    proj = jnp.einsum('btd,de->bte', x, w_in)
    qkv, b_raw, a_raw, z_a, q_b, k_b, v_b, lr_b, z_b = jnp.split(proj, _ab_split_points(), axis=-1)
    qkv, new_conv_a = _causal_conv(qkv, conv_a, conv_w)
    qkv = jax.nn.silu(qkv)
    q_a, k_a, v_a = jnp.split(qkv, [KA, 2 * KA], axis=-1)
    q_a = _l2norm(q_a.reshape(B, T, H_A, DK_A)) * (DK_A ** -0.5)
    k_a = _l2norm(k_a.reshape(B, T, H_A, DK_A))
    v_a = v_a.reshape(B, T, H_A, DV_A)
    beta = jax.nn.sigmoid(b_raw.astype(f32))
    g = -jnp.exp(a_log.astype(f32)) * jax.nn.softplus(a_raw.astype(f32) + dt_bias.astype(f32))
    o_a, S_a_new = _gated_delta_rule(q_a, k_a, v_a, g, beta, S_a)
    o_a = _rmsnorm(o_a.astype(x.dtype), norm_a) * jax.nn.silu(z_a.reshape(B, T, H_A, DV_A))
    q_b = q_b.reshape(B, T, H_B, DK_B) * (DK_B ** -0.5)
    k_b = k_b.reshape(B, T, H_B, DK_B)
    v_b = v_b.reshape(B, T, H_B, DV_B)
    gk = jax.nn.log_sigmoid((jnp.einsum('btr,rk->btk', lr_b, w_lr) + b_lr).astype(f32)) / GLA_NORMALIZER
    o_b, S_b_new = _gla(q_b, k_b, v_b, gk.reshape(B, T, H_B, DK_B), S_b)
    o_b = _rmsnorm(o_b.astype(x.dtype), norm_b) * jax.nn.silu(z_b.reshape(B, T, H_B, DV_B))
    o = jnp.concatenate([o_a.reshape(B, T, VA), o_b.reshape(B, T, VB)], axis=-1)
    return jnp.einsum('bte,ed->btd', o, w_out), S_a_new, new_conv_a, S_b_new


def _lru_mixer(x, h0, conv_buf, w_in, conv_w, conv_b, w_a, b_a, w_x, b_x, lam, w_out, reset_first):
    B, T, _ = x.shape
    f32 = jnp.float32
    xb, gate = jnp.split(jnp.einsum('btd,de->bte', x, w_in), 2, axis=-1)
    xc, new_buf = _causal_conv(xb, conv_buf, conv_w, conv_b)
    xh = xc.reshape(B, T, LRU_BLOCKS, LRU_BW)
    r = jax.nn.sigmoid((jnp.einsum('btnc,ncd->btnd', xh, w_a).reshape(B, T, W_LRU) + b_a).astype(f32))
    i = jax.nn.sigmoid((jnp.einsum('btnc,ncd->btnd', xh, w_x).reshape(B, T, W_LRU) + b_x).astype(f32))
    log_a = -LRU_C * r * jax.nn.softplus(-lam.astype(f32))
    a = jnp.exp(log_a)
    mult = jnp.sqrt(-jnp.expm1(2.0 * log_a))
    if reset_first:
        mult = mult.at[:, 0].set(1.0)
    bx = mult * i * xc.astype(f32)

    def step(h, inp):
        a_t, b_t = inp
        h = a_t * h + b_t
        return h, h

    hT, hs = lax.scan(step, h0.astype(f32), (jnp.swapaxes(a, 0, 1), jnp.swapaxes(bx, 0, 1)))
    y = jnp.swapaxes(hs, 0, 1).astype(x.dtype) * jax.nn.silu(gate)
    return jnp.einsum('btw,wd->btd', y, w_out), hT, new_buf


def _trunk(x, st_delta, st_dconv, st_gla, st_lru, st_lconv, ab_norm, ab_w_in, ab_conv_w, ab_a_log,
           ab_dt_bias, ab_norm_a, ab_gla_w_lr, ab_gla_b_lr, ab_norm_b, ab_w_out, lru_norm, lru_w_in,
           lru_conv_w, lru_conv_b, lru_w_a, lru_b_a, lru_w_x, lru_b_x, lru_lambda, lru_w_out,
           final_norm, reset_first):
    n_delta, n_dconv, n_gla, n_lru, n_lconv = [], [], [], [], []
    for l in range(DEPTH):
        j = l // 2
        if l % 2 == 0:
            h = _rmsnorm(x, ab_norm[j])
            y, sa, ca, sb = _ab_mixer(h, st_delta[j], st_dconv[j], st_gla[j], ab_w_in[j], ab_conv_w[j],
                                      ab_a_log[j], ab_dt_bias[j], ab_norm_a[j], ab_gla_w_lr[j],
                                      ab_gla_b_lr[j], ab_norm_b[j], ab_w_out[j])
            n_delta.append(sa)
            n_dconv.append(ca)
            n_gla.append(sb)
        else:
            h = _rmsnorm(x, lru_norm[j])
            y, hl, cl = _lru_mixer(h, st_lru[j], st_lconv[j], lru_w_in[j], lru_conv_w[j], lru_conv_b[j],
                                   lru_w_a[j], lru_b_a[j], lru_w_x[j], lru_b_x[j], lru_lambda[j],
                                   lru_w_out[j], reset_first)
            n_lru.append(hl)
            n_lconv.append(cl)
        x = x + y
    return (_rmsnorm(x, final_norm), jnp.stack(n_delta), jnp.stack(n_dconv), jnp.stack(n_gla),
            jnp.stack(n_lru), jnp.stack(n_lconv))


def setup_inputs(seed: int = 0) -> dict:
    key = jax.random.key(seed)
    ks = iter(jax.random.split(key, 32))
    f32 = jnp.float32

    def nrm(shape, scale):
        return jax.random.normal(next(ks), shape, f32) * scale

    NA, NL = N_AB_LAYERS, N_LRU_LAYERS
    out_scale = 0.5
    x_prompt = nrm((BATCH, SEQ, D_MODEL), 1.0)
    x_sample = nrm((DEC_BATCH, DEC_SEQ, D_MODEL), 1.0)
    state_delta = nrm((NA, DEC_BATCH, H_A, DK_A, DV_A), DK_A ** -0.5)
    state_delta_conv = nrm((NA, DEC_BATCH, CONV_W - 1, 2 * KA + VA), 1.0)
    state_gla = nrm((NA, DEC_BATCH, H_B, DK_B, DV_B), 0.1)
    state_lru = nrm((NL, DEC_BATCH, W_LRU), 0.5)
    state_lru_conv = nrm((NL, DEC_BATCH, CONV_W - 1, W_LRU), 1.0)
    ab_norm = 1.0 + nrm((NA, D_MODEL), 0.02)
    ab_w_in = nrm((NA, D_MODEL, AB_IN), D_MODEL ** -0.5)
    ab_conv_w = nrm((NA, CONV_W, 2 * KA + VA), CONV_W ** -0.5)
    ab_a_log = jnp.log(jax.random.uniform(next(ks), (NA, H_A), f32, 1.0, 16.0))
    dt = jnp.exp(jax.random.uniform(next(ks), (NA, H_A), f32, float(np.log(1e-3)), float(np.log(1e-1))))
    ab_dt_bias = dt + jnp.log(-jnp.expm1(-dt))
    ab_norm_a = 1.0 + nrm((NA, DV_A), 0.02)
    ab_gla_w_lr = nrm((NA, GLA_LOWRANK, KB), GLA_LOWRANK ** -0.5)
    ab_gla_b_lr = nrm((NA, KB), 0.01)
    ab_norm_b = 1.0 + nrm((NA, DV_B), 0.02)
    ab_w_out = nrm((NA, AB_OUT, D_MODEL), AB_OUT ** -0.5 * out_scale)
    lru_norm = 1.0 + nrm((NL, D_MODEL), 0.02)
    lru_w_in = nrm((NL, D_MODEL, 2 * W_LRU), D_MODEL ** -0.5)
    lru_conv_w = nrm((NL, CONV_W, W_LRU), CONV_W ** -0.5)
    lru_conv_b = nrm((NL, W_LRU), 0.01)
    lru_w_a = nrm((NL, LRU_BLOCKS, LRU_BW, LRU_BW), LRU_BW ** -0.5)
    lru_b_a = nrm((NL, W_LRU), 0.01)
    lru_w_x = nrm((NL, LRU_BLOCKS, LRU_BW, LRU_BW), LRU_BW ** -0.5)
    lru_b_x = nrm((NL, W_LRU), 0.01)
    u = jax.random.uniform(next(ks), (NL, W_LRU), f32, 0.9, 0.999)
    s = u ** (1.0 / LRU_C)
    lru_lambda = jnp.log(s) - jnp.log1p(-s)
    lru_w_out = nrm((NL, W_LRU, D_MODEL), W_LRU ** -0.5 * out_scale)
    final_norm = 1.0 + nrm((D_MODEL,), 0.02)
    return {'x_prompt': x_prompt, 'x_sample': x_sample,
            'state_delta': state_delta, 'state_delta_conv': state_delta_conv, 'state_gla': state_gla,
            'state_lru': state_lru, 'state_lru_conv': state_lru_conv,
            'ab_norm': ab_norm, 'ab_w_in': ab_w_in, 'ab_conv_w': ab_conv_w, 'ab_a_log': ab_a_log,
            'ab_dt_bias': ab_dt_bias, 'ab_norm_a': ab_norm_a, 'ab_gla_w_lr': ab_gla_w_lr,
            'ab_gla_b_lr': ab_gla_b_lr, 'ab_norm_b': ab_norm_b, 'ab_w_out': ab_w_out,
            'lru_norm': lru_norm, 'lru_w_in': lru_w_in, 'lru_conv_w': lru_conv_w, 'lru_conv_b': lru_conv_b,
            'lru_w_a': lru_w_a, 'lru_b_a': lru_b_a, 'lru_w_x': lru_w_x, 'lru_b_x': lru_b_x,
            'lru_lambda': lru_lambda, 'lru_w_out': lru_w_out, 'final_norm': final_norm}


def reference(x_prompt, x_sample, state_delta, state_delta_conv, state_gla, state_lru, state_lru_conv,
              ab_norm, ab_w_in, ab_conv_w, ab_a_log, ab_dt_bias, ab_norm_a, ab_gla_w_lr, ab_gla_b_lr,
              ab_norm_b, ab_w_out, lru_norm, lru_w_in, lru_conv_w, lru_conv_b, lru_w_a, lru_b_a,
              lru_w_x, lru_b_x, lru_lambda, lru_w_out, final_norm):
    weights = (ab_norm, ab_w_in, ab_conv_w, ab_a_log, ab_dt_bias, ab_norm_a, ab_gla_w_lr, ab_gla_b_lr,
               ab_norm_b, ab_w_out, lru_norm, lru_w_in, lru_conv_w, lru_conv_b, lru_w_a, lru_b_a,
               lru_w_x, lru_b_x, lru_lambda, lru_w_out, final_norm)
    Bp = x_prompt.shape[0]
    f32 = jnp.float32
    z_delta = jnp.zeros((N_AB_LAYERS, Bp, H_A, DK_A, DV_A), f32)
    z_dconv = jnp.zeros((N_AB_LAYERS, Bp, CONV_W - 1, 2 * KA + VA), x_prompt.dtype)
    z_gla = jnp.zeros((N_AB_LAYERS, Bp, H_B, DK_B, DV_B), f32)
    z_lru = jnp.zeros((N_LRU_LAYERS, Bp, W_LRU), f32)
    z_lconv = jnp.zeros((N_LRU_LAYERS, Bp, CONV_W - 1, W_LRU), x_prompt.dtype)
    y_prompt, p_delta, p_dconv, p_gla, p_lru, p_lconv = _trunk(
        x_prompt, z_delta, z_dconv, z_gla, z_lru, z_lconv, *weights, True)
    y_sample, s_delta, s_dconv, s_gla, s_lru, s_lconv = _trunk(
        x_sample, state_delta, state_delta_conv, state_gla, state_lru, state_lru_conv, *weights, False)
    return (y_prompt, y_sample, p_delta, p_dconv, p_gla, p_lru, p_lconv,
            s_delta, s_dconv, s_gla, s_lru, s_lconv)
```

```python
import functools

import jax
import jax.numpy as jnp
from jax import lax
from jax.experimental import pallas as pl
from jax.experimental.pallas import tpu as pltpu

F32 = jnp.float32
BF16 = jnp.bfloat16
HIGHEST = lax.Precision.HIGHEST

D_MODEL = 2048
N_HEAD_A, DK_A, DV_A = 8, 128, 128
N_HEAD_B, DK_B, DV_B = 4, 128, 256
KA = N_HEAD_A * DK_A
VA = N_HEAD_A * DV_A
KB = N_HEAD_B * DK_B
VB = N_HEAD_B * DV_B
QKV_A = 2 * KA + VA
LOWRANK = 16
GLA_NORMALIZER = 16.0
W_LRU = D_MODEL
LRU_BLOCKS = 16
LRU_BW = W_LRU // LRU_BLOCKS
LRU_C = 8.0
CONV_W = 4
EPS = 1e-6

OFF_QKV = 0
OFF_ZA = OFF_QKV + QKV_A
OFF_QB = OFF_ZA + VA
OFF_KB = OFF_QB + KB
OFF_VB = OFF_KB + KB
OFF_ZB = OFF_VB + VB
OFF_TAIL = OFF_ZB + VB
TAIL_W = 128
AB_TN = 768
AB_N = 7680
LANE_BETA, LANE_A, LANE_LR = 0, N_HEAD_A, 2 * N_HEAD_A

ROWS = 64
HIST = 8
VMEM_LIMIT = 56 * 1024 * 1024

NN = (((1,), (0,)), ((), ()))
NT = (((1,), (1,)), ((), ()))
TN = (((0,), (0,)), ((), ()))


def _mm(a, b, dims=NN):
    return lax.dot_general(a, b, dims, precision=HIGHEST, preferred_element_type=F32)


def _silu(x):
    return x * jax.nn.sigmoid(x)


def _norm_mm_kernel(x_ref, g_ref, w_ref, o_ref, h_ref):
    @pl.when(pl.program_id(1) == 0)
    def _():
        x = x_ref[...]
        ms = jnp.mean(x * x, axis=-1, keepdims=True)
        h_ref[...] = (x * lax.rsqrt(ms + EPS) * g_ref[...]).astype(BF16)

    o_ref[...] = jnp.dot(h_ref[...], w_ref[...], preferred_element_type=F32)


def _norm_mm(x, g, w, tn):
    m, k = x.shape
    n = w.shape[1]
    tm = min(m, 1024)
    return pl.pallas_call(
        _norm_mm_kernel,
        out_shape=jax.ShapeDtypeStruct((m, n), F32),
        grid=(m // tm, n // tn),
        in_specs=[
            pl.BlockSpec((tm, k), lambda i, j: (i, 0)),
            pl.BlockSpec((1, k), lambda i, j: (0, 0)),
            pl.BlockSpec((k, tn), lambda i, j: (0, j)),
        ],
        out_specs=pl.BlockSpec((tm, tn), lambda i, j: (i, j)),
        scratch_shapes=[pltpu.VMEM((tm, k), BF16)],
        compiler_params=pltpu.CompilerParams(
            dimension_semantics=("parallel", "arbitrary"), vmem_limit_bytes=VMEM_LIMIT),
        name="norm_mm",
    )(x, g.reshape(1, k), w)


def _out_mm_kernel(*refs, n_in, final):
    o_refs = refs[:n_in]
    w_ref, x_ref = refs[n_in], refs[n_in + 1]
    out_ref = refs[-1]
    acc = x_ref[...]
    k0 = 0
    for o_ref in o_refs:
        kk = o_ref.shape[1]
        acc = acc + jnp.dot(o_ref[...], w_ref[k0:k0 + kk, :], preferred_element_type=F32)
        k0 += kk
    if final:
        fg_ref = refs[n_in + 2]
        ms = jnp.mean(acc * acc, axis=-1, keepdims=True)
        acc = acc * lax.rsqrt(ms + EPS) * fg_ref[...]
    out_ref[...] = acc


def _out_mm(os_, w, x, final_g=None):
    m, d = x.shape
    tm = 512
    final = final_g is not None
    in_specs = [pl.BlockSpec((tm, o.shape[1]), lambda i: (i, 0)) for o in os_]
    in_specs += [pl.BlockSpec(w.shape, lambda i: (0, 0)), pl.BlockSpec((tm, d), lambda i: (i, 0))]
    args = list(os_) + [w, x]
    if final:
        in_specs.append(pl.BlockSpec((1, d), lambda i: (0, 0)))
        args.append(final_g.reshape(1, d))
    return pl.pallas_call(
        functools.partial(_out_mm_kernel, n_in=len(os_), final=final),
        out_shape=jax.ShapeDtypeStruct((m, d), F32),
        grid=(m // tm,),
        in_specs=in_specs,
        out_specs=pl.BlockSpec((tm, d), lambda i: (i, 0)),
        compiler_params=pltpu.CompilerParams(
            dimension_semantics=("parallel",), vmem_limit_bytes=VMEM_LIMIT),
        name="out_mm",
    )(*args)


def _causal_conv_block(x_ref, xp_ref, cw_ref, n_seq, rows, width, emit, bias_ref=None):
    lc = 512
    for s in range(n_seq):
        xp_ref[s, HIST:HIST + rows, :] = x_ref[s * rows:(s + 1) * rows, :]
        for c0 in range(0, width, lc):
            cs = slice(c0, c0 + lc)
            base = HIST - (CONV_W - 1)
            acc = xp_ref[s, base:base + rows, cs] * cw_ref[0:1, cs]
            for j in range(1, CONV_W):
                acc = acc + xp_ref[s, base + j:base + j + rows, cs] * cw_ref[j:j + 1, cs]
            if bias_ref is not None:
                acc = acc + bias_ref[:, cs]
            emit(s, cs, acc)
        xp_ref[s, 0:HIST, :] = xp_ref[s, rows:rows + HIST, :]


def _seq_masks(n, rows):
    ri = lax.broadcasted_iota(jnp.int32, (n, n), 0)
    ci = lax.broadcasted_iota(jnp.int32, (n, n), 1)
    if rows == n:
        same = ri >= 0
    else:
        sh = rows.bit_length() - 1
        same = (ri >> sh) == (ci >> sh)
    return ri, ci, same


def _unit_lower_inverse(a, ri, ci, rows):
    n = a.shape[0]
    t = jnp.where(ri == ci, 1.0, 0.0).astype(F32)
    k = 1
    while k < rows:
        sh = k.bit_length() - 1
        in_k = (ri >> sh) == (ci >> sh)
        in_2k = (ri >> (sh + 1)) == (ci >> (sh + 1))
        off = jnp.where(in_2k, jnp.where(in_k, 0.0, a), 0.0)
        t = t - _mm(_mm(t, off), t)
        k *= 2
    del n
    return t


def _delta_kernel(*refs, n_seq, rows, has_state):
    qkv_ref, z_ref, tail_ref = refs[:3]
    p = 3
    if has_state:
        s0_ref, c0_ref = refs[p], refs[p + 1]
        p += 2
    cw_ref, gp_ref, na_ref = refs[p:p + 3]
    o_ref, s_ref, cn_ref = refs[p + 3:p + 6]
    xp_ref, act_ref = refs[p + 6:p + 8]
    n = n_seq * rows
    c = pl.program_id(1)

    @pl.when(c == 0)
    def _init():
        if has_state:
            s_ref[...] = s0_ref[...]
            for s in range(n_seq):
                xp_ref[s, HIST - (CONV_W - 1):HIST, :] = c0_ref[s]
        else:
            s_ref[...] = jnp.zeros(s_ref.shape, F32)
            xp_ref[:, 0:HIST, :] = jnp.zeros((n_seq, HIST, QKV_A), F32)

    def emit(s, cs, acc):
        act_ref[s * rows:(s + 1) * rows, cs] = _silu(acc)

    _causal_conv_block(qkv_ref, xp_ref, cw_ref, n_seq, rows, QKV_A, emit)

    @pl.when(c == pl.num_programs(1) - 1)
    def _conv_out():
        for s in range(n_seq):
            cn_ref[s] = xp_ref[s, HIST - (CONV_W - 1):HIST, :]

    ri, ci, same = _seq_masks(n, rows)
    incl = same & (ri >= ci)
    strict = same & (ri > ci)
    tail = tail_ref[...]
    btile = jax.nn.sigmoid(tail)
    gtile = -jnp.exp(gp_ref[0:1, :]) * jax.nn.softplus(tail + gp_ref[1:2, :])
    g_cum = _mm(jnp.where(incl, 1.0, 0.0).astype(F32), gtile)
    g_tot = _mm(jnp.where(same, 1.0, 0.0).astype(F32), gtile)
    e_cum = jnp.exp(g_cum)
    e_rest = jnp.exp(g_tot - g_cum)
    lane = lax.broadcasted_iota(jnp.int32, (n, TAIL_W), 1)

    for h in range(N_HEAD_A):
        hl = slice(h * DK_A, (h + 1) * DK_A)
        q = act_ref[:, hl]
        k = act_ref[:, KA + h * DK_A:KA + (h + 1) * DK_A]
        v = act_ref[:, 2 * KA + h * DV_A:2 * KA + (h + 1) * DV_A]
        qn = q * lax.rsqrt(jnp.sum(q * q, axis=-1, keepdims=True) + EPS) * (DK_A ** -0.5)
        kn = k * lax.rsqrt(jnp.sum(k * k, axis=-1, keepdims=True) + EPS)
        beta = btile[:, LANE_BETA + h:LANE_BETA + h + 1]
        la = LANE_A + h
        gc = g_cum[:, la:la + 1]
        eg = e_cum[:, la:la + 1]
        er = e_rest[:, la:la + 1]
        g_row = _mm(jnp.where(lane == la, 1.0, 0.0).astype(F32), g_cum, NT)
        decay = jnp.where(incl, jnp.exp(jnp.where(incl, gc - g_row, 0.0)), 0.0)
        kb = kn * beta
        a = jnp.where(strict, _mm(kb, kn, NT) * decay, 0.0)
        t_inv = _unit_lower_inverse(a, ri, ci, rows)
        sol = _mm(t_inv, jnp.concatenate([v * beta, kb * eg], axis=1))
        u, w = sol[:, :DV_A], sol[:, DV_A:]
        qk = _mm(qn, kn, NT) * decay
        qg = qn * eg
        kdec = kn * er
        v_new, o_inter = [], []
        for s in range(n_seq):
            sl = slice(s * rows, (s + 1) * rows)
            st = s_ref[s, h]
            v_new.append(u[sl] - _mm(w[sl], st))
            o_inter.append(_mm(qg[sl], st))
        v_new = v_new[0] if n_seq == 1 else jnp.concatenate(v_new, axis=0)
        o_inter = o_inter[0] if n_seq == 1 else jnp.concatenate(o_inter, axis=0)
        o = o_inter + _mm(qk, v_new)
        for s in range(n_seq):
            sl = slice(s * rows, (s + 1) * rows)
            g_last = g_tot[s * rows:s * rows + 1, la:la + 1]
            s_ref[s, h] = s_ref[s, h] * jnp.exp(g_last) + _mm(kdec[sl], v_new[sl], TN)
        o = o * lax.rsqrt(jnp.mean(o * o, axis=-1, keepdims=True) + EPS) * na_ref[...]
        o_ref[:, hl] = (o * _silu(z_ref[:, hl])).astype(BF16)


def _delta(proj, s0, c0, conv_w, gate_params, norm_a, n_batch, t_len):
    rows = min(ROWS, t_len)
    n_seq = ROWS // rows
    nb = n_batch // n_seq
    nc = t_len // rows
    has_state = s0 is not None
    row_map = lambda i, c: (i * nc + c, 0)
    in_specs = [
        pl.BlockSpec((ROWS, QKV_A), row_map),
        pl.BlockSpec((ROWS, VA), lambda i, c: (i * nc + c, OFF_ZA // VA)),
        pl.BlockSpec((ROWS, TAIL_W), lambda i, c: (i * nc + c, OFF_TAIL // TAIL_W)),
    ]
    args = [proj, proj, proj]
    if has_state:
        in_specs += [
            pl.BlockSpec((n_seq, N_HEAD_A, DK_A, DV_A), lambda i, c: (i, 0, 0, 0)),
            pl.BlockSpec((n_seq, CONV_W - 1, QKV_A), lambda i, c: (i, 0, 0)),
        ]
        args += [s0, c0]
    in_specs += [
        pl.BlockSpec((CONV_W, QKV_A), lambda i, c: (0, 0)),
        pl.BlockSpec((2, TAIL_W), lambda i, c: (0, 0)),
        pl.BlockSpec((1, DV_A), lambda i, c: (0, 0)),
    ]
    args += [conv_w, gate_params, norm_a.reshape(1, DV_A)]
    return pl.pallas_call(
        functools.partial(_delta_kernel, n_seq=n_seq, rows=rows, has_state=has_state),
        out_shape=(
            jax.ShapeDtypeStruct((n_batch * t_len, VA), BF16),
            jax.ShapeDtypeStruct((n_batch, N_HEAD_A, DK_A, DV_A), F32),
            jax.ShapeDtypeStruct((n_batch, CONV_W - 1, QKV_A), F32),
        ),
        grid=(nb, nc),
        in_specs=in_specs,
        out_specs=(
            pl.BlockSpec((ROWS, VA), row_map),
            pl.BlockSpec((n_seq, N_HEAD_A, DK_A, DV_A), lambda i, c: (i, 0, 0, 0)),
            pl.BlockSpec((n_seq, CONV_W - 1, QKV_A), lambda i, c: (i, 0, 0)),
        ),
        scratch_shapes=[
            pltpu.VMEM((n_seq, HIST + rows, QKV_A), F32),
            pltpu.VMEM((ROWS, QKV_A), F32),
        ],
        compiler_params=pltpu.CompilerParams(
            dimension_semantics=("parallel", "arbitrary"), vmem_limit_bytes=VMEM_LIMIT),
        name="delta",
    )(*args)


def _gla_kernel(*refs, n_state, rows, has_state):
    q_ref, k_ref, v_ref, z_ref, tail_ref = refs[:5]
    p = 5
    if has_state:
        s0_ref = refs[p]
        p += 1
    wlr_ref, blr_ref, nb_ref = refs[p:p + 3]
    o_ref, s_ref = refs[p + 3:p + 5]
    n = ROWS
    n_sub = n // rows
    c = pl.program_id(1)

    @pl.when(c == 0)
    def _init():
        if has_state:
            s_ref[...] = s0_ref[...]
        else:
            s_ref[...] = jnp.zeros(s_ref.shape, F32)

    ri, ci, same = _seq_masks(n, rows)
    incl = same & (ri >= ci)
    gk = jax.nn.log_sigmoid(_mm(tail_ref[...], wlr_ref[...]) + blr_ref[...]) / GLA_NORMALIZER
    b_cum = _mm(jnp.where(incl, 1.0, 0.0).astype(F32), gk)
    b_tot = _mm(jnp.where(same, 1.0, 0.0).astype(F32), gk)
    qg = q_ref[...] * (DK_B ** -0.5) * jnp.exp(b_cum)
    k = k_ref[...]
    kd = k * jnp.exp(-b_cum)
    kdec = k * jnp.exp(b_tot - b_cum)

    for h in range(N_HEAD_B):
        hl = slice(h * DK_B, (h + 1) * DK_B)
        vl = slice(h * DV_B, (h + 1) * DV_B)
        v = v_ref[:, vl]
        a = jnp.where(incl, _mm(qg[:, hl], kd[:, hl], NT), 0.0)
        intra = _mm(a, v)
        outs = []
        for s in range(n_sub):
            sl = slice(s * rows, (s + 1) * rows)
            si = s if n_state > 1 else 0
            st = s_ref[si, h]
            outs.append(_mm(qg[sl, hl], st) + intra[sl])
            e_col = jnp.exp(b_tot[s * rows:s * rows + 8, hl].T[:, 0:1])
            s_ref[si, h] = st * e_col + _mm(kdec[sl, hl], v[sl], TN)
        o = jnp.concatenate(outs, axis=0)
        o = o * lax.rsqrt(jnp.mean(o * o, axis=-1, keepdims=True) + EPS) * nb_ref[...]
        o_ref[:, vl] = (o * _silu(z_ref[:, vl])).astype(BF16)


def _gla(proj, s0, w_lr_pad, b_lr, norm_b, n_batch, t_len, chunk):
    rows = min(chunk, t_len)
    has_state = s0 is not None
    if t_len >= ROWS:
        n_state, nb, nc = 1, n_batch, t_len // ROWS
    else:
        n_state, nb, nc = ROWS // t_len, n_batch // (ROWS // t_len), 1
    in_specs = [
        pl.BlockSpec((ROWS, KB), lambda i, c: (i * nc + c, OFF_QB // KB)),
        pl.BlockSpec((ROWS, KB), lambda i, c: (i * nc + c, OFF_KB // KB)),
        pl.BlockSpec((ROWS, VB), lambda i, c: (i * nc + c, OFF_VB // VB)),
        pl.BlockSpec((ROWS, VB), lambda i, c: (i * nc + c, OFF_ZB // VB)),
        pl.BlockSpec((ROWS, TAIL_W), lambda i, c: (i * nc + c, OFF_TAIL // TAIL_W)),
    ]
    args = [proj] * 5
    if has_state:
        in_specs.append(pl.BlockSpec((n_state, N_HEAD_B, DK_B, DV_B), lambda i, c: (i, 0, 0, 0)))
        args.append(s0)
    in_specs += [
        pl.BlockSpec((TAIL_W, KB), lambda i, c: (0, 0)),
        pl.BlockSpec((1, KB), lambda i, c: (0, 0)),
        pl.BlockSpec((1, DV_B), lambda i, c: (0, 0)),
    ]
    args += [w_lr_pad, b_lr.reshape(1, KB), norm_b.reshape(1, DV_B)]
    return pl.pallas_call(
        functools.partial(_gla_kernel, n_state=n_state, rows=rows, has_state=has_state),
        out_shape=(
            jax.ShapeDtypeStruct((n_batch * t_len, VB), BF16),
            jax.ShapeDtypeStruct((n_batch, N_HEAD_B, DK_B, DV_B), F32),
        ),
        grid=(nb, nc),
        in_specs=in_specs,
        out_specs=(
            pl.BlockSpec((ROWS, VB), lambda i, c: (i * nc + c, 0)),
            pl.BlockSpec((n_state, N_HEAD_B, DK_B, DV_B), lambda i, c: (i, 0, 0, 0)),
        ),
        compiler_params=pltpu.CompilerParams(
            dimension_semantics=("parallel", "arbitrary"), vmem_limit_bytes=VMEM_LIMIT),
        name="gla",
    )(*args)


def _lru_kernel(*refs, n_seq, rows, has_state, reset_first):
    xb_ref, gate_ref = refs[:2]
    p = 2
    if has_state:
        h0_ref, c0_ref = refs[p], refs[p + 1]
        p += 2
    cw_ref, cb_ref, wa_ref, ba_ref, wx_ref, bx_ref, lam_ref = refs[p:p + 7]
    y_ref, h_ref, cn_ref = refs[p + 7:p + 10]
    xp_ref, xc_ref, a_ref, b_ref = refs[p + 10:p + 14]
    n = n_seq * rows
    c = pl.program_id(1)

    @pl.when(c == 0)
    def _init():
        if has_state:
            h_ref[...] = h0_ref[...]
            for s in range(n_seq):
                xp_ref[s, HIST - (CONV_W - 1):HIST, :] = c0_ref[s]
        else:
            h_ref[...] = jnp.zeros(h_ref.shape, F32)
            xp_ref[:, 0:HIST, :] = jnp.zeros((n_seq, HIST, W_LRU), F32)

    def emit(s, cs, acc):
        xc_ref[s * rows:(s + 1) * rows, cs] = acc

    _causal_conv_block(xb_ref, xp_ref, cw_ref, n_seq, rows, W_LRU, emit, bias_ref=cb_ref)

    @pl.when(c == pl.num_programs(1) - 1)
    def _conv_out():
        for s in range(n_seq):
            cn_ref[s] = xp_ref[s, HIST - (CONV_W - 1):HIST, :]

    row = lax.broadcasted_iota(jnp.int32, (n, LRU_BW), 0)
    sub = row & 7
    for blk in range(LRU_BLOCKS):
        bl = slice(blk * LRU_BW, (blk + 1) * LRU_BW)
        xc = xc_ref[:, bl]
        r = jax.nn.sigmoid(_mm(xc, wa_ref[blk]) + ba_ref[:, bl])
        gi = jax.nn.sigmoid(_mm(xc, wx_ref[blk]) + bx_ref[:, bl])
        log_a = -LRU_C * r * jax.nn.softplus(-lam_ref[:, bl])
        a = jnp.exp(log_a)
        mult = jnp.sqrt(-jnp.tanh(log_a) * (a * a + 1.0))
        if reset_first:
            mult = jnp.where((row == 0) & (c == 0), 1.0, mult)
        b = mult * gi * xc
        for d in (1, 2, 4):
            a_sh = pltpu.roll(a, d, axis=0)
            b_sh = pltpu.roll(b, d, axis=0)
            m = sub >= d
            b = jnp.where(m, a * b_sh + b, b)
            a = jnp.where(m, a * a_sh, a)
        a_ref[:, bl] = a
        b_ref[:, bl] = b

    for s in range(n_seq):
        def body(g, hp, s=s):
            r0 = pl.multiple_of(s * rows + g * 8, 8)
            hs = a_ref[pl.ds(r0, 8), :] * hp + b_ref[pl.ds(r0, 8), :]
            b_ref[pl.ds(r0, 8), :] = hs
            return hs[7:8, :]

        h_ref[s] = lax.fori_loop(0, rows // 8, body, h_ref[s])

    lc = 512
    for c0 in range(0, W_LRU, lc):
        cs = slice(c0, c0 + lc)
        y_ref[:, cs] = (b_ref[:, cs] * _silu(gate_ref[:, cs])).astype(BF16)


def _lru(proj, h0, c0, conv_w, conv_b, w_a, b_a, w_x, b_x, lam, n_batch, t_len, reset_first):
    has_state = h0 is not None
    if t_len >= 256:
        rows, n_seq = 256, 1
    else:
        rows, n_seq = t_len, 128 // t_len
    n = rows * n_seq
    nb = n_batch // n_seq
    nc = t_len // rows
    in_specs = [
        pl.BlockSpec((n, W_LRU), lambda i, c: (i * nc + c, 0)),
        pl.BlockSpec((n, W_LRU), lambda i, c: (i * nc + c, 1)),
    ]
    args = [proj, proj]
    if has_state:
        in_specs += [
            pl.BlockSpec((n_seq, 1, W_LRU), lambda i, c: (i, 0, 0)),
            pl.BlockSpec((n_seq, CONV_W - 1, W_LRU), lambda i, c: (i, 0, 0)),
        ]
        args += [h0.reshape(n_batch, 1, W_LRU), c0]
    vec = pl.BlockSpec((1, W_LRU), lambda i, c: (0, 0))
    blk = pl.BlockSpec((LRU_BLOCKS, LRU_BW, LRU_BW), lambda i, c: (0, 0, 0))
    in_specs += [pl.BlockSpec((CONV_W, W_LRU), lambda i, c: (0, 0)), vec, blk, vec, blk, vec, vec]
    args += [conv_w, conv_b.reshape(1, W_LRU), w_a, b_a.reshape(1, W_LRU), w_x,
             b_x.reshape(1, W_LRU), lam.reshape(1, W_LRU)]
    y, h, cn = pl.pallas_call(
        functools.partial(_lru_kernel, n_seq=n_seq, rows=rows, has_state=has_state,
                          reset_first=reset_first),
        out_shape=(
            jax.ShapeDtypeStruct((n_batch * t_len, W_LRU), BF16),
            jax.ShapeDtypeStruct((n_batch, 1, W_LRU), F32),
            jax.ShapeDtypeStruct((n_batch, CONV_W - 1, W_LRU), F32),
        ),
        grid=(nb, nc),
        in_specs=in_specs,
        out_specs=(
            pl.BlockSpec((n, W_LRU), lambda i, c: (i * nc + c, 0)),
            pl.BlockSpec((n_seq, 1, W_LRU), lambda i, c: (i, 0, 0)),
            pl.BlockSpec((n_seq, CONV_W - 1, W_LRU), lambda i, c: (i, 0, 0)),
        ),
        scratch_shapes=[
            pltpu.VMEM((n_seq, HIST + rows, W_LRU), F32),
            pltpu.VMEM((n, W_LRU), F32),
            pltpu.VMEM((n, W_LRU), F32),
            pltpu.VMEM((n, W_LRU), F32),
        ],
        compiler_params=pltpu.CompilerParams(
            dimension_semantics=("parallel", "arbitrary"), vmem_limit_bytes=VMEM_LIMIT),
        name="lru",
    )(*args)
    return y, h.reshape(n_batch, W_LRU), cn


def _ab_w_in_layout(w):
    sizes = [QKV_A, N_HEAD_A, N_HEAD_A, VA, KB, KB, VB, LOWRANK, VB]
    offs = [0]
    for s in sizes:
        offs.append(offs[-1] + s)
    qkv, b_raw, a_raw, z_a, q_b, k_b, v_b, lr_b, z_b = [w[:, offs[i]:offs[i + 1]] for i in range(9)]
    used = OFF_TAIL + 2 * N_HEAD_A + LOWRANK
    pad = jnp.zeros((w.shape[0], AB_N - used), w.dtype)
    return jnp.concatenate([qkv, z_a, q_b, k_b, v_b, z_b, b_raw, a_raw, lr_b, pad], axis=1).astype(BF16)


def _tail_row(vals, lane0):
    return jnp.zeros((TAIL_W,), F32).at[lane0:lane0 + vals.shape[0]].set(vals.astype(F32))


def kernel(x_prompt, x_sample, state_delta, state_delta_conv, state_gla, state_lru, state_lru_conv,
           ab_norm, ab_w_in, ab_conv_w, ab_a_log, ab_dt_bias, ab_norm_a, ab_gla_w_lr, ab_gla_b_lr,
           ab_norm_b, ab_w_out, lru_norm, lru_w_in, lru_conv_w, lru_conv_b, lru_w_a, lru_b_a,
           lru_w_x, lru_b_x, lru_lambda, lru_w_out, final_norm):
    depth = ab_norm.shape[0] + lru_norm.shape[0]
    groups = []
    for x, carried in ((x_prompt, False), (x_sample, True)):
        groups.append(dict(x=x.reshape(-1, D_MODEL), nb=x.shape[0], t=x.shape[1], carried=carried,
                           delta=[], dconv=[], gla=[], lru=[], lconv=[]))

    for layer in range(depth):
        j = layer // 2
        last = layer == depth - 1
        if layer % 2 == 0:
            w_in = _ab_w_in_layout(ab_w_in[j])
            w_out = ab_w_out[j].astype(BF16)
            gate_params = jnp.stack([_tail_row(ab_a_log[j], LANE_A), _tail_row(ab_dt_bias[j], LANE_A)])
            w_lr_pad = jnp.zeros((TAIL_W, KB), F32).at[LANE_LR:LANE_LR + LOWRANK].set(ab_gla_w_lr[j])
            for g in groups:
                proj = _norm_mm(g["x"], ab_norm[j], w_in, AB_TN)
                s_a = state_delta[j] if g["carried"] else None
                c_a = state_delta_conv[j] if g["carried"] else None
                s_b = state_gla[j] if g["carried"] else None
                o_a, n_sa, n_ca = _delta(proj, s_a, c_a, ab_conv_w[j], gate_params, ab_norm_a[j],
                                         g["nb"], g["t"])
                o_b, n_sb = _gla(proj, s_b, w_lr_pad, ab_gla_b_lr[j], ab_norm_b[j], g["nb"], g["t"], 16)
                g["x"] = _out_mm([o_a, o_b], w_out, g["x"], final_norm if last else None)
                g["delta"].append(n_sa)
                g["dconv"].append(n_ca)
                g["gla"].append(n_sb)
        else:
            w_in = lru_w_in[j].astype(BF16)
            w_out = lru_w_out[j].astype(BF16)
            for g in groups:
                proj = _norm_mm(g["x"], lru_norm[j], w_in, 1024)
                h0 = state_lru[j] if g["carried"] else None
                c0 = state_lru_conv[j] if g["carried"] else None
                y, n_h, n_c = _lru(proj, h0, c0, lru_conv_w[j], lru_conv_b[j], lru_w_a[j], lru_b_a[j],
                                   lru_w_x[j], lru_b_x[j], lru_lambda[j], g["nb"], g["t"],
                                   reset_first=not g["carried"])
                g["x"] = _out_mm([y], w_out, g["x"], final_norm if last else None)
                g["lru"].append(n_h)
                g["lconv"].append(n_c)

    outs = []
    for g, x in zip(groups, (x_prompt, x_sample)):
        outs.append([g["x"].reshape(x.shape), jnp.stack(g["delta"]), jnp.stack(g["dconv"]),
                     jnp.stack(g["gla"]), jnp.stack(g["lru"]), jnp.stack(g["lconv"])])
    p, s = outs
    return (p[0], s[0], p[1], p[2], p[3], p[4], p[5], s[1], s[2], s[3], s[4], s[5])
```

```python
import functools

import jax
import jax.numpy as jnp
from jax import lax
from jax.experimental import pallas as pl
from jax.experimental.pallas import tpu as pltpu

F32 = jnp.float32
BF16 = jnp.bfloat16
HIGHEST = lax.Precision.HIGHEST

D_MODEL = 2048
N_HEAD_A, DK_A, DV_A = 8, 128, 128
N_HEAD_B, DK_B, DV_B = 4, 128, 256
KA = N_HEAD_A * DK_A
VA = N_HEAD_A * DV_A
KB = N_HEAD_B * DK_B
VB = N_HEAD_B * DV_B
QKV_A = 2 * KA + VA
LOWRANK = 16
GLA_NORMALIZER = 16.0
W_LRU = D_MODEL
LRU_BLOCKS = 16
LRU_BW = W_LRU // LRU_BLOCKS
LRU_C = 8.0
CONV_W = 4
EPS = 1e-6

OFF_QKV = 0
OFF_ZA = OFF_QKV + QKV_A
OFF_QB = OFF_ZA + VA
OFF_KB = OFF_QB + KB
OFF_VB = OFF_KB + KB
OFF_ZB = OFF_VB + VB
OFF_TAIL = OFF_ZB + VB
TAIL_W = 128
AB_TN = 768
AB_N = 7680
LANE_BETA, LANE_A, LANE_LR = 0, N_HEAD_A, 2 * N_HEAD_A

ROWS = 64
HEAD_GROUP = 4
HIST = 8
VMEM_LIMIT = 56 * 1024 * 1024

NN = (((1,), (0,)), ((), ()))
NT = (((1,), (1,)), ((), ()))
TN = (((0,), (0,)), ((), ()))


def _mm(a, b, dims=NN):
    return lax.dot_general(a, b, dims, precision=HIGHEST, preferred_element_type=F32)


def _split(a):
    hi = a.astype(BF16)
    return hi, (a - hi.astype(F32)).astype(BF16)


def _mm3(a, b, dims=NN):
    ah, al = a if isinstance(a, tuple) else _split(a)
    bh, bl = b if isinstance(b, tuple) else _split(b)
    dot = functools.partial(lax.dot_general, dimension_numbers=dims, preferred_element_type=F32)
    return dot(ah, bh) + (dot(ah, bl) + dot(al, bh))


def _silu(x):
    return x * jax.nn.sigmoid(x)


def _norm_mm_kernel(x_ref, g_ref, w_ref, o_ref, h_ref):
    @pl.when(pl.program_id(1) == 0)
    def _():
        x = x_ref[...]
        ms = jnp.mean(x * x, axis=-1, keepdims=True)
        h_ref[...] = (x * lax.rsqrt(ms + EPS) * g_ref[...]).astype(BF16)

    o_ref[...] = jnp.dot(h_ref[...], w_ref[...], preferred_element_type=F32)


def _norm_mm(x, g, w, tn):
    m, k = x.shape
    n = w.shape[1]
    tm = min(m, 1024)
    return pl.pallas_call(
        _norm_mm_kernel,
        out_shape=jax.ShapeDtypeStruct((m, n), F32),
        grid=(m // tm, n // tn),
        in_specs=[
            pl.BlockSpec((tm, k), lambda i, j: (i, 0)),
            pl.BlockSpec((1, k), lambda i, j: (0, 0)),
            pl.BlockSpec((k, tn), lambda i, j: (0, j)),
        ],
        out_specs=pl.BlockSpec((tm, tn), lambda i, j: (i, j)),
        scratch_shapes=[pltpu.VMEM((tm, k), BF16)],
        compiler_params=pltpu.CompilerParams(
            dimension_semantics=("parallel", "arbitrary"), vmem_limit_bytes=VMEM_LIMIT),
        name="norm_mm",
    )(x, g.reshape(1, k), w)


def _out_mm_kernel(*refs, n_in, final):
    o_refs = refs[:n_in]
    w_ref, x_ref = refs[n_in], refs[n_in + 1]
    out_ref = refs[-1]
    acc = x_ref[...]
    k0 = 0
    for o_ref in o_refs:
        kk = o_ref.shape[1]
        acc = acc + jnp.dot(o_ref[...], w_ref[k0:k0 + kk, :], preferred_element_type=F32)
        k0 += kk
    if final:
        fg_ref = refs[n_in + 2]
        ms = jnp.mean(acc * acc, axis=-1, keepdims=True)
        acc = acc * lax.rsqrt(ms + EPS) * fg_ref[...]
    out_ref[...] = acc


def _out_mm(os_, w, x, final_g=None):
    m, d = x.shape
    tm = 512
    final = final_g is not None
    in_specs = [pl.BlockSpec((tm, o.shape[1]), lambda i: (i, 0)) for o in os_]
    in_specs += [pl.BlockSpec(w.shape, lambda i: (0, 0)), pl.BlockSpec((tm, d), lambda i: (i, 0))]
    args = list(os_) + [w, x]
    if final:
        in_specs.append(pl.BlockSpec((1, d), lambda i: (0, 0)))
        args.append(final_g.reshape(1, d))
    return pl.pallas_call(
        functools.partial(_out_mm_kernel, n_in=len(os_), final=final),
        out_shape=jax.ShapeDtypeStruct((m, d), F32),
        grid=(m // tm,),
        in_specs=in_specs,
        out_specs=pl.BlockSpec((tm, d), lambda i: (i, 0)),
        compiler_params=pltpu.CompilerParams(
            dimension_semantics=("parallel",), vmem_limit_bytes=VMEM_LIMIT),
        name="out_mm",
    )(*args)


def _causal_conv_block(x_ref, xp_ref, cw_ref, n_seq, rows, width, emit, bias_ref=None):
    lc = 512
    for s in range(n_seq):
        xp_ref[s, HIST:HIST + rows, :] = x_ref[s * rows:(s + 1) * rows, :]
        for c0 in range(0, width, lc):
            cs = slice(c0, c0 + lc)
            base = HIST - (CONV_W - 1)
            acc = xp_ref[s, base:base + rows, cs] * cw_ref[0:1, cs]
            for j in range(1, CONV_W):
                acc = acc + xp_ref[s, base + j:base + j + rows, cs] * cw_ref[j:j + 1, cs]
            if bias_ref is not None:
                acc = acc + bias_ref[:, cs]
            emit(s, cs, acc)
        xp_ref[s, 0:HIST, :] = xp_ref[s, rows:rows + HIST, :]


def _seq_masks(n, rows):
    ri = lax.broadcasted_iota(jnp.int32, (n, n), 0)
    ci = lax.broadcasted_iota(jnp.int32, (n, n), 1)
    if rows == n:
        same = ri >= 0
    else:
        sh = rows.bit_length() - 1
        same = (ri >> sh) == (ci >> sh)
    return ri, ci, same


def _unit_lower_inverse(a, ri, ci, rows):
    t = jnp.where(ri == ci, 1.0, 0.0).astype(F32)
    k = 1
    while k < rows:
        sh = k.bit_length() - 1
        in_k = (ri >> sh) == (ci >> sh)
        in_2k = (ri >> (sh + 1)) == (ci >> (sh + 1))
        off = jnp.where(in_2k, jnp.where(in_k, 0.0, a), 0.0)
        ts = _split(t)
        t = t - _mm3(_mm3(ts, off), ts)
        k *= 2
    return t


def _delta_kernel(*refs, n_seq, rows, has_state):
    qkv_ref, z_ref, tail_ref = refs[:3]
    p = 3
    if has_state:
        s0_ref, c0_ref = refs[p], refs[p + 1]
        p += 2
    cw_ref, gp_ref, na_ref = refs[p:p + 3]
    o_ref, s_ref, cn_ref = refs[p + 3:p + 6]
    xp_ref, act_ref = refs[p + 6:p + 8]
    n = n_seq * rows
    c = pl.program_id(1)

    @pl.when(c == 0)
    def _init():
        if has_state:
            s_ref[...] = s0_ref[...]
            for s in range(n_seq):
                xp_ref[s, HIST - (CONV_W - 1):HIST, :] = c0_ref[s]
        else:
            s_ref[...] = jnp.zeros(s_ref.shape, F32)
            xp_ref[:, 0:HIST, :] = jnp.zeros((n_seq, HIST, QKV_A), F32)

    def emit(s, cs, acc):
        act_ref[s * rows:(s + 1) * rows, cs] = _silu(acc)

    _causal_conv_block(qkv_ref, xp_ref, cw_ref, n_seq, rows, QKV_A, emit)

    @pl.when(c == pl.num_programs(1) - 1)
    def _conv_out():
        for s in range(n_seq):
            cn_ref[s] = xp_ref[s, HIST - (CONV_W - 1):HIST, :]

    _, _, same_n = _seq_masks(n, rows)
    rn = lax.broadcasted_iota(jnp.int32, (n, n), 0)
    cn = lax.broadcasted_iota(jnp.int32, (n, n), 1)
    tail = tail_ref[...]
    btile = jax.nn.sigmoid(tail)
    gtile = -jnp.exp(gp_ref[0:1, :]) * jax.nn.softplus(tail + gp_ref[1:2, :])
    g_cum = _mm(jnp.where(same_n & (rn >= cn), 1.0, 0.0).astype(F32), gtile)
    g_tot = _mm(jnp.where(same_n, 1.0, 0.0).astype(F32), gtile)
    e_cum = jnp.exp(g_cum)
    e_rest = jnp.exp(g_tot - g_cum)
    g_cum_t = g_cum.T

    m = HEAD_GROUP * n
    ri, ci, same = _seq_masks(m, rows)
    incl = same & (ri >= ci)
    strict = same & (ri > ci)
    stack = lambda xs: jnp.concatenate(xs, axis=0)
    for grp in range(N_HEAD_A // HEAD_GROUP):
        heads = range(grp * HEAD_GROUP, (grp + 1) * HEAD_GROUP)
        qn, kn, v = [], [], []
        for h in heads:
            q = act_ref[:, h * DK_A:(h + 1) * DK_A]
            k = act_ref[:, KA + h * DK_A:KA + (h + 1) * DK_A]
            qn.append(q * lax.rsqrt(jnp.sum(q * q, axis=-1, keepdims=True) + EPS) * (DK_A ** -0.5))
            kn.append(k * lax.rsqrt(jnp.sum(k * k, axis=-1, keepdims=True) + EPS))
            v.append(act_ref[:, 2 * KA + h * DV_A:2 * KA + (h + 1) * DV_A])
        qn, kn, v = stack(qn), stack(kn), stack(v)
        beta = stack([btile[:, LANE_BETA + h:LANE_BETA + h + 1] for h in heads])
        gc = stack([g_cum[:, LANE_A + h:LANE_A + h + 1] for h in heads])
        eg = stack([e_cum[:, LANE_A + h:LANE_A + h + 1] for h in heads])
        er = stack([e_rest[:, LANE_A + h:LANE_A + h + 1] for h in heads])
        g_row = jnp.concatenate([g_cum_t[LANE_A + h:LANE_A + h + 1, :] for h in heads], axis=1)
        decay = jnp.where(incl, jnp.exp(jnp.where(incl, gc - g_row, 0.0)), 0.0)
        kb = kn * beta
        prod = _mm3(stack([kb, qn]), kn, NT)
        a = jnp.where(strict, prod[:m] * decay, 0.0)
        qk = prod[m:] * decay
        t_inv = _unit_lower_inverse(a, ri, ci, rows)
        sol = _mm3(t_inv, jnp.concatenate([v * beta, kb * eg], axis=1))
        u, w = sol[:, :DV_A], sol[:, DV_A:]
        qg = qn * eg
        kdec = kn * er
        v_new, o_inter = [], []
        for i, h in enumerate(heads):
            for s in range(n_seq):
                sl = slice(i * n + s * rows, i * n + (s + 1) * rows)
                r = _mm3(stack([w[sl], qg[sl]]), s_ref[s, h])
                v_new.append(u[sl] - r[:rows])
                o_inter.append(r[rows:])
        v_new, o_inter = stack(v_new), stack(o_inter)
        o = o_inter + _mm3(qk, v_new)
        for i, h in enumerate(heads):
            la = LANE_A + h
            for s in range(n_seq):
                sl = slice(i * n + s * rows, i * n + (s + 1) * rows)
                g_last = g_tot[s * rows:s * rows + 1, la:la + 1]
                s_ref[s, h] = s_ref[s, h] * jnp.exp(g_last) + _mm3(kdec[sl], v_new[sl], TN)
            hl = slice(h * DV_A, (h + 1) * DV_A)
            oh = o[i * n:(i + 1) * n]
            oh = oh * lax.rsqrt(jnp.mean(oh * oh, axis=-1, keepdims=True) + EPS) * na_ref[...]
            o_ref[:, hl] = (oh * _silu(z_ref[:, hl])).astype(BF16)


def _delta(proj, s0, c0, conv_w, gate_params, norm_a, n_batch, t_len):
    rows = min(ROWS, t_len)
    n_seq = ROWS // rows
    nb = n_batch // n_seq
    nc = t_len // rows
    has_state = s0 is not None
    row_map = lambda i, c: (i * nc + c, 0)
    in_specs = [
        pl.BlockSpec((ROWS, QKV_A), row_map),
        pl.BlockSpec((ROWS, VA), lambda i, c: (i * nc + c, OFF_ZA // VA)),
        pl.BlockSpec((ROWS, TAIL_W), lambda i, c: (i * nc + c, OFF_TAIL // TAIL_W)),
    ]
    args = [proj, proj, proj]
    if has_state:
        in_specs += [
            pl.BlockSpec((n_seq, N_HEAD_A, DK_A, DV_A), lambda i, c: (i, 0, 0, 0)),
            pl.BlockSpec((n_seq, CONV_W - 1, QKV_A), lambda i, c: (i, 0, 0)),
        ]
        args += [s0, c0]
    in_specs += [
        pl.BlockSpec((CONV_W, QKV_A), lambda i, c: (0, 0)),
        pl.BlockSpec((2, TAIL_W), lambda i, c: (0, 0)),
        pl.BlockSpec((1, DV_A), lambda i, c: (0, 0)),
    ]
    args += [conv_w, gate_params, norm_a.reshape(1, DV_A)]
    return pl.pallas_call(
        functools.partial(_delta_kernel, n_seq=n_seq, rows=rows, has_state=has_state),
        out_shape=(
            jax.ShapeDtypeStruct((n_batch * t_len, VA), BF16),
            jax.ShapeDtypeStruct((n_batch, N_HEAD_A, DK_A, DV_A), F32),
            jax.ShapeDtypeStruct((n_batch, CONV_W - 1, QKV_A), F32),
        ),
        grid=(nb, nc),
        in_specs=in_specs,
        out_specs=(
            pl.BlockSpec((ROWS, VA), row_map),
            pl.BlockSpec((n_seq, N_HEAD_A, DK_A, DV_A), lambda i, c: (i, 0, 0, 0)),
            pl.BlockSpec((n_seq, CONV_W - 1, QKV_A), lambda i, c: (i, 0, 0)),
        ),
        scratch_shapes=[
            pltpu.VMEM((n_seq, HIST + rows, QKV_A), F32),
            pltpu.VMEM((ROWS, QKV_A), F32),
        ],
        compiler_params=pltpu.CompilerParams(
            dimension_semantics=("parallel", "arbitrary"), vmem_limit_bytes=VMEM_LIMIT),
        name="delta",
    )(*args)


def _gla_kernel(*refs, n_state, rows, has_state):
    q_ref, k_ref, v_ref, z_ref, tail_ref = refs[:5]
    p = 5
    if has_state:
        s0_ref = refs[p]
        p += 1
    wlr_ref, blr_ref, nb_ref = refs[p:p + 3]
    o_ref, s_ref = refs[p + 3:p + 5]
    n = ROWS
    n_sub = n // rows
    c = pl.program_id(1)

    @pl.when(c == 0)
    def _init():
        if has_state:
            s_ref[...] = s0_ref[...]
        else:
            s_ref[...] = jnp.zeros(s_ref.shape, F32)

    ri, ci, same = _seq_masks(n, rows)
    incl = same & (ri >= ci)
    gk = jax.nn.log_sigmoid(_mm(tail_ref[...], wlr_ref[...]) + blr_ref[...]) / GLA_NORMALIZER
    b_cum = _mm(jnp.where(incl, 1.0, 0.0).astype(F32), gk)
    b_tot = _mm(jnp.where(same, 1.0, 0.0).astype(F32), gk)
    qg = q_ref[...] * (DK_B ** -0.5) * jnp.exp(b_cum)
    k = k_ref[...]
    kd = k * jnp.exp(-b_cum)
    kdec = k * jnp.exp(b_tot - b_cum)

    for h in range(N_HEAD_B):
        hl = slice(h * DK_B, (h + 1) * DK_B)
        vl = slice(h * DV_B, (h + 1) * DV_B)
        v = v_ref[:, vl]
        a = jnp.where(incl, _mm3(qg[:, hl], kd[:, hl], NT), 0.0)
        intra = _mm3(a, v)
        outs = []
        for s in range(n_sub):
            sl = slice(s * rows, (s + 1) * rows)
            si = s if n_state > 1 else 0
            st = s_ref[si, h]
            outs.append(_mm3(qg[sl, hl], st) + intra[sl])
            e_col = jnp.exp(b_tot[s * rows:s * rows + 8, hl].T[:, 0:1])
            s_ref[si, h] = st * e_col + _mm3(kdec[sl, hl], v[sl], TN)
        o = jnp.concatenate(outs, axis=0)
        o = o * lax.rsqrt(jnp.mean(o * o, axis=-1, keepdims=True) + EPS) * nb_ref[...]
        o_ref[:, vl] = (o * _silu(z_ref[:, vl])).astype(BF16)


def _gla(proj, s0, w_lr_pad, b_lr, norm_b, n_batch, t_len, chunk):
    rows = min(chunk, t_len)
    has_state = s0 is not None
    if t_len >= ROWS:
        n_state, nb, nc = 1, n_batch, t_len // ROWS
    else:
        n_state, nb, nc = ROWS // t_len, n_batch // (ROWS // t_len), 1
    in_specs = [
        pl.BlockSpec((ROWS, KB), lambda i, c: (i * nc + c, OFF_QB // KB)),
        pl.BlockSpec((ROWS, KB), lambda i, c: (i * nc + c, OFF_KB // KB)),
        pl.BlockSpec((ROWS, VB), lambda i, c: (i * nc + c, OFF_VB // VB)),
        pl.BlockSpec((ROWS, VB), lambda i, c: (i * nc + c, OFF_ZB // VB)),
        pl.BlockSpec((ROWS, TAIL_W), lambda i, c: (i * nc + c, OFF_TAIL // TAIL_W)),
    ]
    args = [proj] * 5
    if has_state:
        in_specs.append(pl.BlockSpec((n_state, N_HEAD_B, DK_B, DV_B), lambda i, c: (i, 0, 0, 0)))
        args.append(s0)
    in_specs += [
        pl.BlockSpec((TAIL_W, KB), lambda i, c: (0, 0)),
        pl.BlockSpec((1, KB), lambda i, c: (0, 0)),
        pl.BlockSpec((1, DV_B), lambda i, c: (0, 0)),
    ]
    args += [w_lr_pad, b_lr.reshape(1, KB), norm_b.reshape(1, DV_B)]
    return pl.pallas_call(
        functools.partial(_gla_kernel, n_state=n_state, rows=rows, has_state=has_state),
        out_shape=(
            jax.ShapeDtypeStruct((n_batch * t_len, VB), BF16),
            jax.ShapeDtypeStruct((n_batch, N_HEAD_B, DK_B, DV_B), F32),
        ),
        grid=(nb, nc),
        in_specs=in_specs,
        out_specs=(
            pl.BlockSpec((ROWS, VB), lambda i, c: (i * nc + c, 0)),
            pl.BlockSpec((n_state, N_HEAD_B, DK_B, DV_B), lambda i, c: (i, 0, 0, 0)),
        ),
        compiler_params=pltpu.CompilerParams(
            dimension_semantics=("parallel", "arbitrary"), vmem_limit_bytes=VMEM_LIMIT),
        name="gla",
    )(*args)


def _lru_kernel(*refs, n_seq, rows, has_state, reset_first):
    xb_ref, gate_ref = refs[:2]
    p = 2
    if has_state:
        h0_ref, c0_ref = refs[p], refs[p + 1]
        p += 2
    cw_ref, cb_ref, wa_ref, ba_ref, wx_ref, bx_ref, lam_ref = refs[p:p + 7]
    y_ref, h_ref, cn_ref = refs[p + 7:p + 10]
    xp_ref, xc_ref, a_ref, b_ref = refs[p + 10:p + 14]
    n = n_seq * rows
    c = pl.program_id(1)

    @pl.when(c == 0)
    def _init():
        if has_state:
            h_ref[...] = h0_ref[...]
            for s in range(n_seq):
                xp_ref[s, HIST - (CONV_W - 1):HIST, :] = c0_ref[s]
        else:
            h_ref[...] = jnp.zeros(h_ref.shape, F32)
            xp_ref[:, 0:HIST, :] = jnp.zeros((n_seq, HIST, W_LRU), F32)

    def emit(s, cs, acc):
        xc_ref[s * rows:(s + 1) * rows, cs] = acc

    _causal_conv_block(xb_ref, xp_ref, cw_ref, n_seq, rows, W_LRU, emit, bias_ref=cb_ref)

    @pl.when(c == pl.num_programs(1) - 1)
    def _conv_out():
        for s in range(n_seq):
            cn_ref[s] = xp_ref[s, HIST - (CONV_W - 1):HIST, :]

    row = lax.broadcasted_iota(jnp.int32, (n, LRU_BW), 0)
    sub = row & 7
    for blk in range(LRU_BLOCKS):
        bl = slice(blk * LRU_BW, (blk + 1) * LRU_BW)
        xc = xc_ref[:, bl]
        xcs = _split(xc)
        r = jax.nn.sigmoid(_mm3(xcs, wa_ref[blk]) + ba_ref[:, bl])
        gi = jax.nn.sigmoid(_mm3(xcs, wx_ref[blk]) + bx_ref[:, bl])
        log_a = -LRU_C * r * jax.nn.softplus(-lam_ref[:, bl])
        a = jnp.exp(log_a)
        mult = jnp.sqrt(-jnp.tanh(log_a) * (a * a + 1.0))
        if reset_first:
            mult = jnp.where((row == 0) & (c == 0), 1.0, mult)
        b = mult * gi * xc
        for d in (1, 2, 4):
            a_sh = pltpu.roll(a, d, axis=0)
            b_sh = pltpu.roll(b, d, axis=0)
            m = sub >= d
            b = jnp.where(m, a * b_sh + b, b)
            a = jnp.where(m, a * a_sh, a)
        a_ref[:, bl] = a
        b_ref[:, bl] = b

    for s in range(n_seq):
        def body(g, hp, s=s):
            r0 = pl.multiple_of(s * rows + g * 8, 8)
            hs = a_ref[pl.ds(r0, 8), :] * hp + b_ref[pl.ds(r0, 8), :]
            b_ref[pl.ds(r0, 8), :] = hs
            return hs[7:8, :]

        h_ref[s] = lax.fori_loop(0, rows // 8, body, h_ref[s])

    lc = 512
    for c0 in range(0, W_LRU, lc):
        cs = slice(c0, c0 + lc)
        y_ref[:, cs] = (b_ref[:, cs] * _silu(gate_ref[:, cs])).astype(BF16)


def _lru(proj, h0, c0, conv_w, conv_b, w_a, b_a, w_x, b_x, lam, n_batch, t_len, reset_first):
    has_state = h0 is not None
    if t_len >= 256:
        rows, n_seq = 256, 1
    else:
        rows, n_seq = t_len, 128 // t_len
    n = rows * n_seq
    nb = n_batch // n_seq
    nc = t_len // rows
    in_specs = [
        pl.BlockSpec((n, W_LRU), lambda i, c: (i * nc + c, 0)),
        pl.BlockSpec((n, W_LRU), lambda i, c: (i * nc + c, 1)),
    ]
    args = [proj, proj]
    if has_state:
        in_specs += [
            pl.BlockSpec((n_seq, 1, W_LRU), lambda i, c: (i, 0, 0)),
            pl.BlockSpec((n_seq, CONV_W - 1, W_LRU), lambda i, c: (i, 0, 0)),
        ]
        args += [h0.reshape(n_batch, 1, W_LRU), c0]
    vec = pl.BlockSpec((1, W_LRU), lambda i, c: (0, 0))
    blk = pl.BlockSpec((LRU_BLOCKS, LRU_BW, LRU_BW), lambda i, c: (0, 0, 0))
    in_specs += [pl.BlockSpec((CONV_W, W_LRU), lambda i, c: (0, 0)), vec, blk, vec, blk, vec, vec]
    args += [conv_w, conv_b.reshape(1, W_LRU), w_a, b_a.reshape(1, W_LRU), w_x,
             b_x.reshape(1, W_LRU), lam.reshape(1, W_LRU)]
    y, h, cn = pl.pallas_call(
        functools.partial(_lru_kernel, n_seq=n_seq, rows=rows, has_state=has_state,
                          reset_first=reset_first),
        out_shape=(
            jax.ShapeDtypeStruct((n_batch * t_len, W_LRU), BF16),
            jax.ShapeDtypeStruct((n_batch, 1, W_LRU), F32),
            jax.ShapeDtypeStruct((n_batch, CONV_W - 1, W_LRU), F32),
        ),
        grid=(nb, nc),
        in_specs=in_specs,
        out_specs=(
            pl.BlockSpec((n, W_LRU), lambda i, c: (i * nc + c, 0)),
            pl.BlockSpec((n_seq, 1, W_LRU), lambda i, c: (i, 0, 0)),
            pl.BlockSpec((n_seq, CONV_W - 1, W_LRU), lambda i, c: (i, 0, 0)),
        ),
        scratch_shapes=[
            pltpu.VMEM((n_seq, HIST + rows, W_LRU), F32),
            pltpu.VMEM((n, W_LRU), F32),
            pltpu.VMEM((n, W_LRU), F32),
            pltpu.VMEM((n, W_LRU), F32),
        ],
        compiler_params=pltpu.CompilerParams(
            dimension_semantics=("parallel", "arbitrary"), vmem_limit_bytes=VMEM_LIMIT),
        name="lru",
    )(*args)
    return y, h.reshape(n_batch, W_LRU), cn


def _ab_w_in_layout(w):
    sizes = [QKV_A, N_HEAD_A, N_HEAD_A, VA, KB, KB, VB, LOWRANK, VB]
    offs = [0]
    for s in sizes:
        offs.append(offs[-1] + s)
    qkv, b_raw, a_raw, z_a, q_b, k_b, v_b, lr_b, z_b = [w[:, offs[i]:offs[i + 1]] for i in range(9)]
    used = OFF_TAIL + 2 * N_HEAD_A + LOWRANK
    pad = jnp.zeros((w.shape[0], AB_N - used), w.dtype)
    return jnp.concatenate([qkv, z_a, q_b, k_b, v_b, z_b, b_raw, a_raw, lr_b, pad], axis=1).astype(BF16)


def _tail_row(vals, lane0):
    return jnp.zeros((TAIL_W,), F32).at[lane0:lane0 + vals.shape[0]].set(vals.astype(F32))


def kernel(x_prompt, x_sample, state_delta, state_delta_conv, state_gla, state_lru, state_lru_conv,
           ab_norm, ab_w_in, ab_conv_w, ab_a_log, ab_dt_bias, ab_norm_a, ab_gla_w_lr, ab_gla_b_lr,
           ab_norm_b, ab_w_out, lru_norm, lru_w_in, lru_conv_w, lru_conv_b, lru_w_a, lru_b_a,
           lru_w_x, lru_b_x, lru_lambda, lru_w_out, final_norm):
    depth = ab_norm.shape[0] + lru_norm.shape[0]
    groups = []
    for x, carried in ((x_prompt, False), (x_sample, True)):
        groups.append(dict(x=x.reshape(-1, D_MODEL), nb=x.shape[0], t=x.shape[1], carried=carried,
                           delta=[], dconv=[], gla=[], lru=[], lconv=[]))

    for layer in range(depth):
        j = layer // 2
        last = layer == depth - 1
        if layer % 2 == 0:
            w_in = _ab_w_in_layout(ab_w_in[j])
            w_out = ab_w_out[j].astype(BF16)
            gate_params = jnp.stack([_tail_row(ab_a_log[j], LANE_A), _tail_row(ab_dt_bias[j], LANE_A)])
            w_lr_pad = jnp.zeros((TAIL_W, KB), F32).at[LANE_LR:LANE_LR + LOWRANK].set(ab_gla_w_lr[j])
            for g in groups:
                proj = _norm_mm(g["x"], ab_norm[j], w_in, AB_TN)
                s_a = state_delta[j] if g["carried"] else None
                c_a = state_delta_conv[j] if g["carried"] else None
                s_b = state_gla[j] if g["carried"] else None
                o_a, n_sa, n_ca = _delta(proj, s_a, c_a, ab_conv_w[j], gate_params, ab_norm_a[j],
                                         g["nb"], g["t"])
                o_b, n_sb = _gla(proj, s_b, w_lr_pad, ab_gla_b_lr[j], ab_norm_b[j], g["nb"], g["t"], 16)
                g["x"] = _out_mm([o_a, o_b], w_out, g["x"], final_norm if last else None)
                g["delta"].append(n_sa)
                g["dconv"].append(n_ca)
                g["gla"].append(n_sb)
        else:
            w_in = lru_w_in[j].astype(BF16)
            w_out = lru_w_out[j].astype(BF16)
            for g in groups:
                proj = _norm_mm(g["x"], lru_norm[j], w_in, 1024)
                h0 = state_lru[j] if g["carried"] else None
                c0 = state_lru_conv[j] if g["carried"] else None
                y, n_h, n_c = _lru(proj, h0, c0, lru_conv_w[j], lru_conv_b[j], lru_w_a[j], lru_b_a[j],
                                   lru_w_x[j], lru_b_x[j], lru_lambda[j], g["nb"], g["t"],
                                   reset_first=not g["carried"])
                g["x"] = _out_mm([y], w_out, g["x"], final_norm if last else None)
                g["lru"].append(n_h)
                g["lconv"].append(n_c)

    outs = []
    for g, x in zip(groups, (x_prompt, x_sample)):
        outs.append([g["x"].reshape(x.shape), jnp.stack(g["delta"]), jnp.stack(g["dconv"]),
                     jnp.stack(g["gla"]), jnp.stack(g["lru"]), jnp.stack(g["lconv"])])
    p, s = outs
    return (p[0], s[0], p[1], p[2], p[3], p[4], p[5], s[1], s[2], s[3], s[4], s[5])
```

```python
import functools

import jax
import jax.numpy as jnp
from jax import lax
from jax.experimental import pallas as pl
from jax.experimental.pallas import tpu as pltpu

F32 = jnp.float32
BF16 = jnp.bfloat16
HIGHEST = lax.Precision.HIGHEST

D_MODEL = 2048
N_HEAD_A, DK_A, DV_A = 8, 128, 128
N_HEAD_B, DK_B, DV_B = 4, 128, 256
KA = N_HEAD_A * DK_A
VA = N_HEAD_A * DV_A
KB = N_HEAD_B * DK_B
VB = N_HEAD_B * DV_B
QKV_A = 2 * KA + VA
LOWRANK = 16
GLA_NORMALIZER = 16.0
W_LRU = D_MODEL
LRU_BLOCKS = 16
LRU_BW = W_LRU // LRU_BLOCKS
LRU_C = 8.0
CONV_W = 4
EPS = 1e-6

OFF_QKV = 0
OFF_ZA = OFF_QKV + QKV_A
OFF_QB = OFF_ZA + VA
OFF_KB = OFF_QB + KB
OFF_VB = OFF_KB + KB
OFF_ZB = OFF_VB + VB
OFF_TAIL = OFF_ZB + VB
TAIL_W = 128
AB_TN = 768
AB_N = 7680
LANE_BETA, LANE_A, LANE_LR = 0, N_HEAD_A, 2 * N_HEAD_A

ROWS = 64
HEAD_GROUP = 2
HIST = 8
VMEM_LIMIT = 56 * 1024 * 1024

NN = (((1,), (0,)), ((), ()))
NT = (((1,), (1,)), ((), ()))
TN = (((0,), (0,)), ((), ()))


def _mm(a, b, dims=NN):
    return lax.dot_general(a, b, dims, precision=HIGHEST, preferred_element_type=F32)


def _split(a):
    hi = a.astype(BF16)
    return hi, (a - hi.astype(F32)).astype(BF16)


def _mm3(a, b, dims=NN):
    ah, al = a if isinstance(a, tuple) else _split(a)
    bh, bl = b if isinstance(b, tuple) else _split(b)
    dot = functools.partial(lax.dot_general, dimension_numbers=dims, preferred_element_type=F32)
    return dot(ah, bh) + (dot(ah, bl) + dot(al, bh))


def _split_each(xs):
    return [x if isinstance(x, tuple) else _split(x) for x in xs]


def _mm3_each(as_, bs, dims=NN):
    sa, sb = _split_each(as_), _split_each(bs)
    dot = functools.partial(lax.dot_general, dimension_numbers=dims, preferred_element_type=F32)
    hh = [dot(a[0], b[0]) for a, b in zip(sa, sb)]
    hl = [dot(a[0], b[1]) for a, b in zip(sa, sb)]
    lh = [dot(a[1], b[0]) for a, b in zip(sa, sb)]
    return [x + (y + z) for x, y, z in zip(hh, hl, lh)]


def _silu(x):
    return x * jax.nn.sigmoid(x)


def _norm_mm_kernel(x_ref, g_ref, w_ref, o_ref, h_ref):
    @pl.when(pl.program_id(1) == 0)
    def _():
        x = x_ref[...]
        ms = jnp.mean(x * x, axis=-1, keepdims=True)
        h_ref[...] = (x * lax.rsqrt(ms + EPS) * g_ref[...]).astype(BF16)

    o_ref[...] = jnp.dot(h_ref[...], w_ref[...], preferred_element_type=F32)


def _norm_mm(x, g, w, tn):
    m, k = x.shape
    n = w.shape[1]
    tm = min(m, 1024)
    return pl.pallas_call(
        _norm_mm_kernel,
        out_shape=jax.ShapeDtypeStruct((m, n), F32),
        grid=(m // tm, n // tn),
        in_specs=[
            pl.BlockSpec((tm, k), lambda i, j: (i, 0)),
            pl.BlockSpec((1, k), lambda i, j: (0, 0)),
            pl.BlockSpec((k, tn), lambda i, j: (0, j)),
        ],
        out_specs=pl.BlockSpec((tm, tn), lambda i, j: (i, j)),
        scratch_shapes=[pltpu.VMEM((tm, k), BF16)],
        compiler_params=pltpu.CompilerParams(
            dimension_semantics=("parallel", "arbitrary"), vmem_limit_bytes=VMEM_LIMIT),
        name="norm_mm",
    )(x, g.reshape(1, k), w)


def _out_mm_kernel(*refs, n_in, final):
    o_refs = refs[:n_in]
    w_ref, x_ref = refs[n_in], refs[n_in + 1]
    out_ref = refs[-1]
    acc = x_ref[...]
    k0 = 0
    for o_ref in o_refs:
        kk = o_ref.shape[1]
        acc = acc + jnp.dot(o_ref[...], w_ref[k0:k0 + kk, :], preferred_element_type=F32)
        k0 += kk
    if final:
        fg_ref = refs[n_in + 2]
        ms = jnp.mean(acc * acc, axis=-1, keepdims=True)
        acc = acc * lax.rsqrt(ms + EPS) * fg_ref[...]
    out_ref[...] = acc


def _out_mm(os_, w, x, final_g=None):
    m, d = x.shape
    tm = 512
    final = final_g is not None
    in_specs = [pl.BlockSpec((tm, o.shape[1]), lambda i: (i, 0)) for o in os_]
    in_specs += [pl.BlockSpec(w.shape, lambda i: (0, 0)), pl.BlockSpec((tm, d), lambda i: (i, 0))]
    args = list(os_) + [w, x]
    if final:
        in_specs.append(pl.BlockSpec((1, d), lambda i: (0, 0)))
        args.append(final_g.reshape(1, d))
    return pl.pallas_call(
        functools.partial(_out_mm_kernel, n_in=len(os_), final=final),
        out_shape=jax.ShapeDtypeStruct((m, d), F32),
        grid=(m // tm,),
        in_specs=in_specs,
        out_specs=pl.BlockSpec((tm, d), lambda i: (i, 0)),
        compiler_params=pltpu.CompilerParams(
            dimension_semantics=("parallel",), vmem_limit_bytes=VMEM_LIMIT),
        name="out_mm",
    )(*args)


def _causal_conv_block(x_ref, xp_ref, cw_ref, n_seq, rows, width, emit, bias_ref=None):
    lc = 512
    for s in range(n_seq):
        xp_ref[s, HIST:HIST + rows, :] = x_ref[s * rows:(s + 1) * rows, :]
        for c0 in range(0, width, lc):
            cs = slice(c0, c0 + lc)
            base = HIST - (CONV_W - 1)
            acc = xp_ref[s, base:base + rows, cs] * cw_ref[0:1, cs]
            for j in range(1, CONV_W):
                acc = acc + xp_ref[s, base + j:base + j + rows, cs] * cw_ref[j:j + 1, cs]
            if bias_ref is not None:
                acc = acc + bias_ref[:, cs]
            emit(s, cs, acc)
        xp_ref[s, 0:HIST, :] = xp_ref[s, rows:rows + HIST, :]


def _seq_masks(n, rows):
    ri = lax.broadcasted_iota(jnp.int32, (n, n), 0)
    ci = lax.broadcasted_iota(jnp.int32, (n, n), 1)
    if rows == n:
        same = ri >= 0
    else:
        sh = rows.bit_length() - 1
        same = (ri >> sh) == (ci >> sh)
    return ri, ci, same


NEUMANN_BLOCK = 8


def _inverse_masks(ri, ci, rows):
    assert rows >= NEUMANN_BLOCK
    blk = lambda k: (ri >> (k.bit_length() - 1)) == (ci >> (k.bit_length() - 1))
    eye = jnp.where(ri == ci, 1.0, 0.0).astype(F32)
    levels = []
    k = NEUMANN_BLOCK
    while k < rows:
        levels.append(blk(2 * k) & jnp.logical_not(blk(k)))
        k *= 2
    return eye, blk(NEUMANN_BLOCK), levels


def _unit_lower_inverse_each(as_, masks):
    eye, base, levels = masks
    a8 = [jnp.where(base, a, 0.0) for a in as_]
    a8s = _split_each(a8)
    p2 = _mm3_each(a8s, a8s)
    p2s = _split_each(p2)
    p4s = _split_each(_mm3_each(p2s, p2s))
    a8p2 = _mm3_each(a8s, p2s)
    x = [(eye - a) + (p - ap) for a, p, ap in zip(a8, p2, a8p2)]
    t = [xi + xp for xi, xp in zip(x, _mm3_each(x, p4s))]
    for lvl in levels:
        ts = _split_each(t)
        off = [jnp.where(lvl, a, 0.0) for a in as_]
        t = [ti - d for ti, d in zip(t, _mm3_each(_mm3_each(ts, off), ts))]
    return t


def _delta_kernel(*refs, n_seq, rows, has_state, n_alias):
    refs = refs[n_alias:]
    qkv_ref, z_ref, tail_ref = refs[:3]
    p = 3
    if has_state:
        s0_ref, c0_ref = refs[p], refs[p + 1]
        p += 2
    cw_ref, gp_ref, na_ref = refs[p:p + 3]
    o_ref, s_ref, cn_ref = refs[p + 3:p + 6]
    xp_ref, act_ref = refs[p + 6:p + 8]
    n = n_seq * rows
    c = pl.program_id(1)

    @pl.when(c == 0)
    def _init():
        if has_state:
            s_ref[...] = s0_ref[...]
            for s in range(n_seq):
                xp_ref[s, HIST - (CONV_W - 1):HIST, :] = c0_ref[s]
        else:
            s_ref[...] = jnp.zeros(s_ref.shape, F32)
            xp_ref[:, 0:HIST, :] = jnp.zeros((n_seq, HIST, QKV_A), F32)

    def emit(s, cs, acc):
        act_ref[s * rows:(s + 1) * rows, cs] = _silu(acc)

    _causal_conv_block(qkv_ref, xp_ref, cw_ref, n_seq, rows, QKV_A, emit)

    @pl.when(c == pl.num_programs(1) - 1)
    def _conv_out():
        for s in range(n_seq):
            cn_ref[s] = xp_ref[s, HIST - (CONV_W - 1):HIST, :]

    _, _, same_n = _seq_masks(n, rows)
    rn = lax.broadcasted_iota(jnp.int32, (n, n), 0)
    cn = lax.broadcasted_iota(jnp.int32, (n, n), 1)
    tail = tail_ref[...]
    btile = jax.nn.sigmoid(tail)
    gtile = -jnp.exp(gp_ref[0:1, :]) * jax.nn.softplus(tail + gp_ref[1:2, :])
    g_cum = _mm(jnp.where(same_n & (rn >= cn), 1.0, 0.0).astype(F32), gtile)
    g_tot = _mm(jnp.where(same_n, 1.0, 0.0).astype(F32), gtile)
    e_cum = jnp.exp(g_cum)
    e_rest = jnp.exp(g_tot - g_cum)
    g_cum_t = g_cum.T

    m = HEAD_GROUP * n
    ri, ci, same = _seq_masks(m, rows)
    incl = same & (ri >= ci)
    strict = same & (ri > ci)
    inv_masks = _inverse_masks(ri, ci, rows)
    stack = lambda xs: jnp.concatenate(xs, axis=0)
    groups = [range(g * HEAD_GROUP, (g + 1) * HEAD_GROUP) for g in range(N_HEAD_A // HEAD_GROUP)]
    qn, kn, v, beta, gc, eg, er, g_row = ([] for _ in range(8))
    for heads in groups:
        q_h, k_h, v_h = [], [], []
        for h in heads:
            q = act_ref[:, h * DK_A:(h + 1) * DK_A]
            k = act_ref[:, KA + h * DK_A:KA + (h + 1) * DK_A]
            q_h.append(q * lax.rsqrt(jnp.sum(q * q, axis=-1, keepdims=True) + EPS) * (DK_A ** -0.5))
            k_h.append(k * lax.rsqrt(jnp.sum(k * k, axis=-1, keepdims=True) + EPS))
            v_h.append(act_ref[:, 2 * KA + h * DV_A:2 * KA + (h + 1) * DV_A])
        qn.append(stack(q_h))
        kn.append(stack(k_h))
        v.append(stack(v_h))
        beta.append(stack([btile[:, LANE_BETA + h:LANE_BETA + h + 1] for h in heads]))
        gc.append(stack([g_cum[:, LANE_A + h:LANE_A + h + 1] for h in heads]))
        eg.append(stack([e_cum[:, LANE_A + h:LANE_A + h + 1] for h in heads]))
        er.append(stack([e_rest[:, LANE_A + h:LANE_A + h + 1] for h in heads]))
        g_row.append(jnp.concatenate([g_cum_t[LANE_A + h:LANE_A + h + 1, :] for h in heads], axis=1))
    decay = [jnp.where(incl, jnp.exp(jnp.where(incl, c_ - r_, 0.0)), 0.0) for c_, r_ in zip(gc, g_row)]
    kb = [k_ * b_ for k_, b_ in zip(kn, beta)]
    prod = _mm3_each([stack([kb_, q_]) for kb_, q_ in zip(kb, qn)], kn, NT)
    a = [jnp.where(strict, p_[:m] * d_, 0.0) for p_, d_ in zip(prod, decay)]
    qk = [p_[m:] * d_ for p_, d_ in zip(prod, decay)]
    t_inv = _unit_lower_inverse_each(a, inv_masks)
    rhs = [jnp.concatenate([v_ * b_, kb_ * e_], axis=1) for v_, b_, kb_, e_ in zip(v, beta, kb, eg)]
    sol = _mm3_each(t_inv, rhs)
    qg = [q_ * e_ for q_, e_ in zip(qn, eg)]
    kdec = [k_ * e_ for k_, e_ in zip(kn, er)]
    probs = [(g, i, h, s, slice(i * n + s * rows, i * n + (s + 1) * rows))
             for g, heads in enumerate(groups) for i, h in enumerate(heads) for s in range(n_seq)]
    states = [s_ref[s, h] for (_, _, h, s, _) in probs]
    r = _mm3_each([stack([sol[g][sl, DV_A:], qg[g][sl]]) for (g, _, _, _, sl) in probs], states)
    v_new = [sol[g][sl, :DV_A] - r_[:rows] for (g, _, _, _, sl), r_ in zip(probs, r)]
    per_group = lambda xs: [stack([x for (g2, *_), x in zip(probs, xs) if g2 == g]) for g in range(len(groups))]
    v_new_g = per_group(v_new)
    o_inter_g = per_group([r_[rows:] for r_ in r])
    o = [oi + d for oi, d in zip(o_inter_g, _mm3_each(qk, v_new_g))]
    upd = _mm3_each([kdec[g][sl] for (g, _, _, _, sl) in probs], v_new, TN)
    for (g, i, h, s, sl), st, du in zip(probs, states, upd):
        g_last = g_tot[s * rows:s * rows + 1, LANE_A + h:LANE_A + h + 1]
        s_ref[s, h] = st * jnp.exp(g_last) + du
    for g, heads in enumerate(groups):
        for i, h in enumerate(heads):
            hl = slice(h * DV_A, (h + 1) * DV_A)
            oh = o[g][i * n:(i + 1) * n]
            oh = oh * lax.rsqrt(jnp.mean(oh * oh, axis=-1, keepdims=True) + EPS) * na_ref[...]
            o_ref[:, hl] = (oh * _silu(z_ref[:, hl])).astype(BF16)


def _layer_spec(block, layer):
    zeros = (0,) * (len(block) - 1)
    return pl.BlockSpec((None,) + tuple(block), lambda i, c: (layer, i) + zeros)


def _carry_outputs(prev, out_start):
    if prev is None:
        return [], [], {}
    specs = [pl.BlockSpec(memory_space=pl.ANY) for _ in prev]
    return list(prev), specs, {k: out_start + k for k in range(len(prev))}


def _delta(proj, layer, n_layers, s0, c0, prev, conv_w, gate_params, norm_a, n_batch, t_len):
    rows = min(ROWS, t_len)
    n_seq = ROWS // rows
    nb = n_batch // n_seq
    nc = t_len // rows
    has_state = s0 is not None
    row_map = lambda i, c: (i * nc + c, 0)
    args, in_specs, aliases = _carry_outputs(prev, 1)
    n_alias = len(args)
    in_specs += [
        pl.BlockSpec((ROWS, QKV_A), row_map),
        pl.BlockSpec((ROWS, VA), lambda i, c: (i * nc + c, OFF_ZA // VA)),
        pl.BlockSpec((ROWS, TAIL_W), lambda i, c: (i * nc + c, OFF_TAIL // TAIL_W)),
    ]
    args += [proj, proj, proj]
    s_block = (n_seq, N_HEAD_A, DK_A, DV_A)
    c_block = (n_seq, CONV_W - 1, QKV_A)
    if has_state:
        in_specs += [_layer_spec(s_block, layer), _layer_spec(c_block, layer)]
        args += [s0, c0]
    in_specs += [
        pl.BlockSpec((CONV_W, QKV_A), lambda i, c: (0, 0)),
        pl.BlockSpec((2, TAIL_W), lambda i, c: (0, 0)),
        pl.BlockSpec((1, DV_A), lambda i, c: (0, 0)),
    ]
    args += [conv_w, gate_params, norm_a.reshape(1, DV_A)]
    return pl.pallas_call(
        functools.partial(_delta_kernel, n_seq=n_seq, rows=rows, has_state=has_state, n_alias=n_alias),
        out_shape=(
            jax.ShapeDtypeStruct((n_batch * t_len, VA), BF16),
            jax.ShapeDtypeStruct((n_layers, n_batch, N_HEAD_A, DK_A, DV_A), F32),
            jax.ShapeDtypeStruct((n_layers, n_batch, CONV_W - 1, QKV_A), F32),
        ),
        grid=(nb, nc),
        in_specs=in_specs,
        out_specs=(
            pl.BlockSpec((ROWS, VA), row_map),
            _layer_spec(s_block, layer),
            _layer_spec(c_block, layer),
        ),
        scratch_shapes=[
            pltpu.VMEM((n_seq, HIST + rows, QKV_A), F32),
            pltpu.VMEM((ROWS, QKV_A), F32),
        ],
        input_output_aliases=aliases,
        compiler_params=pltpu.CompilerParams(
            dimension_semantics=("parallel", "arbitrary"), vmem_limit_bytes=VMEM_LIMIT),
        name="delta",
    )(*args)


def _gla_kernel(*refs, n_state, rows, has_state, n_alias):
    refs = refs[n_alias:]
    q_ref, k_ref, v_ref, z_ref, tail_ref = refs[:5]
    p = 5
    if has_state:
        s0_ref = refs[p]
        p += 1
    wlr_ref, blr_ref, nb_ref = refs[p:p + 3]
    o_ref, s_ref = refs[p + 3:p + 5]
    n = ROWS
    n_sub = n // rows
    chained = n_state == 1
    c = pl.program_id(1)

    @pl.when(c == 0)
    def _init():
        if has_state:
            s_ref[...] = s0_ref[...]
        else:
            s_ref[...] = jnp.zeros(s_ref.shape, F32)

    ri, ci, same = _seq_masks(n, rows)
    incl = same & (ri >= ci)
    gk = jax.nn.log_sigmoid(_mm(tail_ref[...], wlr_ref[...]) + blr_ref[...]) / GLA_NORMALIZER
    b_loc = _mm(jnp.where(incl, 1.0, 0.0).astype(F32), gk)
    q = q_ref[...] * (DK_B ** -0.5)
    k = k_ref[...]
    q_loc = q * jnp.exp(b_loc)
    heads = range(N_HEAD_B)
    hls = [slice(h * DK_B, (h + 1) * DK_B) for h in heads]
    vs = [v_ref[:, h * DV_B:(h + 1) * DV_B] for h in heads]

    def finish(h, o):
        vl = slice(h * DV_B, (h + 1) * DV_B)
        o = o * lax.rsqrt(jnp.mean(o * o, axis=-1, keepdims=True) + EPS) * nb_ref[...]
        o_ref[:, vl] = (o * _silu(z_ref[:, vl])).astype(BF16)

    def key_column(row_vals):
        return jnp.broadcast_to(row_vals, (8, DK_B)).T[:, 0:1]

    if chained:
        lower = ri >= ci
        b_cum = _mm(jnp.where(lower, 1.0, 0.0).astype(F32), gk)
        b_last = b_cum[n - 1:n]
        q_cum = q * jnp.exp(b_cum)
        kdec = k * jnp.exp(b_last - b_cum)
        row = lax.broadcasted_iota(jnp.int32, (n, KB), 0)
        k_rel = []
        for s in range(n_sub):
            start = b_cum[s * rows - 1:s * rows] if s else jnp.zeros((1, KB), F32)
            k_rel.append(k * jnp.exp(jnp.where(row < (s + 1) * rows, start - b_cum, 0.0)))
        a_parts = _mm3_each([q_loc[s * rows:(s + 1) * rows, hl] for hl in hls for s in range(n_sub)],
                            [k_rel[s][:, hl] for hl in hls for s in range(n_sub)], NT)
        a = [jnp.where(lower, jnp.concatenate(a_parts[h * n_sub:(h + 1) * n_sub], axis=0), 0.0) for h in heads]
        sts = [s_ref[0, h] for h in heads]
        o_state = _mm3_each([q_cum[:, hl] for hl in hls], sts)
        intra = _mm3_each(a, vs)
        upd = _mm3_each([kdec[:, hl] for hl in hls], vs, TN)
        for h in heads:
            finish(h, o_state[h] + intra[h])
            s_ref[0, h] = sts[h] * jnp.exp(key_column(b_last[:, hls[h]])) + upd[h]
    else:
        b_tot = _mm(jnp.where(same, 1.0, 0.0).astype(F32), gk)
        kd = k * jnp.exp(-b_loc)
        kdec = k * jnp.exp(b_tot - b_loc)
        a = [jnp.where(incl, x, 0.0) for x in _mm3_each([q_loc[:, hl] for hl in hls], [kd[:, hl] for hl in hls], NT)]
        intra = _mm3_each(a, vs)
        probs = [(h, s, slice(s * rows, (s + 1) * rows)) for h in heads for s in range(n_sub)]
        sts = [s_ref[s, h] for (h, s, _) in probs]
        o_state = _mm3_each([q_loc[sl, hls[h]] for (h, _, sl) in probs], sts)
        upd = _mm3_each([kdec[sl, hls[h]] for (h, _, sl) in probs], [vs[h][sl] for (h, _, sl) in probs], TN)
        for (h, s, sl), st, du in zip(probs, sts, upd):
            s_ref[s, h] = st * jnp.exp(key_column(b_tot[s * rows:s * rows + 1, hls[h]])) + du
        for h in heads:
            finish(h, jnp.concatenate(o_state[h * n_sub:(h + 1) * n_sub], axis=0) + intra[h])


def _gla(proj, layer, n_layers, s0, prev, w_lr_pad, b_lr, norm_b, n_batch, t_len, chunk):
    rows = min(chunk, t_len)
    has_state = s0 is not None
    if t_len >= ROWS:
        n_state, nb, nc = 1, n_batch, t_len // ROWS
    else:
        n_state, nb, nc = ROWS // t_len, n_batch // (ROWS // t_len), 1
    args, in_specs, aliases = _carry_outputs(prev, 1)
    n_alias = len(args)
    in_specs += [
        pl.BlockSpec((ROWS, KB), lambda i, c: (i * nc + c, OFF_QB // KB)),
        pl.BlockSpec((ROWS, KB), lambda i, c: (i * nc + c, OFF_KB // KB)),
        pl.BlockSpec((ROWS, VB), lambda i, c: (i * nc + c, OFF_VB // VB)),
        pl.BlockSpec((ROWS, VB), lambda i, c: (i * nc + c, OFF_ZB // VB)),
        pl.BlockSpec((ROWS, TAIL_W), lambda i, c: (i * nc + c, OFF_TAIL // TAIL_W)),
    ]
    args += [proj] * 5
    s_block = (n_state, N_HEAD_B, DK_B, DV_B)
    if has_state:
        in_specs.append(_layer_spec(s_block, layer))
        args.append(s0)
    in_specs += [
        pl.BlockSpec((TAIL_W, KB), lambda i, c: (0, 0)),
        pl.BlockSpec((1, KB), lambda i, c: (0, 0)),
        pl.BlockSpec((1, DV_B), lambda i, c: (0, 0)),
    ]
    args += [w_lr_pad, b_lr.reshape(1, KB), norm_b.reshape(1, DV_B)]
    return pl.pallas_call(
        functools.partial(_gla_kernel, n_state=n_state, rows=rows, has_state=has_state, n_alias=n_alias),
        out_shape=(
            jax.ShapeDtypeStruct((n_batch * t_len, VB), BF16),
            jax.ShapeDtypeStruct((n_layers, n_batch, N_HEAD_B, DK_B, DV_B), F32),
        ),
        grid=(nb, nc),
        in_specs=in_specs,
        out_specs=(
            pl.BlockSpec((ROWS, VB), lambda i, c: (i * nc + c, 0)),
            _layer_spec(s_block, layer),
        ),
        input_output_aliases=aliases,
        compiler_params=pltpu.CompilerParams(
            dimension_semantics=("parallel", "arbitrary"), vmem_limit_bytes=VMEM_LIMIT),
        name="gla",
    )(*args)


def _lru_kernel(*refs, n_seq, rows, has_state, reset_first, n_alias):
    refs = refs[n_alias:]
    xb_ref, gate_ref = refs[:2]
    p = 2
    if has_state:
        h0_ref, c0_ref = refs[p], refs[p + 1]
        p += 2
    cw_ref, cb_ref, wa_ref, ba_ref, wx_ref, bx_ref, lam_ref = refs[p:p + 7]
    y_ref, h_ref, cn_ref = refs[p + 7:p + 10]
    xp_ref, xc_ref, a_ref, b_ref = refs[p + 10:p + 14]
    n = n_seq * rows
    c = pl.program_id(1)

    @pl.when(c == 0)
    def _init():
        if has_state:
            h_ref[...] = h0_ref[...]
            for s in range(n_seq):
                xp_ref[s, HIST - (CONV_W - 1):HIST, :] = c0_ref[s]
        else:
            h_ref[...] = jnp.zeros(h_ref.shape, F32)
            xp_ref[:, 0:HIST, :] = jnp.zeros((n_seq, HIST, W_LRU), F32)

    def emit(s, cs, acc):
        xc_ref[s * rows:(s + 1) * rows, cs] = acc

    _causal_conv_block(xb_ref, xp_ref, cw_ref, n_seq, rows, W_LRU, emit, bias_ref=cb_ref)

    @pl.when(c == pl.num_programs(1) - 1)
    def _conv_out():
        for s in range(n_seq):
            cn_ref[s] = xp_ref[s, HIST - (CONV_W - 1):HIST, :]

    row = lax.broadcasted_iota(jnp.int32, (n, LRU_BW), 0)
    sub = row & 7
    for blk in range(LRU_BLOCKS):
        bl = slice(blk * LRU_BW, (blk + 1) * LRU_BW)
        xc = xc_ref[:, bl]
        xcs = _split(xc)
        r = jax.nn.sigmoid(_mm3(xcs, wa_ref[blk]) + ba_ref[:, bl])
        gi = jax.nn.sigmoid(_mm3(xcs, wx_ref[blk]) + bx_ref[:, bl])
        log_a = -LRU_C * r * jax.nn.softplus(-lam_ref[:, bl])
        a = jnp.exp(log_a)
        mult = jnp.sqrt(-jnp.tanh(log_a) * (a * a + 1.0))
        if reset_first:
            mult = jnp.where((row == 0) & (c == 0), 1.0, mult)
        b = mult * gi * xc
        for d in (1, 2, 4):
            a_sh = pltpu.roll(a, d, axis=0)
            b_sh = pltpu.roll(b, d, axis=0)
            m = sub >= d
            b = jnp.where(m, a * b_sh + b, b)
            a = jnp.where(m, a * a_sh, a)
        a_ref[:, bl] = a
        b_ref[:, bl] = b

    for s in range(n_seq):
        def body(g, hp, s=s):
            r0 = pl.multiple_of(s * rows + g * 8, 8)
            hs = a_ref[pl.ds(r0, 8), :] * hp + b_ref[pl.ds(r0, 8), :]
            b_ref[pl.ds(r0, 8), :] = hs
            return hs[7:8, :]

        h_ref[s] = lax.fori_loop(0, rows // 8, body, h_ref[s])

    lc = 512
    for c0 in range(0, W_LRU, lc):
        cs = slice(c0, c0 + lc)
        y_ref[:, cs] = (b_ref[:, cs] * _silu(gate_ref[:, cs])).astype(BF16)


def _lru(proj, layer, n_layers, h0, c0, prev, conv_w, conv_b, w_a, b_a, w_x, b_x, lam, n_batch, t_len,
         reset_first):
    has_state = h0 is not None
    if t_len >= 256:
        rows, n_seq = 256, 1
    else:
        rows, n_seq = t_len, 128 // t_len
    n = rows * n_seq
    nb = n_batch // n_seq
    nc = t_len // rows
    args, in_specs, aliases = _carry_outputs(prev, 1)
    n_alias = len(args)
    in_specs += [
        pl.BlockSpec((n, W_LRU), lambda i, c: (i * nc + c, 0)),
        pl.BlockSpec((n, W_LRU), lambda i, c: (i * nc + c, 1)),
    ]
    args += [proj, proj]
    h_block = (n_seq, 1, W_LRU)
    c_block = (n_seq, CONV_W - 1, W_LRU)
    if has_state:
        in_specs += [_layer_spec(h_block, layer), _layer_spec(c_block, layer)]
        args += [h0, c0]
    vec = pl.BlockSpec((1, W_LRU), lambda i, c: (0, 0))
    blk = pl.BlockSpec((LRU_BLOCKS, LRU_BW, LRU_BW), lambda i, c: (0, 0, 0))
    in_specs += [pl.BlockSpec((CONV_W, W_LRU), lambda i, c: (0, 0)), vec, blk, vec, blk, vec, vec]
    args += [conv_w, conv_b.reshape(1, W_LRU), w_a, b_a.reshape(1, W_LRU), w_x,
             b_x.reshape(1, W_LRU), lam.reshape(1, W_LRU)]
    return pl.pallas_call(
        functools.partial(_lru_kernel, n_seq=n_seq, rows=rows, has_state=has_state,
                          reset_first=reset_first, n_alias=n_alias),
        out_shape=(
            jax.ShapeDtypeStruct((n_batch * t_len, W_LRU), BF16),
            jax.ShapeDtypeStruct((n_layers, n_batch, 1, W_LRU), F32),
            jax.ShapeDtypeStruct((n_layers, n_batch, CONV_W - 1, W_LRU), F32),
        ),
        grid=(nb, nc),
        in_specs=in_specs,
        out_specs=(
            pl.BlockSpec((n, W_LRU), lambda i, c: (i * nc + c, 0)),
            _layer_spec(h_block, layer),
            _layer_spec(c_block, layer),
        ),
        input_output_aliases=aliases,
        scratch_shapes=[
            pltpu.VMEM((n_seq, HIST + rows, W_LRU), F32),
            pltpu.VMEM((n, W_LRU), F32),
            pltpu.VMEM((n, W_LRU), F32),
            pltpu.VMEM((n, W_LRU), F32),
        ],
        compiler_params=pltpu.CompilerParams(
            dimension_semantics=("parallel", "arbitrary"), vmem_limit_bytes=VMEM_LIMIT),
        name="lru",
    )(*args)


def _ab_w_in_layout(w):
    sizes = [QKV_A, N_HEAD_A, N_HEAD_A, VA, KB, KB, VB, LOWRANK, VB]
    offs = [0]
    for s in sizes:
        offs.append(offs[-1] + s)
    qkv, b_raw, a_raw, z_a, q_b, k_b, v_b, lr_b, z_b = [w[:, offs[i]:offs[i + 1]] for i in range(9)]
    used = OFF_TAIL + 2 * N_HEAD_A + LOWRANK
    pad = jnp.zeros((w.shape[0], AB_N - used), w.dtype)
    return jnp.concatenate([qkv, z_a, q_b, k_b, v_b, z_b, b_raw, a_raw, lr_b, pad], axis=1).astype(BF16)


def _tail_row(vals, lane0):
    return jnp.zeros((TAIL_W,), F32).at[lane0:lane0 + vals.shape[0]].set(vals.astype(F32))


def kernel(x_prompt, x_sample, state_delta, state_delta_conv, state_gla, state_lru, state_lru_conv,
           ab_norm, ab_w_in, ab_conv_w, ab_a_log, ab_dt_bias, ab_norm_a, ab_gla_w_lr, ab_gla_b_lr,
           ab_norm_b, ab_w_out, lru_norm, lru_w_in, lru_conv_w, lru_conv_b, lru_w_a, lru_b_a,
           lru_w_x, lru_b_x, lru_lambda, lru_w_out, final_norm):
    n_ab, n_lru = ab_norm.shape[0], lru_norm.shape[0]
    depth = n_ab + n_lru
    lru_h0 = state_lru.reshape(n_lru, -1, 1, W_LRU)
    groups = []
    for x, carried in ((x_prompt, False), (x_sample, True)):
        groups.append(dict(x=x.reshape(-1, D_MODEL), nb=x.shape[0], t=x.shape[1], carried=carried,
                           delta=None, gla=None, lru=None))

    for layer in range(depth):
        j = layer // 2
        last = layer == depth - 1
        if layer % 2 == 0:
            w_in = _ab_w_in_layout(ab_w_in[j])
            w_out = ab_w_out[j].astype(BF16)
            gate_params = jnp.stack([_tail_row(ab_a_log[j], LANE_A), _tail_row(ab_dt_bias[j], LANE_A)])
            w_lr_pad = jnp.zeros((TAIL_W, KB), F32).at[LANE_LR:LANE_LR + LOWRANK].set(ab_gla_w_lr[j])
            for g in groups:
                proj = _norm_mm(g["x"], ab_norm[j], w_in, AB_TN)
                s_a, c_a, s_b = (state_delta, state_delta_conv, state_gla) if g["carried"] else (None,) * 3
                o_a, *g["delta"] = _delta(proj, j, n_ab, s_a, c_a, g["delta"], ab_conv_w[j], gate_params,
                                          ab_norm_a[j], g["nb"], g["t"])
                o_b, *g["gla"] = _gla(proj, j, n_ab, s_b, g["gla"], w_lr_pad, ab_gla_b_lr[j], ab_norm_b[j],
                                      g["nb"], g["t"], 16)
                g["x"] = _out_mm([o_a, o_b], w_out, g["x"], final_norm if last else None)
        else:
            w_in = lru_w_in[j].astype(BF16)
            w_out = lru_w_out[j].astype(BF16)
            for g in groups:
                proj = _norm_mm(g["x"], lru_norm[j], w_in, 1024)
                h0, c0 = (lru_h0, state_lru_conv) if g["carried"] else (None, None)
                y, *g["lru"] = _lru(proj, j, n_lru, h0, c0, g["lru"], lru_conv_w[j], lru_conv_b[j],
                                    lru_w_a[j], lru_b_a[j], lru_w_x[j], lru_b_x[j], lru_lambda[j],
                                    g["nb"], g["t"], reset_first=not g["carried"])
                g["x"] = _out_mm([y], w_out, g["x"], final_norm if last else None)

    outs = []
    for g, x in zip(groups, (x_prompt, x_sample)):
        h_all, lconv = g["lru"]
        outs.append([g["x"].reshape(x.shape), g["delta"][0], g["delta"][1], g["gla"][0],
                     h_all.reshape(n_lru, g["nb"], W_LRU), lconv])
    p, s = outs
    return (p[0], s[0], p[1], p[2], p[3], p[4], p[5], s[1], s[2], s[3], s[4], s[5])
```

```python
import functools

import jax
import jax.numpy as jnp
from jax import lax
from jax.experimental import pallas as pl
from jax.experimental.pallas import tpu as pltpu

F32 = jnp.float32
BF16 = jnp.bfloat16
HIGHEST = lax.Precision.HIGHEST

D_MODEL = 2048
N_HEAD_A, DK_A, DV_A = 8, 128, 128
N_HEAD_B, DK_B, DV_B = 4, 128, 256
KA = N_HEAD_A * DK_A
VA = N_HEAD_A * DV_A
KB = N_HEAD_B * DK_B
VB = N_HEAD_B * DV_B
QKV_A = 2 * KA + VA
LOWRANK = 16
GLA_NORMALIZER = 16.0
W_LRU = D_MODEL
LRU_BLOCKS = 16
LRU_BW = W_LRU // LRU_BLOCKS
LRU_C = 8.0
CONV_W = 4
EPS = 1e-6
LANES = 128

OFF_QKV = 0
OFF_ZA = OFF_QKV + QKV_A
OFF_QB = OFF_ZA + VA
OFF_KB = OFF_QB + KB
OFF_VB = OFF_KB + KB
OFF_ZB = OFF_VB + VB
OFF_TAIL = OFF_ZB + VB
TAIL_W = 128
AB_TN = 768
AB_N = 7680
LANE_BETA, LANE_A, LANE_LR = 0, N_HEAD_A, 2 * N_HEAD_A

ROWS = 64
HEAD_GROUP = 2
HIST = 8
VMEM_LIMIT = 56 * 1024 * 1024

NN = (((1,), (0,)), ((), ()))
NT = (((1,), (1,)), ((), ()))
TN = (((0,), (0,)), ((), ()))


def _mm(a, b, dims=NN):
    return lax.dot_general(a, b, dims, precision=HIGHEST, preferred_element_type=F32)


def _split(a):
    hi = a.astype(BF16)
    return hi, (a - hi.astype(F32)).astype(BF16)


def _mm3(a, b, dims=NN):
    ah, al = a if isinstance(a, tuple) else _split(a)
    bh, bl = b if isinstance(b, tuple) else _split(b)
    (ca,), (cb,) = dims[0]
    dot = functools.partial(lax.dot_general, dimension_numbers=dims, preferred_element_type=F32)
    if ah.shape[ca] % LANES:
        return dot(ah, bh) + (dot(ah, bl) + dot(al, bh))
    return dot(jnp.concatenate([ah, ah, al], axis=ca), jnp.concatenate([bh, bl, bh], axis=cb))


def _split_each(xs):
    return [x if isinstance(x, tuple) else _split(x) for x in xs]


def _mm3_each(as_, bs, dims=NN):
    sa, sb = _split_each(as_), _split_each(bs)
    return [_mm3(a, b, dims) for a, b in zip(sa, sb)]


def _sigmoid(x):
    return 0.5 * jnp.tanh(0.5 * x) + 0.5


def _silu(x):
    h = 0.5 * x
    return h + h * jnp.tanh(h)


def _norm_mm_kernel(x_ref, g_ref, w_ref, o_ref, h_ref):
    @pl.when(pl.program_id(1) == 0)
    def _():
        x = x_ref[...]
        ms = jnp.mean(x * x, axis=-1, keepdims=True)
        h_ref[...] = (x * lax.rsqrt(ms + EPS) * g_ref[...]).astype(BF16)

    o_ref[...] = jnp.dot(h_ref[...], w_ref[...], preferred_element_type=F32)


def _norm_mm(x, g, w, tn):
    m, k = x.shape
    n = w.shape[1]
    tm = min(m, 1024)
    return pl.pallas_call(
        _norm_mm_kernel,
        out_shape=jax.ShapeDtypeStruct((m, n), F32),
        grid=(m // tm, n // tn),
        in_specs=[
            pl.BlockSpec((tm, k), lambda i, j: (i, 0)),
            pl.BlockSpec((1, k), lambda i, j: (0, 0)),
            pl.BlockSpec((k, tn), lambda i, j: (0, j)),
        ],
        out_specs=pl.BlockSpec((tm, tn), lambda i, j: (i, j)),
        scratch_shapes=[pltpu.VMEM((tm, k), BF16)],
        compiler_params=pltpu.CompilerParams(
            dimension_semantics=("parallel", "arbitrary"), vmem_limit_bytes=VMEM_LIMIT),
        name="norm_mm",
    )(x, g.reshape(1, k), w)


def _out_mm_kernel(*refs, n_in, final):
    o_refs = refs[:n_in]
    w_ref, x_ref = refs[n_in], refs[n_in + 1]
    out_ref = refs[-1]
    acc = x_ref[...]
    k0 = 0
    for o_ref in o_refs:
        kk = o_ref.shape[1]
        acc = acc + jnp.dot(o_ref[...], w_ref[k0:k0 + kk, :], preferred_element_type=F32)
        k0 += kk
    if final:
        fg_ref = refs[n_in + 2]
        ms = jnp.mean(acc * acc, axis=-1, keepdims=True)
        acc = acc * lax.rsqrt(ms + EPS) * fg_ref[...]
    out_ref[...] = acc


def _out_mm(os_, w, x, final_g=None):
    m, d = x.shape
    tm = 512
    final = final_g is not None
    in_specs = [pl.BlockSpec((tm, o.shape[1]), lambda i: (i, 0)) for o in os_]
    in_specs += [pl.BlockSpec(w.shape, lambda i: (0, 0)), pl.BlockSpec((tm, d), lambda i: (i, 0))]
    args = list(os_) + [w, x]
    if final:
        in_specs.append(pl.BlockSpec((1, d), lambda i: (0, 0)))
        args.append(final_g.reshape(1, d))
    return pl.pallas_call(
        functools.partial(_out_mm_kernel, n_in=len(os_), final=final),
        out_shape=jax.ShapeDtypeStruct((m, d), F32),
        grid=(m // tm,),
        in_specs=in_specs,
        out_specs=pl.BlockSpec((tm, d), lambda i: (i, 0)),
        compiler_params=pltpu.CompilerParams(
            dimension_semantics=("parallel",), vmem_limit_bytes=VMEM_LIMIT),
        name="out_mm",
    )(*args)


def _causal_conv_block(x_ref, xp_ref, cw_ref, n_seq, rows, width, emit, bias_ref=None):
    lc = 512
    for s in range(n_seq):
        xp_ref[s, HIST:HIST + rows, :] = x_ref[s * rows:(s + 1) * rows, :]
        for c0 in range(0, width, lc):
            cs = slice(c0, c0 + lc)
            base = HIST - (CONV_W - 1)
            acc = xp_ref[s, base:base + rows, cs] * cw_ref[0:1, cs]
            for j in range(1, CONV_W):
                acc = acc + xp_ref[s, base + j:base + j + rows, cs] * cw_ref[j:j + 1, cs]
            if bias_ref is not None:
                acc = acc + bias_ref[:, cs]
            emit(s, cs, acc)
        xp_ref[s, 0:HIST, :] = xp_ref[s, rows:rows + HIST, :]


def _seq_masks(n, rows):
    ri = lax.broadcasted_iota(jnp.int32, (n, n), 0)
    ci = lax.broadcasted_iota(jnp.int32, (n, n), 1)
    if rows == n:
        same = ri >= 0
    else:
        sh = rows.bit_length() - 1
        same = (ri >> sh) == (ci >> sh)
    return ri, ci, same


NEUMANN_BLOCK = 8


def _inverse_masks(ri, ci, rows):
    assert rows >= NEUMANN_BLOCK
    blk = lambda k: (ri >> (k.bit_length() - 1)) == (ci >> (k.bit_length() - 1))
    eye = jnp.where(ri == ci, 1.0, 0.0).astype(F32)
    levels = []
    k = NEUMANN_BLOCK
    while k < rows:
        levels.append(blk(2 * k) & jnp.logical_not(blk(k)))
        k *= 2
    return eye, blk(NEUMANN_BLOCK), levels


def _unit_lower_inverse_each(as_, masks):
    eye, base, levels = masks
    a8 = [jnp.where(base, a, 0.0) for a in as_]
    a8s = _split_each(a8)
    p2 = _mm3_each(a8s, a8s)
    p2s = _split_each(p2)
    p4s = _split_each(_mm3_each(p2s, p2s))
    a8p2 = _mm3_each(a8s, p2s)
    x = [(eye - a) + (p - ap) for a, p, ap in zip(a8, p2, a8p2)]
    t = [xi + xp for xi, xp in zip(x, _mm3_each(x, p4s))]
    for lvl in levels:
        ts = _split_each(t)
        off = [jnp.where(lvl, a, 0.0) for a in as_]
        t = [ti - d for ti, d in zip(t, _mm3_each(_mm3_each(ts, off), ts))]
    return t


def _delta_kernel(*refs, n_seq, rows, has_state, n_alias):
    refs = refs[n_alias:]
    qkv_ref, z_ref, tail_ref = refs[:3]
    p = 3
    if has_state:
        s0_ref, c0_ref = refs[p], refs[p + 1]
        p += 2
    cw_ref, gp_ref, na_ref = refs[p:p + 3]
    o_ref, s_ref, cn_ref = refs[p + 3:p + 6]
    xp_ref, act_ref = refs[p + 6:p + 8]
    n = n_seq * rows
    c = pl.program_id(1)

    @pl.when(c == 0)
    def _init():
        if has_state:
            s_ref[...] = s0_ref[...]
            for s in range(n_seq):
                xp_ref[s, HIST - (CONV_W - 1):HIST, :] = c0_ref[s]
        else:
            s_ref[...] = jnp.zeros(s_ref.shape, F32)
            xp_ref[:, 0:HIST, :] = jnp.zeros((n_seq, HIST, QKV_A), F32)

    def emit(s, cs, acc):
        act_ref[s * rows:(s + 1) * rows, cs] = _silu(acc)

    _causal_conv_block(qkv_ref, xp_ref, cw_ref, n_seq, rows, QKV_A, emit)

    @pl.when(c == pl.num_programs(1) - 1)
    def _conv_out():
        for s in range(n_seq):
            cn_ref[s] = xp_ref[s, HIST - (CONV_W - 1):HIST, :]

    _, _, same_n = _seq_masks(n, rows)
    rn = lax.broadcasted_iota(jnp.int32, (n, n), 0)
    cn = lax.broadcasted_iota(jnp.int32, (n, n), 1)
    tail = tail_ref[...]
    btile = _sigmoid(tail)
    gtile = -jnp.exp(gp_ref[0:1, :]) * jax.nn.softplus(tail + gp_ref[1:2, :])
    g_cum = _mm(jnp.where(same_n & (rn >= cn), 1.0, 0.0).astype(F32), gtile)
    g_tot = _mm(jnp.where(same_n, 1.0, 0.0).astype(F32), gtile)
    e_cum = jnp.exp(g_cum)
    e_rest = jnp.exp(g_tot - g_cum)
    g_cum_t = g_cum.T

    m = HEAD_GROUP * n
    ri, ci, same = _seq_masks(m, rows)
    incl = same & (ri >= ci)
    strict = same & (ri > ci)
    inv_masks = _inverse_masks(ri, ci, rows)
    stack = lambda xs: jnp.concatenate(xs, axis=0)
    groups = [range(g * HEAD_GROUP, (g + 1) * HEAD_GROUP) for g in range(N_HEAD_A // HEAD_GROUP)]
    qn, kn, v, beta, gc, eg, er, g_row = ([] for _ in range(8))
    for heads in groups:
        q_h, k_h, v_h = [], [], []
        for h in heads:
            q = act_ref[:, h * DK_A:(h + 1) * DK_A]
            k = act_ref[:, KA + h * DK_A:KA + (h + 1) * DK_A]
            q_h.append(q * lax.rsqrt(jnp.sum(q * q, axis=-1, keepdims=True) + EPS) * (DK_A ** -0.5))
            k_h.append(k * lax.rsqrt(jnp.sum(k * k, axis=-1, keepdims=True) + EPS))
            v_h.append(act_ref[:, 2 * KA + h * DV_A:2 * KA + (h + 1) * DV_A])
        qn.append(stack(q_h))
        kn.append(stack(k_h))
        v.append(stack(v_h))
        beta.append(stack([btile[:, LANE_BETA + h:LANE_BETA + h + 1] for h in heads]))
        gc.append(stack([g_cum[:, LANE_A + h:LANE_A + h + 1] for h in heads]))
        eg.append(stack([e_cum[:, LANE_A + h:LANE_A + h + 1] for h in heads]))
        er.append(stack([e_rest[:, LANE_A + h:LANE_A + h + 1] for h in heads]))
        g_row.append(jnp.concatenate([g_cum_t[LANE_A + h:LANE_A + h + 1, :] for h in heads], axis=1))
    decay = [jnp.where(incl, jnp.exp(jnp.where(incl, c_ - r_, 0.0)), 0.0) for c_, r_ in zip(gc, g_row)]
    kb = [k_ * b_ for k_, b_ in zip(kn, beta)]
    prod = _mm3_each([stack([kb_, q_]) for kb_, q_ in zip(kb, qn)], kn, NT)
    a = [jnp.where(strict, p_[:m] * d_, 0.0) for p_, d_ in zip(prod, decay)]
    qk = [p_[m:] * d_ for p_, d_ in zip(prod, decay)]
    t_inv = _unit_lower_inverse_each(a, inv_masks)
    rhs = [jnp.concatenate([v_ * b_, kb_ * e_], axis=1) for v_, b_, kb_, e_ in zip(v, beta, kb, eg)]
    sol = _mm3_each(t_inv, rhs)
    qg = [q_ * e_ for q_, e_ in zip(qn, eg)]
    kdec = [k_ * e_ for k_, e_ in zip(kn, er)]
    probs = [(g, i, h, s, slice(i * n + s * rows, i * n + (s + 1) * rows))
             for g, heads in enumerate(groups) for i, h in enumerate(heads) for s in range(n_seq)]
    states = [s_ref[s, h] for (_, _, h, s, _) in probs]
    r = _mm3_each([stack([sol[g][sl, DV_A:], qg[g][sl]]) for (g, _, _, _, sl) in probs], states)
    v_new = [sol[g][sl, :DV_A] - r_[:rows] for (g, _, _, _, sl), r_ in zip(probs, r)]
    per_group = lambda xs: [stack([x for (g2, *_), x in zip(probs, xs) if g2 == g]) for g in range(len(groups))]
    v_new_g = per_group(v_new)
    o_inter_g = per_group([r_[rows:] for r_ in r])
    o = [oi + d for oi, d in zip(o_inter_g, _mm3_each(qk, v_new_g))]
    upd = _mm3_each([kdec[g][sl] for (g, _, _, _, sl) in probs], v_new, TN)
    for (g, i, h, s, sl), st, du in zip(probs, states, upd):
        g_last = g_tot[s * rows:s * rows + 1, LANE_A + h:LANE_A + h + 1]
        s_ref[s, h] = st * jnp.exp(g_last) + du
    for g, heads in enumerate(groups):
        for i, h in enumerate(heads):
            hl = slice(h * DV_A, (h + 1) * DV_A)
            oh = o[g][i * n:(i + 1) * n]
            oh = oh * lax.rsqrt(jnp.mean(oh * oh, axis=-1, keepdims=True) + EPS) * na_ref[...]
            o_ref[:, hl] = (oh * _silu(z_ref[:, hl])).astype(BF16)


def _layer_spec(block, layer):
    zeros = (0,) * (len(block) - 1)
    return pl.BlockSpec((None,) + tuple(block), lambda i, c: (layer, i) + zeros)


def _carry_outputs(prev, out_start):
    if prev is None:
        return [], [], {}
    specs = [pl.BlockSpec(memory_space=pl.ANY) for _ in prev]
    return list(prev), specs, {k: out_start + k for k in range(len(prev))}


def _delta(proj, layer, n_layers, s0, c0, prev, conv_w, gate_params, norm_a, n_batch, t_len):
    rows = min(ROWS, t_len)
    n_seq = ROWS // rows
    nb = n_batch // n_seq
    nc = t_len // rows
    has_state = s0 is not None
    row_map = lambda i, c: (i * nc + c, 0)
    args, in_specs, aliases = _carry_outputs(prev, 1)
    n_alias = len(args)
    in_specs += [
        pl.BlockSpec((ROWS, QKV_A), row_map),
        pl.BlockSpec((ROWS, VA), lambda i, c: (i * nc + c, OFF_ZA // VA)),
        pl.BlockSpec((ROWS, TAIL_W), lambda i, c: (i * nc + c, OFF_TAIL // TAIL_W)),
    ]
    args += [proj, proj, proj]
    s_block = (n_seq, N_HEAD_A, DK_A, DV_A)
    c_block = (n_seq, CONV_W - 1, QKV_A)
    if has_state:
        in_specs += [_layer_spec(s_block, layer), _layer_spec(c_block, layer)]
        args += [s0, c0]
    in_specs += [
        pl.BlockSpec((CONV_W, QKV_A), lambda i, c: (0, 0)),
        pl.BlockSpec((2, TAIL_W), lambda i, c: (0, 0)),
        pl.BlockSpec((1, DV_A), lambda i, c: (0, 0)),
    ]
    args += [conv_w, gate_params, norm_a.reshape(1, DV_A)]
    return pl.pallas_call(
        functools.partial(_delta_kernel, n_seq=n_seq, rows=rows, has_state=has_state, n_alias=n_alias),
        out_shape=(
            jax.ShapeDtypeStruct((n_batch * t_len, VA), BF16),
            jax.ShapeDtypeStruct((n_layers, n_batch, N_HEAD_A, DK_A, DV_A), F32),
            jax.ShapeDtypeStruct((n_layers, n_batch, CONV_W - 1, QKV_A), F32),
        ),
        grid=(nb, nc),
        in_specs=in_specs,
        out_specs=(
            pl.BlockSpec((ROWS, VA), row_map),
            _layer_spec(s_block, layer),
            _layer_spec(c_block, layer),
        ),
        scratch_shapes=[
            pltpu.VMEM((n_seq, HIST + rows, QKV_A), F32),
            pltpu.VMEM((ROWS, QKV_A), F32),
        ],
        input_output_aliases=aliases,
        compiler_params=pltpu.CompilerParams(
            dimension_semantics=("parallel", "arbitrary"), vmem_limit_bytes=VMEM_LIMIT),
        name="delta",
    )(*args)


def _gla_kernel(*refs, n_state, rows, has_state, n_alias):
    refs = refs[n_alias:]
    q_ref, k_ref, v_ref, z_ref, tail_ref = refs[:5]
    p = 5
    if has_state:
        s0_ref = refs[p]
        p += 1
    wlr_ref, blr_ref, nb_ref = refs[p:p + 3]
    o_ref, s_ref = refs[p + 3:p + 5]
    n = ROWS
    n_sub = n // rows
    chained = n_state == 1
    c = pl.program_id(1)

    @pl.when(c == 0)
    def _init():
        if has_state:
            s_ref[...] = s0_ref[...]
        else:
            s_ref[...] = jnp.zeros(s_ref.shape, F32)

    ri, ci, same = _seq_masks(n, rows)
    incl = same & (ri >= ci)
    gk = jax.nn.log_sigmoid(_mm(tail_ref[...], wlr_ref[...]) + blr_ref[...]) / GLA_NORMALIZER
    b_loc = _mm(jnp.where(incl, 1.0, 0.0).astype(F32), gk)
    q = q_ref[...] * (DK_B ** -0.5)
    k = k_ref[...]
    q_loc = q * jnp.exp(b_loc)
    heads = range(N_HEAD_B)
    hls = [slice(h * DK_B, (h + 1) * DK_B) for h in heads]
    vs = [v_ref[:, h * DV_B:(h + 1) * DV_B] for h in heads]

    def finish(h, o):
        vl = slice(h * DV_B, (h + 1) * DV_B)
        o = o * lax.rsqrt(jnp.mean(o * o, axis=-1, keepdims=True) + EPS) * nb_ref[...]
        o_ref[:, vl] = (o * _silu(z_ref[:, vl])).astype(BF16)

    def key_column(row_vals):
        return jnp.broadcast_to(row_vals, (8, DK_B)).T[:, 0:1]

    if chained:
        lower = ri >= ci
        b_cum = _mm(jnp.where(lower, 1.0, 0.0).astype(F32), gk)
        b_last = b_cum[n - 1:n]
        q_cum = q * jnp.exp(b_cum)
        kdec = k * jnp.exp(b_last - b_cum)
        row = lax.broadcasted_iota(jnp.int32, (n, KB), 0)
        k_rel = []
        for s in range(n_sub):
            start = b_cum[s * rows - 1:s * rows] if s else jnp.zeros((1, KB), F32)
            k_rel.append(k * jnp.exp(jnp.where(row < (s + 1) * rows, start - b_cum, 0.0)))
        a_parts = _mm3_each([q_loc[s * rows:(s + 1) * rows, hl] for hl in hls for s in range(n_sub)],
                            [k_rel[s][:, hl] for hl in hls for s in range(n_sub)], NT)
        a = [jnp.where(lower, jnp.concatenate(a_parts[h * n_sub:(h + 1) * n_sub], axis=0), 0.0) for h in heads]
        sts = [s_ref[0, h] for h in heads]
        o_state = _mm3_each([q_cum[:, hl] for hl in hls], sts)
        intra = _mm3_each(a, vs)
        upd = _mm3_each([kdec[:, hl] for hl in hls], vs, TN)
        for h in heads:
            finish(h, o_state[h] + intra[h])
            s_ref[0, h] = sts[h] * jnp.exp(key_column(b_last[:, hls[h]])) + upd[h]
    else:
        b_tot = _mm(jnp.where(same, 1.0, 0.0).astype(F32), gk)
        kd = k * jnp.exp(-b_loc)
        kdec = k * jnp.exp(b_tot - b_loc)
        a = [jnp.where(incl, x, 0.0) for x in _mm3_each([q_loc[:, hl] for hl in hls], [kd[:, hl] for hl in hls], NT)]
        intra = _mm3_each(a, vs)
        probs = [(h, s, slice(s * rows, (s + 1) * rows)) for h in heads for s in range(n_sub)]
        sts = [s_ref[s, h] for (h, s, _) in probs]
        o_state = _mm3_each([q_loc[sl, hls[h]] for (h, _, sl) in probs], sts)
        upd = _mm3_each([kdec[sl, hls[h]] for (h, _, sl) in probs], [vs[h][sl] for (h, _, sl) in probs], TN)
        for (h, s, sl), st, du in zip(probs, sts, upd):
            s_ref[s, h] = st * jnp.exp(key_column(b_tot[s * rows:s * rows + 1, hls[h]])) + du
        for h in heads:
            finish(h, jnp.concatenate(o_state[h * n_sub:(h + 1) * n_sub], axis=0) + intra[h])


def _gla(proj, layer, n_layers, s0, prev, w_lr_pad, b_lr, norm_b, n_batch, t_len, chunk):
    rows = min(chunk, t_len)
    has_state = s0 is not None
    if t_len >= ROWS:
        n_state, nb, nc = 1, n_batch, t_len // ROWS
    else:
        n_state, nb, nc = ROWS // t_len, n_batch // (ROWS // t_len), 1
    args, in_specs, aliases = _carry_outputs(prev, 1)
    n_alias = len(args)
    in_specs += [
        pl.BlockSpec((ROWS, KB), lambda i, c: (i * nc + c, OFF_QB // KB)),
        pl.BlockSpec((ROWS, KB), lambda i, c: (i * nc + c, OFF_KB // KB)),
        pl.BlockSpec((ROWS, VB), lambda i, c: (i * nc + c, OFF_VB // VB)),
        pl.BlockSpec((ROWS, VB), lambda i, c: (i * nc + c, OFF_ZB // VB)),
        pl.BlockSpec((ROWS, TAIL_W), lambda i, c: (i * nc + c, OFF_TAIL // TAIL_W)),
    ]
    args += [proj] * 5
    s_block = (n_state, N_HEAD_B, DK_B, DV_B)
    if has_state:
        in_specs.append(_layer_spec(s_block, layer))
        args.append(s0)
    in_specs += [
        pl.BlockSpec((TAIL_W, KB), lambda i, c: (0, 0)),
        pl.BlockSpec((1, KB), lambda i, c: (0, 0)),
        pl.BlockSpec((1, DV_B), lambda i, c: (0, 0)),
    ]
    args += [w_lr_pad, b_lr.reshape(1, KB), norm_b.reshape(1, DV_B)]
    return pl.pallas_call(
        functools.partial(_gla_kernel, n_state=n_state, rows=rows, has_state=has_state, n_alias=n_alias),
        out_shape=(
            jax.ShapeDtypeStruct((n_batch * t_len, VB), BF16),
            jax.ShapeDtypeStruct((n_layers, n_batch, N_HEAD_B, DK_B, DV_B), F32),
        ),
        grid=(nb, nc),
        in_specs=in_specs,
        out_specs=(
            pl.BlockSpec((ROWS, VB), lambda i, c: (i * nc + c, 0)),
            _layer_spec(s_block, layer),
        ),
        input_output_aliases=aliases,
        compiler_params=pltpu.CompilerParams(
            dimension_semantics=("parallel", "arbitrary"), vmem_limit_bytes=VMEM_LIMIT),
        name="gla",
    )(*args)


def _lru_kernel(*refs, n_seq, rows, has_state, reset_first, n_alias):
    refs = refs[n_alias:]
    xb_ref, gate_ref = refs[:2]
    p = 2
    if has_state:
        h0_ref, c0_ref = refs[p], refs[p + 1]
        p += 2
    cw_ref, cb_ref, wg_ref, ba_ref, bx_ref, lam_ref = refs[p:p + 6]
    y_ref, h_ref, cn_ref = refs[p + 6:p + 9]
    xp_ref, xc_ref, a_ref, b_ref = refs[p + 9:p + 13]
    n = n_seq * rows
    c = pl.program_id(1)

    @pl.when(c == 0)
    def _init():
        if has_state:
            h_ref[...] = h0_ref[...]
            for s in range(n_seq):
                xp_ref[s, HIST - (CONV_W - 1):HIST, :] = c0_ref[s]
        else:
            h_ref[...] = jnp.zeros(h_ref.shape, F32)
            xp_ref[:, 0:HIST, :] = jnp.zeros((n_seq, HIST, W_LRU), F32)

    def emit(s, cs, acc):
        xc_ref[s * rows:(s + 1) * rows, cs] = acc

    _causal_conv_block(xb_ref, xp_ref, cw_ref, n_seq, rows, W_LRU, emit, bias_ref=cb_ref)

    @pl.when(c == pl.num_programs(1) - 1)
    def _conv_out():
        for s in range(n_seq):
            cn_ref[s] = xp_ref[s, HIST - (CONV_W - 1):HIST, :]

    row = lax.broadcasted_iota(jnp.int32, (n, LRU_BW), 0)
    scan_steps = [(d, (row & 7) >= d) for d in (1, 2, 4)]
    first_row = (row == 0) & (c == 0)
    for blk in range(LRU_BLOCKS):
        bl = slice(blk * LRU_BW, (blk + 1) * LRU_BW)
        xc = xc_ref[:, bl]
        xh, xl = _split(xc)
        pre = jnp.dot(jnp.concatenate([xh, xh, xl], axis=1), wg_ref[blk], preferred_element_type=F32)
        r = _sigmoid(pre[:, :LRU_BW] + ba_ref[:, bl])
        gi = _sigmoid(pre[:, LRU_BW:] + bx_ref[:, bl])
        log_a = -LRU_C * r * jax.nn.softplus(-lam_ref[:, bl])
        a = jnp.exp(log_a)
        mult = jnp.sqrt(-jnp.tanh(log_a) * (a * a + 1.0))
        if reset_first:
            mult = jnp.where(first_row, 1.0, mult)
        b = mult * gi * xc
        for d, m in scan_steps:
            a_sh = pltpu.roll(a, d, axis=0)
            b_sh = pltpu.roll(b, d, axis=0)
            b = jnp.where(m, a * b_sh + b, b)
            a = jnp.where(m, a * a_sh, a)
        a_ref[:, bl] = a
        b_ref[:, bl] = b

    for s in range(n_seq):
        def body(g, hp, s=s):
            r0 = pl.multiple_of(s * rows + g * 8, 8)
            hs = a_ref[pl.ds(r0, 8), :] * hp + b_ref[pl.ds(r0, 8), :]
            b_ref[pl.ds(r0, 8), :] = hs
            return hs[7:8, :]

        h_ref[s] = lax.fori_loop(0, rows // 8, body, h_ref[s])

    lc = 512
    for c0 in range(0, W_LRU, lc):
        cs = slice(c0, c0 + lc)
        y_ref[:, cs] = (b_ref[:, cs] * _silu(gate_ref[:, cs])).astype(BF16)


def _lru(proj, layer, n_layers, h0, c0, prev, conv_w, conv_b, w_a, b_a, w_x, b_x, lam, n_batch, t_len,
         reset_first):
    has_state = h0 is not None
    if t_len >= 256:
        rows, n_seq = 256, 1
    else:
        rows, n_seq = t_len, 128 // t_len
    n = rows * n_seq
    nb = n_batch // n_seq
    nc = t_len // rows
    args, in_specs, aliases = _carry_outputs(prev, 1)
    n_alias = len(args)
    in_specs += [
        pl.BlockSpec((n, W_LRU), lambda i, c: (i * nc + c, 0)),
        pl.BlockSpec((n, W_LRU), lambda i, c: (i * nc + c, 1)),
    ]
    args += [proj, proj]
    h_block = (n_seq, 1, W_LRU)
    c_block = (n_seq, CONV_W - 1, W_LRU)
    if has_state:
        in_specs += [_layer_spec(h_block, layer), _layer_spec(c_block, layer)]
        args += [h0, c0]
    vec = pl.BlockSpec((1, W_LRU), lambda i, c: (0, 0))
    w_hi, w_lo = _split(jnp.concatenate([w_a, w_x], axis=-1))
    w_gates = jnp.concatenate([w_hi, w_lo, w_hi], axis=1)
    blk = pl.BlockSpec(w_gates.shape, lambda i, c: (0, 0, 0))
    in_specs += [pl.BlockSpec((CONV_W, W_LRU), lambda i, c: (0, 0)), vec, blk, vec, vec, vec]
    args += [conv_w, conv_b.reshape(1, W_LRU), w_gates, b_a.reshape(1, W_LRU),
             b_x.reshape(1, W_LRU), lam.reshape(1, W_LRU)]
    return pl.pallas_call(
        functools.partial(_lru_kernel, n_seq=n_seq, rows=rows, has_state=has_state,
                          reset_first=reset_first, n_alias=n_alias),
        out_shape=(
            jax.ShapeDtypeStruct((n_batch * t_len, W_LRU), BF16),
            jax.ShapeDtypeStruct((n_layers, n_batch, 1, W_LRU), F32),
            jax.ShapeDtypeStruct((n_layers, n_batch, CONV_W - 1, W_LRU), F32),
        ),
        grid=(nb, nc),
        in_specs=in_specs,
        out_specs=(
            pl.BlockSpec((n, W_LRU), lambda i, c: (i * nc + c, 0)),
            _layer_spec(h_block, layer),
            _layer_spec(c_block, layer),
        ),
        input_output_aliases=aliases,
        scratch_shapes=[
            pltpu.VMEM((n_seq, HIST + rows, W_LRU), F32),
            pltpu.VMEM((n, W_LRU), F32),
            pltpu.VMEM((n, W_LRU), F32),
            pltpu.VMEM((n, W_LRU), F32),
        ],
        compiler_params=pltpu.CompilerParams(
            dimension_semantics=("parallel", "arbitrary"), vmem_limit_bytes=VMEM_LIMIT),
        name="lru",
    )(*args)


def _ab_w_in_layout(w):
    sizes = [QKV_A, N_HEAD_A, N_HEAD_A, VA, KB, KB, VB, LOWRANK, VB]
    offs = [0]
    for s in sizes:
        offs.append(offs[-1] + s)
    qkv, b_raw, a_raw, z_a, q_b, k_b, v_b, lr_b, z_b = [w[:, offs[i]:offs[i + 1]] for i in range(9)]
    used = OFF_TAIL + 2 * N_HEAD_A + LOWRANK
    pad = jnp.zeros((w.shape[0], AB_N - used), w.dtype)
    return jnp.concatenate([qkv, z_a, q_b, k_b, v_b, z_b, b_raw, a_raw, lr_b, pad], axis=1).astype(BF16)


def _tail_row(vals, lane0):
    return jnp.zeros((TAIL_W,), F32).at[lane0:lane0 + vals.shape[0]].set(vals.astype(F32))


def kernel(x_prompt, x_sample, state_delta, state_delta_conv, state_gla, state_lru, state_lru_conv,
           ab_norm, ab_w_in, ab_conv_w, ab_a_log, ab_dt_bias, ab_norm_a, ab_gla_w_lr, ab_gla_b_lr,
           ab_norm_b, ab_w_out, lru_norm, lru_w_in, lru_conv_w, lru_conv_b, lru_w_a, lru_b_a,
           lru_w_x, lru_b_x, lru_lambda, lru_w_out, final_norm):
    n_ab, n_lru = ab_norm.shape[0], lru_norm.shape[0]
    depth = n_ab + n_lru
    lru_h0 = state_lru.reshape(n_lru, -1, 1, W_LRU)
    groups = []
    for x, carried in ((x_prompt, False), (x_sample, True)):
        groups.append(dict(x=x.reshape(-1, D_MODEL), nb=x.shape[0], t=x.shape[1], carried=carried,
                           delta=None, gla=None, lru=None))

    for layer in range(depth):
        j = layer // 2
        last = layer == depth - 1
        if layer % 2 == 0:
            w_in = _ab_w_in_layout(ab_w_in[j])
            w_out = ab_w_out[j].astype(BF16)
            gate_params = jnp.stack([_tail_row(ab_a_log[j], LANE_A), _tail_row(ab_dt_bias[j], LANE_A)])
            w_lr_pad = jnp.zeros((TAIL_W, KB), F32).at[LANE_LR:LANE_LR + LOWRANK].set(ab_gla_w_lr[j])
            for g in groups:
                proj = _norm_mm(g["x"], ab_norm[j], w_in, AB_TN)
                s_a, c_a, s_b = (state_delta, state_delta_conv, state_gla) if g["carried"] else (None,) * 3
                o_a, *g["delta"] = _delta(proj, j, n_ab, s_a, c_a, g["delta"], ab_conv_w[j], gate_params,
                                          ab_norm_a[j], g["nb"], g["t"])
                o_b, *g["gla"] = _gla(proj, j, n_ab, s_b, g["gla"], w_lr_pad, ab_gla_b_lr[j], ab_norm_b[j],
                                      g["nb"], g["t"], 16)
                g["x"] = _out_mm([o_a, o_b], w_out, g["x"], final_norm if last else None)
        else:
            w_in = lru_w_in[j].astype(BF16)
            w_out = lru_w_out[j].astype(BF16)
            for g in groups:
                proj = _norm_mm(g["x"], lru_norm[j], w_in, 1024)
                h0, c0 = (lru_h0, state_lru_conv) if g["carried"] else (None, None)
                y, *g["lru"] = _lru(proj, j, n_lru, h0, c0, g["lru"], lru_conv_w[j], lru_conv_b[j],
                                    lru_w_a[j], lru_b_a[j], lru_w_x[j], lru_b_x[j], lru_lambda[j],
                                    g["nb"], g["t"], reset_first=not g["carried"])
                g["x"] = _out_mm([y], w_out, g["x"], final_norm if last else None)

    outs = []
    for g, x in zip(groups, (x_prompt, x_sample)):
        h_all, lconv = g["lru"]
        outs.append([g["x"].reshape(x.shape), g["delta"][0], g["delta"][1], g["gla"][0],
                     h_all.reshape(n_lru, g["nb"], W_LRU), lconv])
    p, s = outs
    return (p[0], s[0], p[1], p[2], p[3], p[4], p[5], s[1], s[2], s[3], s[4], s[5])
```

```python
import functools

import jax
import jax.numpy as jnp
from jax import lax
from jax.experimental import pallas as pl
from jax.experimental.pallas import tpu as pltpu

F32 = jnp.float32
BF16 = jnp.bfloat16
HIGHEST = lax.Precision.HIGHEST

D_MODEL = 2048
N_HEAD_A, DK_A, DV_A = 8, 128, 128
N_HEAD_B, DK_B, DV_B = 4, 128, 256
KA = N_HEAD_A * DK_A
VA = N_HEAD_A * DV_A
KB = N_HEAD_B * DK_B
VB = N_HEAD_B * DV_B
QKV_A = 2 * KA + VA
LOWRANK = 16
GLA_NORMALIZER = 16.0
W_LRU = D_MODEL
LRU_BLOCKS = 16
LRU_BW = W_LRU // LRU_BLOCKS
LRU_C = 8.0
CONV_W = 4
EPS = 1e-6
LANES = 128
TINY = 1.1754944e-38

OFF_QKV = 0
OFF_ZA = OFF_QKV + QKV_A
OFF_QB = OFF_ZA + VA
OFF_KB = OFF_QB + KB
OFF_VB = OFF_KB + KB
OFF_ZB = OFF_VB + VB
OFF_TAIL = OFF_ZB + VB
TAIL_W = 128
AB_TN = 1536
AB_N = 7680
LANE_BETA, LANE_A, LANE_LR = 0, N_HEAD_A, 2 * N_HEAD_A

ROWS = 64
HEAD_GROUP = 2
HIST = 8
VMEM_LIMIT = 56 * 1024 * 1024

NN = (((1,), (0,)), ((), ()))
NT = (((1,), (1,)), ((), ()))
TN = (((0,), (0,)), ((), ()))


def _mm(a, b, dims=NN):
    return lax.dot_general(a, b, dims, precision=HIGHEST, preferred_element_type=F32)


def _split(a):
    hi = a.astype(BF16)
    return hi, (a - hi.astype(F32)).astype(BF16)


def _mm3(a, b, dims=NN):
    ah, al = a if isinstance(a, tuple) else _split(a)
    bh, bl = b if isinstance(b, tuple) else _split(b)
    (ca,), (cb,) = dims[0]
    dot = functools.partial(lax.dot_general, dimension_numbers=dims, preferred_element_type=F32)
    if ah.shape[ca] % LANES:
        return dot(ah, bh) + (dot(ah, bl) + dot(al, bh))
    return dot(jnp.concatenate([ah, ah, al], axis=ca), jnp.concatenate([bh, bl, bh], axis=cb))


def _split_each(xs):
    return [x if isinstance(x, tuple) else _split(x) for x in xs]


def _mm3_each(as_, bs, dims=NN):
    sa, sb = _split_each(as_), _split_each(bs)
    return [_mm3(a, b, dims) for a, b in zip(sa, sb)]


def _mm1_each(as_, bs, dims=NN):
    ca = [a.astype(BF16) for a in as_]
    cb = [b.astype(BF16) for b in bs]
    return [lax.dot_general(a, b, dims, preferred_element_type=F32) for a, b in zip(ca, cb)]


def _sigmoid(x):
    return 0.5 * jnp.tanh(0.5 * x) + 0.5


def _silu(x):
    h = 0.5 * x
    return h + h * jnp.tanh(h)


def _norm_mm_kernel(x_ref, g_ref, w_ref, o_ref, h_ref):
    @pl.when(pl.program_id(1) == 0)
    def _():
        x = x_ref[...]
        ms = jnp.mean(x * x, axis=-1, keepdims=True)
        h_ref[...] = (x * lax.rsqrt(ms + EPS) * g_ref[...]).astype(BF16)

    o_ref[...] = jnp.dot(h_ref[...], w_ref[...], preferred_element_type=F32)


def _norm_mm(x, g, w, tn):
    m, k = x.shape
    n = w.shape[1]
    tm = min(m, 1024)
    return pl.pallas_call(
        _norm_mm_kernel,
        out_shape=jax.ShapeDtypeStruct((m, n), F32),
        grid=(m // tm, n // tn),
        in_specs=[
            pl.BlockSpec((tm, k), lambda i, j: (i, 0)),
            pl.BlockSpec((1, k), lambda i, j: (0, 0)),
            pl.BlockSpec((k, tn), lambda i, j: (0, j)),
        ],
        out_specs=pl.BlockSpec((tm, tn), lambda i, j: (i, j)),
        scratch_shapes=[pltpu.VMEM((tm, k), BF16)],
        compiler_params=pltpu.CompilerParams(
            dimension_semantics=("parallel", "arbitrary"), vmem_limit_bytes=VMEM_LIMIT),
        name="norm_mm",
    )(x, g.reshape(1, k), w)


def _out_mm_kernel(*refs, n_in, final):
    o_refs = refs[:n_in]
    w_ref, x_ref = refs[n_in], refs[n_in + 1]
    out_ref = refs[-1]
    acc = x_ref[...]
    k0 = 0
    for o_ref in o_refs:
        kk = o_ref.shape[1]
        acc = acc + jnp.dot(o_ref[...], w_ref[k0:k0 + kk, :], preferred_element_type=F32)
        k0 += kk
    if final:
        fg_ref = refs[n_in + 2]
        ms = jnp.mean(acc * acc, axis=-1, keepdims=True)
        acc = acc * lax.rsqrt(ms + EPS) * fg_ref[...]
    out_ref[...] = acc


def _out_mm(os_, w, x, final_g=None):
    m, d = x.shape
    tm = 512
    final = final_g is not None
    in_specs = [pl.BlockSpec((tm, o.shape[1]), lambda i: (i, 0)) for o in os_]
    in_specs += [pl.BlockSpec(w.shape, lambda i: (0, 0)), pl.BlockSpec((tm, d), lambda i: (i, 0))]
    args = list(os_) + [w, x]
    if final:
        in_specs.append(pl.BlockSpec((1, d), lambda i: (0, 0)))
        args.append(final_g.reshape(1, d))
    return pl.pallas_call(
        functools.partial(_out_mm_kernel, n_in=len(os_), final=final),
        out_shape=jax.ShapeDtypeStruct((m, d), F32),
        grid=(m // tm,),
        in_specs=in_specs,
        out_specs=pl.BlockSpec((tm, d), lambda i: (i, 0)),
        compiler_params=pltpu.CompilerParams(
            dimension_semantics=("parallel",), vmem_limit_bytes=VMEM_LIMIT),
        name="out_mm",
    )(*args)


def _causal_conv_block(x_ref, xp_ref, cw_ref, n_seq, rows, width, emit, bias_ref=None):
    lc = 512
    for s in range(n_seq):
        xp_ref[s, HIST:HIST + rows, :] = x_ref[s * rows:(s + 1) * rows, :]
        for c0 in range(0, width, lc):
            cs = slice(c0, c0 + lc)
            base = HIST - (CONV_W - 1)
            acc = xp_ref[s, base:base + rows, cs] * cw_ref[0:1, cs]
            for j in range(1, CONV_W):
                acc = acc + xp_ref[s, base + j:base + j + rows, cs] * cw_ref[j:j + 1, cs]
            if bias_ref is not None:
                acc = acc + bias_ref[:, cs]
            emit(s, cs, acc)
        xp_ref[s, 0:HIST, :] = xp_ref[s, rows:rows + HIST, :]


def _seq_masks(n, rows):
    ri = lax.broadcasted_iota(jnp.int32, (n, n), 0)
    ci = lax.broadcasted_iota(jnp.int32, (n, n), 1)
    if rows == n:
        same = ri >= 0
    else:
        sh = rows.bit_length() - 1
        same = (ri >> sh) == (ci >> sh)
    return ri, ci, same


NEUMANN_BLOCK = 8


def _inverse_masks(ri, ci, rows):
    assert rows >= NEUMANN_BLOCK
    blk = lambda k: (ri >> (k.bit_length() - 1)) == (ci >> (k.bit_length() - 1))
    eye = jnp.where(ri == ci, 1.0, 0.0).astype(F32)
    levels = []
    k = NEUMANN_BLOCK
    while k < rows:
        levels.append(blk(2 * k) & jnp.logical_not(blk(k)))
        k *= 2
    return eye, blk(NEUMANN_BLOCK), levels


def _unit_lower_inverse_each(as_, masks):
    eye, base, levels = masks
    a8 = [jnp.where(base, a, 0.0) for a in as_]
    a8s = _split_each(a8)
    p2 = _mm3_each(a8s, a8s)
    p2s = _split_each(p2)
    p4s = _split_each(_mm3_each(p2s, p2s))
    a8p2 = _mm3_each(a8s, p2s)
    x = [(eye - a) + (p - ap) for a, p, ap in zip(a8, p2, a8p2)]
    t = [xi + xp for xi, xp in zip(x, _mm3_each(x, p4s))]
    for lvl in levels:
        ts = _split_each(t)
        off = [jnp.where(lvl, a, 0.0) for a in as_]
        t = [ti - d for ti, d in zip(t, _mm3_each(_mm3_each(ts, off), ts))]
    return t


def _delta_kernel(*refs, n_seq, rows, has_state, n_alias):
    refs = refs[n_alias:]
    qkv_ref, z_ref, tail_ref = refs[:3]
    p = 3
    if has_state:
        s0_ref, c0_ref = refs[p], refs[p + 1]
        p += 2
    cw_ref, gp_ref, na_ref = refs[p:p + 3]
    o_ref, s_ref, cn_ref = refs[p + 3:p + 6]
    xp_ref, act_ref = refs[p + 6:p + 8]
    n = n_seq * rows
    c = pl.program_id(1)

    @pl.when(c == 0)
    def _init():
        if has_state:
            s_ref[...] = s0_ref[...]
            for s in range(n_seq):
                xp_ref[s, HIST - (CONV_W - 1):HIST, :] = c0_ref[s]
        else:
            s_ref[...] = jnp.zeros(s_ref.shape, F32)
            xp_ref[:, 0:HIST, :] = jnp.zeros((n_seq, HIST, QKV_A), F32)

    def emit(s, cs, acc):
        act_ref[s * rows:(s + 1) * rows, cs] = _silu(acc)

    _causal_conv_block(qkv_ref, xp_ref, cw_ref, n_seq, rows, QKV_A, emit)

    @pl.when(c == pl.num_programs(1) - 1)
    def _conv_out():
        for s in range(n_seq):
            cn_ref[s] = xp_ref[s, HIST - (CONV_W - 1):HIST, :]

    _, _, same_n = _seq_masks(n, rows)
    rn = lax.broadcasted_iota(jnp.int32, (n, n), 0)
    cn = lax.broadcasted_iota(jnp.int32, (n, n), 1)
    tail = tail_ref[...]
    btile = _sigmoid(tail)
    gtile = -jnp.exp(gp_ref[0:1, :]) * jax.nn.softplus(tail + gp_ref[1:2, :])
    g_cum = _mm(jnp.where(same_n & (rn >= cn), 1.0, 0.0).astype(F32), gtile)
    g_tot = _mm(jnp.where(same_n, 1.0, 0.0).astype(F32), gtile)
    e_cum = jnp.exp(g_cum)
    e_rest = jnp.exp(g_tot - g_cum)
    g_cum_t = g_cum.T

    m = HEAD_GROUP * n
    ri, ci, same = _seq_masks(m, rows)
    incl = same & (ri >= ci)
    strict = same & (ri > ci)
    inv_masks = _inverse_masks(ri, ci, rows)
    stack = lambda xs: jnp.concatenate(xs, axis=0)
    groups = [range(g * HEAD_GROUP, (g + 1) * HEAD_GROUP) for g in range(N_HEAD_A // HEAD_GROUP)]
    qn, kn, v, beta, gc, eg, er, g_row = ([] for _ in range(8))
    for heads in groups:
        q_h, k_h, v_h = [], [], []
        for h in heads:
            q = act_ref[:, h * DK_A:(h + 1) * DK_A]
            k = act_ref[:, KA + h * DK_A:KA + (h + 1) * DK_A]
            q_h.append(q * lax.rsqrt(jnp.sum(q * q, axis=-1, keepdims=True) + EPS) * (DK_A ** -0.5))
            k_h.append(k * lax.rsqrt(jnp.sum(k * k, axis=-1, keepdims=True) + EPS))
            v_h.append(act_ref[:, 2 * KA + h * DV_A:2 * KA + (h + 1) * DV_A])
        qn.append(stack(q_h))
        kn.append(stack(k_h))
        v.append(stack(v_h))
        beta.append(stack([btile[:, LANE_BETA + h:LANE_BETA + h + 1] for h in heads]))
        gc.append(stack([g_cum[:, LANE_A + h:LANE_A + h + 1] for h in heads]))
        eg.append(stack([e_cum[:, LANE_A + h:LANE_A + h + 1] for h in heads]))
        er.append(stack([e_rest[:, LANE_A + h:LANE_A + h + 1] for h in heads]))
        g_row.append(jnp.concatenate([g_cum_t[LANE_A + h:LANE_A + h + 1, :] for h in heads], axis=1))
    decay = [jnp.where(incl, jnp.exp(jnp.where(incl, c_ - r_, 0.0)), 0.0) for c_, r_ in zip(gc, g_row)]
    kb = [k_ * b_ for k_, b_ in zip(kn, beta)]
    prod = _mm3_each([stack([kb_, q_]) for kb_, q_ in zip(kb, qn)], kn, NT)
    a = [jnp.where(strict, p_[:m] * d_, 0.0) for p_, d_ in zip(prod, decay)]
    qk = [p_[m:] * d_ for p_, d_ in zip(prod, decay)]
    t_inv = _unit_lower_inverse_each(a, inv_masks)
    rhs = [jnp.concatenate([v_ * b_, kb_ * e_], axis=1) for v_, b_, kb_, e_ in zip(v, beta, kb, eg)]
    sol = _mm3_each(t_inv, rhs)
    qg = [q_ * e_ for q_, e_ in zip(qn, eg)]
    kdec = [k_ * e_ for k_, e_ in zip(kn, er)]
    probs = [(g, i, h, s, slice(i * n + s * rows, i * n + (s + 1) * rows))
             for g, heads in enumerate(groups) for i, h in enumerate(heads) for s in range(n_seq)]
    states = [s_ref[s, h] for (_, _, h, s, _) in probs]
    r = _mm1_each([stack([sol[g][sl, DV_A:], qg[g][sl]]) for (g, _, _, _, sl) in probs], states)
    v_new = [sol[g][sl, :DV_A] - r_[:rows] for (g, _, _, _, sl), r_ in zip(probs, r)]
    per_group = lambda xs: [stack([x for (g2, *_), x in zip(probs, xs) if g2 == g]) for g in range(len(groups))]
    v_new_g = per_group(v_new)
    o_inter_g = per_group([r_[rows:] for r_ in r])
    o = [oi + d for oi, d in zip(o_inter_g, _mm1_each(qk, v_new_g))]
    upd = _mm1_each([kdec[g][sl] for (g, _, _, _, sl) in probs], v_new, TN)
    for (g, i, h, s, sl), st, du in zip(probs, states, upd):
        g_last = g_tot[s * rows:s * rows + 1, LANE_A + h:LANE_A + h + 1]
        s_ref[s, h] = st * jnp.exp(g_last) + du
    for g, heads in enumerate(groups):
        for i, h in enumerate(heads):
            hl = slice(h * DV_A, (h + 1) * DV_A)
            oh = o[g][i * n:(i + 1) * n]
            oh = oh * lax.rsqrt(jnp.mean(oh * oh, axis=-1, keepdims=True) + EPS) * na_ref[...]
            o_ref[:, hl] = (oh * _silu(z_ref[:, hl])).astype(BF16)


def _layer_spec(block, layer):
    zeros = (0,) * (len(block) - 1)
    return pl.BlockSpec((None,) + tuple(block), lambda i, c: (layer, i) + zeros)


def _carry_outputs(prev, out_start):
    if prev is None:
        return [], [], {}
    specs = [pl.BlockSpec(memory_space=pl.ANY) for _ in prev]
    return list(prev), specs, {k: out_start + k for k in range(len(prev))}


def _delta(proj, layer, n_layers, s0, c0, prev, conv_w, gate_params, norm_a, n_batch, t_len):
    rows = min(ROWS, t_len)
    n_seq = ROWS // rows
    nb = n_batch // n_seq
    nc = t_len // rows
    has_state = s0 is not None
    row_map = lambda i, c: (i * nc + c, 0)
    args, in_specs, aliases = _carry_outputs(prev, 1)
    n_alias = len(args)
    in_specs += [
        pl.BlockSpec((ROWS, QKV_A), row_map),
        pl.BlockSpec((ROWS, VA), lambda i, c: (i * nc + c, OFF_ZA // VA)),
        pl.BlockSpec((ROWS, TAIL_W), lambda i, c: (i * nc + c, OFF_TAIL // TAIL_W)),
    ]
    args += [proj, proj, proj]
    s_block = (n_seq, N_HEAD_A, DK_A, DV_A)
    c_block = (n_seq, CONV_W - 1, QKV_A)
    if has_state:
        in_specs += [_layer_spec(s_block, layer), _layer_spec(c_block, layer)]
        args += [s0, c0]
    in_specs += [
        pl.BlockSpec((CONV_W, QKV_A), lambda i, c: (0, 0)),
        pl.BlockSpec((2, TAIL_W), lambda i, c: (0, 0)),
        pl.BlockSpec((1, DV_A), lambda i, c: (0, 0)),
    ]
    args += [conv_w, gate_params, norm_a.reshape(1, DV_A)]
    return pl.pallas_call(
        functools.partial(_delta_kernel, n_seq=n_seq, rows=rows, has_state=has_state, n_alias=n_alias),
        out_shape=(
            jax.ShapeDtypeStruct((n_batch * t_len, VA), BF16),
            jax.ShapeDtypeStruct((n_layers, n_batch, N_HEAD_A, DK_A, DV_A), F32),
            jax.ShapeDtypeStruct((n_layers, n_batch, CONV_W - 1, QKV_A), F32),
        ),
        grid=(nb, nc),
        in_specs=in_specs,
        out_specs=(
            pl.BlockSpec((ROWS, VA), row_map),
            _layer_spec(s_block, layer),
            _layer_spec(c_block, layer),
        ),
        scratch_shapes=[
            pltpu.VMEM((n_seq, HIST + rows, QKV_A), F32),
            pltpu.VMEM((ROWS, QKV_A), F32),
        ],
        input_output_aliases=aliases,
        compiler_params=pltpu.CompilerParams(
            dimension_semantics=("parallel", "arbitrary"), vmem_limit_bytes=VMEM_LIMIT),
        name="delta",
    )(*args)


def _gla_kernel(*refs, n_state, rows, has_state, n_alias):
    refs = refs[n_alias:]
    q_ref, k_ref, v_ref, z_ref, tail_ref = refs[:5]
    p = 5
    if has_state:
        s0_ref = refs[p]
        p += 1
    wlr_ref, blr_ref, nb_ref = refs[p:p + 3]
    o_ref, s_ref = refs[p + 3:p + 5]
    n = ROWS
    n_sub = n // rows
    chained = n_state == 1
    c = pl.program_id(1)

    @pl.when(c == 0)
    def _init():
        if has_state:
            s_ref[...] = s0_ref[...]
        else:
            s_ref[...] = jnp.zeros(s_ref.shape, F32)

    ri, ci, same = _seq_masks(n, rows)
    incl = same & (ri >= ci)
    gk = jax.nn.log_sigmoid(_mm(tail_ref[...], wlr_ref[...]) + blr_ref[...]) / GLA_NORMALIZER
    b_loc = _mm(jnp.where(incl, 1.0, 0.0).astype(F32), gk)
    q = q_ref[...] * (DK_B ** -0.5)
    k = k_ref[...]
    q_loc = q * jnp.exp(b_loc)
    heads = range(N_HEAD_B)
    hls = [slice(h * DK_B, (h + 1) * DK_B) for h in heads]
    vs = [v_ref[:, h * DV_B:(h + 1) * DV_B] for h in heads]

    def finish(h, o):
        vl = slice(h * DV_B, (h + 1) * DV_B)
        o = o * lax.rsqrt(jnp.mean(o * o, axis=-1, keepdims=True) + EPS) * nb_ref[...]
        o_ref[:, vl] = (o * _silu(z_ref[:, vl])).astype(BF16)

    def key_column(row_vals):
        return jnp.broadcast_to(row_vals, (8, DK_B)).T[:, 0:1]

    if chained:
        lower = ri >= ci
        b_cum = _mm(jnp.where(lower, 1.0, 0.0).astype(F32), gk)
        b_last = b_cum[n - 1:n]
        q_cum = q * jnp.exp(b_cum)
        kdec = k * jnp.exp(b_last - b_cum)
        row = lax.broadcasted_iota(jnp.int32, (n, KB), 0)
        k_rel = []
        for s in range(n_sub):
            start = b_cum[s * rows - 1:s * rows] if s else jnp.zeros((1, KB), F32)
            k_rel.append(k * jnp.exp(jnp.where(row < (s + 1) * rows, start - b_cum, 0.0)))
        a_parts = _mm3_each([q_loc[s * rows:(s + 1) * rows, hl] for hl in hls for s in range(n_sub)],
                            [k_rel[s][:, hl] for hl in hls for s in range(n_sub)], NT)
        a = [jnp.where(lower, jnp.concatenate(a_parts[h * n_sub:(h + 1) * n_sub], axis=0), 0.0) for h in heads]
        sts = [s_ref[0, h] for h in heads]
        o_state = _mm1_each([q_cum[:, hl] for hl in hls], sts)
        intra = _mm1_each(a, vs)
        upd = _mm1_each([kdec[:, hl] for hl in hls], vs, TN)
        for h in heads:
            finish(h, o_state[h] + intra[h])
            s_ref[0, h] = sts[h] * jnp.exp(key_column(b_last[:, hls[h]])) + upd[h]
    else:
        b_tot = _mm(jnp.where(same, 1.0, 0.0).astype(F32), gk)
        kd = k * jnp.exp(-b_loc)
        kdec = k * jnp.exp(b_tot - b_loc)
        a = [jnp.where(incl, x, 0.0) for x in _mm3_each([q_loc[:, hl] for hl in hls], [kd[:, hl] for hl in hls], NT)]
        intra = _mm1_each(a, vs)
        probs = [(h, s, slice(s * rows, (s + 1) * rows)) for h in heads for s in range(n_sub)]
        sts = [s_ref[s, h] for (h, s, _) in probs]
        o_state = _mm1_each([q_loc[sl, hls[h]] for (h, _, sl) in probs], sts)
        upd = _mm1_each([kdec[sl, hls[h]] for (h, _, sl) in probs], [vs[h][sl] for (h, _, sl) in probs], TN)
        for (h, s, sl), st, du in zip(probs, sts, upd):
            s_ref[s, h] = st * jnp.exp(key_column(b_tot[s * rows:s * rows + 1, hls[h]])) + du
        for h in heads:
            finish(h, jnp.concatenate(o_state[h * n_sub:(h + 1) * n_sub], axis=0) + intra[h])


def _gla(proj, layer, n_layers, s0, prev, w_lr_pad, b_lr, norm_b, n_batch, t_len, chunk):
    rows = min(chunk, t_len)
    has_state = s0 is not None
    if t_len >= ROWS:
        n_state, nb, nc = 1, n_batch, t_len // ROWS
    else:
        n_state, nb, nc = ROWS // t_len, n_batch // (ROWS // t_len), 1
    args, in_specs, aliases = _carry_outputs(prev, 1)
    n_alias = len(args)
    in_specs += [
        pl.BlockSpec((ROWS, KB), lambda i, c: (i * nc + c, OFF_QB // KB)),
        pl.BlockSpec((ROWS, KB), lambda i, c: (i * nc + c, OFF_KB // KB)),
        pl.BlockSpec((ROWS, VB), lambda i, c: (i * nc + c, OFF_VB // VB)),
        pl.BlockSpec((ROWS, VB), lambda i, c: (i * nc + c, OFF_ZB // VB)),
        pl.BlockSpec((ROWS, TAIL_W), lambda i, c: (i * nc + c, OFF_TAIL // TAIL_W)),
    ]
    args += [proj] * 5
    s_block = (n_state, N_HEAD_B, DK_B, DV_B)
    if has_state:
        in_specs.append(_layer_spec(s_block, layer))
        args.append(s0)
    in_specs += [
        pl.BlockSpec((TAIL_W, KB), lambda i, c: (0, 0)),
        pl.BlockSpec((1, KB), lambda i, c: (0, 0)),
        pl.BlockSpec((1, DV_B), lambda i, c: (0, 0)),
    ]
    args += [w_lr_pad, b_lr.reshape(1, KB), norm_b.reshape(1, DV_B)]
    return pl.pallas_call(
        functools.partial(_gla_kernel, n_state=n_state, rows=rows, has_state=has_state, n_alias=n_alias),
        out_shape=(
            jax.ShapeDtypeStruct((n_batch * t_len, VB), BF16),
            jax.ShapeDtypeStruct((n_layers, n_batch, N_HEAD_B, DK_B, DV_B), F32),
        ),
        grid=(nb, nc),
        in_specs=in_specs,
        out_specs=(
            pl.BlockSpec((ROWS, VB), lambda i, c: (i * nc + c, 0)),
            _layer_spec(s_block, layer),
        ),
        input_output_aliases=aliases,
        compiler_params=pltpu.CompilerParams(
            dimension_semantics=("parallel", "arbitrary"), vmem_limit_bytes=VMEM_LIMIT),
        name="gla",
    )(*args)


def _lru_kernel(*refs, n_seq, rows, has_state, reset_first, n_alias):
    refs = refs[n_alias:]
    xb_ref, gate_ref = refs[:2]
    p = 2
    if has_state:
        h0_ref, c0_ref = refs[p], refs[p + 1]
        p += 2
    cw_ref, cb_ref, wg_ref, ba_ref, bx_ref, lam_ref = refs[p:p + 6]
    y_ref, h_ref, cn_ref = refs[p + 6:p + 9]
    xp_ref, xc_ref, a_ref, b_ref = refs[p + 9:p + 13]
    n = n_seq * rows
    c = pl.program_id(1)

    @pl.when(c == 0)
    def _init():
        if has_state:
            h_ref[...] = h0_ref[...]
            for s in range(n_seq):
                xp_ref[s, HIST - (CONV_W - 1):HIST, :] = c0_ref[s]
        else:
            h_ref[...] = jnp.zeros(h_ref.shape, F32)
            xp_ref[:, 0:HIST, :] = jnp.zeros((n_seq, HIST, W_LRU), F32)

    def emit(s, cs, acc):
        xc_ref[s * rows:(s + 1) * rows, cs] = acc

    _causal_conv_block(xb_ref, xp_ref, cw_ref, n_seq, rows, W_LRU, emit, bias_ref=cb_ref)

    @pl.when(c == pl.num_programs(1) - 1)
    def _conv_out():
        for s in range(n_seq):
            cn_ref[s] = xp_ref[s, HIST - (CONV_W - 1):HIST, :]

    row = lax.broadcasted_iota(jnp.int32, (n, LRU_BW), 0)
    sub = lax.broadcasted_iota(jnp.int32, (n // 8, 8, LRU_BW), 1)
    scan_steps = [(d, sub >= d) for d in (1, 2, 4)]
    first_row = (row == 0) & (c == 0)
    for blk in range(LRU_BLOCKS):
        bl = slice(blk * LRU_BW, (blk + 1) * LRU_BW)
        xc = xc_ref[:, bl]
        xh, xl = _split(xc)
        pre = jnp.dot(jnp.concatenate([xh, xh, xl], axis=1), wg_ref[blk], preferred_element_type=F32)
        r = _sigmoid(pre[:, :LRU_BW] + ba_ref[:, bl])
        gi = _sigmoid(pre[:, LRU_BW:] + bx_ref[:, bl])
        log_a = -LRU_C * r * jax.nn.softplus(-lam_ref[:, bl])
        a = jnp.exp(log_a)
        z = -jnp.tanh(log_a) * (a * a + 1.0)
        mult = z * lax.rsqrt(jnp.maximum(z, TINY))
        if reset_first:
            mult = jnp.where(first_row, 1.0, mult)
        b = mult * gi * xc
        a = a.reshape(n // 8, 8, LRU_BW)
        b = b.reshape(n // 8, 8, LRU_BW)
        for d, m in scan_steps:
            a_sh = pltpu.roll(a, d, axis=1)
            b_sh = pltpu.roll(b, d, axis=1)
            b = jnp.where(m, a * b_sh + b, b)
            a = jnp.where(m, a * a_sh, a)
        a_ref[:, bl] = a.reshape(n, LRU_BW)
        b_ref[:, bl] = b.reshape(n, LRU_BW)

    for s in range(n_seq):
        def body(g, hp, s=s):
            r0 = pl.multiple_of(s * rows + g * 8, 8)
            hs = a_ref[pl.ds(r0, 8), :] * hp + b_ref[pl.ds(r0, 8), :]
            b_ref[pl.ds(r0, 8), :] = hs
            return hs[7:8, :]

        h_ref[s] = lax.fori_loop(0, rows // 8, body, h_ref[s])

    lc = 512
    for c0 in range(0, W_LRU, lc):
        cs = slice(c0, c0 + lc)
        y_ref[:, cs] = (b_ref[:, cs] * _silu(gate_ref[:, cs])).astype(BF16)


def _lru(proj, layer, n_layers, h0, c0, prev, conv_w, conv_b, w_a, b_a, w_x, b_x, lam, n_batch, t_len,
         reset_first):
    has_state = h0 is not None
    if t_len >= 256:
        rows, n_seq = 256, 1
    else:
        rows, n_seq = t_len, 128 // t_len
    n = rows * n_seq
    nb = n_batch // n_seq
    nc = t_len // rows
    args, in_specs, aliases = _carry_outputs(prev, 1)
    n_alias = len(args)
    in_specs += [
        pl.BlockSpec((n, W_LRU), lambda i, c: (i * nc + c, 0)),
        pl.BlockSpec((n, W_LRU), lambda i, c: (i * nc + c, 1)),
    ]
    args += [proj, proj]
    h_block = (n_seq, 1, W_LRU)
    c_block = (n_seq, CONV_W - 1, W_LRU)
    if has_state:
        in_specs += [_layer_spec(h_block, layer), _layer_spec(c_block, layer)]
        args += [h0, c0]
    vec = pl.BlockSpec((1, W_LRU), lambda i, c: (0, 0))
    w_hi, w_lo = _split(jnp.concatenate([w_a, w_x], axis=-1))
    w_gates = jnp.concatenate([w_hi, w_lo, w_hi], axis=1)
    blk = pl.BlockSpec(w_gates.shape, lambda i, c: (0, 0, 0))
    in_specs += [pl.BlockSpec((CONV_W, W_LRU), lambda i, c: (0, 0)), vec, blk, vec, vec, vec]
    args += [conv_w, conv_b.reshape(1, W_LRU), w_gates, b_a.reshape(1, W_LRU),
             b_x.reshape(1, W_LRU), lam.reshape(1, W_LRU)]
    return pl.pallas_call(
        functools.partial(_lru_kernel, n_seq=n_seq, rows=rows, has_state=has_state,
                          reset_first=reset_first, n_alias=n_alias),
        out_shape=(
            jax.ShapeDtypeStruct((n_batch * t_len, W_LRU), BF16),
            jax.ShapeDtypeStruct((n_layers, n_batch, 1, W_LRU), F32),
            jax.ShapeDtypeStruct((n_layers, n_batch, CONV_W - 1, W_LRU), F32),
        ),
        grid=(nb, nc),
        in_specs=in_specs,
        out_specs=(
            pl.BlockSpec((n, W_LRU), lambda i, c: (i * nc + c, 0)),
            _layer_spec(h_block, layer),
            _layer_spec(c_block, layer),
        ),
        input_output_aliases=aliases,
        scratch_shapes=[
            pltpu.VMEM((n_seq, HIST + rows, W_LRU), F32),
            pltpu.VMEM((n, W_LRU), F32),
            pltpu.VMEM((n, W_LRU), F32),
            pltpu.VMEM((n, W_LRU), F32),
        ],
        compiler_params=pltpu.CompilerParams(
            dimension_semantics=("parallel", "arbitrary"), vmem_limit_bytes=VMEM_LIMIT),
        name="lru",
    )(*args)


def _ab_w_in_kernel(w_ref, o_ref):
    sizes = [QKV_A, N_HEAD_A, N_HEAD_A, VA, KB, KB, VB, LOWRANK, VB]
    offs = [0]
    for s in sizes:
        offs.append(offs[-1] + s)
    w = w_ref[...]
    qkv, b_raw, a_raw, z_a, q_b, k_b, v_b, lr_b, z_b = [w[:, offs[i]:offs[i + 1]] for i in range(9)]
    used = OFF_TAIL + 2 * N_HEAD_A + LOWRANK
    pad = jnp.zeros((w.shape[0], AB_N - used), w.dtype)
    o_ref[...] = jnp.concatenate([qkv, z_a, q_b, k_b, v_b, z_b, b_raw, a_raw, lr_b, pad], axis=1).astype(BF16)


def _ab_w_in_layout(w, layer):
    _, d, n = w.shape
    tr = 128
    return pl.pallas_call(
        _ab_w_in_kernel,
        out_shape=jax.ShapeDtypeStruct((d, AB_N), BF16),
        grid=(d // tr,),
        in_specs=[pl.BlockSpec((None, tr, n), lambda i: (layer, i, 0))],
        out_specs=pl.BlockSpec((tr, AB_N), lambda i: (i, 0)),
        compiler_params=pltpu.CompilerParams(dimension_semantics=("parallel",)),
        name="w_in_layout",
    )(w)


def _cast_kernel(w_ref, o_ref):
    o_ref[...] = w_ref[...].astype(BF16)


def _layer_bf16(w, layer):
    _, r, c = w.shape
    tr = 256
    return pl.pallas_call(
        _cast_kernel,
        out_shape=jax.ShapeDtypeStruct((r, c), BF16),
        grid=(r // tr,),
        in_specs=[pl.BlockSpec((None, tr, c), lambda i: (layer, i, 0))],
        out_specs=pl.BlockSpec((tr, c), lambda i: (i, 0)),
        compiler_params=pltpu.CompilerParams(dimension_semantics=("parallel",)),
        name="w_cast",
    )(w)


def _tail_row(vals, lane0):
    return jnp.zeros((TAIL_W,), F32).at[lane0:lane0 + vals.shape[0]].set(vals.astype(F32))


def kernel(x_prompt, x_sample, state_delta, state_delta_conv, state_gla, state_lru, state_lru_conv,
           ab_norm, ab_w_in, ab_conv_w, ab_a_log, ab_dt_bias, ab_norm_a, ab_gla_w_lr, ab_gla_b_lr,
           ab_norm_b, ab_w_out, lru_norm, lru_w_in, lru_conv_w, lru_conv_b, lru_w_a, lru_b_a,
           lru_w_x, lru_b_x, lru_lambda, lru_w_out, final_norm):
    n_ab, n_lru = ab_norm.shape[0], lru_norm.shape[0]
    depth = n_ab + n_lru
    lru_h0 = state_lru.reshape(n_lru, -1, 1, W_LRU)
    groups = []
    for x, carried in ((x_prompt, False), (x_sample, True)):
        groups.append(dict(x=x.reshape(-1, D_MODEL), nb=x.shape[0], t=x.shape[1], carried=carried,
                           delta=None, gla=None, lru=None))

    for layer in range(depth):
        j = layer // 2
        last = layer == depth - 1
        if layer % 2 == 0:
            w_in = _ab_w_in_layout(ab_w_in, j)
            w_out = _layer_bf16(ab_w_out, j)
            gate_params = jnp.stack([_tail_row(ab_a_log[j], LANE_A), _tail_row(ab_dt_bias[j], LANE_A)])
            w_lr_pad = jnp.zeros((TAIL_W, KB), F32).at[LANE_LR:LANE_LR + LOWRANK].set(ab_gla_w_lr[j])
            for g in groups:
                proj = _norm_mm(g["x"], ab_norm[j], w_in, AB_TN)
                s_a, c_a, s_b = (state_delta, state_delta_conv, state_gla) if g["carried"] else (None,) * 3
                o_a, *g["delta"] = _delta(proj, j, n_ab, s_a, c_a, g["delta"], ab_conv_w[j], gate_params,
                                          ab_norm_a[j], g["nb"], g["t"])
                o_b, *g["gla"] = _gla(proj, j, n_ab, s_b, g["gla"], w_lr_pad, ab_gla_b_lr[j], ab_norm_b[j],
                                      g["nb"], g["t"], 16)
                g["x"] = _out_mm([o_a, o_b], w_out, g["x"], final_norm if last else None)
        else:
            w_in = _layer_bf16(lru_w_in, j)
            w_out = _layer_bf16(lru_w_out, j)
            for g in groups:
                proj = _norm_mm(g["x"], lru_norm[j], w_in, 1024)
                h0, c0 = (lru_h0, state_lru_conv) if g["carried"] else (None, None)
                y, *g["lru"] = _lru(proj, j, n_lru, h0, c0, g["lru"], lru_conv_w[j], lru_conv_b[j],
                                    lru_w_a[j], lru_b_a[j], lru_w_x[j], lru_b_x[j], lru_lambda[j],
                                    g["nb"], g["t"], reset_first=not g["carried"])
                g["x"] = _out_mm([y], w_out, g["x"], final_norm if last else None)

    outs = []
    for g, x in zip(groups, (x_prompt, x_sample)):
        h_all, lconv = g["lru"]
        outs.append([g["x"].reshape(x.shape), g["delta"][0], g["delta"][1], g["gla"][0],
                     h_all.reshape(n_lru, g["nb"], W_LRU), lconv])
    p, s = outs
    return (p[0], s[0], p[1], p[2], p[3], p[4], p[5], s[1], s[2], s[3], s[4], s[5])
```

```python
import functools

import jax
import jax.numpy as jnp
from jax import lax
from jax.experimental import pallas as pl
from jax.experimental.pallas import tpu as pltpu

F32 = jnp.float32
BF16 = jnp.bfloat16
HIGHEST = lax.Precision.HIGHEST

D_MODEL = 2048
N_HEAD_A, DK_A, DV_A = 8, 128, 128
N_HEAD_B, DK_B, DV_B = 4, 128, 256
KA = N_HEAD_A * DK_A
VA = N_HEAD_A * DV_A
KB = N_HEAD_B * DK_B
VB = N_HEAD_B * DV_B
QKV_A = 2 * KA + VA
LOWRANK = 16
GLA_NORMALIZER = 16.0
W_LRU = D_MODEL
LRU_BLOCKS = 16
LRU_BW = W_LRU // LRU_BLOCKS
LRU_C = 8.0
CONV_W = 4
EPS = 1e-6
LANES = 128
TINY = 1.1754944e-38

OFF_QKV = 0
OFF_ZA = OFF_QKV + QKV_A
OFF_QB = OFF_ZA + VA
OFF_KB = OFF_QB + KB
OFF_VB = OFF_KB + KB
OFF_ZB = OFF_VB + VB
OFF_TAIL = OFF_ZB + VB
TAIL_W = 128
AB_TN = 1536
AB_N = 7680
LANE_BETA, LANE_A, LANE_LR = 0, N_HEAD_A, 2 * N_HEAD_A

ROWS = 64
DELTA_M = 128
DELTA_SEQS = 2
HIST = 8
VMEM_LIMIT = 56 * 1024 * 1024

NN = (((1,), (0,)), ((), ()))
NT = (((1,), (1,)), ((), ()))
TN = (((0,), (0,)), ((), ()))


def _mm(a, b, dims=NN):
    return lax.dot_general(a, b, dims, precision=HIGHEST, preferred_element_type=F32)


def _split(a):
    hi = a.astype(BF16)
    return hi, (a - hi.astype(F32)).astype(BF16)


def _mm3(a, b, dims=NN):
    ah, al = a if isinstance(a, tuple) else _split(a)
    bh, bl = b if isinstance(b, tuple) else _split(b)
    (ca,), (cb,) = dims[0]
    dot = functools.partial(lax.dot_general, dimension_numbers=dims, preferred_element_type=F32)
    if ah.shape[ca] % LANES:
        return dot(ah, bh) + (dot(ah, bl) + dot(al, bh))
    return dot(jnp.concatenate([ah, ah, al], axis=ca), jnp.concatenate([bh, bl, bh], axis=cb))


def _split_each(xs):
    return [x if isinstance(x, tuple) else _split(x) for x in xs]


def _mm3_each(as_, bs, dims=NN):
    sa, sb = _split_each(as_), _split_each(bs)
    return [_mm3(a, b, dims) for a, b in zip(sa, sb)]


def _mm2_each(as_, bs, rounded, dims=NN):
    out = []
    for a, b in zip(as_, bs):
        if rounded == "a":
            ah = a.astype(BF16)
            bh, bl = b if isinstance(b, tuple) else _split(b)
            pa, pb = [ah, ah], [bh, bl]
        else:
            ah, al = a if isinstance(a, tuple) else _split(a)
            bh = b.astype(BF16)
            pa, pb = [ah, al], [bh, bh]
        (ca,), (cb,) = dims[0]
        assert pa[0].shape[ca] % LANES == 0
        out.append(lax.dot_general(jnp.concatenate(pa, axis=ca), jnp.concatenate(pb, axis=cb), dims,
                                   preferred_element_type=F32))
    return out


def _mm1_each(as_, bs, dims=NN):
    ca = [a.astype(BF16) for a in as_]
    cb = [b.astype(BF16) for b in bs]
    return [lax.dot_general(a, b, dims, preferred_element_type=F32) for a, b in zip(ca, cb)]


def _sigmoid(x):
    return 0.5 * jnp.tanh(0.5 * x) + 0.5


def _silu(x):
    h = 0.5 * x
    return h + h * jnp.tanh(h)


def _norm_mm_kernel(x_ref, g_ref, w_ref, o_ref, h_ref):
    @pl.when(pl.program_id(1) == 0)
    def _():
        x = x_ref[...]
        ms = jnp.mean(x * x, axis=-1, keepdims=True)
        h_ref[...] = (x * lax.rsqrt(ms + EPS) * g_ref[...]).astype(BF16)

    o_ref[...] = jnp.dot(h_ref[...], w_ref[...], preferred_element_type=F32)


def _norm_mm(x, g, w, tn):
    m, k = x.shape
    n = w.shape[1]
    tm = min(m, 1024)
    return pl.pallas_call(
        _norm_mm_kernel,
        out_shape=jax.ShapeDtypeStruct((m, n), F32),
        grid=(m // tm, n // tn),
        in_specs=[
            pl.BlockSpec((tm, k), lambda i, j: (i, 0)),
            pl.BlockSpec((1, k), lambda i, j: (0, 0)),
            pl.BlockSpec((k, tn), lambda i, j: (0, j)),
        ],
        out_specs=pl.BlockSpec((tm, tn), lambda i, j: (i, j)),
        scratch_shapes=[pltpu.VMEM((tm, k), BF16)],
        compiler_params=pltpu.CompilerParams(
            dimension_semantics=("parallel", "arbitrary"), vmem_limit_bytes=VMEM_LIMIT),
        name="norm_mm",
    )(x, g.reshape(1, k), w)


def _out_mm_kernel(*refs, n_in, final):
    o_refs = refs[:n_in]
    w_ref, x_ref = refs[n_in], refs[n_in + 1]
    out_ref = refs[-1]
    acc = x_ref[...]
    k0 = 0
    for o_ref in o_refs:
        kk = o_ref.shape[1]
        acc = acc + jnp.dot(o_ref[...], w_ref[k0:k0 + kk, :], preferred_element_type=F32)
        k0 += kk
    if final:
        fg_ref = refs[n_in + 2]
        ms = jnp.mean(acc * acc, axis=-1, keepdims=True)
        acc = acc * lax.rsqrt(ms + EPS) * fg_ref[...]
    out_ref[...] = acc


def _out_mm(os_, w, x, final_g=None):
    m, d = x.shape
    tm = 512
    final = final_g is not None
    in_specs = [pl.BlockSpec((tm, o.shape[1]), lambda i: (i, 0)) for o in os_]
    in_specs += [pl.BlockSpec(w.shape, lambda i: (0, 0)), pl.BlockSpec((tm, d), lambda i: (i, 0))]
    args = list(os_) + [w, x]
    if final:
        in_specs.append(pl.BlockSpec((1, d), lambda i: (0, 0)))
        args.append(final_g.reshape(1, d))
    return pl.pallas_call(
        functools.partial(_out_mm_kernel, n_in=len(os_), final=final),
        out_shape=jax.ShapeDtypeStruct((m, d), F32),
        grid=(m // tm,),
        in_specs=in_specs,
        out_specs=pl.BlockSpec((tm, d), lambda i: (i, 0)),
        compiler_params=pltpu.CompilerParams(
            dimension_semantics=("parallel",), vmem_limit_bytes=VMEM_LIMIT),
        name="out_mm",
    )(*args)


def _causal_conv_block(seq_rows, xp_ref, cw_ref, n_seq, rows, width, emit, bias_ref=None):
    lc = 512
    for s in range(n_seq):
        xp_ref[s, HIST:HIST + rows, :] = seq_rows(s)
        for c0 in range(0, width, lc):
            cs = slice(c0, c0 + lc)
            base = HIST - (CONV_W - 1)
            acc = xp_ref[s, base:base + rows, cs] * cw_ref[0:1, cs]
            for j in range(1, CONV_W):
                acc = acc + xp_ref[s, base + j:base + j + rows, cs] * cw_ref[j:j + 1, cs]
            if bias_ref is not None:
                acc = acc + bias_ref[:, cs]
            emit(s, cs, acc)
        xp_ref[s, 0:HIST, :] = xp_ref[s, rows:rows + HIST, :]


def _seq_masks(n, rows):
    ri = lax.broadcasted_iota(jnp.int32, (n, n), 0)
    ci = lax.broadcasted_iota(jnp.int32, (n, n), 1)
    if rows == n:
        same = ri >= 0
    else:
        sh = rows.bit_length() - 1
        same = (ri >> sh) == (ci >> sh)
    return ri, ci, same


NEUMANN_BLOCK = 8


def _inverse_masks(ri, ci, rows):
    assert rows >= NEUMANN_BLOCK
    blk = lambda k: (ri >> (k.bit_length() - 1)) == (ci >> (k.bit_length() - 1))
    eye = jnp.where(ri == ci, 1.0, 0.0).astype(F32)
    levels = []
    k = NEUMANN_BLOCK
    while k < rows:
        levels.append(blk(2 * k) & jnp.logical_not(blk(k)))
        k *= 2
    return eye, blk(NEUMANN_BLOCK), levels


def _unit_lower_inverse_each(as_, masks):
    eye, base, levels = masks
    a8 = [jnp.where(base, a, 0.0) for a in as_]
    a8s = _split_each(a8)
    p2 = _mm2_each(a8s, a8, "b")
    p2s = _split_each(p2)
    p4 = _mm2_each(p2s, p2, "b")
    a8p2 = _mm2_each(a8s, p2, "b")
    x = [(eye - a) + (p - ap) for a, p, ap in zip(a8, p2, a8p2)]
    t = [xi + xp for xi, xp in zip(x, _mm2_each(x, p4, "a"))]
    for lvl in levels:
        off = [jnp.where(lvl, a, 0.0) for a in as_]
        t = [ti - d for ti, d in zip(t, _mm2_each(_mm2_each(t, off, "a"), t, "b"))]
    return t


def _delta_kernel(*refs, n_seq, rows, has_state, n_alias):
    refs = refs[n_alias:]
    qkv_ref, z_ref, tail_ref = refs[:3]
    p = 3
    if has_state:
        s0_ref, c0_ref = refs[p], refs[p + 1]
        p += 2
    cw_ref, gp_ref, na_ref = refs[p:p + 3]
    o_ref, s_ref, cn_ref = refs[p + 3:p + 6]
    xp_ref, act_ref = refs[p + 6:p + 8]
    n = n_seq * rows
    c = pl.program_id(1)

    @pl.when(c == 0)
    def _init():
        if has_state:
            s_ref[...] = s0_ref[...]
            for s in range(n_seq):
                xp_ref[s, HIST - (CONV_W - 1):HIST, :] = c0_ref[s]
        else:
            s_ref[...] = jnp.zeros(s_ref.shape, F32)
            xp_ref[:, 0:HIST, :] = jnp.zeros((n_seq, HIST, QKV_A), F32)

    def emit(s, cs, acc):
        act_ref[s * rows:(s + 1) * rows, cs] = _silu(acc)

    def seq_rows(s):
        slab, off = divmod(s * rows, ROWS)
        return qkv_ref[slab, off:off + rows, :]

    _causal_conv_block(seq_rows, xp_ref, cw_ref, n_seq, rows, QKV_A, emit)

    @pl.when(c == pl.num_programs(1) - 1)
    def _conv_out():
        for s in range(n_seq):
            cn_ref[s] = xp_ref[s, HIST - (CONV_W - 1):HIST, :]

    n_slab = n // ROWS
    head_group = DELTA_M // n
    _, _, same_n = _seq_masks(n, rows)
    rn = lax.broadcasted_iota(jnp.int32, (n, n), 0)
    cn = lax.broadcasted_iota(jnp.int32, (n, n), 1)
    tail = tail_ref[...].reshape(n, TAIL_W)
    btile = _sigmoid(tail)
    gtile = -jnp.exp(gp_ref[0:1, :]) * jax.nn.softplus(tail + gp_ref[1:2, :])
    g_cum = _mm(jnp.where(same_n & (rn >= cn), 1.0, 0.0).astype(F32), gtile)
    g_tot = _mm(jnp.where(same_n, 1.0, 0.0).astype(F32), gtile)
    e_cum = jnp.exp(g_cum)
    e_rest = jnp.exp(g_tot - g_cum)
    g_cum_t = g_cum.T

    m = DELTA_M
    ri, ci, same = _seq_masks(m, rows)
    incl = same & (ri >= ci)
    strict = same & (ri > ci)
    inv_masks = _inverse_masks(ri, ci, rows)
    stack = lambda xs: xs[0] if len(xs) == 1 else jnp.concatenate(xs, axis=0)
    groups = [range(g * head_group, (g + 1) * head_group) for g in range(N_HEAD_A // head_group)]
    qn, kn, v, beta, gc, eg, er, g_row = ([] for _ in range(8))
    for heads in groups:
        q_h, k_h, v_h = [], [], []
        for h in heads:
            q = act_ref[:, h * DK_A:(h + 1) * DK_A]
            k = act_ref[:, KA + h * DK_A:KA + (h + 1) * DK_A]
            q_h.append(q * lax.rsqrt(jnp.sum(q * q, axis=-1, keepdims=True) + EPS) * (DK_A ** -0.5))
            k_h.append(k * lax.rsqrt(jnp.sum(k * k, axis=-1, keepdims=True) + EPS))
            v_h.append(act_ref[:, 2 * KA + h * DV_A:2 * KA + (h + 1) * DV_A])
        qn.append(stack(q_h))
        kn.append(stack(k_h))
        v.append(stack(v_h))
        beta.append(stack([btile[:, LANE_BETA + h:LANE_BETA + h + 1] for h in heads]))
        gc.append(stack([g_cum[:, LANE_A + h:LANE_A + h + 1] for h in heads]))
        eg.append(stack([e_cum[:, LANE_A + h:LANE_A + h + 1] for h in heads]))
        er.append(stack([e_rest[:, LANE_A + h:LANE_A + h + 1] for h in heads]))
        g_row.append(jnp.concatenate([g_cum_t[LANE_A + h:LANE_A + h + 1, :] for h in heads], axis=1))
    decay = [jnp.where(incl, jnp.exp(jnp.where(incl, c_ - r_, 0.0)), 0.0) for c_, r_ in zip(gc, g_row)]
    kb = [k_ * b_ for k_, b_ in zip(kn, beta)]
    prod = _mm3_each([stack([kb_, q_]) for kb_, q_ in zip(kb, qn)], kn, NT)
    a = [jnp.where(strict, p_[:m] * d_, 0.0) for p_, d_ in zip(prod, decay)]
    qk = [p_[m:] * d_ for p_, d_ in zip(prod, decay)]
    t_inv = _unit_lower_inverse_each(a, inv_masks)
    rhs = [jnp.concatenate([v_ * b_, kb_ * e_], axis=1) for v_, b_, kb_, e_ in zip(v, beta, kb, eg)]
    sol = _mm2_each(t_inv, rhs, "a")
    qg = [q_ * e_ for q_, e_ in zip(qn, eg)]
    kdec = [k_ * e_ for k_, e_ in zip(kn, er)]
    probs = [(g, i, h, s, slice(i * n + s * rows, i * n + (s + 1) * rows))
             for g, heads in enumerate(groups) for i, h in enumerate(heads) for s in range(n_seq)]
    states = [s_ref[s, h] for (_, _, h, s, _) in probs]
    r = _mm1_each([stack([sol[g][sl, DV_A:], qg[g][sl]]) for (g, _, _, _, sl) in probs], states)
    v_new = [sol[g][sl, :DV_A] - r_[:rows] for (g, _, _, _, sl), r_ in zip(probs, r)]
    per_group = lambda xs: [stack([x for (g2, *_), x in zip(probs, xs) if g2 == g]) for g in range(len(groups))]
    v_new_g = per_group(v_new)
    o_inter_g = per_group([r_[rows:] for r_ in r])
    o = [oi + d for oi, d in zip(o_inter_g, _mm1_each(qk, v_new_g))]
    upd = _mm1_each([kdec[g][sl] for (g, _, _, _, sl) in probs], v_new, TN)
    for (g, i, h, s, sl), st, du in zip(probs, states, upd):
        g_last = g_tot[s * rows:s * rows + 1, LANE_A + h:LANE_A + h + 1]
        s_ref[s, h] = st * jnp.exp(g_last) + du
    for g, heads in enumerate(groups):
        for i, h in enumerate(heads):
            hl = slice(h * DV_A, (h + 1) * DV_A)
            oh = o[g][i * n:(i + 1) * n]
            oh = oh * lax.rsqrt(jnp.mean(oh * oh, axis=-1, keepdims=True) + EPS) * na_ref[...]
            oh = oh * _silu(z_ref[:, :, hl].reshape(n, DV_A))
            o_ref[:, :, hl] = oh.reshape(n_slab, ROWS, DV_A).astype(BF16)


def _layer_spec(block, layer):
    zeros = (0,) * (len(block) - 1)
    return pl.BlockSpec((None,) + tuple(block), lambda i, c: (layer, i) + zeros)


def _carry_outputs(prev, out_start):
    if prev is None:
        return [], [], {}
    specs = [pl.BlockSpec(memory_space=pl.ANY) for _ in prev]
    return list(prev), specs, {k: out_start + k for k in range(len(prev))}


def _delta(proj, layer, n_layers, s0, c0, prev, conv_w, gate_params, norm_a, n_batch, t_len):
    rows = min(ROWS, t_len)
    if t_len >= ROWS:
        n_seq, n_slab = DELTA_SEQS, DELTA_SEQS
        slabs = proj.reshape(n_batch, t_len // ROWS, ROWS, AB_N)
    else:
        n_seq, n_slab = ROWS // t_len, 1
        slabs = proj.reshape(n_batch * t_len // ROWS, 1, ROWS, AB_N)
    n = n_seq * rows
    nb = n_batch // n_seq
    nc = t_len // rows
    has_state = s0 is not None
    slab_spec = lambda width, col: pl.BlockSpec((n_slab, None, ROWS, width), lambda i, c: (i, c, 0, col))
    args, in_specs, aliases = _carry_outputs(prev, 1)
    n_alias = len(args)
    in_specs += [slab_spec(QKV_A, 0), slab_spec(VA, OFF_ZA // VA), slab_spec(TAIL_W, OFF_TAIL // TAIL_W)]
    args += [slabs, slabs, slabs]
    s_block = (n_seq, N_HEAD_A, DK_A, DV_A)
    c_block = (n_seq, CONV_W - 1, QKV_A)
    if has_state:
        in_specs += [_layer_spec(s_block, layer), _layer_spec(c_block, layer)]
        args += [s0, c0]
    in_specs += [
        pl.BlockSpec((CONV_W, QKV_A), lambda i, c: (0, 0)),
        pl.BlockSpec((2, TAIL_W), lambda i, c: (0, 0)),
        pl.BlockSpec((1, DV_A), lambda i, c: (0, 0)),
    ]
    args += [conv_w, gate_params, norm_a.reshape(1, DV_A)]
    o, s_all, conv_all = pl.pallas_call(
        functools.partial(_delta_kernel, n_seq=n_seq, rows=rows, has_state=has_state, n_alias=n_alias),
        out_shape=(
            jax.ShapeDtypeStruct(slabs.shape[:3] + (VA,), BF16),
            jax.ShapeDtypeStruct((n_layers, n_batch, N_HEAD_A, DK_A, DV_A), F32),
            jax.ShapeDtypeStruct((n_layers, n_batch, CONV_W - 1, QKV_A), F32),
        ),
        grid=(nb, nc),
        in_specs=in_specs,
        out_specs=(
            slab_spec(VA, 0),
            _layer_spec(s_block, layer),
            _layer_spec(c_block, layer),
        ),
        scratch_shapes=[
            pltpu.VMEM((n_seq, HIST + rows, QKV_A), F32),
            pltpu.VMEM((n, QKV_A), F32),
        ],
        input_output_aliases=aliases,
        compiler_params=pltpu.CompilerParams(
            dimension_semantics=("parallel", "arbitrary"), vmem_limit_bytes=VMEM_LIMIT),
        name="delta",
    )(*args)
    return o.reshape(n_batch * t_len, VA), s_all, conv_all


def _gla_kernel(*refs, n_state, rows, has_state, n_alias):
    refs = refs[n_alias:]
    q_ref, k_ref, v_ref, z_ref, tail_ref = refs[:5]
    p = 5
    if has_state:
        s0_ref = refs[p]
        p += 1
    wlr_ref, blr_ref, nb_ref = refs[p:p + 3]
    o_ref, s_ref = refs[p + 3:p + 5]
    n = ROWS
    n_sub = n // rows
    chained = n_state == 1
    c = pl.program_id(1)

    @pl.when(c == 0)
    def _init():
        if has_state:
            s_ref[...] = s0_ref[...]
        else:
            s_ref[...] = jnp.zeros(s_ref.shape, F32)

    ri, ci, same = _seq_masks(n, rows)
    incl = same & (ri >= ci)
    gk = jax.nn.log_sigmoid(_mm(tail_ref[...], wlr_ref[...]) + blr_ref[...]) / GLA_NORMALIZER
    b_loc = _mm(jnp.where(incl, 1.0, 0.0).astype(F32), gk)
    q = q_ref[...] * (DK_B ** -0.5)
    k = k_ref[...]
    q_loc = q * jnp.exp(b_loc)
    heads = range(N_HEAD_B)
    hls = [slice(h * DK_B, (h + 1) * DK_B) for h in heads]
    vs = [v_ref[:, h * DV_B:(h + 1) * DV_B] for h in heads]

    def finish(h, o):
        vl = slice(h * DV_B, (h + 1) * DV_B)
        o = o * lax.rsqrt(jnp.mean(o * o, axis=-1, keepdims=True) + EPS) * nb_ref[...]
        o_ref[:, vl] = (o * _silu(z_ref[:, vl])).astype(BF16)

    def key_column(row_vals):
        return jnp.broadcast_to(row_vals, (8, DK_B)).T[:, 0:1]

    if chained:
        lower = ri >= ci
        b_cum = _mm(jnp.where(lower, 1.0, 0.0).astype(F32), gk)
        b_last = b_cum[n - 1:n]
        q_cum = q * jnp.exp(b_cum)
        kdec = k * jnp.exp(b_last - b_cum)
        row = lax.broadcasted_iota(jnp.int32, (n, KB), 0)
        k_rel = []
        for s in range(n_sub):
            start = b_cum[s * rows - 1:s * rows] if s else jnp.zeros((1, KB), F32)
            k_rel.append(k * jnp.exp(jnp.where(row < (s + 1) * rows, start - b_cum, 0.0)))
        a_parts = _mm3_each([q_loc[s * rows:(s + 1) * rows, hl] for hl in hls for s in range(n_sub)],
                            [k_rel[s][:, hl] for hl in hls for s in range(n_sub)], NT)
        a = [jnp.where(lower, jnp.concatenate(a_parts[h * n_sub:(h + 1) * n_sub], axis=0), 0.0) for h in heads]
        sts = [s_ref[0, h] for h in heads]
        o_state = _mm1_each([q_cum[:, hl] for hl in hls], sts)
        intra = _mm1_each(a, vs)
        upd = _mm1_each([kdec[:, hl] for hl in hls], vs, TN)
        for h in heads:
            finish(h, o_state[h] + intra[h])
            s_ref[0, h] = sts[h] * jnp.exp(key_column(b_last[:, hls[h]])) + upd[h]
    else:
        b_tot = _mm(jnp.where(same, 1.0, 0.0).astype(F32), gk)
        kd = k * jnp.exp(-b_loc)
        kdec = k * jnp.exp(b_tot - b_loc)
        a = [jnp.where(incl, x, 0.0) for x in _mm3_each([q_loc[:, hl] for hl in hls], [kd[:, hl] for hl in hls], NT)]
        intra = _mm1_each(a, vs)
        probs = [(h, s, slice(s * rows, (s + 1) * rows)) for h in heads for s in range(n_sub)]
        sts = [s_ref[s, h] for (h, s, _) in probs]
        o_state = _mm1_each([q_loc[sl, hls[h]] for (h, _, sl) in probs], sts)
        upd = _mm1_each([kdec[sl, hls[h]] for (h, _, sl) in probs], [vs[h][sl] for (h, _, sl) in probs], TN)
        for (h, s, sl), st, du in zip(probs, sts, upd):
            s_ref[s, h] = st * jnp.exp(key_column(b_tot[s * rows:s * rows + 1, hls[h]])) + du
        for h in heads:
            finish(h, jnp.concatenate(o_state[h * n_sub:(h + 1) * n_sub], axis=0) + intra[h])


def _gla(proj, layer, n_layers, s0, prev, w_lr_pad, b_lr, norm_b, n_batch, t_len, chunk):
    rows = min(chunk, t_len)
    has_state = s0 is not None
    if t_len >= ROWS:
        n_state, nb, nc = 1, n_batch, t_len // ROWS
    else:
        n_state, nb, nc = ROWS // t_len, n_batch // (ROWS // t_len), 1
    args, in_specs, aliases = _carry_outputs(prev, 1)
    n_alias = len(args)
    in_specs += [
        pl.BlockSpec((ROWS, KB), lambda i, c: (i * nc + c, OFF_QB // KB)),
        pl.BlockSpec((ROWS, KB), lambda i, c: (i * nc + c, OFF_KB // KB)),
        pl.BlockSpec((ROWS, VB), lambda i, c: (i * nc + c, OFF_VB // VB)),
        pl.BlockSpec((ROWS, VB), lambda i, c: (i * nc + c, OFF_ZB // VB)),
        pl.BlockSpec((ROWS, TAIL_W), lambda i, c: (i * nc + c, OFF_TAIL // TAIL_W)),
    ]
    args += [proj] * 5
    s_block = (n_state, N_HEAD_B, DK_B, DV_B)
    if has_state:
        in_specs.append(_layer_spec(s_block, layer))
        args.append(s0)
    in_specs += [
        pl.BlockSpec((TAIL_W, KB), lambda i, c: (0, 0)),
        pl.BlockSpec((1, KB), lambda i, c: (0, 0)),
        pl.BlockSpec((1, DV_B), lambda i, c: (0, 0)),
    ]
    args += [w_lr_pad, b_lr.reshape(1, KB), norm_b.reshape(1, DV_B)]
    return pl.pallas_call(
        functools.partial(_gla_kernel, n_state=n_state, rows=rows, has_state=has_state, n_alias=n_alias),
        out_shape=(
            jax.ShapeDtypeStruct((n_batch * t_len, VB), BF16),
            jax.ShapeDtypeStruct((n_layers, n_batch, N_HEAD_B, DK_B, DV_B), F32),
        ),
        grid=(nb, nc),
        in_specs=in_specs,
        out_specs=(
            pl.BlockSpec((ROWS, VB), lambda i, c: (i * nc + c, 0)),
            _layer_spec(s_block, layer),
        ),
        input_output_aliases=aliases,
        compiler_params=pltpu.CompilerParams(
            dimension_semantics=("parallel", "arbitrary"), vmem_limit_bytes=VMEM_LIMIT),
        name="gla",
    )(*args)


def _lru_kernel(*refs, n_seq, rows, has_state, reset_first, n_alias):
    refs = refs[n_alias:]
    xb_ref, gate_ref = refs[:2]
    p = 2
    if has_state:
        h0_ref, c0_ref = refs[p], refs[p + 1]
        p += 2
    cw_ref, cb_ref, wg_ref, ba_ref, bx_ref, lam_ref = refs[p:p + 6]
    y_ref, h_ref, cn_ref = refs[p + 6:p + 9]
    xp_ref, xc_ref, a_ref, b_ref = refs[p + 9:p + 13]
    n = n_seq * rows
    c = pl.program_id(1)

    @pl.when(c == 0)
    def _init():
        if has_state:
            h_ref[...] = h0_ref[...]
            for s in range(n_seq):
                xp_ref[s, HIST - (CONV_W - 1):HIST, :] = c0_ref[s]
        else:
            h_ref[...] = jnp.zeros(h_ref.shape, F32)
            xp_ref[:, 0:HIST, :] = jnp.zeros((n_seq, HIST, W_LRU), F32)

    def emit(s, cs, acc):
        xc_ref[s * rows:(s + 1) * rows, cs] = acc

    _causal_conv_block(lambda s: xb_ref[s * rows:(s + 1) * rows, :], xp_ref, cw_ref, n_seq, rows, W_LRU,
                       emit, bias_ref=cb_ref)

    @pl.when(c == pl.num_programs(1) - 1)
    def _conv_out():
        for s in range(n_seq):
            cn_ref[s] = xp_ref[s, HIST - (CONV_W - 1):HIST, :]

    row = lax.broadcasted_iota(jnp.int32, (n, LRU_BW), 0)
    sub = lax.broadcasted_iota(jnp.int32, (n // 8, 8, LRU_BW), 1)
    scan_steps = [(d, sub >= d) for d in (1, 2, 4)]
    first_row = (row == 0) & (c == 0)
    for blk in range(LRU_BLOCKS):
        bl = slice(blk * LRU_BW, (blk + 1) * LRU_BW)
        xc = xc_ref[:, bl]
        pre = jnp.dot(xc.astype(BF16), wg_ref[blk], preferred_element_type=F32)
        r = _sigmoid(pre[:, :LRU_BW] + ba_ref[:, bl])
        gi = _sigmoid(pre[:, LRU_BW:] + bx_ref[:, bl])
        log_a = -LRU_C * r * jax.nn.softplus(-lam_ref[:, bl])
        a = jnp.exp(log_a)
        z = -jnp.tanh(log_a) * (a * a + 1.0)
        mult = z * lax.rsqrt(jnp.maximum(z, TINY))
        if reset_first:
            mult = jnp.where(first_row, 1.0, mult)
        b = mult * gi * xc
        a = a.reshape(n // 8, 8, LRU_BW)
        b = b.reshape(n // 8, 8, LRU_BW)
        for d, m in scan_steps:
            a_sh = pltpu.roll(a, d, axis=1)
            b_sh = pltpu.roll(b, d, axis=1)
            b = jnp.where(m, a * b_sh + b, b)
            a = jnp.where(m, a * a_sh, a)
        a_ref[:, bl] = a.reshape(n, LRU_BW)
        b_ref[:, bl] = b.reshape(n, LRU_BW)

    for s in range(n_seq):
        def body(g, hp, s=s):
            r0 = pl.multiple_of(s * rows + g * 8, 8)
            hs = a_ref[pl.ds(r0, 8), :] * hp + b_ref[pl.ds(r0, 8), :]
            b_ref[pl.ds(r0, 8), :] = hs
            return hs[7:8, :]

        h_ref[s] = lax.fori_loop(0, rows // 8, body, h_ref[s])

    lc = 512
    for c0 in range(0, W_LRU, lc):
        cs = slice(c0, c0 + lc)
        y_ref[:, cs] = (b_ref[:, cs] * _silu(gate_ref[:, cs])).astype(BF16)


def _lru(proj, layer, n_layers, h0, c0, prev, conv_w, conv_b, w_a, b_a, w_x, b_x, lam, n_batch, t_len,
         reset_first):
    has_state = h0 is not None
    if t_len >= 256:
        rows, n_seq = 256, 1
    else:
        rows, n_seq = t_len, 128 // t_len
    n = rows * n_seq
    nb = n_batch // n_seq
    nc = t_len // rows
    args, in_specs, aliases = _carry_outputs(prev, 1)
    n_alias = len(args)
    in_specs += [
        pl.BlockSpec((n, W_LRU), lambda i, c: (i * nc + c, 0)),
        pl.BlockSpec((n, W_LRU), lambda i, c: (i * nc + c, 1)),
    ]
    args += [proj, proj]
    h_block = (n_seq, 1, W_LRU)
    c_block = (n_seq, CONV_W - 1, W_LRU)
    if has_state:
        in_specs += [_layer_spec(h_block, layer), _layer_spec(c_block, layer)]
        args += [h0, c0]
    vec = pl.BlockSpec((1, W_LRU), lambda i, c: (0, 0))
    w_gates = jnp.concatenate([w_a, w_x], axis=-1).astype(BF16)
    blk = pl.BlockSpec(w_gates.shape, lambda i, c: (0, 0, 0))
    in_specs += [pl.BlockSpec((CONV_W, W_LRU), lambda i, c: (0, 0)), vec, blk, vec, vec, vec]
    args += [conv_w, conv_b.reshape(1, W_LRU), w_gates, b_a.reshape(1, W_LRU),
             b_x.reshape(1, W_LRU), lam.reshape(1, W_LRU)]
    return pl.pallas_call(
        functools.partial(_lru_kernel, n_seq=n_seq, rows=rows, has_state=has_state,
                          reset_first=reset_first, n_alias=n_alias),
        out_shape=(
            jax.ShapeDtypeStruct((n_batch * t_len, W_LRU), BF16),
            jax.ShapeDtypeStruct((n_layers, n_batch, 1, W_LRU), F32),
            jax.ShapeDtypeStruct((n_layers, n_batch, CONV_W - 1, W_LRU), F32),
        ),
        grid=(nb, nc),
        in_specs=in_specs,
        out_specs=(
            pl.BlockSpec((n, W_LRU), lambda i, c: (i * nc + c, 0)),
            _layer_spec(h_block, layer),
            _layer_spec(c_block, layer),
        ),
        input_output_aliases=aliases,
        scratch_shapes=[
            pltpu.VMEM((n_seq, HIST + rows, W_LRU), F32),
            pltpu.VMEM((n, W_LRU), F32),
            pltpu.VMEM((n, W_LRU), F32),
            pltpu.VMEM((n, W_LRU), F32),
        ],
        compiler_params=pltpu.CompilerParams(
            dimension_semantics=("parallel", "arbitrary"), vmem_limit_bytes=VMEM_LIMIT),
        name="lru",
    )(*args)


def _ab_w_in_layout(w):
    sizes = [QKV_A, N_HEAD_A, N_HEAD_A, VA, KB, KB, VB, LOWRANK, VB]
    offs = [0]
    for s in sizes:
        offs.append(offs[-1] + s)
    qkv, b_raw, a_raw, z_a, q_b, k_b, v_b, lr_b, z_b = [w[:, offs[i]:offs[i + 1]] for i in range(9)]
    used = OFF_TAIL + 2 * N_HEAD_A + LOWRANK
    pad = jnp.zeros((w.shape[0], AB_N - used), w.dtype)
    return jnp.concatenate([qkv, z_a, q_b, k_b, v_b, z_b, b_raw, a_raw, lr_b, pad], axis=1).astype(BF16)


def _cast_kernel(w_ref, o_ref):
    o_ref[...] = w_ref[...].astype(BF16)


def _layer_bf16(w, layer):
    _, r, c = w.shape
    tr = 256
    return pl.pallas_call(
        _cast_kernel,
        out_shape=jax.ShapeDtypeStruct((r, c), BF16),
        grid=(r // tr,),
        in_specs=[pl.BlockSpec((None, tr, c), lambda i: (layer, i, 0))],
        out_specs=pl.BlockSpec((tr, c), lambda i: (i, 0)),
        compiler_params=pltpu.CompilerParams(dimension_semantics=("parallel",)),
        name="w_cast",
    )(w)


def _tail_row(vals, lane0):
    return jnp.zeros((TAIL_W,), F32).at[lane0:lane0 + vals.shape[0]].set(vals.astype(F32))


def kernel(x_prompt, x_sample, state_delta, state_delta_conv, state_gla, state_lru, state_lru_conv,
           ab_norm, ab_w_in, ab_conv_w, ab_a_log, ab_dt_bias, ab_norm_a, ab_gla_w_lr, ab_gla_b_lr,
           ab_norm_b, ab_w_out, lru_norm, lru_w_in, lru_conv_w, lru_conv_b, lru_w_a, lru_b_a,
           lru_w_x, lru_b_x, lru_lambda, lru_w_out, final_norm):
    n_ab, n_lru = ab_norm.shape[0], lru_norm.shape[0]
    depth = n_ab + n_lru
    lru_h0 = state_lru.reshape(n_lru, -1, 1, W_LRU)
    groups = []
    for x, carried in ((x_prompt, False), (x_sample, True)):
        groups.append(dict(x=x.reshape(-1, D_MODEL), nb=x.shape[0], t=x.shape[1], carried=carried,
                           delta=None, gla=None, lru=None))

    for layer in range(depth):
        j = layer // 2
        last = layer == depth - 1
        if layer % 2 == 0:
            w_in = _ab_w_in_layout(ab_w_in[j])
            w_out = _layer_bf16(ab_w_out, j)
            gate_params = jnp.stack([_tail_row(ab_a_log[j], LANE_A), _tail_row(ab_dt_bias[j], LANE_A)])
            w_lr_pad = jnp.zeros((TAIL_W, KB), F32).at[LANE_LR:LANE_LR + LOWRANK].set(ab_gla_w_lr[j])
            for g in groups:
                proj = _norm_mm(g["x"], ab_norm[j], w_in, AB_TN)
                s_a, c_a, s_b = (state_delta, state_delta_conv, state_gla) if g["carried"] else (None,) * 3
                o_a, *g["delta"] = _delta(proj, j, n_ab, s_a, c_a, g["delta"], ab_conv_w[j], gate_params,
                                          ab_norm_a[j], g["nb"], g["t"])
                o_b, *g["gla"] = _gla(proj, j, n_ab, s_b, g["gla"], w_lr_pad, ab_gla_b_lr[j], ab_norm_b[j],
                                      g["nb"], g["t"], 16)
                g["x"] = _out_mm([o_a, o_b], w_out, g["x"], final_norm if last else None)
        else:
            w_in = _layer_bf16(lru_w_in, j)
            w_out = _layer_bf16(lru_w_out, j)
            for g in groups:
                proj = _norm_mm(g["x"], lru_norm[j], w_in, 1024)
                h0, c0 = (lru_h0, state_lru_conv) if g["carried"] else (None, None)
                y, *g["lru"] = _lru(proj, j, n_lru, h0, c0, g["lru"], lru_conv_w[j], lru_conv_b[j],
                                    lru_w_a[j], lru_b_a[j], lru_w_x[j], lru_b_x[j], lru_lambda[j],
                                    g["nb"], g["t"], reset_first=not g["carried"])
                g["x"] = _out_mm([y], w_out, g["x"], final_norm if last else None)

    outs = []
    for g, x in zip(groups, (x_prompt, x_sample)):
        h_all, lconv = g["lru"]
        outs.append([g["x"].reshape(x.shape), g["delta"][0], g["delta"][1], g["gla"][0],
                     h_all.reshape(n_lru, g["nb"], W_LRU), lconv])
    p, s = outs
    return (p[0], s[0], p[1], p[2], p[3], p[4], p[5], s[1], s[2], s[3], s[4], s[5])
```

```python
import functools

import jax
import jax.numpy as jnp
from jax import lax
from jax.experimental import pallas as pl
from jax.experimental.pallas import tpu as pltpu

F32 = jnp.float32
BF16 = jnp.bfloat16
HIGHEST = lax.Precision.HIGHEST

D_MODEL = 2048
N_HEAD_A, DK_A, DV_A = 8, 128, 128
N_HEAD_B, DK_B, DV_B = 4, 128, 256
KA = N_HEAD_A * DK_A
VA = N_HEAD_A * DV_A
KB = N_HEAD_B * DK_B
VB = N_HEAD_B * DV_B
QKV_A = 2 * KA + VA
LOWRANK = 16
GLA_NORMALIZER = 16.0
W_LRU = D_MODEL
LRU_BLOCKS = 16
LRU_BW = W_LRU // LRU_BLOCKS
LRU_C = 8.0
CONV_W = 4
EPS = 1e-6
LANES = 128
TINY = 1.1754944e-38

OFF_QKV = 0
OFF_ZA = OFF_QKV + QKV_A
OFF_QB = OFF_ZA + VA
OFF_KB = OFF_QB + KB
OFF_VB = OFF_KB + KB
OFF_ZB = OFF_VB + VB
OFF_TAIL = OFF_ZB + VB
TAIL_W = 128
AB_TN = 1536
AB_N = 7680
LANE_BETA, LANE_A, LANE_LR = 0, N_HEAD_A, 2 * N_HEAD_A

ROWS = 64
DELTA_M = 128
DELTA_SEQS = 2
GLA_SEQS = 4
HIST = 8
VMEM_LIMIT = 56 * 1024 * 1024

NN = (((1,), (0,)), ((), ()))
NT = (((1,), (1,)), ((), ()))
TN = (((0,), (0,)), ((), ()))


def _mm(a, b, dims=NN):
    return lax.dot_general(a, b, dims, precision=HIGHEST, preferred_element_type=F32)


def _split(a):
    hi = a.astype(BF16)
    return hi, (a - hi.astype(F32)).astype(BF16)


def _mm3(a, b, dims=NN):
    ah, al = a if isinstance(a, tuple) else _split(a)
    bh, bl = b if isinstance(b, tuple) else _split(b)
    (ca,), (cb,) = dims[0]
    dot = functools.partial(lax.dot_general, dimension_numbers=dims, preferred_element_type=F32)
    if ah.shape[ca] % LANES:
        return dot(ah, bh) + (dot(ah, bl) + dot(al, bh))
    return dot(jnp.concatenate([ah, ah, al], axis=ca), jnp.concatenate([bh, bl, bh], axis=cb))


def _split_each(xs):
    return [x if isinstance(x, tuple) else _split(x) for x in xs]


def _mm3_each(as_, bs, dims=NN):
    sa, sb = _split_each(as_), _split_each(bs)
    return [_mm3(a, b, dims) for a, b in zip(sa, sb)]


def _mm2_each(as_, bs, rounded, dims=NN):
    out = []
    for a, b in zip(as_, bs):
        if rounded == "a":
            ah = a.astype(BF16)
            bh, bl = b if isinstance(b, tuple) else _split(b)
            pa, pb = [ah, ah], [bh, bl]
        else:
            ah, al = a if isinstance(a, tuple) else _split(a)
            bh = b.astype(BF16)
            pa, pb = [ah, al], [bh, bh]
        (ca,), (cb,) = dims[0]
        assert pa[0].shape[ca] % LANES == 0
        out.append(lax.dot_general(jnp.concatenate(pa, axis=ca), jnp.concatenate(pb, axis=cb), dims,
                                   preferred_element_type=F32))
    return out


def _mm1_each(as_, bs, dims=NN):
    ca = [a.astype(BF16) for a in as_]
    cb = [b.astype(BF16) for b in bs]
    return [lax.dot_general(a, b, dims, preferred_element_type=F32) for a, b in zip(ca, cb)]


def _sigmoid(x):
    return 0.5 * jnp.tanh(0.5 * x) + 0.5


def _silu(x):
    h = 0.5 * x
    return h + h * jnp.tanh(h)


def _norm_mm_kernel(x_ref, g_ref, w_ref, o_ref, h_ref):
    @pl.when(pl.program_id(1) == 0)
    def _():
        x = x_ref[...]
        ms = jnp.mean(x * x, axis=-1, keepdims=True)
        h_ref[...] = (x * lax.rsqrt(ms + EPS) * g_ref[...]).astype(BF16)

    o_ref[...] = jnp.dot(h_ref[...], w_ref[...], preferred_element_type=F32)


def _norm_mm(x, g, w, tn):
    m, k = x.shape
    n = w.shape[1]
    tm = min(m, 1024)
    return pl.pallas_call(
        _norm_mm_kernel,
        out_shape=jax.ShapeDtypeStruct((m, n), F32),
        grid=(m // tm, n // tn),
        in_specs=[
            pl.BlockSpec((tm, k), lambda i, j: (i, 0)),
            pl.BlockSpec((1, k), lambda i, j: (0, 0)),
            pl.BlockSpec((k, tn), lambda i, j: (0, j)),
        ],
        out_specs=pl.BlockSpec((tm, tn), lambda i, j: (i, j)),
        scratch_shapes=[pltpu.VMEM((tm, k), BF16)],
        compiler_params=pltpu.CompilerParams(
            dimension_semantics=("parallel", "arbitrary"), vmem_limit_bytes=VMEM_LIMIT),
        name="norm_mm",
    )(x, g.reshape(1, k), w)


def _out_mm_kernel(*refs, n_in, final):
    o_refs = refs[:n_in]
    w_ref, x_ref = refs[n_in], refs[n_in + 1]
    out_ref = refs[-1]
    acc = x_ref[...]
    k0 = 0
    for o_ref in o_refs:
        kk = o_ref.shape[1]
        acc = acc + jnp.dot(o_ref[...], w_ref[k0:k0 + kk, :], preferred_element_type=F32)
        k0 += kk
    if final:
        fg_ref = refs[n_in + 2]
        ms = jnp.mean(acc * acc, axis=-1, keepdims=True)
        acc = acc * lax.rsqrt(ms + EPS) * fg_ref[...]
    out_ref[...] = acc


def _out_mm(os_, w, x, final_g=None):
    m, d = x.shape
    tm = 512
    final = final_g is not None
    in_specs = [pl.BlockSpec((tm, o.shape[1]), lambda i: (i, 0)) for o in os_]
    in_specs += [pl.BlockSpec(w.shape, lambda i: (0, 0)), pl.BlockSpec((tm, d), lambda i: (i, 0))]
    args = list(os_) + [w, x]
    if final:
        in_specs.append(pl.BlockSpec((1, d), lambda i: (0, 0)))
        args.append(final_g.reshape(1, d))
    return pl.pallas_call(
        functools.partial(_out_mm_kernel, n_in=len(os_), final=final),
        out_shape=jax.ShapeDtypeStruct((m, d), F32),
        grid=(m // tm,),
        in_specs=in_specs,
        out_specs=pl.BlockSpec((tm, d), lambda i: (i, 0)),
        compiler_params=pltpu.CompilerParams(
            dimension_semantics=("parallel",), vmem_limit_bytes=VMEM_LIMIT),
        name="out_mm",
    )(*args)


def _causal_conv_block(seq_rows, xp_ref, cw_ref, n_seq, rows, width, emit, bias_ref=None):
    lc = 512
    groups = rows // 8
    sub = lax.broadcasted_iota(jnp.int32, (groups, 8, lc), 1)
    for s in range(n_seq):
        xp_ref[s, HIST:HIST + rows, :] = seq_rows(s)
        for c0 in range(0, width, lc):
            cs = slice(c0, c0 + lc)
            x3 = xp_ref[s, :, cs].reshape(groups + HIST // 8, 8, lc)
            acc = x3[1:] * cw_ref[CONV_W - 1:CONV_W, cs]
            for d in range(1, CONV_W):
                rolled = pltpu.roll(x3, d, axis=1)
                tap = jnp.where(sub >= d, rolled[1:], rolled[:-1])
                acc = acc + tap * cw_ref[CONV_W - 1 - d:CONV_W - d, cs]
            acc = acc.reshape(rows, lc)
            if bias_ref is not None:
                acc = acc + bias_ref[:, cs]
            emit(s, cs, acc)
        xp_ref[s, 0:HIST, :] = xp_ref[s, rows:rows + HIST, :]


def _seq_masks(n, rows):
    ri = lax.broadcasted_iota(jnp.int32, (n, n), 0)
    ci = lax.broadcasted_iota(jnp.int32, (n, n), 1)
    if rows == n:
        same = ri >= 0
    else:
        sh = rows.bit_length() - 1
        same = (ri >> sh) == (ci >> sh)
    return ri, ci, same


NEUMANN_BLOCK = 8


def _inverse_masks(ri, ci, rows):
    assert rows >= NEUMANN_BLOCK
    blk = lambda k: (ri >> (k.bit_length() - 1)) == (ci >> (k.bit_length() - 1))
    eye = jnp.where(ri == ci, 1.0, 0.0).astype(F32)
    levels = []
    k = NEUMANN_BLOCK
    while k < rows:
        levels.append(blk(2 * k) & jnp.logical_not(blk(k)))
        k *= 2
    return eye, blk(NEUMANN_BLOCK), levels


def _unit_lower_inverse_each(as_, masks):
    eye, base, levels = masks
    a8 = [jnp.where(base, a, 0.0) for a in as_]
    a8s = _split_each(a8)
    p2 = _mm2_each(a8s, a8, "b")
    p2s = _split_each(p2)
    p4 = _mm2_each(p2s, p2, "b")
    a8p2 = _mm2_each(a8s, p2, "b")
    x = [(eye - a) + (p - ap) for a, p, ap in zip(a8, p2, a8p2)]
    t = [xi + xp for xi, xp in zip(x, _mm2_each(x, p4, "a"))]
    for lvl in levels:
        off = [jnp.where(lvl, a, 0.0) for a in as_]
        t = [ti - d for ti, d in zip(t, _mm2_each(_mm2_each(t, off, "a"), t, "b"))]
    return t


def _delta_kernel(*refs, n_seq, rows, has_state, n_alias):
    refs = refs[n_alias:]
    qkv_ref, z_ref, tail_ref = refs[:3]
    p = 3
    if has_state:
        s0_ref, c0_ref = refs[p], refs[p + 1]
        p += 2
    cw_ref, gp_ref, na_ref = refs[p:p + 3]
    o_ref, s_ref, cn_ref = refs[p + 3:p + 6]
    xp_ref, act_ref = refs[p + 6:p + 8]
    n = n_seq * rows
    c = pl.program_id(1)

    @pl.when(c == 0)
    def _init():
        if has_state:
            s_ref[...] = s0_ref[...]
            for s in range(n_seq):
                xp_ref[s, HIST - (CONV_W - 1):HIST, :] = c0_ref[s]
        else:
            s_ref[...] = jnp.zeros(s_ref.shape, F32)
            xp_ref[:, 0:HIST, :] = jnp.zeros((n_seq, HIST, QKV_A), F32)

    def emit(s, cs, acc):
        act_ref[s * rows:(s + 1) * rows, cs] = _silu(acc)

    def seq_rows(s):
        slab, off = divmod(s * rows, ROWS)
        return qkv_ref[slab, off:off + rows, :]

    _causal_conv_block(seq_rows, xp_ref, cw_ref, n_seq, rows, QKV_A, emit)

    @pl.when(c == pl.num_programs(1) - 1)
    def _conv_out():
        for s in range(n_seq):
            cn_ref[s] = xp_ref[s, HIST - (CONV_W - 1):HIST, :]

    n_slab = n // ROWS
    head_group = DELTA_M // n
    _, _, same_n = _seq_masks(n, rows)
    rn = lax.broadcasted_iota(jnp.int32, (n, n), 0)
    cn = lax.broadcasted_iota(jnp.int32, (n, n), 1)
    tail = tail_ref[...].reshape(n, TAIL_W)
    btile = _sigmoid(tail)
    gtile = -jnp.exp(gp_ref[0:1, :]) * jax.nn.softplus(tail + gp_ref[1:2, :])
    g_cum = _mm(jnp.where(same_n & (rn >= cn), 1.0, 0.0).astype(F32), gtile)
    g_tot = _mm(jnp.where(same_n, 1.0, 0.0).astype(F32), gtile)
    e_cum = jnp.exp(g_cum)
    e_rest = jnp.exp(g_tot - g_cum)
    g_cum_t = g_cum.T

    m = DELTA_M
    ri, ci, same = _seq_masks(m, rows)
    incl = same & (ri >= ci)
    strict = same & (ri > ci)
    inv_masks = _inverse_masks(ri, ci, rows)
    stack = lambda xs: xs[0] if len(xs) == 1 else jnp.concatenate(xs, axis=0)
    groups = [range(g * head_group, (g + 1) * head_group) for g in range(N_HEAD_A // head_group)]
    qn, kn, v, beta, gc, eg, er, g_row = ([] for _ in range(8))
    for heads in groups:
        q_h, k_h, v_h = [], [], []
        for h in heads:
            q = act_ref[:, h * DK_A:(h + 1) * DK_A]
            k = act_ref[:, KA + h * DK_A:KA + (h + 1) * DK_A]
            q_h.append(q * lax.rsqrt(jnp.sum(q * q, axis=-1, keepdims=True) + EPS) * (DK_A ** -0.5))
            k_h.append(k * lax.rsqrt(jnp.sum(k * k, axis=-1, keepdims=True) + EPS))
            v_h.append(act_ref[:, 2 * KA + h * DV_A:2 * KA + (h + 1) * DV_A])
        qn.append(stack(q_h))
        kn.append(stack(k_h))
        v.append(stack(v_h))
        beta.append(stack([btile[:, LANE_BETA + h:LANE_BETA + h + 1] for h in heads]))
        gc.append(stack([g_cum[:, LANE_A + h:LANE_A + h + 1] for h in heads]))
        eg.append(stack([e_cum[:, LANE_A + h:LANE_A + h + 1] for h in heads]))
        er.append(stack([e_rest[:, LANE_A + h:LANE_A + h + 1] for h in heads]))
        g_row.append(jnp.concatenate([g_cum_t[LANE_A + h:LANE_A + h + 1, :] for h in heads], axis=1))
    decay = [jnp.where(incl, jnp.exp(jnp.where(incl, c_ - r_, 0.0)), 0.0) for c_, r_ in zip(gc, g_row)]
    kb = [k_ * b_ for k_, b_ in zip(kn, beta)]
    prod = _mm3_each([stack([kb_, q_]) for kb_, q_ in zip(kb, qn)], kn, NT)
    a = [jnp.where(strict, p_[:m] * d_, 0.0) for p_, d_ in zip(prod, decay)]
    qk = [p_[m:] * d_ for p_, d_ in zip(prod, decay)]
    t_inv = _unit_lower_inverse_each(a, inv_masks)
    rhs = [jnp.concatenate([v_ * b_, kb_ * e_], axis=1) for v_, b_, kb_, e_ in zip(v, beta, kb, eg)]
    sol = _mm2_each(t_inv, rhs, "a")
    qg = [q_ * e_ for q_, e_ in zip(qn, eg)]
    kdec = [k_ * e_ for k_, e_ in zip(kn, er)]
    probs = [(g, i, h, s, slice(i * n + s * rows, i * n + (s + 1) * rows))
             for g, heads in enumerate(groups) for i, h in enumerate(heads) for s in range(n_seq)]
    states = [s_ref[s, h] for (_, _, h, s, _) in probs]
    r = _mm1_each([stack([sol[g][sl, DV_A:], qg[g][sl]]) for (g, _, _, _, sl) in probs], states)
    v_new = [sol[g][sl, :DV_A] - r_[:rows] for (g, _, _, _, sl), r_ in zip(probs, r)]
    per_group = lambda xs: [stack([x for (g2, *_), x in zip(probs, xs) if g2 == g]) for g in range(len(groups))]
    v_new_g = per_group(v_new)
    o_inter_g = per_group([r_[rows:] for r_ in r])
    o = [oi + d for oi, d in zip(o_inter_g, _mm1_each(qk, v_new_g))]
    upd = _mm1_each([kdec[g][sl] for (g, _, _, _, sl) in probs], v_new, TN)
    for (g, i, h, s, sl), st, du in zip(probs, states, upd):
        g_last = g_tot[s * rows:s * rows + 1, LANE_A + h:LANE_A + h + 1]
        s_ref[s, h] = st * jnp.exp(g_last) + du
    for g, heads in enumerate(groups):
        for i, h in enumerate(heads):
            hl = slice(h * DV_A, (h + 1) * DV_A)
            oh = o[g][i * n:(i + 1) * n]
            oh = oh * lax.rsqrt(jnp.mean(oh * oh, axis=-1, keepdims=True) + EPS) * na_ref[...]
            oh = oh * _silu(z_ref[:, :, hl].reshape(n, DV_A))
            o_ref[:, :, hl] = oh.reshape(n_slab, ROWS, DV_A).astype(BF16)


def _layer_spec(block, layer):
    zeros = (0,) * (len(block) - 1)
    return pl.BlockSpec((None,) + tuple(block), lambda i, c: (layer, i) + zeros)


def _carry_outputs(prev, out_start):
    if prev is None:
        return [], [], {}
    specs = [pl.BlockSpec(memory_space=pl.ANY) for _ in prev]
    return list(prev), specs, {k: out_start + k for k in range(len(prev))}


def _delta(proj, layer, n_layers, s0, c0, prev, conv_w, gate_params, norm_a, n_batch, t_len):
    rows = min(ROWS, t_len)
    if t_len >= ROWS:
        n_seq, n_slab = DELTA_SEQS, DELTA_SEQS
        slabs = proj.reshape(n_batch, t_len // ROWS, ROWS, AB_N)
    else:
        n_seq, n_slab = ROWS // t_len, 1
        slabs = proj.reshape(n_batch * t_len // ROWS, 1, ROWS, AB_N)
    n = n_seq * rows
    nb = n_batch // n_seq
    nc = t_len // rows
    has_state = s0 is not None
    slab_spec = lambda width, col: pl.BlockSpec((n_slab, None, ROWS, width), lambda i, c: (i, c, 0, col))
    args, in_specs, aliases = _carry_outputs(prev, 1)
    n_alias = len(args)
    in_specs += [slab_spec(QKV_A, 0), slab_spec(VA, OFF_ZA // VA), slab_spec(TAIL_W, OFF_TAIL // TAIL_W)]
    args += [slabs, slabs, slabs]
    s_block = (n_seq, N_HEAD_A, DK_A, DV_A)
    c_block = (n_seq, CONV_W - 1, QKV_A)
    if has_state:
        in_specs += [_layer_spec(s_block, layer), _layer_spec(c_block, layer)]
        args += [s0, c0]
    in_specs += [
        pl.BlockSpec((CONV_W, QKV_A), lambda i, c: (0, 0)),
        pl.BlockSpec((2, TAIL_W), lambda i, c: (0, 0)),
        pl.BlockSpec((1, DV_A), lambda i, c: (0, 0)),
    ]
    args += [conv_w, gate_params, norm_a.reshape(1, DV_A)]
    o, s_all, conv_all = pl.pallas_call(
        functools.partial(_delta_kernel, n_seq=n_seq, rows=rows, has_state=has_state, n_alias=n_alias),
        out_shape=(
            jax.ShapeDtypeStruct(slabs.shape[:3] + (VA,), BF16),
            jax.ShapeDtypeStruct((n_layers, n_batch, N_HEAD_A, DK_A, DV_A), F32),
            jax.ShapeDtypeStruct((n_layers, n_batch, CONV_W - 1, QKV_A), F32),
        ),
        grid=(nb, nc),
        in_specs=in_specs,
        out_specs=(
            slab_spec(VA, 0),
            _layer_spec(s_block, layer),
            _layer_spec(c_block, layer),
        ),
        scratch_shapes=[
            pltpu.VMEM((n_seq, HIST + rows, QKV_A), F32),
            pltpu.VMEM((n, QKV_A), F32),
        ],
        input_output_aliases=aliases,
        compiler_params=pltpu.CompilerParams(
            dimension_semantics=("parallel", "arbitrary"), vmem_limit_bytes=VMEM_LIMIT),
        name="delta",
    )(*args)
    return o.reshape(n_batch * t_len, VA), s_all, conv_all


def _gla_kernel(*refs, rows, chained, has_state, n_alias):
    refs = refs[n_alias:]
    q_ref, k_ref, v_ref, z_ref, tail_ref = refs[:5]
    p = 5
    if has_state:
        s0_ref = refs[p]
        p += 1
    wlr_ref, blr_ref, nb_ref = refs[p:p + 3]
    o_ref, s_ref = refs[p + 3:p + 5]
    n_slab = q_ref.shape[0]
    n = n_slab * ROWS
    n_sub = ROWS // rows
    c = pl.program_id(1)

    @pl.when(c == 0)
    def _init():
        if has_state:
            s_ref[...] = s0_ref[...]
        else:
            s_ref[...] = jnp.zeros(s_ref.shape, F32)

    ri, ci, same = _seq_masks(ROWS, rows)
    incl = same & (ri >= ci)
    slabs = [slice(u * ROWS, (u + 1) * ROWS) for u in range(n_slab)]
    cumulate = lambda mask, x: jnp.concatenate(
        [_mm(jnp.where(mask, 1.0, 0.0).astype(F32), x[sl]) for sl in slabs], axis=0)
    gk = jax.nn.log_sigmoid(_mm(tail_ref[...].reshape(n, TAIL_W), wlr_ref[...]) + blr_ref[...]) / GLA_NORMALIZER
    b_loc = cumulate(incl, gk)
    q = q_ref[...].reshape(n, KB) * (DK_B ** -0.5)
    k = k_ref[...].reshape(n, KB)
    v_all = v_ref[...].reshape(n, VB)
    q_loc = q * jnp.exp(b_loc)
    heads = range(N_HEAD_B)
    hls = [slice(h * DK_B, (h + 1) * DK_B) for h in heads]
    vls = [slice(h * DV_B, (h + 1) * DV_B) for h in heads]

    def finish(h, o):
        o = o * lax.rsqrt(jnp.mean(o * o, axis=-1, keepdims=True) + EPS) * nb_ref[...]
        o = o * _silu(z_ref[:, :, vls[h]].reshape(n, DV_B))
        o_ref[:, :, vls[h]] = o.reshape(n_slab, ROWS, DV_B).astype(BF16)

    def key_column(row_vals):
        return jnp.broadcast_to(row_vals, (8, DK_B)).T[:, 0:1]

    if chained:
        tri = ri >= ci
        b_cum = cumulate(tri, gk)
        per_slab = lambda rws: jnp.concatenate([jnp.broadcast_to(r_, (ROWS, KB)) for r_ in rws], axis=0)
        b_last = [b_cum[sl.stop - 1:sl.stop] for sl in slabs]
        q_cum = q * jnp.exp(b_cum)
        kdec = k * jnp.exp(per_slab(b_last) - b_cum)
        row = lax.broadcasted_iota(jnp.int32, (n, KB), 0) & (ROWS - 1)
        k_rel = []
        for s in range(n_sub):
            start = [b_cum[sl.start + s * rows - 1:sl.start + s * rows] if s else jnp.zeros((1, KB), F32)
                     for sl in slabs]
            k_rel.append(k * jnp.exp(jnp.where(row < (s + 1) * rows, per_slab(start) - b_cum, 0.0)))
        probs = [(u, h) for u in range(n_slab) for h in heads]
        a_parts = _mm3_each(
            [q_loc[slabs[u].start + s * rows:slabs[u].start + (s + 1) * rows, hls[h]]
             for (u, h) in probs for s in range(n_sub)],
            [k_rel[s][slabs[u], hls[h]] for (u, h) in probs for s in range(n_sub)], NT)
        a = [jnp.where(tri, jnp.concatenate(a_parts[j * n_sub:(j + 1) * n_sub], axis=0), 0.0)
             for j in range(len(probs))]
        vs = [v_all[slabs[u], vls[h]] for (u, h) in probs]
        sts = [s_ref[u, h] for (u, h) in probs]
        o_state = _mm1_each([q_cum[slabs[u], hls[h]] for (u, h) in probs], sts)
        intra = _mm1_each(a, vs)
        upd = _mm1_each([kdec[slabs[u], hls[h]] for (u, h) in probs], vs, TN)
        for j, (u, h) in enumerate(probs):
            s_ref[u, h] = sts[j] * jnp.exp(key_column(b_last[u][:, hls[h]])) + upd[j]
        for h in heads:
            finish(h, jnp.concatenate([o_state[j] + intra[j] for j, (_, h2) in enumerate(probs) if h2 == h], axis=0))
    else:
        vs = [v_all[:, vl] for vl in vls]
        b_tot = _mm(jnp.where(same, 1.0, 0.0).astype(F32), gk)
        kd = k * jnp.exp(-b_loc)
        kdec = k * jnp.exp(b_tot - b_loc)
        a = [jnp.where(incl, x, 0.0) for x in _mm3_each([q_loc[:, hl] for hl in hls], [kd[:, hl] for hl in hls], NT)]
        intra = _mm1_each(a, vs)
        probs = [(h, s, slice(s * rows, (s + 1) * rows)) for h in heads for s in range(n_sub)]
        sts = [s_ref[s, h] for (h, s, _) in probs]
        o_state = _mm1_each([q_loc[sl, hls[h]] for (h, _, sl) in probs], sts)
        upd = _mm1_each([kdec[sl, hls[h]] for (h, _, sl) in probs], [vs[h][sl] for (h, _, sl) in probs], TN)
        for (h, s, sl), st, du in zip(probs, sts, upd):
            s_ref[s, h] = st * jnp.exp(key_column(b_tot[s * rows:s * rows + 1, hls[h]])) + du
        for h in heads:
            finish(h, jnp.concatenate(o_state[h * n_sub:(h + 1) * n_sub], axis=0) + intra[h])


def _gla(proj, layer, n_layers, s0, prev, w_lr_pad, b_lr, norm_b, n_batch, t_len, chunk):
    rows = min(chunk, t_len)
    has_state = s0 is not None
    chained = t_len >= ROWS
    if chained:
        n_slab, n_state, nc = GLA_SEQS, GLA_SEQS, t_len // ROWS
        slabs = proj.reshape(n_batch, nc, ROWS, AB_N)
    else:
        n_slab, n_state, nc = 1, ROWS // t_len, 1
        slabs = proj.reshape(n_batch * t_len // ROWS, 1, ROWS, AB_N)
    nb = n_batch // n_state
    slab_spec = lambda width, col: pl.BlockSpec((n_slab, None, ROWS, width), lambda i, c: (i, c, 0, col))
    args, in_specs, aliases = _carry_outputs(prev, 1)
    n_alias = len(args)
    in_specs += [slab_spec(KB, OFF_QB // KB), slab_spec(KB, OFF_KB // KB), slab_spec(VB, OFF_VB // VB),
                 slab_spec(VB, OFF_ZB // VB), slab_spec(TAIL_W, OFF_TAIL // TAIL_W)]
    args += [slabs] * 5
    s_block = (n_state, N_HEAD_B, DK_B, DV_B)
    if has_state:
        in_specs.append(_layer_spec(s_block, layer))
        args.append(s0)
    in_specs += [
        pl.BlockSpec((TAIL_W, KB), lambda i, c: (0, 0)),
        pl.BlockSpec((1, KB), lambda i, c: (0, 0)),
        pl.BlockSpec((1, DV_B), lambda i, c: (0, 0)),
    ]
    args += [w_lr_pad, b_lr.reshape(1, KB), norm_b.reshape(1, DV_B)]
    o, s_all = pl.pallas_call(
        functools.partial(_gla_kernel, rows=rows, chained=chained, has_state=has_state, n_alias=n_alias),
        out_shape=(
            jax.ShapeDtypeStruct(slabs.shape[:3] + (VB,), BF16),
            jax.ShapeDtypeStruct((n_layers, n_batch, N_HEAD_B, DK_B, DV_B), F32),
        ),
        grid=(nb, nc),
        in_specs=in_specs,
        out_specs=(slab_spec(VB, 0), _layer_spec(s_block, layer)),
        input_output_aliases=aliases,
        compiler_params=pltpu.CompilerParams(
            dimension_semantics=("parallel", "arbitrary"), vmem_limit_bytes=VMEM_LIMIT),
        name="gla",
    )(*args)
    return o.reshape(n_batch * t_len, VB), s_all


def _lru_kernel(*refs, n_seq, rows, has_state, reset_first, n_alias):
    refs = refs[n_alias:]
    xb_ref, gate_ref = refs[:2]
    p = 2
    if has_state:
        h0_ref, c0_ref = refs[p], refs[p + 1]
        p += 2
    cw_ref, cb_ref, wg_ref, ba_ref, bx_ref, lam_ref = refs[p:p + 6]
    y_ref, h_ref, cn_ref = refs[p + 6:p + 9]
    xp_ref, xc_ref, a_ref, b_ref = refs[p + 9:p + 13]
    n = n_seq * rows
    c = pl.program_id(1)

    @pl.when(c == 0)
    def _init():
        if has_state:
            h_ref[...] = h0_ref[...]
            for s in range(n_seq):
                xp_ref[s, HIST - (CONV_W - 1):HIST, :] = c0_ref[s]
        else:
            h_ref[...] = jnp.zeros(h_ref.shape, F32)
            xp_ref[:, 0:HIST, :] = jnp.zeros((n_seq, HIST, W_LRU), F32)

    def emit(s, cs, acc):
        xc_ref[s * rows:(s + 1) * rows, cs] = acc

    _causal_conv_block(lambda s: xb_ref[s * rows:(s + 1) * rows, :], xp_ref, cw_ref, n_seq, rows, W_LRU,
                       emit, bias_ref=cb_ref)

    @pl.when(c == pl.num_programs(1) - 1)
    def _conv_out():
        for s in range(n_seq):
            cn_ref[s] = xp_ref[s, HIST - (CONV_W - 1):HIST, :]

    row = lax.broadcasted_iota(jnp.int32, (n, LRU_BW), 0)
    sub = lax.broadcasted_iota(jnp.int32, (n // 8, 8, LRU_BW), 1)
    scan_steps = [(d, sub >= d) for d in (1, 2, 4)]
    first_row = (row == 0) & (c == 0)
    for blk in range(LRU_BLOCKS):
        bl = slice(blk * LRU_BW, (blk + 1) * LRU_BW)
        xc = xc_ref[:, bl]
        pre = jnp.dot(xc.astype(BF16), wg_ref[blk], preferred_element_type=F32)
        r = _sigmoid(pre[:, :LRU_BW] + ba_ref[:, bl])
        gi = _sigmoid(pre[:, LRU_BW:] + bx_ref[:, bl])
        log_a = -LRU_C * r * jax.nn.softplus(-lam_ref[:, bl])
        a = jnp.exp(log_a)
        z = -jnp.tanh(log_a) * (a * a + 1.0)
        mult = z * lax.rsqrt(jnp.maximum(z, TINY))
        if reset_first:
            mult = jnp.where(first_row, 1.0, mult)
        b = mult * gi * xc
        a = a.reshape(n // 8, 8, LRU_BW)
        b = b.reshape(n // 8, 8, LRU_BW)
        for d, m in scan_steps:
            a_sh = pltpu.roll(a, d, axis=1)
            b_sh = pltpu.roll(b, d, axis=1)
            b = jnp.where(m, a * b_sh + b, b)
            a = jnp.where(m, a * a_sh, a)
        a_ref[:, bl] = a.reshape(n, LRU_BW)
        b_ref[:, bl] = b.reshape(n, LRU_BW)

    for s in range(n_seq):
        def body(g, hp, s=s):
            r0 = pl.multiple_of(s * rows + g * 8, 8)
            hs = a_ref[pl.ds(r0, 8), :] * hp + b_ref[pl.ds(r0, 8), :]
            b_ref[pl.ds(r0, 8), :] = hs
            return hs[7:8, :]

        h_ref[s] = lax.fori_loop(0, rows // 8, body, h_ref[s])

    lc = 512
    for c0 in range(0, W_LRU, lc):
        cs = slice(c0, c0 + lc)
        y_ref[:, cs] = (b_ref[:, cs] * _silu(gate_ref[:, cs])).astype(BF16)


def _lru(proj, layer, n_layers, h0, c0, prev, conv_w, conv_b, w_a, b_a, w_x, b_x, lam, n_batch, t_len,
         reset_first):
    has_state = h0 is not None
    if t_len >= 256:
        rows, n_seq = 256, 1
    else:
        rows, n_seq = t_len, 128 // t_len
    n = rows * n_seq
    nb = n_batch // n_seq
    nc = t_len // rows
    args, in_specs, aliases = _carry_outputs(prev, 1)
    n_alias = len(args)
    in_specs += [
        pl.BlockSpec((n, W_LRU), lambda i, c: (i * nc + c, 0)),
        pl.BlockSpec((n, W_LRU), lambda i, c: (i * nc + c, 1)),
    ]
    args += [proj, proj]
    h_block = (n_seq, 1, W_LRU)
    c_block = (n_seq, CONV_W - 1, W_LRU)
    if has_state:
        in_specs += [_layer_spec(h_block, layer), _layer_spec(c_block, layer)]
        args += [h0, c0]
    vec = pl.BlockSpec((1, W_LRU), lambda i, c: (0, 0))
    w_gates = jnp.concatenate([w_a, w_x], axis=-1).astype(BF16)
    blk = pl.BlockSpec(w_gates.shape, lambda i, c: (0, 0, 0))
    in_specs += [pl.BlockSpec((CONV_W, W_LRU), lambda i, c: (0, 0)), vec, blk, vec, vec, vec]
    args += [conv_w, conv_b.reshape(1, W_LRU), w_gates, b_a.reshape(1, W_LRU),
             b_x.reshape(1, W_LRU), lam.reshape(1, W_LRU)]
    return pl.pallas_call(
        functools.partial(_lru_kernel, n_seq=n_seq, rows=rows, has_state=has_state,
                          reset_first=reset_first, n_alias=n_alias),
        out_shape=(
            jax.ShapeDtypeStruct((n_batch * t_len, W_LRU), BF16),
            jax.ShapeDtypeStruct((n_layers, n_batch, 1, W_LRU), F32),
            jax.ShapeDtypeStruct((n_layers, n_batch, CONV_W - 1, W_LRU), F32),
        ),
        grid=(nb, nc),
        in_specs=in_specs,
        out_specs=(
            pl.BlockSpec((n, W_LRU), lambda i, c: (i * nc + c, 0)),
            _layer_spec(h_block, layer),
            _layer_spec(c_block, layer),
        ),
        input_output_aliases=aliases,
        scratch_shapes=[
            pltpu.VMEM((n_seq, HIST + rows, W_LRU), F32),
            pltpu.VMEM((n, W_LRU), F32),
            pltpu.VMEM((n, W_LRU), F32),
            pltpu.VMEM((n, W_LRU), F32),
        ],
        compiler_params=pltpu.CompilerParams(
            dimension_semantics=("parallel", "arbitrary"), vmem_limit_bytes=VMEM_LIMIT),
        name="lru",
    )(*args)


def _ab_w_in_layout(w):
    sizes = [QKV_A, N_HEAD_A, N_HEAD_A, VA, KB, KB, VB, LOWRANK, VB]
    offs = [0]
    for s in sizes:
        offs.append(offs[-1] + s)
    qkv, b_raw, a_raw, z_a, q_b, k_b, v_b, lr_b, z_b = [w[:, offs[i]:offs[i + 1]] for i in range(9)]
    used = OFF_TAIL + 2 * N_HEAD_A + LOWRANK
    pad = jnp.zeros((w.shape[0], AB_N - used), w.dtype)
    return jnp.concatenate([qkv, z_a, q_b, k_b, v_b, z_b, b_raw, a_raw, lr_b, pad], axis=1).astype(BF16)


def _cast_kernel(w_ref, o_ref):
    o_ref[...] = w_ref[...].astype(BF16)


def _layer_bf16(w, layer):
    _, r, c = w.shape
    tr = 256
    return pl.pallas_call(
        _cast_kernel,
        out_shape=jax.ShapeDtypeStruct((r, c), BF16),
        grid=(r // tr,),
        in_specs=[pl.BlockSpec((None, tr, c), lambda i: (layer, i, 0))],
        out_specs=pl.BlockSpec((tr, c), lambda i: (i, 0)),
        compiler_params=pltpu.CompilerParams(dimension_semantics=("parallel",)),
        name="w_cast",
    )(w)


def _tail_row(vals, lane0):
    return jnp.zeros((TAIL_W,), F32).at[lane0:lane0 + vals.shape[0]].set(vals.astype(F32))


def kernel(x_prompt, x_sample, state_delta, state_delta_conv, state_gla, state_lru, state_lru_conv,
           ab_norm, ab_w_in, ab_conv_w, ab_a_log, ab_dt_bias, ab_norm_a, ab_gla_w_lr, ab_gla_b_lr,
           ab_norm_b, ab_w_out, lru_norm, lru_w_in, lru_conv_w, lru_conv_b, lru_w_a, lru_b_a,
           lru_w_x, lru_b_x, lru_lambda, lru_w_out, final_norm):
    n_ab, n_lru = ab_norm.shape[0], lru_norm.shape[0]
    depth = n_ab + n_lru
    lru_h0 = state_lru.reshape(n_lru, -1, 1, W_LRU)
    groups = []
    for x, carried in ((x_prompt, False), (x_sample, True)):
        groups.append(dict(x=x.reshape(-1, D_MODEL), nb=x.shape[0], t=x.shape[1], carried=carried,
                           delta=None, gla=None, lru=None))

    for layer in range(depth):
        j = layer // 2
        last = layer == depth - 1
        if layer % 2 == 0:
            w_in = _ab_w_in_layout(ab_w_in[j])
            w_out = _layer_bf16(ab_w_out, j)
            gate_params = jnp.stack([_tail_row(ab_a_log[j], LANE_A), _tail_row(ab_dt_bias[j], LANE_A)])
            w_lr_pad = jnp.zeros((TAIL_W, KB), F32).at[LANE_LR:LANE_LR + LOWRANK].set(ab_gla_w_lr[j])
            for g in groups:
                proj = _norm_mm(g["x"], ab_norm[j], w_in, AB_TN)
                s_a, c_a, s_b = (state_delta, state_delta_conv, state_gla) if g["carried"] else (None,) * 3
                o_a, *g["delta"] = _delta(proj, j, n_ab, s_a, c_a, g["delta"], ab_conv_w[j], gate_params,
                                          ab_norm_a[j], g["nb"], g["t"])
                o_b, *g["gla"] = _gla(proj, j, n_ab, s_b, g["gla"], w_lr_pad, ab_gla_b_lr[j], ab_norm_b[j],
                                      g["nb"], g["t"], 16)
                g["x"] = _out_mm([o_a, o_b], w_out, g["x"], final_norm if last else None)
        else:
            w_in = _layer_bf16(lru_w_in, j)
            w_out = _layer_bf16(lru_w_out, j)
            for g in groups:
                proj = _norm_mm(g["x"], lru_norm[j], w_in, 1024)
                h0, c0 = (lru_h0, state_lru_conv) if g["carried"] else (None, None)
                y, *g["lru"] = _lru(proj, j, n_lru, h0, c0, g["lru"], lru_conv_w[j], lru_conv_b[j],
                                    lru_w_a[j], lru_b_a[j], lru_w_x[j], lru_b_x[j], lru_lambda[j],
                                    g["nb"], g["t"], reset_first=not g["carried"])
                g["x"] = _out_mm([y], w_out, g["x"], final_norm if last else None)

    outs = []
    for g, x in zip(groups, (x_prompt, x_sample)):
        h_all, lconv = g["lru"]
        outs.append([g["x"].reshape(x.shape), g["delta"][0], g["delta"][1], g["gla"][0],
                     h_all.reshape(n_lru, g["nb"], W_LRU), lconv])
    p, s = outs
    return (p[0], s[0], p[1], p[2], p[3], p[4], p[5], s[1], s[2], s[3], s[4], s[5])
```

```python
import functools

import jax
import jax.numpy as jnp
from jax import lax
from jax.experimental import pallas as pl
from jax.experimental.pallas import tpu as pltpu

F32 = jnp.float32
BF16 = jnp.bfloat16

D_MODEL = 2048
N_HEAD_A, DK_A, DV_A = 8, 128, 128
N_HEAD_B, DK_B, DV_B = 4, 128, 256
KA = N_HEAD_A * DK_A
VA = N_HEAD_A * DV_A
KB = N_HEAD_B * DK_B
VB = N_HEAD_B * DV_B
QKV_A = 2 * KA + VA
LOWRANK = 16
GLA_NORMALIZER = 16.0
W_LRU = D_MODEL
LRU_BLOCKS = 16
LRU_BW = W_LRU // LRU_BLOCKS
LRU_C = 8.0
CONV_W = 4
EPS = 1e-6
LANES = 128
TINY = 1.1754944e-38

OFF_QKV = 0
OFF_ZA = OFF_QKV + QKV_A
OFF_QB = OFF_ZA + VA
OFF_KB = OFF_QB + KB
OFF_VB = OFF_KB + KB
OFF_ZB = OFF_VB + VB
OFF_TAIL = OFF_ZB + VB
TAIL_W = 128
AB_TN = 1536
AB_N = 7680
LANE_BETA, LANE_A, LANE_LR = 0, N_HEAD_A, 2 * N_HEAD_A

ROWS = 64
DELTA_M = 128
DELTA_SEQS = 2
GLA_SEQS = 4
HIST = 8
VMEM_LIMIT = 56 * 1024 * 1024

NN = (((1,), (0,)), ((), ()))
NT = (((1,), (1,)), ((), ()))
TN = (((0,), (0,)), ((), ()))


def _masked_sum(mask, x):
    m = jnp.where(mask, 1.0, 0.0).astype(BF16)
    hi = x.astype(BF16)
    rest = x - hi.astype(F32)
    mid = rest.astype(BF16)
    lo = (rest - mid.astype(F32)).astype(BF16)
    if m.shape[1] % LANES:
        dot = functools.partial(jnp.dot, preferred_element_type=F32)
        return dot(m, hi) + (dot(m, mid) + dot(m, lo))
    return jnp.dot(jnp.concatenate([m, m, m], axis=1), jnp.concatenate([hi, mid, lo], axis=0),
                   preferred_element_type=F32)


def _split(a):
    hi = a.astype(BF16)
    return hi, (a - hi.astype(F32)).astype(BF16)


def _mm3(a, b, dims=NN):
    ah, al = a if isinstance(a, tuple) else _split(a)
    bh, bl = b if isinstance(b, tuple) else _split(b)
    (ca,), (cb,) = dims[0]
    dot = functools.partial(lax.dot_general, dimension_numbers=dims, preferred_element_type=F32)
    if ah.shape[ca] % LANES:
        return dot(ah, bh) + (dot(ah, bl) + dot(al, bh))
    return dot(jnp.concatenate([ah, ah, al], axis=ca), jnp.concatenate([bh, bl, bh], axis=cb))


def _split_each(xs):
    return [x if isinstance(x, tuple) else _split(x) for x in xs]


def _mm3_each(as_, bs, dims=NN):
    sa, sb = _split_each(as_), _split_each(bs)
    return [_mm3(a, b, dims) for a, b in zip(sa, sb)]


def _mm2_each(as_, bs, rounded, dims=NN):
    out = []
    for a, b in zip(as_, bs):
        if rounded == "a":
            ah = a.astype(BF16)
            bh, bl = b if isinstance(b, tuple) else _split(b)
            pa, pb = [ah, ah], [bh, bl]
        else:
            ah, al = a if isinstance(a, tuple) else _split(a)
            bh = b.astype(BF16)
            pa, pb = [ah, al], [bh, bh]
        (ca,), (cb,) = dims[0]
        assert pa[0].shape[ca] % LANES == 0
        out.append(lax.dot_general(jnp.concatenate(pa, axis=ca), jnp.concatenate(pb, axis=cb), dims,
                                   preferred_element_type=F32))
    return out


def _mm1_each(as_, bs, dims=NN):
    ca = [a.astype(BF16) for a in as_]
    cb = [b.astype(BF16) for b in bs]
    return [lax.dot_general(a, b, dims, preferred_element_type=F32) for a, b in zip(ca, cb)]


def _sigmoid(x):
    return 0.5 * jnp.tanh(0.5 * x) + 0.5


def _silu(x):
    h = 0.5 * x
    return h + h * jnp.tanh(h)


def _norm_mm_kernel(x_ref, g_ref, w_ref, o_ref, h_ref):
    @pl.when(pl.program_id(1) == 0)
    def _():
        x = x_ref[...]
        ms = jnp.mean(x * x, axis=-1, keepdims=True)
        h_ref[...] = (x * lax.rsqrt(ms + EPS) * g_ref[...]).astype(BF16)

    o_ref[...] = jnp.dot(h_ref[...], w_ref[...], preferred_element_type=F32)


def _norm_mm(x, g, w, tn):
    m, k = x.shape
    n = w.shape[1]
    tm = min(m, 1024)
    return pl.pallas_call(
        _norm_mm_kernel,
        out_shape=jax.ShapeDtypeStruct((m, n), F32),
        grid=(m // tm, n // tn),
        in_specs=[
            pl.BlockSpec((tm, k), lambda i, j: (i, 0)),
            pl.BlockSpec((1, k), lambda i, j: (0, 0)),
            pl.BlockSpec((k, tn), lambda i, j: (0, j)),
        ],
        out_specs=pl.BlockSpec((tm, tn), lambda i, j: (i, j)),
        scratch_shapes=[pltpu.VMEM((tm, k), BF16)],
        compiler_params=pltpu.CompilerParams(
            dimension_semantics=("parallel", "arbitrary"), vmem_limit_bytes=VMEM_LIMIT),
        name="norm_mm",
    )(x, g.reshape(1, k), w)


def _out_mm_kernel(*refs, n_in, final):
    o_refs = refs[:n_in]
    w_ref, x_ref = refs[n_in], refs[n_in + 1]
    out_ref = refs[-1]
    acc = x_ref[...]
    k0 = 0
    for o_ref in o_refs:
        kk = o_ref.shape[1]
        acc = acc + jnp.dot(o_ref[...], w_ref[k0:k0 + kk, :], preferred_element_type=F32)
        k0 += kk
    if final:
        fg_ref = refs[n_in + 2]
        ms = jnp.mean(acc * acc, axis=-1, keepdims=True)
        acc = acc * lax.rsqrt(ms + EPS) * fg_ref[...]
    out_ref[...] = acc


def _out_mm(os_, w, x, final_g=None):
    m, d = x.shape
    tm = 512
    final = final_g is not None
    in_specs = [pl.BlockSpec((tm, o.shape[1]), lambda i: (i, 0)) for o in os_]
    in_specs += [pl.BlockSpec(w.shape, lambda i: (0, 0)), pl.BlockSpec((tm, d), lambda i: (i, 0))]
    args = list(os_) + [w, x]
    if final:
        in_specs.append(pl.BlockSpec((1, d), lambda i: (0, 0)))
        args.append(final_g.reshape(1, d))
    return pl.pallas_call(
        functools.partial(_out_mm_kernel, n_in=len(os_), final=final),
        out_shape=jax.ShapeDtypeStruct((m, d), F32),
        grid=(m // tm,),
        in_specs=in_specs,
        out_specs=pl.BlockSpec((tm, d), lambda i: (i, 0)),
        compiler_params=pltpu.CompilerParams(
            dimension_semantics=("parallel",), vmem_limit_bytes=VMEM_LIMIT),
        name="out_mm",
    )(*args)


def _causal_conv_block(seq_rows, xp_ref, cw_ref, n_seq, rows, width, emit, bias_ref=None):
    lc = 512
    groups = rows // 8
    sub = lax.broadcasted_iota(jnp.int32, (groups, 8, lc), 1)
    for s in range(n_seq):
        xp_ref[s, HIST:HIST + rows, :] = seq_rows(s)
        for c0 in range(0, width, lc):
            cs = slice(c0, c0 + lc)
            x3 = xp_ref[s, :, cs].reshape(groups + HIST // 8, 8, lc)
            acc = x3[1:] * cw_ref[CONV_W - 1:CONV_W, cs]
            for d in range(1, CONV_W):
                rolled = pltpu.roll(x3, d, axis=1)
                tap = jnp.where(sub >= d, rolled[1:], rolled[:-1])
                acc = acc + tap * cw_ref[CONV_W - 1 - d:CONV_W - d, cs]
            acc = acc.reshape(rows, lc)
            if bias_ref is not None:
                acc = acc + bias_ref[:, cs]
            emit(s, cs, acc)
        xp_ref[s, 0:HIST, :] = xp_ref[s, rows:rows + HIST, :]


def _seq_masks(n, rows):
    ri = lax.broadcasted_iota(jnp.int32, (n, n), 0)
    ci = lax.broadcasted_iota(jnp.int32, (n, n), 1)
    if rows == n:
        same = ri >= 0
    else:
        sh = rows.bit_length() - 1
        same = (ri >> sh) == (ci >> sh)
    return ri, ci, same


NEUMANN_BLOCK = 8


def _inverse_masks(ri, ci, rows):
    assert rows >= NEUMANN_BLOCK
    blk = lambda k: (ri >> (k.bit_length() - 1)) == (ci >> (k.bit_length() - 1))
    eye = jnp.where(ri == ci, 1.0, 0.0).astype(F32)
    levels = []
    k = NEUMANN_BLOCK
    while k < rows:
        levels.append(blk(2 * k) & jnp.logical_not(blk(k)))
        k *= 2
    return eye, blk(NEUMANN_BLOCK), levels


def _unit_lower_inverse_each(as_, masks):
    eye, base, levels = masks
    a8 = [jnp.where(base, a, 0.0) for a in as_]
    a8s = _split_each(a8)
    p2 = _mm2_each(a8s, a8, "b")
    p2s = _split_each(p2)
    p4 = _mm2_each(p2s, p2, "b")
    a8p2 = _mm2_each(a8s, p2, "b")
    x = [(eye - a) + (p - ap) for a, p, ap in zip(a8, p2, a8p2)]
    t = [xi + xp for xi, xp in zip(x, _mm2_each(x, p4, "a"))]
    for lvl in levels:
        off = [jnp.where(lvl, a, 0.0) for a in as_]
        t = [ti - d for ti, d in zip(t, _mm2_each(_mm2_each(t, off, "a"), t, "b"))]
    return t


def _delta_kernel(*refs, n_seq, rows, has_state, n_alias):
    refs = refs[n_alias:]
    qkv_ref, z_ref, tail_ref = refs[:3]
    p = 3
    if has_state:
        s0_ref, c0_ref = refs[p], refs[p + 1]
        p += 2
    cw_ref, gp_ref, na_ref = refs[p:p + 3]
    o_ref, s_ref, cn_ref = refs[p + 3:p + 6]
    xp_ref, act_ref = refs[p + 6:p + 8]
    n = n_seq * rows
    c = pl.program_id(1)

    @pl.when(c == 0)
    def _init():
        if has_state:
            s_ref[...] = s0_ref[...]
            for s in range(n_seq):
                xp_ref[s, HIST - (CONV_W - 1):HIST, :] = c0_ref[s]
        else:
            s_ref[...] = jnp.zeros(s_ref.shape, F32)
            xp_ref[:, 0:HIST, :] = jnp.zeros((n_seq, HIST, QKV_A), F32)

    def emit(s, cs, acc):
        act_ref[s * rows:(s + 1) * rows, cs] = _silu(acc)

    def seq_rows(s):
        slab, off = divmod(s * rows, ROWS)
        return qkv_ref[slab, off:off + rows, :]

    _causal_conv_block(seq_rows, xp_ref, cw_ref, n_seq, rows, QKV_A, emit)

    @pl.when(c == pl.num_programs(1) - 1)
    def _conv_out():
        for s in range(n_seq):
            cn_ref[s] = xp_ref[s, HIST - (CONV_W - 1):HIST, :]

    n_slab = n // ROWS
    head_group = DELTA_M // n
    _, _, same_n = _seq_masks(n, rows)
    rn = lax.broadcasted_iota(jnp.int32, (n, n), 0)
    cn = lax.broadcasted_iota(jnp.int32, (n, n), 1)
    tail = tail_ref[...].reshape(n, TAIL_W)
    btile = _sigmoid(tail)
    gtile = -jnp.exp(gp_ref[0:1, :]) * jax.nn.softplus(tail + gp_ref[1:2, :])
    g_cum = _masked_sum(same_n & (rn >= cn), gtile)
    g_tot = _masked_sum(same_n, gtile)
    e_cum = jnp.exp(g_cum)
    e_rest = jnp.exp(g_tot - g_cum)
    g_cum_t = g_cum.T

    m = DELTA_M
    ri, ci, same = _seq_masks(m, rows)
    incl = same & (ri >= ci)
    strict = same & (ri > ci)
    inv_masks = _inverse_masks(ri, ci, rows)
    stack = lambda xs: xs[0] if len(xs) == 1 else jnp.concatenate(xs, axis=0)
    groups = [range(g * head_group, (g + 1) * head_group) for g in range(N_HEAD_A // head_group)]
    qn, kn, v, beta, gc, eg, er, g_row = ([] for _ in range(8))
    for heads in groups:
        q_h, k_h, v_h = [], [], []
        for h in heads:
            q = act_ref[:, h * DK_A:(h + 1) * DK_A]
            k = act_ref[:, KA + h * DK_A:KA + (h + 1) * DK_A]
            q_h.append(q * lax.rsqrt(jnp.sum(q * q, axis=-1, keepdims=True) + EPS) * (DK_A ** -0.5))
            k_h.append(k * lax.rsqrt(jnp.sum(k * k, axis=-1, keepdims=True) + EPS))
            v_h.append(act_ref[:, 2 * KA + h * DV_A:2 * KA + (h + 1) * DV_A])
        qn.append(stack(q_h))
        kn.append(stack(k_h))
        v.append(stack(v_h))
        beta.append(stack([btile[:, LANE_BETA + h:LANE_BETA + h + 1] for h in heads]))
        gc.append(stack([g_cum[:, LANE_A + h:LANE_A + h + 1] for h in heads]))
        eg.append(stack([e_cum[:, LANE_A + h:LANE_A + h + 1] for h in heads]))
        er.append(stack([e_rest[:, LANE_A + h:LANE_A + h + 1] for h in heads]))
        g_row.append(jnp.concatenate([g_cum_t[LANE_A + h:LANE_A + h + 1, :] for h in heads], axis=1))
    decay = [jnp.where(incl, jnp.exp(jnp.where(incl, c_ - r_, 0.0)), 0.0) for c_, r_ in zip(gc, g_row)]
    kb = [k_ * b_ for k_, b_ in zip(kn, beta)]
    prod = _mm3_each([stack([kb_, q_]) for kb_, q_ in zip(kb, qn)], kn, NT)
    a = [jnp.where(strict, p_[:m] * d_, 0.0) for p_, d_ in zip(prod, decay)]
    qk = [p_[m:] * d_ for p_, d_ in zip(prod, decay)]
    t_inv = _unit_lower_inverse_each(a, inv_masks)
    rhs = [jnp.concatenate([v_ * b_, kb_ * e_], axis=1) for v_, b_, kb_, e_ in zip(v, beta, kb, eg)]
    sol = _mm2_each(t_inv, rhs, "a")
    qg = [q_ * e_ for q_, e_ in zip(qn, eg)]
    kdec = [k_ * e_ for k_, e_ in zip(kn, er)]
    probs = [(g, i, h, s, slice(i * n + s * rows, i * n + (s + 1) * rows))
             for g, heads in enumerate(groups) for i, h in enumerate(heads) for s in range(n_seq)]
    states = [s_ref[s, h] for (_, _, h, s, _) in probs]
    r = _mm1_each([stack([sol[g][sl, DV_A:], qg[g][sl]]) for (g, _, _, _, sl) in probs], states)
    v_new = [sol[g][sl, :DV_A] - r_[:rows] for (g, _, _, _, sl), r_ in zip(probs, r)]
    per_group = lambda xs: [stack([x for (g2, *_), x in zip(probs, xs) if g2 == g]) for g in range(len(groups))]
    v_new_g = per_group(v_new)
    o_inter_g = per_group([r_[rows:] for r_ in r])
    o = [oi + d for oi, d in zip(o_inter_g, _mm1_each(qk, v_new_g))]
    upd = _mm1_each([kdec[g][sl] for (g, _, _, _, sl) in probs], v_new, TN)
    for (g, i, h, s, sl), st, du in zip(probs, states, upd):
        g_last = g_tot[s * rows:s * rows + 1, LANE_A + h:LANE_A + h + 1]
        s_ref[s, h] = st * jnp.exp(g_last) + du
    for g, heads in enumerate(groups):
        for i, h in enumerate(heads):
            hl = slice(h * DV_A, (h + 1) * DV_A)
            oh = o[g][i * n:(i + 1) * n]
            oh = oh * lax.rsqrt(jnp.mean(oh * oh, axis=-1, keepdims=True) + EPS) * na_ref[...]
            oh = oh * _silu(z_ref[:, :, hl].reshape(n, DV_A))
            o_ref[:, :, hl] = oh.reshape(n_slab, ROWS, DV_A).astype(BF16)


def _layer_spec(block, layer):
    zeros = (0,) * (len(block) - 1)
    return pl.BlockSpec((None,) + tuple(block), lambda i, c: (layer, i) + zeros)


def _carry_outputs(prev, out_start):
    if prev is None:
        return [], [], {}
    specs = [pl.BlockSpec(memory_space=pl.ANY) for _ in prev]
    return list(prev), specs, {k: out_start + k for k in range(len(prev))}


def _delta(proj, layer, n_layers, s0, c0, prev, conv_w, gate_params, norm_a, n_batch, t_len):
    rows = min(ROWS, t_len)
    if t_len >= ROWS:
        n_seq, n_slab = DELTA_SEQS, DELTA_SEQS
        slabs = proj.reshape(n_batch, t_len // ROWS, ROWS, AB_N)
    else:
        n_seq, n_slab = ROWS // t_len, 1
        slabs = proj.reshape(n_batch * t_len // ROWS, 1, ROWS, AB_N)
    n = n_seq * rows
    nb = n_batch // n_seq
    nc = t_len // rows
    has_state = s0 is not None
    slab_spec = lambda width, col: pl.BlockSpec((n_slab, None, ROWS, width), lambda i, c: (i, c, 0, col))
    args, in_specs, aliases = _carry_outputs(prev, 1)
    n_alias = len(args)
    in_specs += [slab_spec(QKV_A, 0), slab_spec(VA, OFF_ZA // VA), slab_spec(TAIL_W, OFF_TAIL // TAIL_W)]
    args += [slabs, slabs, slabs]
    s_block = (n_seq, N_HEAD_A, DK_A, DV_A)
    c_block = (n_seq, CONV_W - 1, QKV_A)
    if has_state:
        in_specs += [_layer_spec(s_block, layer), _layer_spec(c_block, layer)]
        args += [s0, c0]
    in_specs += [
        pl.BlockSpec((CONV_W, QKV_A), lambda i, c: (0, 0)),
        pl.BlockSpec((2, TAIL_W), lambda i, c: (0, 0)),
        pl.BlockSpec((1, DV_A), lambda i, c: (0, 0)),
    ]
    args += [conv_w, gate_params, norm_a.reshape(1, DV_A)]
    o, s_all, conv_all = pl.pallas_call(
        functools.partial(_delta_kernel, n_seq=n_seq, rows=rows, has_state=has_state, n_alias=n_alias),
        out_shape=(
            jax.ShapeDtypeStruct(slabs.shape[:3] + (VA,), BF16),
            jax.ShapeDtypeStruct((n_layers, n_batch, N_HEAD_A, DK_A, DV_A), F32),
            jax.ShapeDtypeStruct((n_layers, n_batch, CONV_W - 1, QKV_A), F32),
        ),
        grid=(nb, nc),
        in_specs=in_specs,
        out_specs=(
            slab_spec(VA, 0),
            _layer_spec(s_block, layer),
            _layer_spec(c_block, layer),
        ),
        scratch_shapes=[
            pltpu.VMEM((n_seq, HIST + rows, QKV_A), F32),
            pltpu.VMEM((n, QKV_A), F32),
        ],
        input_output_aliases=aliases,
        compiler_params=pltpu.CompilerParams(
            dimension_semantics=("parallel", "arbitrary"), vmem_limit_bytes=VMEM_LIMIT),
        name="delta",
    )(*args)
    return o.reshape(n_batch * t_len, VA), s_all, conv_all


def _gla_kernel(*refs, rows, chained, has_state, n_alias):
    refs = refs[n_alias:]
    q_ref, k_ref, v_ref, z_ref, tail_ref = refs[:5]
    p = 5
    if has_state:
        s0_ref = refs[p]
        p += 1
    wlr_ref, blr_ref, nb_ref = refs[p:p + 3]
    o_ref, s_ref = refs[p + 3:p + 5]
    n_slab = q_ref.shape[0]
    n = n_slab * ROWS
    n_sub = ROWS // rows
    c = pl.program_id(1)

    @pl.when(c == 0)
    def _init():
        if has_state:
            s_ref[...] = s0_ref[...]
        else:
            s_ref[...] = jnp.zeros(s_ref.shape, F32)

    ri, ci, same = _seq_masks(ROWS, rows)
    incl = same & (ri >= ci)
    slabs = [slice(u * ROWS, (u + 1) * ROWS) for u in range(n_slab)]
    cumulate = lambda mask, x: jnp.concatenate(
        [_masked_sum(mask, x[sl]) for sl in slabs], axis=0)
    gk = jax.nn.log_sigmoid(_mm3(tail_ref[...].reshape(n, TAIL_W), wlr_ref[...]) + blr_ref[...]) / GLA_NORMALIZER
    b_loc = cumulate(incl, gk)
    q = q_ref[...].reshape(n, KB) * (DK_B ** -0.5)
    k = k_ref[...].reshape(n, KB)
    v_all = v_ref[...].reshape(n, VB)
    q_loc = q * jnp.exp(b_loc)
    heads = range(N_HEAD_B)
    hls = [slice(h * DK_B, (h + 1) * DK_B) for h in heads]
    vls = [slice(h * DV_B, (h + 1) * DV_B) for h in heads]

    def finish(h, o):
        o = o * lax.rsqrt(jnp.mean(o * o, axis=-1, keepdims=True) + EPS) * nb_ref[...]
        o = o * _silu(z_ref[:, :, vls[h]].reshape(n, DV_B))
        o_ref[:, :, vls[h]] = o.reshape(n_slab, ROWS, DV_B).astype(BF16)

    def key_column(row_vals):
        return jnp.broadcast_to(row_vals, (8, DK_B)).T[:, 0:1]

    if chained:
        tri = ri >= ci
        b_cum = cumulate(tri, gk)
        per_slab = lambda rws: jnp.concatenate([jnp.broadcast_to(r_, (ROWS, KB)) for r_ in rws], axis=0)
        b_last = [b_cum[sl.stop - 1:sl.stop] for sl in slabs]
        q_cum = q * jnp.exp(b_cum)
        kdec = k * jnp.exp(per_slab(b_last) - b_cum)
        row = lax.broadcasted_iota(jnp.int32, (n, KB), 0) & (ROWS - 1)
        k_rel = []
        for s in range(n_sub):
            start = [b_cum[sl.start + s * rows - 1:sl.start + s * rows] if s else jnp.zeros((1, KB), F32)
                     for sl in slabs]
            k_rel.append(k * jnp.exp(jnp.where(row < (s + 1) * rows, per_slab(start) - b_cum, 0.0)))
        probs = [(u, h) for u in range(n_slab) for h in heads]
        a_parts = _mm3_each(
            [q_loc[slabs[u].start + s * rows:slabs[u].start + (s + 1) * rows, hls[h]]
             for (u, h) in probs for s in range(n_sub)],
            [k_rel[s][slabs[u], hls[h]] for (u, h) in probs for s in range(n_sub)], NT)
        a = [jnp.where(tri, jnp.concatenate(a_parts[j * n_sub:(j + 1) * n_sub], axis=0), 0.0)
             for j in range(len(probs))]
        vs = [v_all[slabs[u], vls[h]] for (u, h) in probs]
        sts = [s_ref[u, h] for (u, h) in probs]
        o_state = _mm1_each([q_cum[slabs[u], hls[h]] for (u, h) in probs], sts)
        intra = _mm1_each(a, vs)
        upd = _mm1_each([kdec[slabs[u], hls[h]] for (u, h) in probs], vs, TN)
        for j, (u, h) in enumerate(probs):
            s_ref[u, h] = sts[j] * jnp.exp(key_column(b_last[u][:, hls[h]])) + upd[j]
        for h in heads:
            finish(h, jnp.concatenate([o_state[j] + intra[j] for j, (_, h2) in enumerate(probs) if h2 == h], axis=0))
    else:
        vs = [v_all[:, vl] for vl in vls]
        b_tot = _masked_sum(same, gk)
        kd = k * jnp.exp(-b_loc)
        kdec = k * jnp.exp(b_tot - b_loc)
        a = [jnp.where(incl, x, 0.0) for x in _mm3_each([q_loc[:, hl] for hl in hls], [kd[:, hl] for hl in hls], NT)]
        intra = _mm1_each(a, vs)
        probs = [(h, s, slice(s * rows, (s + 1) * rows)) for h in heads for s in range(n_sub)]
        sts = [s_ref[s, h] for (h, s, _) in probs]
        o_state = _mm1_each([q_loc[sl, hls[h]] for (h, _, sl) in probs], sts)
        upd = _mm1_each([kdec[sl, hls[h]] for (h, _, sl) in probs], [vs[h][sl] for (h, _, sl) in probs], TN)
        for (h, s, sl), st, du in zip(probs, sts, upd):
            s_ref[s, h] = st * jnp.exp(key_column(b_tot[s * rows:s * rows + 1, hls[h]])) + du
        for h in heads:
            finish(h, jnp.concatenate(o_state[h * n_sub:(h + 1) * n_sub], axis=0) + intra[h])


def _gla(proj, layer, n_layers, s0, prev, w_lr_pad, b_lr, norm_b, n_batch, t_len, chunk):
    rows = min(chunk, t_len)
    has_state = s0 is not None
    chained = t_len >= ROWS
    if chained:
        n_slab, n_state, nc = GLA_SEQS, GLA_SEQS, t_len // ROWS
        slabs = proj.reshape(n_batch, nc, ROWS, AB_N)
    else:
        n_slab, n_state, nc = 1, ROWS // t_len, 1
        slabs = proj.reshape(n_batch * t_len // ROWS, 1, ROWS, AB_N)
    nb = n_batch // n_state
    slab_spec = lambda width, col: pl.BlockSpec((n_slab, None, ROWS, width), lambda i, c: (i, c, 0, col))
    args, in_specs, aliases = _carry_outputs(prev, 1)
    n_alias = len(args)
    in_specs += [slab_spec(KB, OFF_QB // KB), slab_spec(KB, OFF_KB // KB), slab_spec(VB, OFF_VB // VB),
                 slab_spec(VB, OFF_ZB // VB), slab_spec(TAIL_W, OFF_TAIL // TAIL_W)]
    args += [slabs] * 5
    s_block = (n_state, N_HEAD_B, DK_B, DV_B)
    if has_state:
        in_specs.append(_layer_spec(s_block, layer))
        args.append(s0)
    in_specs += [
        pl.BlockSpec((TAIL_W, KB), lambda i, c: (0, 0)),
        pl.BlockSpec((1, KB), lambda i, c: (0, 0)),
        pl.BlockSpec((1, DV_B), lambda i, c: (0, 0)),
    ]
    args += [w_lr_pad, b_lr.reshape(1, KB), norm_b.reshape(1, DV_B)]
    o, s_all = pl.pallas_call(
        functools.partial(_gla_kernel, rows=rows, chained=chained, has_state=has_state, n_alias=n_alias),
        out_shape=(
            jax.ShapeDtypeStruct(slabs.shape[:3] + (VB,), BF16),
            jax.ShapeDtypeStruct((n_layers, n_batch, N_HEAD_B, DK_B, DV_B), F32),
        ),
        grid=(nb, nc),
        in_specs=in_specs,
        out_specs=(slab_spec(VB, 0), _layer_spec(s_block, layer)),
        input_output_aliases=aliases,
        compiler_params=pltpu.CompilerParams(
            dimension_semantics=("parallel", "arbitrary"), vmem_limit_bytes=VMEM_LIMIT),
        name="gla",
    )(*args)
    return o.reshape(n_batch * t_len, VB), s_all


def _lru_kernel(*refs, n_seq, rows, has_state, reset_first, n_alias):
    refs = refs[n_alias:]
    xb_ref, gate_ref = refs[:2]
    p = 2
    if has_state:
        h0_ref, c0_ref = refs[p], refs[p + 1]
        p += 2
    cw_ref, cb_ref, wg_ref, ba_ref, bx_ref, lam_ref = refs[p:p + 6]
    y_ref, h_ref, cn_ref = refs[p + 6:p + 9]
    xp_ref, xc_ref, a_ref, b_ref = refs[p + 9:p + 13]
    n = n_seq * rows
    c = pl.program_id(1)

    @pl.when(c == 0)
    def _init():
        if has_state:
            h_ref[...] = h0_ref[...]
            for s in range(n_seq):
                xp_ref[s, HIST - (CONV_W - 1):HIST, :] = c0_ref[s]
        else:
            h_ref[...] = jnp.zeros(h_ref.shape, F32)
            xp_ref[:, 0:HIST, :] = jnp.zeros((n_seq, HIST, W_LRU), F32)

    def emit(s, cs, acc):
        xc_ref[s * rows:(s + 1) * rows, cs] = acc

    _causal_conv_block(lambda s: xb_ref[s * rows:(s + 1) * rows, :], xp_ref, cw_ref, n_seq, rows, W_LRU,
                       emit, bias_ref=cb_ref)

    @pl.when(c == pl.num_programs(1) - 1)
    def _conv_out():
        for s in range(n_seq):
            cn_ref[s] = xp_ref[s, HIST - (CONV_W - 1):HIST, :]

    row = lax.broadcasted_iota(jnp.int32, (n, LRU_BW), 0)
    sub = lax.broadcasted_iota(jnp.int32, (n // 8, 8, LRU_BW), 1)
    scan_steps = [(d, sub >= d) for d in (1, 2, 4)]
    first_row = (row == 0) & (c == 0)
    for blk in range(LRU_BLOCKS):
        bl = slice(blk * LRU_BW, (blk + 1) * LRU_BW)
        xc = xc_ref[:, bl]
        pre = jnp.dot(xc.astype(BF16), wg_ref[blk], preferred_element_type=F32)
        t_r = jnp.tanh(pre[:, :LRU_BW] + ba_ref[:, bl])
        t_i = jnp.tanh(pre[:, LRU_BW:] + bx_ref[:, bl])
        gi = 0.5 * t_i + 0.5
        log_a = (t_r + 1.0) * (-0.5 * LRU_C * jax.nn.softplus(-lam_ref[:, bl]))
        a = jnp.exp(log_a)
        z = -jnp.tanh(log_a) * (a * a + 1.0)
        mult = z * lax.rsqrt(jnp.maximum(z, TINY))
        if reset_first:
            mult = jnp.where(first_row, 1.0, mult)
        b = mult * gi * xc
        a = a.reshape(n // 8, 8, LRU_BW)
        b = b.reshape(n // 8, 8, LRU_BW)
        for d, m in scan_steps:
            a_sh = pltpu.roll(a, d, axis=1)
            b_sh = pltpu.roll(b, d, axis=1)
            b = jnp.where(m, a * b_sh + b, b)
            a = jnp.where(m, a * a_sh, a)
        a_ref[:, bl] = a.reshape(n, LRU_BW)
        b_ref[:, bl] = b.reshape(n, LRU_BW)

    for s in range(n_seq):
        def body(g, hp, s=s):
            r0 = pl.multiple_of(s * rows + g * 8, 8)
            hs = a_ref[pl.ds(r0, 8), :] * hp + b_ref[pl.ds(r0, 8), :]
            b_ref[pl.ds(r0, 8), :] = hs
            return hs[7:8, :]

        h_ref[s] = lax.fori_loop(0, rows // 8, body, h_ref[s])

    lc = 512
    for c0 in range(0, W_LRU, lc):
        cs = slice(c0, c0 + lc)
        y_ref[:, cs] = (b_ref[:, cs] * _silu(gate_ref[:, cs])).astype(BF16)


def _lru(proj, layer, n_layers, h0, c0, prev, conv_w, conv_b, w_a, b_a, w_x, b_x, lam, n_batch, t_len,
         reset_first):
    has_state = h0 is not None
    if t_len >= 256:
        rows, n_seq = 256, 1
    else:
        rows, n_seq = t_len, 128 // t_len
    n = rows * n_seq
    nb = n_batch // n_seq
    nc = t_len // rows
    args, in_specs, aliases = _carry_outputs(prev, 1)
    n_alias = len(args)
    in_specs += [
        pl.BlockSpec((n, W_LRU), lambda i, c: (i * nc + c, 0)),
        pl.BlockSpec((n, W_LRU), lambda i, c: (i * nc + c, 1)),
    ]
    args += [proj, proj]
    h_block = (n_seq, 1, W_LRU)
    c_block = (n_seq, CONV_W - 1, W_LRU)
    if has_state:
        in_specs += [_layer_spec(h_block, layer), _layer_spec(c_block, layer)]
        args += [h0, c0]
    vec = pl.BlockSpec((1, W_LRU), lambda i, c: (0, 0))
    w_gates = (0.5 * jnp.concatenate([w_a, w_x], axis=-1)).astype(BF16)
    blk = pl.BlockSpec(w_gates.shape, lambda i, c: (0, 0, 0))
    in_specs += [pl.BlockSpec((CONV_W, W_LRU), lambda i, c: (0, 0)), vec, blk, vec, vec, vec]
    args += [conv_w, conv_b.reshape(1, W_LRU), w_gates, 0.5 * b_a.reshape(1, W_LRU),
             0.5 * b_x.reshape(1, W_LRU), lam.reshape(1, W_LRU)]
    return pl.pallas_call(
        functools.partial(_lru_kernel, n_seq=n_seq, rows=rows, has_state=has_state,
                          reset_first=reset_first, n_alias=n_alias),
        out_shape=(
            jax.ShapeDtypeStruct((n_batch * t_len, W_LRU), BF16),
            jax.ShapeDtypeStruct((n_layers, n_batch, 1, W_LRU), F32),
            jax.ShapeDtypeStruct((n_layers, n_batch, CONV_W - 1, W_LRU), F32),
        ),
        grid=(nb, nc),
        in_specs=in_specs,
        out_specs=(
            pl.BlockSpec((n, W_LRU), lambda i, c: (i * nc + c, 0)),
            _layer_spec(h_block, layer),
            _layer_spec(c_block, layer),
        ),
        input_output_aliases=aliases,
        scratch_shapes=[
            pltpu.VMEM((n_seq, HIST + rows, W_LRU), F32),
            pltpu.VMEM((n, W_LRU), F32),
            pltpu.VMEM((n, W_LRU), F32),
            pltpu.VMEM((n, W_LRU), F32),
        ],
        compiler_params=pltpu.CompilerParams(
            dimension_semantics=("parallel", "arbitrary"), vmem_limit_bytes=VMEM_LIMIT),
        name="lru",
    )(*args)


def _ab_w_in_layout(w):
    sizes = [QKV_A, N_HEAD_A, N_HEAD_A, VA, KB, KB, VB, LOWRANK, VB]
    offs = [0]
    for s in sizes:
        offs.append(offs[-1] + s)
    qkv, b_raw, a_raw, z_a, q_b, k_b, v_b, lr_b, z_b = [w[:, offs[i]:offs[i + 1]] for i in range(9)]
    used = OFF_TAIL + 2 * N_HEAD_A + LOWRANK
    pad = jnp.zeros((w.shape[0], AB_N - used), w.dtype)
    return jnp.concatenate([qkv, z_a, q_b, k_b, v_b, z_b, b_raw, a_raw, lr_b, pad], axis=1).astype(BF16)


def _cast_kernel(w_ref, o_ref):
    o_ref[...] = w_ref[...].astype(BF16)


def _layer_bf16(w, layer):
    _, r, c = w.shape
    tr = 256
    return pl.pallas_call(
        _cast_kernel,
        out_shape=jax.ShapeDtypeStruct((r, c), BF16),
        grid=(r // tr,),
        in_specs=[pl.BlockSpec((None, tr, c), lambda i: (layer, i, 0))],
        out_specs=pl.BlockSpec((tr, c), lambda i: (i, 0)),
        compiler_params=pltpu.CompilerParams(dimension_semantics=("parallel",)),
        name="w_cast",
    )(w)


def _tail_row(vals, lane0):
    return jnp.zeros((TAIL_W,), F32).at[lane0:lane0 + vals.shape[0]].set(vals.astype(F32))


def kernel(x_prompt, x_sample, state_delta, state_delta_conv, state_gla, state_lru, state_lru_conv,
           ab_norm, ab_w_in, ab_conv_w, ab_a_log, ab_dt_bias, ab_norm_a, ab_gla_w_lr, ab_gla_b_lr,
           ab_norm_b, ab_w_out, lru_norm, lru_w_in, lru_conv_w, lru_conv_b, lru_w_a, lru_b_a,
           lru_w_x, lru_b_x, lru_lambda, lru_w_out, final_norm):
    n_ab, n_lru = ab_norm.shape[0], lru_norm.shape[0]
    depth = n_ab + n_lru
    lru_h0 = state_lru.reshape(n_lru, -1, 1, W_LRU)
    groups = []
    for x, carried in ((x_prompt, False), (x_sample, True)):
        groups.append(dict(x=x.reshape(-1, D_MODEL), nb=x.shape[0], t=x.shape[1], carried=carried,
                           delta=None, gla=None, lru=None))

    for layer in range(depth):
        j = layer // 2
        last = layer == depth - 1
        if layer % 2 == 0:
            w_in = _ab_w_in_layout(ab_w_in[j])
            w_out = _layer_bf16(ab_w_out, j)
            gate_params = jnp.stack([_tail_row(ab_a_log[j], LANE_A), _tail_row(ab_dt_bias[j], LANE_A)])
            w_lr_pad = jnp.zeros((TAIL_W, KB), F32).at[LANE_LR:LANE_LR + LOWRANK].set(ab_gla_w_lr[j])
            for g in groups:
                proj = _norm_mm(g["x"], ab_norm[j], w_in, AB_TN)
                s_a, c_a, s_b = (state_delta, state_delta_conv, state_gla) if g["carried"] else (None,) * 3
                o_a, *g["delta"] = _delta(proj, j, n_ab, s_a, c_a, g["delta"], ab_conv_w[j], gate_params,
                                          ab_norm_a[j], g["nb"], g["t"])
                o_b, *g["gla"] = _gla(proj, j, n_ab, s_b, g["gla"], w_lr_pad, ab_gla_b_lr[j], ab_norm_b[j],
                                      g["nb"], g["t"], 16)
                g["x"] = _out_mm([o_a, o_b], w_out, g["x"], final_norm if last else None)
        else:
            w_in = _layer_bf16(lru_w_in, j)
            w_out = _layer_bf16(lru_w_out, j)
            for g in groups:
                proj = _norm_mm(g["x"], lru_norm[j], w_in, 1024)
                h0, c0 = (lru_h0, state_lru_conv) if g["carried"] else (None, None)
                y, *g["lru"] = _lru(proj, j, n_lru, h0, c0, g["lru"], lru_conv_w[j], lru_conv_b[j],
                                    lru_w_a[j], lru_b_a[j], lru_w_x[j], lru_b_x[j], lru_lambda[j],
                                    g["nb"], g["t"], reset_first=not g["carried"])
                g["x"] = _out_mm([y], w_out, g["x"], final_norm if last else None)

    outs = []
    for g, x in zip(groups, (x_prompt, x_sample)):
        h_all, lconv = g["lru"]
        outs.append([g["x"].reshape(x.shape), g["delta"][0], g["delta"][1], g["gla"][0],
                     h_all.reshape(n_lru, g["nb"], W_LRU), lconv])
    p, s = outs
    return (p[0], s[0], p[1], p[2], p[3], p[4], p[5], s[1], s[2], s[3], s[4], s[5])
```

```python
import functools

import jax
import jax.numpy as jnp
from jax import lax
from jax.experimental import pallas as pl
from jax.experimental.pallas import tpu as pltpu

F32 = jnp.float32
BF16 = jnp.bfloat16

D_MODEL = 2048
N_HEAD_A, DK_A, DV_A = 8, 128, 128
N_HEAD_B, DK_B, DV_B = 4, 128, 256
KA = N_HEAD_A * DK_A
VA = N_HEAD_A * DV_A
KB = N_HEAD_B * DK_B
VB = N_HEAD_B * DV_B
QKV_A = 2 * KA + VA
LOWRANK = 16
GLA_NORMALIZER = 16.0
W_LRU = D_MODEL
LRU_BLOCKS = 16
LRU_BW = W_LRU // LRU_BLOCKS
LRU_C = 8.0
CONV_W = 4
EPS = 1e-6
LANES = 128
TINY = 1.1754944e-38

OFF_QKV = 0
OFF_ZA = OFF_QKV + QKV_A
OFF_QB = OFF_ZA + VA
OFF_KB = OFF_QB + KB
OFF_VB = OFF_KB + KB
OFF_ZB = OFF_VB + VB
OFF_TAIL = OFF_ZB + VB
TAIL_W = 128
AB_TN = 1536
AB_N = 7680
LANE_BETA, LANE_A, LANE_LR = 0, N_HEAD_A, 2 * N_HEAD_A

ROWS = 64
DELTA_M = 128
DELTA_SEQS = 2
GLA_SEQS = 4
HIST = 8
VMEM_LIMIT = 56 * 1024 * 1024

NN = (((1,), (0,)), ((), ()))
NT = (((1,), (1,)), ((), ()))
TN = (((0,), (0,)), ((), ()))


def _masked_sum(mask, x):
    m = jnp.where(mask, 1.0, 0.0).astype(BF16)
    hi = x.astype(BF16)
    rest = x - hi.astype(F32)
    mid = rest.astype(BF16)
    lo = (rest - mid.astype(F32)).astype(BF16)
    if m.shape[1] % LANES:
        dot = functools.partial(jnp.dot, preferred_element_type=F32)
        return dot(m, hi) + (dot(m, mid) + dot(m, lo))
    return jnp.dot(jnp.concatenate([m, m, m], axis=1), jnp.concatenate([hi, mid, lo], axis=0),
                   preferred_element_type=F32)


def _split(a):
    hi = a.astype(BF16)
    return hi, (a - hi.astype(F32)).astype(BF16)


def _mm3(a, b, dims=NN):
    ah, al = a if isinstance(a, tuple) else _split(a)
    bh, bl = b if isinstance(b, tuple) else _split(b)
    (ca,), (cb,) = dims[0]
    dot = functools.partial(lax.dot_general, dimension_numbers=dims, preferred_element_type=F32)
    if ah.shape[ca] % LANES:
        return dot(ah, bh) + (dot(ah, bl) + dot(al, bh))
    return dot(jnp.concatenate([ah, ah, al], axis=ca), jnp.concatenate([bh, bl, bh], axis=cb))


def _split_each(xs):
    return [x if isinstance(x, tuple) else _split(x) for x in xs]


def _mm3_each(as_, bs, dims=NN):
    sa, sb = _split_each(as_), _split_each(bs)
    return [_mm3(a, b, dims) for a, b in zip(sa, sb)]


def _mm2_each(as_, bs, rounded, dims=NN):
    out = []
    for a, b in zip(as_, bs):
        if rounded == "a":
            ah = a.astype(BF16)
            bh, bl = b if isinstance(b, tuple) else _split(b)
            pa, pb = [ah, ah], [bh, bl]
        else:
            ah, al = a if isinstance(a, tuple) else _split(a)
            bh = b.astype(BF16)
            pa, pb = [ah, al], [bh, bh]
        (ca,), (cb,) = dims[0]
        assert pa[0].shape[ca] % LANES == 0
        out.append(lax.dot_general(jnp.concatenate(pa, axis=ca), jnp.concatenate(pb, axis=cb), dims,
                                   preferred_element_type=F32))
    return out


def _mm1_each(as_, bs, dims=NN):
    ca = [a.astype(BF16) for a in as_]
    cb = [b.astype(BF16) for b in bs]
    return [lax.dot_general(a, b, dims, preferred_element_type=F32) for a, b in zip(ca, cb)]


def _sigmoid(x):
    return 0.5 * jnp.tanh(0.5 * x) + 0.5


def _silu(x):
    h = 0.5 * x
    return h + h * jnp.tanh(h)


def _norm_mm_kernel(x_ref, g_ref, w_ref, o_ref, h_ref):
    @pl.when(pl.program_id(1) == 0)
    def _():
        x = x_ref[...]
        ms = jnp.mean(x * x, axis=-1, keepdims=True)
        h_ref[...] = (x * lax.rsqrt(ms + EPS) * g_ref[...]).astype(BF16)

    o_ref[...] = jnp.dot(h_ref[...], w_ref[...], preferred_element_type=F32)


def _norm_mm(x, g, w, tn, layer=None):
    m, k = x.shape
    n = w.shape[-1]
    tm = min(m, 1024)
    if layer is None:
        w_spec = pl.BlockSpec((k, tn), lambda i, j: (0, j))
    else:
        w_spec = pl.BlockSpec((None, k, tn), lambda i, j: (layer, 0, j))
    return pl.pallas_call(
        _norm_mm_kernel,
        out_shape=jax.ShapeDtypeStruct((m, n), F32),
        grid=(m // tm, n // tn),
        in_specs=[
            pl.BlockSpec((tm, k), lambda i, j: (i, 0)),
            pl.BlockSpec((1, k), lambda i, j: (0, 0)),
            w_spec,
        ],
        out_specs=pl.BlockSpec((tm, tn), lambda i, j: (i, j)),
        scratch_shapes=[pltpu.VMEM((tm, k), BF16)],
        compiler_params=pltpu.CompilerParams(
            dimension_semantics=("parallel", "arbitrary"), vmem_limit_bytes=VMEM_LIMIT),
        name="norm_mm",
    )(x, g.reshape(1, k), w)


def _out_mm_kernel(*refs, n_in, final):
    o_refs = refs[:n_in]
    w_ref, x_ref = refs[n_in], refs[n_in + 1]
    out_ref = refs[-1]
    acc = x_ref[...]
    k0 = 0
    for o_ref in o_refs:
        kk = o_ref.shape[1]
        acc = acc + jnp.dot(o_ref[...], w_ref[k0:k0 + kk, :], preferred_element_type=F32)
        k0 += kk
    if final:
        fg_ref = refs[n_in + 2]
        ms = jnp.mean(acc * acc, axis=-1, keepdims=True)
        acc = acc * lax.rsqrt(ms + EPS) * fg_ref[...]
    out_ref[...] = acc


def _out_mm(os_, w, x, final_g=None):
    m, d = x.shape
    tm = 512
    final = final_g is not None
    in_specs = [pl.BlockSpec((tm, o.shape[1]), lambda i: (i, 0)) for o in os_]
    in_specs += [pl.BlockSpec(w.shape, lambda i: (0, 0)), pl.BlockSpec((tm, d), lambda i: (i, 0))]
    args = list(os_) + [w, x]
    if final:
        in_specs.append(pl.BlockSpec((1, d), lambda i: (0, 0)))
        args.append(final_g.reshape(1, d))
    return pl.pallas_call(
        functools.partial(_out_mm_kernel, n_in=len(os_), final=final),
        out_shape=jax.ShapeDtypeStruct((m, d), F32),
        grid=(m // tm,),
        in_specs=in_specs,
        out_specs=pl.BlockSpec((tm, d), lambda i: (i, 0)),
        compiler_params=pltpu.CompilerParams(
            dimension_semantics=("parallel",), vmem_limit_bytes=VMEM_LIMIT),
        name="out_mm",
    )(*args)


def _causal_conv_block(seq_rows, xp_ref, cw_ref, n_seq, rows, width, emit, bias_ref=None):
    lc = 512
    groups = rows // 8
    sub = lax.broadcasted_iota(jnp.int32, (groups, 8, lc), 1)
    for s in range(n_seq):
        xp_ref[s, HIST:HIST + rows, :] = seq_rows(s)
        for c0 in range(0, width, lc):
            cs = slice(c0, c0 + lc)
            x3 = xp_ref[s, :, cs].reshape(groups + HIST // 8, 8, lc)
            acc = x3[1:] * cw_ref[CONV_W - 1:CONV_W, cs]
            for d in range(1, CONV_W):
                rolled = pltpu.roll(x3, d, axis=1)
                tap = jnp.where(sub >= d, rolled[1:], rolled[:-1])
                acc = acc + tap * cw_ref[CONV_W - 1 - d:CONV_W - d, cs]
            acc = acc.reshape(rows, lc)
            if bias_ref is not None:
                acc = acc + bias_ref[:, cs]
            emit(s, cs, acc)
        xp_ref[s, 0:HIST, :] = xp_ref[s, rows:rows + HIST, :]


def _seq_masks(n, rows):
    ri = lax.broadcasted_iota(jnp.int32, (n, n), 0)
    ci = lax.broadcasted_iota(jnp.int32, (n, n), 1)
    if rows == n:
        same = ri >= 0
    else:
        sh = rows.bit_length() - 1
        same = (ri >> sh) == (ci >> sh)
    return ri, ci, same


NEUMANN_BLOCK = 8


def _inverse_masks(ri, ci, rows):
    assert rows >= NEUMANN_BLOCK
    blk = lambda k: (ri >> (k.bit_length() - 1)) == (ci >> (k.bit_length() - 1))
    eye = jnp.where(ri == ci, 1.0, 0.0).astype(F32)
    levels = []
    k = NEUMANN_BLOCK
    while k < rows:
        levels.append(blk(2 * k) & jnp.logical_not(blk(k)))
        k *= 2
    return eye, blk(NEUMANN_BLOCK), levels


def _unit_lower_inverse_each(as_, masks):
    eye, base, levels = masks
    a8 = [jnp.where(base, a, 0.0) for a in as_]
    a8s = _split_each(a8)
    p2 = _mm2_each(a8s, a8, "b")
    p2s = _split_each(p2)
    p4 = _mm2_each(p2s, p2, "b")
    a8p2 = _mm2_each(a8s, p2, "b")
    x = [(eye - a) + (p - ap) for a, p, ap in zip(a8, p2, a8p2)]
    t = [xi + xp for xi, xp in zip(x, _mm2_each(x, p4, "a"))]
    for lvl in levels:
        off = [jnp.where(lvl, a, 0.0) for a in as_]
        t = [ti - d for ti, d in zip(t, _mm2_each(_mm2_each(t, off, "a"), t, "b"))]
    return t


def _delta_kernel(*refs, n_seq, rows, has_state, single_step, n_alias):
    refs = refs[n_alias:]
    qkv_ref, z_ref, tail_ref = refs[:3]
    p = 3
    if has_state:
        s0_ref, c0_ref = refs[p], refs[p + 1]
        p += 2
    cw_ref, gp_ref, na_ref = refs[p:p + 3]
    o_ref, s_ref, cn_ref = refs[p + 3:p + 6]
    xp_ref, act_ref = refs[p + 6:p + 8]
    n = n_seq * rows
    c = pl.program_id(1)

    @pl.when(c == 0)
    def _init():
        if has_state:
            if not single_step:
                s_ref[...] = s0_ref[...]
            for s in range(n_seq):
                xp_ref[s, HIST - (CONV_W - 1):HIST, :] = c0_ref[s]
        else:
            s_ref[...] = jnp.zeros(s_ref.shape, F32)
            xp_ref[:, 0:HIST, :] = jnp.zeros((n_seq, HIST, QKV_A), F32)

    def emit(s, cs, acc):
        act_ref[s * rows:(s + 1) * rows, cs] = _silu(acc)

    def seq_rows(s):
        slab, off = divmod(s * rows, ROWS)
        return qkv_ref[slab, off:off + rows, :]

    _causal_conv_block(seq_rows, xp_ref, cw_ref, n_seq, rows, QKV_A, emit)

    @pl.when(c == pl.num_programs(1) - 1)
    def _conv_out():
        for s in range(n_seq):
            cn_ref[s] = xp_ref[s, HIST - (CONV_W - 1):HIST, :]

    n_slab = n // ROWS
    head_group = DELTA_M // n
    _, _, same_n = _seq_masks(n, rows)
    rn = lax.broadcasted_iota(jnp.int32, (n, n), 0)
    cn = lax.broadcasted_iota(jnp.int32, (n, n), 1)
    tail = tail_ref[...].reshape(n, TAIL_W)
    btile = _sigmoid(tail)
    gtile = -jnp.exp(gp_ref[0:1, :]) * jax.nn.softplus(tail + gp_ref[1:2, :])
    g_cum = _masked_sum(same_n & (rn >= cn), gtile)
    g_tot = _masked_sum(same_n, gtile)
    e_cum = jnp.exp(g_cum)
    e_rest = jnp.exp(g_tot - g_cum)
    g_cum_t = g_cum.T

    m = DELTA_M
    ri, ci, same = _seq_masks(m, rows)
    incl = same & (ri >= ci)
    strict = same & (ri > ci)
    inv_masks = _inverse_masks(ri, ci, rows)
    stack = lambda xs: xs[0] if len(xs) == 1 else jnp.concatenate(xs, axis=0)
    groups = [range(g * head_group, (g + 1) * head_group) for g in range(N_HEAD_A // head_group)]
    qn, kn, v, beta, gc, eg, er, g_row = ([] for _ in range(8))
    for heads in groups:
        q_h, k_h, v_h = [], [], []
        for h in heads:
            q = act_ref[:, h * DK_A:(h + 1) * DK_A]
            k = act_ref[:, KA + h * DK_A:KA + (h + 1) * DK_A]
            q_h.append(q * lax.rsqrt(jnp.sum(q * q, axis=-1, keepdims=True) + EPS) * (DK_A ** -0.5))
            k_h.append(k * lax.rsqrt(jnp.sum(k * k, axis=-1, keepdims=True) + EPS))
            v_h.append(act_ref[:, 2 * KA + h * DV_A:2 * KA + (h + 1) * DV_A])
        qn.append(stack(q_h))
        kn.append(stack(k_h))
        v.append(stack(v_h))
        beta.append(stack([btile[:, LANE_BETA + h:LANE_BETA + h + 1] for h in heads]))
        gc.append(stack([g_cum[:, LANE_A + h:LANE_A + h + 1] for h in heads]))
        eg.append(stack([e_cum[:, LANE_A + h:LANE_A + h + 1] for h in heads]))
        er.append(stack([e_rest[:, LANE_A + h:LANE_A + h + 1] for h in heads]))
        g_row.append(jnp.concatenate([g_cum_t[LANE_A + h:LANE_A + h + 1, :] for h in heads], axis=1))
    decay = [jnp.where(incl, jnp.exp(jnp.where(incl, c_ - r_, 0.0)), 0.0) for c_, r_ in zip(gc, g_row)]
    kb = [k_ * b_ for k_, b_ in zip(kn, beta)]
    prod = _mm3_each([stack([kb_, q_]) for kb_, q_ in zip(kb, qn)], kn, NT)
    a = [jnp.where(strict, p_[:m] * d_, 0.0) for p_, d_ in zip(prod, decay)]
    qk = [p_[m:] * d_ for p_, d_ in zip(prod, decay)]
    t_inv = _unit_lower_inverse_each(a, inv_masks)
    rhs = [jnp.concatenate([v_ * b_, kb_ * e_], axis=1) for v_, b_, kb_, e_ in zip(v, beta, kb, eg)]
    sol = _mm2_each(t_inv, rhs, "a")
    qg = [q_ * e_ for q_, e_ in zip(qn, eg)]
    kdec = [k_ * e_ for k_, e_ in zip(kn, er)]
    probs = [(g, i, h, s, slice(i * n + s * rows, i * n + (s + 1) * rows))
             for g, heads in enumerate(groups) for i, h in enumerate(heads) for s in range(n_seq)]
    state_ref = s0_ref if (has_state and single_step) else s_ref
    states = [state_ref[s, h] for (_, _, h, s, _) in probs]
    r = _mm1_each([stack([sol[g][sl, DV_A:], qg[g][sl]]) for (g, _, _, _, sl) in probs], states)
    v_new = [sol[g][sl, :DV_A] - r_[:rows] for (g, _, _, _, sl), r_ in zip(probs, r)]
    per_group = lambda xs: [stack([x for (g2, *_), x in zip(probs, xs) if g2 == g]) for g in range(len(groups))]
    v_new_g = per_group(v_new)
    o_inter_g = per_group([r_[rows:] for r_ in r])
    o = [oi + d for oi, d in zip(o_inter_g, _mm1_each(qk, v_new_g))]
    upd = _mm1_each([kdec[g][sl] for (g, _, _, _, sl) in probs], v_new, TN)
    for (g, i, h, s, sl), st, du in zip(probs, states, upd):
        g_last = g_tot[s * rows:s * rows + 1, LANE_A + h:LANE_A + h + 1]
        s_ref[s, h] = st * jnp.exp(g_last) + du
    for g, heads in enumerate(groups):
        for i, h in enumerate(heads):
            hl = slice(h * DV_A, (h + 1) * DV_A)
            oh = o[g][i * n:(i + 1) * n]
            oh = oh * lax.rsqrt(jnp.mean(oh * oh, axis=-1, keepdims=True) + EPS) * na_ref[...]
            oh = oh * _silu(z_ref[:, :, hl].reshape(n, DV_A))
            o_ref[:, :, hl] = oh.reshape(n_slab, ROWS, DV_A).astype(BF16)


def _layer_spec(block, layer):
    zeros = (0,) * (len(block) - 1)
    return pl.BlockSpec((None,) + tuple(block), lambda i, c: (layer, i) + zeros)


def _carry_outputs(prev, out_start):
    if prev is None:
        return [], [], {}
    specs = [pl.BlockSpec(memory_space=pl.ANY) for _ in prev]
    return list(prev), specs, {k: out_start + k for k in range(len(prev))}


def _delta(proj, layer, n_layers, s0, c0, prev, conv_w, gate_params, norm_a, n_batch, t_len):
    rows = min(ROWS, t_len)
    if t_len >= ROWS:
        n_seq, n_slab = DELTA_SEQS, DELTA_SEQS
        slabs = proj.reshape(n_batch, t_len // ROWS, ROWS, AB_N)
    else:
        n_seq, n_slab = ROWS // t_len, 1
        slabs = proj.reshape(n_batch * t_len // ROWS, 1, ROWS, AB_N)
    n = n_seq * rows
    nb = n_batch // n_seq
    nc = t_len // rows
    has_state = s0 is not None
    slab_spec = lambda width, col: pl.BlockSpec((n_slab, None, ROWS, width), lambda i, c: (i, c, 0, col))
    args, in_specs, aliases = _carry_outputs(prev, 1)
    n_alias = len(args)
    in_specs += [slab_spec(QKV_A, 0), slab_spec(VA, OFF_ZA // VA), slab_spec(TAIL_W, OFF_TAIL // TAIL_W)]
    args += [slabs, slabs, slabs]
    s_block = (n_seq, N_HEAD_A, DK_A, DV_A)
    c_block = (n_seq, CONV_W - 1, QKV_A)
    if has_state:
        in_specs += [_layer_spec(s_block, layer), _layer_spec(c_block, layer)]
        args += [s0, c0]
    in_specs += [
        pl.BlockSpec((CONV_W, QKV_A), lambda i, c: (0, 0)),
        pl.BlockSpec((2, TAIL_W), lambda i, c: (0, 0)),
        pl.BlockSpec((1, DV_A), lambda i, c: (0, 0)),
    ]
    args += [conv_w, gate_params, norm_a.reshape(1, DV_A)]
    o, s_all, conv_all = pl.pallas_call(
        functools.partial(_delta_kernel, n_seq=n_seq, rows=rows, has_state=has_state, single_step=nc == 1,
                          n_alias=n_alias),
        out_shape=(
            jax.ShapeDtypeStruct(slabs.shape[:3] + (VA,), BF16),
            jax.ShapeDtypeStruct((n_layers, n_batch, N_HEAD_A, DK_A, DV_A), F32),
            jax.ShapeDtypeStruct((n_layers, n_batch, CONV_W - 1, QKV_A), F32),
        ),
        grid=(nb, nc),
        in_specs=in_specs,
        out_specs=(
            slab_spec(VA, 0),
            _layer_spec(s_block, layer),
            _layer_spec(c_block, layer),
        ),
        scratch_shapes=[
            pltpu.VMEM((n_seq, HIST + rows, QKV_A), F32),
            pltpu.VMEM((n, QKV_A), F32),
        ],
        input_output_aliases=aliases,
        compiler_params=pltpu.CompilerParams(
            dimension_semantics=("parallel", "arbitrary"), vmem_limit_bytes=VMEM_LIMIT),
        name="delta",
    )(*args)
    return o.reshape(n_batch * t_len, VA), s_all, conv_all


def _gla_kernel(*refs, rows, chained, has_state, n_alias):
    refs = refs[n_alias:]
    q_ref, k_ref, v_ref, z_ref, tail_ref = refs[:5]
    p = 5
    if has_state:
        s0_ref = refs[p]
        p += 1
    wlr_ref, blr_ref, nb_ref = refs[p:p + 3]
    o_ref, s_ref = refs[p + 3:p + 5]
    n_slab = q_ref.shape[0]
    n = n_slab * ROWS
    n_sub = ROWS // rows
    c = pl.program_id(1)

    state_ref = s0_ref if (has_state and not chained) else s_ref

    @pl.when(c == 0)
    def _init():
        if not has_state:
            s_ref[...] = jnp.zeros(s_ref.shape, F32)
        elif chained:
            s_ref[...] = s0_ref[...]

    ri, ci, same = _seq_masks(ROWS, rows)
    incl = same & (ri >= ci)
    slabs = [slice(u * ROWS, (u + 1) * ROWS) for u in range(n_slab)]
    cumulate = lambda mask, x: jnp.concatenate(
        [_masked_sum(mask, x[sl]) for sl in slabs], axis=0)
    gk = jax.nn.log_sigmoid(_mm3(tail_ref[...].reshape(n, TAIL_W), wlr_ref[...]) + blr_ref[...]) / GLA_NORMALIZER
    b_loc = cumulate(incl, gk)
    q = q_ref[...].reshape(n, KB) * (DK_B ** -0.5)
    k = k_ref[...].reshape(n, KB)
    v_all = v_ref[...].reshape(n, VB)
    q_loc = q * jnp.exp(b_loc)
    heads = range(N_HEAD_B)
    hls = [slice(h * DK_B, (h + 1) * DK_B) for h in heads]
    vls = [slice(h * DV_B, (h + 1) * DV_B) for h in heads]

    def finish(h, o):
        o = o * lax.rsqrt(jnp.mean(o * o, axis=-1, keepdims=True) + EPS) * nb_ref[...]
        o = o * _silu(z_ref[:, :, vls[h]].reshape(n, DV_B))
        o_ref[:, :, vls[h]] = o.reshape(n_slab, ROWS, DV_B).astype(BF16)

    def key_column(row_vals):
        return jnp.broadcast_to(row_vals, (8, DK_B)).T[:, 0:1]

    if chained:
        tri = ri >= ci
        b_cum = cumulate(tri, gk)
        per_slab = lambda rws: jnp.concatenate([jnp.broadcast_to(r_, (ROWS, KB)) for r_ in rws], axis=0)
        b_last = [b_cum[sl.stop - 1:sl.stop] for sl in slabs]
        q_cum = q * jnp.exp(b_cum)
        kdec = k * jnp.exp(per_slab(b_last) - b_cum)
        row = lax.broadcasted_iota(jnp.int32, (n, KB), 0) & (ROWS - 1)
        k_rel = []
        for s in range(n_sub):
            start = [b_cum[sl.start + s * rows - 1:sl.start + s * rows] if s else jnp.zeros((1, KB), F32)
                     for sl in slabs]
            k_rel.append(k * jnp.exp(jnp.where(row < (s + 1) * rows, per_slab(start) - b_cum, 0.0)))
        probs = [(u, h) for u in range(n_slab) for h in heads]
        a_parts = _mm3_each(
            [q_loc[slabs[u].start + s * rows:slabs[u].start + (s + 1) * rows, hls[h]]
             for (u, h) in probs for s in range(n_sub)],
            [k_rel[s][slabs[u], hls[h]] for (u, h) in probs for s in range(n_sub)], NT)
        a = [jnp.where(tri, jnp.concatenate(a_parts[j * n_sub:(j + 1) * n_sub], axis=0), 0.0)
             for j in range(len(probs))]
        vs = [v_all[slabs[u], vls[h]] for (u, h) in probs]
        sts = [state_ref[u, h] for (u, h) in probs]
        o_state = _mm1_each([q_cum[slabs[u], hls[h]] for (u, h) in probs], sts)
        intra = _mm1_each(a, vs)
        upd = _mm1_each([kdec[slabs[u], hls[h]] for (u, h) in probs], vs, TN)
        for j, (u, h) in enumerate(probs):
            s_ref[u, h] = sts[j] * jnp.exp(key_column(b_last[u][:, hls[h]])) + upd[j]
        for h in heads:
            finish(h, jnp.concatenate([o_state[j] + intra[j] for j, (_, h2) in enumerate(probs) if h2 == h], axis=0))
    else:
        vs = [v_all[:, vl] for vl in vls]
        b_tot = _masked_sum(same, gk)
        kd = k * jnp.exp(-b_loc)
        kdec = k * jnp.exp(b_tot - b_loc)
        a = [jnp.where(incl, x, 0.0) for x in _mm3_each([q_loc[:, hl] for hl in hls], [kd[:, hl] for hl in hls], NT)]
        intra = _mm1_each(a, vs)
        probs = [(h, s, slice(s * rows, (s + 1) * rows)) for h in heads for s in range(n_sub)]
        sts = [state_ref[s, h] for (h, s, _) in probs]
        o_state = _mm1_each([q_loc[sl, hls[h]] for (h, _, sl) in probs], sts)
        upd = _mm1_each([kdec[sl, hls[h]] for (h, _, sl) in probs], [vs[h][sl] for (h, _, sl) in probs], TN)
        for (h, s, sl), st, du in zip(probs, sts, upd):
            s_ref[s, h] = st * jnp.exp(key_column(b_tot[s * rows:s * rows + 1, hls[h]])) + du
        for h in heads:
            finish(h, jnp.concatenate(o_state[h * n_sub:(h + 1) * n_sub], axis=0) + intra[h])


def _gla(proj, layer, n_layers, s0, prev, w_lr_pad, b_lr, norm_b, n_batch, t_len, chunk):
    rows = min(chunk, t_len)
    has_state = s0 is not None
    chained = t_len >= ROWS
    if chained:
        n_slab, n_state, nc = GLA_SEQS, GLA_SEQS, t_len // ROWS
        slabs = proj.reshape(n_batch, nc, ROWS, AB_N)
    else:
        n_slab, n_state, nc = 1, ROWS // t_len, 1
        slabs = proj.reshape(n_batch * t_len // ROWS, 1, ROWS, AB_N)
    nb = n_batch // n_state
    slab_spec = lambda width, col: pl.BlockSpec((n_slab, None, ROWS, width), lambda i, c: (i, c, 0, col))
    args, in_specs, aliases = _carry_outputs(prev, 1)
    n_alias = len(args)
    in_specs += [slab_spec(KB, OFF_QB // KB), slab_spec(KB, OFF_KB // KB), slab_spec(VB, OFF_VB // VB),
                 slab_spec(VB, OFF_ZB // VB), slab_spec(TAIL_W, OFF_TAIL // TAIL_W)]
    args += [slabs] * 5
    s_block = (n_state, N_HEAD_B, DK_B, DV_B)
    if has_state:
        in_specs.append(_layer_spec(s_block, layer))
        args.append(s0)
    in_specs += [
        pl.BlockSpec((TAIL_W, KB), lambda i, c: (0, 0)),
        pl.BlockSpec((1, KB), lambda i, c: (0, 0)),
        pl.BlockSpec((1, DV_B), lambda i, c: (0, 0)),
    ]
    args += [w_lr_pad, b_lr.reshape(1, KB), norm_b.reshape(1, DV_B)]
    o, s_all = pl.pallas_call(
        functools.partial(_gla_kernel, rows=rows, chained=chained, has_state=has_state, n_alias=n_alias),
        out_shape=(
            jax.ShapeDtypeStruct(slabs.shape[:3] + (VB,), BF16),
            jax.ShapeDtypeStruct((n_layers, n_batch, N_HEAD_B, DK_B, DV_B), F32),
        ),
        grid=(nb, nc),
        in_specs=in_specs,
        out_specs=(slab_spec(VB, 0), _layer_spec(s_block, layer)),
        input_output_aliases=aliases,
        compiler_params=pltpu.CompilerParams(
            dimension_semantics=("parallel", "arbitrary"), vmem_limit_bytes=VMEM_LIMIT),
        name="gla",
    )(*args)
    return o.reshape(n_batch * t_len, VB), s_all


def _lru_kernel(*refs, n_seq, rows, has_state, reset_first, n_alias):
    refs = refs[n_alias:]
    xb_ref, gate_ref = refs[:2]
    p = 2
    if has_state:
        h0_ref, c0_ref = refs[p], refs[p + 1]
        p += 2
    cw_ref, cb_ref, wg_ref, ba_ref, bx_ref, lam_ref = refs[p:p + 6]
    y_ref, h_ref, cn_ref = refs[p + 6:p + 9]
    xp_ref, xc_ref, a_ref, b_ref = refs[p + 9:p + 13]
    n = n_seq * rows
    c = pl.program_id(1)

    @pl.when(c == 0)
    def _init():
        if has_state:
            h_ref[...] = h0_ref[...]
            for s in range(n_seq):
                xp_ref[s, HIST - (CONV_W - 1):HIST, :] = c0_ref[s]
        else:
            h_ref[...] = jnp.zeros(h_ref.shape, F32)
            xp_ref[:, 0:HIST, :] = jnp.zeros((n_seq, HIST, W_LRU), F32)

    def emit(s, cs, acc):
        xc_ref[s * rows:(s + 1) * rows, cs] = acc

    _causal_conv_block(lambda s: xb_ref[s * rows:(s + 1) * rows, :], xp_ref, cw_ref, n_seq, rows, W_LRU,
                       emit, bias_ref=cb_ref)

    @pl.when(c == pl.num_programs(1) - 1)
    def _conv_out():
        for s in range(n_seq):
            cn_ref[s] = xp_ref[s, HIST - (CONV_W - 1):HIST, :]

    row = lax.broadcasted_iota(jnp.int32, (n, LRU_BW), 0)
    sub = lax.broadcasted_iota(jnp.int32, (n // 8, 8, LRU_BW), 1)
    scan_steps = [(d, sub >= d) for d in (1, 2, 4)]
    first_row = (row == 0) & (c == 0)
    for blk in range(LRU_BLOCKS):
        bl = slice(blk * LRU_BW, (blk + 1) * LRU_BW)
        xc = xc_ref[:, bl]
        pre = jnp.dot(xc.astype(BF16), wg_ref[blk], preferred_element_type=F32)
        t_r = jnp.tanh(pre[:, :LRU_BW] + ba_ref[:, bl])
        t_i = jnp.tanh(pre[:, LRU_BW:] + bx_ref[:, bl])
        gi = 0.5 * t_i + 0.5
        log_a = (t_r + 1.0) * (-0.5 * LRU_C * jax.nn.softplus(-lam_ref[:, bl]))
        a = jnp.exp(log_a)
        z = -jnp.tanh(log_a) * (a * a + 1.0)
        mult = z * lax.rsqrt(jnp.maximum(z, TINY))
        if reset_first:
            mult = jnp.where(first_row, 1.0, mult)
        b = mult * gi * xc
        a = a.reshape(n // 8, 8, LRU_BW)
        b = b.reshape(n // 8, 8, LRU_BW)
        for d, m in scan_steps:
            a_sh = pltpu.roll(a, d, axis=1)
            b_sh = pltpu.roll(b, d, axis=1)
            b = jnp.where(m, a * b_sh + b, b)
            a = jnp.where(m, a * a_sh, a)
        a_ref[:, bl] = a.reshape(n, LRU_BW)
        b_ref[:, bl] = b.reshape(n, LRU_BW)

    for s in range(n_seq):
        def body(g, hp, s=s):
            r0 = pl.multiple_of(s * rows + g * 8, 8)
            hs = a_ref[pl.ds(r0, 8), :] * hp + b_ref[pl.ds(r0, 8), :]
            b_ref[pl.ds(r0, 8), :] = hs
            return hs[7:8, :]

        h_ref[s] = lax.fori_loop(0, rows // 8, body, h_ref[s])

    lc = 512
    for c0 in range(0, W_LRU, lc):
        cs = slice(c0, c0 + lc)
        y_ref[:, cs] = (b_ref[:, cs] * _silu(gate_ref[:, cs])).astype(BF16)


def _lru(proj, layer, n_layers, h0, c0, prev, conv_w, conv_b, w_a, b_a, w_x, b_x, lam, n_batch, t_len,
         reset_first):
    has_state = h0 is not None
    if t_len >= 256:
        rows, n_seq = 256, 1
    else:
        rows, n_seq = t_len, 128 // t_len
    n = rows * n_seq
    nb = n_batch // n_seq
    nc = t_len // rows
    args, in_specs, aliases = _carry_outputs(prev, 1)
    n_alias = len(args)
    in_specs += [
        pl.BlockSpec((n, W_LRU), lambda i, c: (i * nc + c, 0)),
        pl.BlockSpec((n, W_LRU), lambda i, c: (i * nc + c, 1)),
    ]
    args += [proj, proj]
    h_block = (n_seq, 1, W_LRU)
    c_block = (n_seq, CONV_W - 1, W_LRU)
    if has_state:
        in_specs += [_layer_spec(h_block, layer), _layer_spec(c_block, layer)]
        args += [h0, c0]
    vec = pl.BlockSpec((1, W_LRU), lambda i, c: (0, 0))
    w_gates = (0.5 * jnp.concatenate([w_a, w_x], axis=-1)).astype(BF16)
    blk = pl.BlockSpec(w_gates.shape, lambda i, c: (0, 0, 0))
    in_specs += [pl.BlockSpec((CONV_W, W_LRU), lambda i, c: (0, 0)), vec, blk, vec, vec, vec]
    args += [conv_w, conv_b.reshape(1, W_LRU), w_gates, 0.5 * b_a.reshape(1, W_LRU),
             0.5 * b_x.reshape(1, W_LRU), lam.reshape(1, W_LRU)]
    return pl.pallas_call(
        functools.partial(_lru_kernel, n_seq=n_seq, rows=rows, has_state=has_state,
                          reset_first=reset_first, n_alias=n_alias),
        out_shape=(
            jax.ShapeDtypeStruct((n_batch * t_len, W_LRU), BF16),
            jax.ShapeDtypeStruct((n_layers, n_batch, 1, W_LRU), F32),
            jax.ShapeDtypeStruct((n_layers, n_batch, CONV_W - 1, W_LRU), F32),
        ),
        grid=(nb, nc),
        in_specs=in_specs,
        out_specs=(
            pl.BlockSpec((n, W_LRU), lambda i, c: (i * nc + c, 0)),
            _layer_spec(h_block, layer),
            _layer_spec(c_block, layer),
        ),
        input_output_aliases=aliases,
        scratch_shapes=[
            pltpu.VMEM((n_seq, HIST + rows, W_LRU), F32),
            pltpu.VMEM((n, W_LRU), F32),
            pltpu.VMEM((n, W_LRU), F32),
            pltpu.VMEM((n, W_LRU), F32),
        ],
        compiler_params=pltpu.CompilerParams(
            dimension_semantics=("parallel", "arbitrary"), vmem_limit_bytes=VMEM_LIMIT),
        name="lru",
    )(*args)


def _ab_w_in_layout(w):
    sizes = [QKV_A, N_HEAD_A, N_HEAD_A, VA, KB, KB, VB, LOWRANK, VB]
    offs = [0]
    for s in sizes:
        offs.append(offs[-1] + s)
    qkv, b_raw, a_raw, z_a, q_b, k_b, v_b, lr_b, z_b = [w[..., offs[i]:offs[i + 1]] for i in range(9)]
    used = OFF_TAIL + 2 * N_HEAD_A + LOWRANK
    pad = jnp.zeros(w.shape[:-1] + (AB_N - used,), w.dtype)
    return jnp.concatenate([qkv, z_a, q_b, k_b, v_b, z_b, b_raw, a_raw, lr_b, pad], axis=-1).astype(BF16)


def _cast_kernel(w_ref, o_ref):
    o_ref[...] = w_ref[...].astype(BF16)


def _layer_bf16(w, layer):
    _, r, c = w.shape
    tr = 512
    return pl.pallas_call(
        _cast_kernel,
        out_shape=jax.ShapeDtypeStruct((r, c), BF16),
        grid=(r // tr,),
        in_specs=[pl.BlockSpec((None, tr, c), lambda i: (layer, i, 0))],
        out_specs=pl.BlockSpec((tr, c), lambda i: (i, 0)),
        compiler_params=pltpu.CompilerParams(dimension_semantics=("parallel",)),
        name="w_cast",
    )(w)


def _tail_row(vals, lane0):
    return jnp.zeros((TAIL_W,), F32).at[lane0:lane0 + vals.shape[0]].set(vals.astype(F32))


def kernel(x_prompt, x_sample, state_delta, state_delta_conv, state_gla, state_lru, state_lru_conv,
           ab_norm, ab_w_in, ab_conv_w, ab_a_log, ab_dt_bias, ab_norm_a, ab_gla_w_lr, ab_gla_b_lr,
           ab_norm_b, ab_w_out, lru_norm, lru_w_in, lru_conv_w, lru_conv_b, lru_w_a, lru_b_a,
           lru_w_x, lru_b_x, lru_lambda, lru_w_out, final_norm):
    n_ab, n_lru = ab_norm.shape[0], lru_norm.shape[0]
    depth = n_ab + n_lru
    lru_h0 = state_lru.reshape(n_lru, -1, 1, W_LRU)
    groups = []
    for x, carried in ((x_prompt, False), (x_sample, True)):
        groups.append(dict(x=x.reshape(-1, D_MODEL), nb=x.shape[0], t=x.shape[1], carried=carried,
                           delta=None, gla=None, lru=None))

    ab_w_in_bf16 = _ab_w_in_layout(ab_w_in)
    for layer in range(depth):
        j = layer // 2
        last = layer == depth - 1
        if layer % 2 == 0:
            w_out = _layer_bf16(ab_w_out, j)
            gate_params = jnp.stack([_tail_row(ab_a_log[j], LANE_A), _tail_row(ab_dt_bias[j], LANE_A)])
            w_lr_pad = jnp.zeros((TAIL_W, KB), F32).at[LANE_LR:LANE_LR + LOWRANK].set(ab_gla_w_lr[j])
            for g in groups:
                proj = _norm_mm(g["x"], ab_norm[j], ab_w_in_bf16, AB_TN, layer=j)
                s_a, c_a, s_b = (state_delta, state_delta_conv, state_gla) if g["carried"] else (None,) * 3
                o_a, *g["delta"] = _delta(proj, j, n_ab, s_a, c_a, g["delta"], ab_conv_w[j], gate_params,
                                          ab_norm_a[j], g["nb"], g["t"])
                o_b, *g["gla"] = _gla(proj, j, n_ab, s_b, g["gla"], w_lr_pad, ab_gla_b_lr[j], ab_norm_b[j],
                                      g["nb"], g["t"], 16)
                g["x"] = _out_mm([o_a, o_b], w_out, g["x"], final_norm if last else None)
        else:
            w_in = _layer_bf16(lru_w_in, j)
            w_out = _layer_bf16(lru_w_out, j)
            for g in groups:
                proj = _norm_mm(g["x"], lru_norm[j], w_in, 1024)
                h0, c0 = (lru_h0, state_lru_conv) if g["carried"] else (None, None)
                y, *g["lru"] = _lru(proj, j, n_lru, h0, c0, g["lru"], lru_conv_w[j], lru_conv_b[j],
                                    lru_w_a[j], lru_b_a[j], lru_w_x[j], lru_b_x[j], lru_lambda[j],
                                    g["nb"], g["t"], reset_first=not g["carried"])
                g["x"] = _out_mm([y], w_out, g["x"], final_norm if last else None)

    outs = []
    for g, x in zip(groups, (x_prompt, x_sample)):
        h_all, lconv = g["lru"]
        outs.append([g["x"].reshape(x.shape), g["delta"][0], g["delta"][1], g["gla"][0],
                     h_all.reshape(n_lru, g["nb"], W_LRU), lconv])
    p, s = outs
    return (p[0], s[0], p[1], p[2], p[3], p[4], p[5], s[1], s[2], s[3], s[4], s[5])
```

```python
import functools

import jax
import jax.numpy as jnp
from jax import lax
from jax.experimental import pallas as pl
from jax.experimental.pallas import tpu as pltpu

F32 = jnp.float32
BF16 = jnp.bfloat16

D_MODEL = 2048
N_HEAD_A, DK_A, DV_A = 8, 128, 128
N_HEAD_B, DK_B, DV_B = 4, 128, 256
KA = N_HEAD_A * DK_A
VA = N_HEAD_A * DV_A
KB = N_HEAD_B * DK_B
VB = N_HEAD_B * DV_B
QKV_A = 2 * KA + VA
LOWRANK = 16
GLA_NORMALIZER = 16.0
W_LRU = D_MODEL
LRU_BLOCKS = 16
LRU_BW = W_LRU // LRU_BLOCKS
LRU_C = 8.0
CONV_W = 4
EPS = 1e-6
LANES = 128
TINY = 1.1754944e-38

OFF_QKV = 0
OFF_ZA = OFF_QKV + QKV_A
OFF_QB = OFF_ZA + VA
OFF_KB = OFF_QB + KB
OFF_VB = OFF_KB + KB
OFF_ZB = OFF_VB + VB
OFF_TAIL = OFF_ZB + VB
TAIL_W = 128
AB_TN = 1536
AB_N = 7680
LANE_BETA, LANE_A, LANE_LR = 0, N_HEAD_A, 2 * N_HEAD_A

ROWS = 64
DELTA_M = 128
DELTA_SEQS = 2
GLA_SEQS = 4
HIST = 8
VMEM_LIMIT = 56 * 1024 * 1024

NN = (((1,), (0,)), ((), ()))
NT = (((1,), (1,)), ((), ()))
TN = (((0,), (0,)), ((), ()))


def _masked_sum(mask, x):
    m = jnp.where(mask, 1.0, 0.0).astype(BF16)
    hi = x.astype(BF16)
    rest = x - hi.astype(F32)
    mid = rest.astype(BF16)
    lo = (rest - mid.astype(F32)).astype(BF16)
    if m.shape[1] % LANES:
        dot = functools.partial(jnp.dot, preferred_element_type=F32)
        return dot(m, hi) + (dot(m, mid) + dot(m, lo))
    return jnp.dot(jnp.concatenate([m, m, m], axis=1), jnp.concatenate([hi, mid, lo], axis=0),
                   preferred_element_type=F32)


def _split(a):
    hi = a.astype(BF16)
    return hi, (a - hi.astype(F32)).astype(BF16)


def _mm3(a, b, dims=NN):
    ah, al = a if isinstance(a, tuple) else _split(a)
    bh, bl = b if isinstance(b, tuple) else _split(b)
    (ca,), (cb,) = dims[0]
    dot = functools.partial(lax.dot_general, dimension_numbers=dims, preferred_element_type=F32)
    if ah.shape[ca] % LANES:
        return dot(ah, bh) + (dot(ah, bl) + dot(al, bh))
    return dot(jnp.concatenate([ah, ah, al], axis=ca), jnp.concatenate([bh, bl, bh], axis=cb))


def _split_each(xs):
    return [x if isinstance(x, tuple) else _split(x) for x in xs]


def _mm3_each(as_, bs, dims=NN):
    sa, sb = _split_each(as_), _split_each(bs)
    return [_mm3(a, b, dims) for a, b in zip(sa, sb)]


def _mm1_each(as_, bs, dims=NN):
    ca = [a.astype(BF16) for a in as_]
    cb = [b.astype(BF16) for b in bs]
    return [lax.dot_general(a, b, dims, preferred_element_type=F32) for a, b in zip(ca, cb)]


def _sigmoid(x):
    return 0.5 * jnp.tanh(0.5 * x) + 0.5


def _silu(x):
    h = 0.5 * x
    return h + h * jnp.tanh(h)


def _norm_mm_kernel(x_ref, g_ref, w_ref, o_ref, h_ref):
    @pl.when(pl.program_id(1) == 0)
    def _():
        x = x_ref[...]
        ms = jnp.mean(x * x, axis=-1, keepdims=True)
        h_ref[...] = (x * lax.rsqrt(ms + EPS) * g_ref[...]).astype(BF16)

    o_ref[...] = jnp.dot(h_ref[...], w_ref[...], preferred_element_type=F32)


def _norm_mm(x, g, w, tn):
    m, k = x.shape
    n = w.shape[1]
    tm = min(m, 1024)
    w_spec = pl.BlockSpec((k, tn), lambda i, j: (0, j))
    return pl.pallas_call(
        _norm_mm_kernel,
        out_shape=jax.ShapeDtypeStruct((m, n), F32),
        grid=(m // tm, n // tn),
        in_specs=[
            pl.BlockSpec((tm, k), lambda i, j: (i, 0)),
            pl.BlockSpec((1, k), lambda i, j: (0, 0)),
            w_spec,
        ],
        out_specs=pl.BlockSpec((tm, tn), lambda i, j: (i, j)),
        scratch_shapes=[pltpu.VMEM((tm, k), BF16)],
        compiler_params=pltpu.CompilerParams(
            dimension_semantics=("parallel", "arbitrary"), vmem_limit_bytes=VMEM_LIMIT),
        name="norm_mm",
    )(x, g.reshape(1, k), w)


def _out_mm_kernel(*refs, n_in, final):
    o_refs = refs[:n_in]
    w_ref, x_ref = refs[n_in], refs[n_in + 1]
    out_ref = refs[-1]
    acc = x_ref[...]
    k0 = 0
    for o_ref in o_refs:
        kk = o_ref.shape[1]
        acc = acc + jnp.dot(o_ref[...], w_ref[k0:k0 + kk, :], preferred_element_type=F32)
        k0 += kk
    if final:
        fg_ref = refs[n_in + 2]
        ms = jnp.mean(acc * acc, axis=-1, keepdims=True)
        acc = acc * lax.rsqrt(ms + EPS) * fg_ref[...]
    out_ref[...] = acc


def _out_mm(os_, w, x, final_g=None):
    m, d = x.shape
    tm = 512
    final = final_g is not None
    in_specs = [pl.BlockSpec((tm, o.shape[1]), lambda i: (i, 0)) for o in os_]
    in_specs += [pl.BlockSpec(w.shape, lambda i: (0, 0)), pl.BlockSpec((tm, d), lambda i: (i, 0))]
    args = list(os_) + [w, x]
    if final:
        in_specs.append(pl.BlockSpec((1, d), lambda i: (0, 0)))
        args.append(final_g.reshape(1, d))
    return pl.pallas_call(
        functools.partial(_out_mm_kernel, n_in=len(os_), final=final),
        out_shape=jax.ShapeDtypeStruct((m, d), F32),
        grid=(m // tm,),
        in_specs=in_specs,
        out_specs=pl.BlockSpec((tm, d), lambda i: (i, 0)),
        compiler_params=pltpu.CompilerParams(
            dimension_semantics=("parallel",), vmem_limit_bytes=VMEM_LIMIT),
        name="out_mm",
    )(*args)


def _causal_conv_block(seq_rows, xp_ref, cw_ref, n_seq, rows, width, emit, bias_ref=None):
    lc = 512
    groups = rows // 8
    sub = lax.broadcasted_iota(jnp.int32, (groups, 8, lc), 1)
    for s in range(n_seq):
        xp_ref[s, HIST:HIST + rows, :] = seq_rows(s)
        for c0 in range(0, width, lc):
            cs = slice(c0, c0 + lc)
            x3 = xp_ref[s, :, cs].reshape(groups + HIST // 8, 8, lc)
            acc = x3[1:] * cw_ref[CONV_W - 1:CONV_W, cs]
            for d in range(1, CONV_W):
                rolled = pltpu.roll(x3, d, axis=1)
                tap = jnp.where(sub >= d, rolled[1:], rolled[:-1])
                acc = acc + tap * cw_ref[CONV_W - 1 - d:CONV_W - d, cs]
            acc = acc.reshape(rows, lc)
            if bias_ref is not None:
                acc = acc + bias_ref[:, cs]
            emit(s, cs, acc)
        xp_ref[s, 0:HIST, :] = xp_ref[s, rows:rows + HIST, :]


def _seq_masks(n, rows):
    ri = lax.broadcasted_iota(jnp.int32, (n, n), 0)
    ci = lax.broadcasted_iota(jnp.int32, (n, n), 1)
    if rows == n:
        same = ri >= 0
    else:
        sh = rows.bit_length() - 1
        same = (ri >> sh) == (ci >> sh)
    return ri, ci, same


NEUMANN_BLOCK = 8


def _inverse_masks(ri, ci, rows):
    assert rows >= NEUMANN_BLOCK
    blk = lambda k: (ri >> (k.bit_length() - 1)) == (ci >> (k.bit_length() - 1))
    levels = []
    k = NEUMANN_BLOCK
    while k < rows:
        levels.append(blk(2 * k) & jnp.logical_not(blk(k)))
        k *= 2
    return blk(NEUMANN_BLOCK), levels


def _unit_lower_inverse_each(as_, masks):
    base, levels = masks
    a8 = [jnp.where(base, a, 0.0) for a in as_]
    p2 = _mm1_each(a8, a8)
    p4 = _mm1_each(p2, p2)
    a8p2 = _mm1_each(a8, p2)
    n = [(p - ap) - a for a, p, ap in zip(a8, p2, a8p2)]
    n = [ni + (p + d) for ni, p, d in zip(n, p4, _mm1_each(n, p4))]
    for lvl in levels:
        off = [jnp.where(lvl, a, 0.0) for a in as_]
        t_off = [o + d for o, d in zip(off, _mm1_each(n, off))]
        n = [ni - (to + d) for ni, to, d in zip(n, t_off, _mm1_each(t_off, n))]
    res = [-((a + ni) + d) for a, ni, d in zip(as_, n, _mm3_each(as_, n))]
    return [ni + (r_ + d) for ni, r_, d in zip(n, res, _mm1_each(n, res))]


def _delta_kernel(*refs, n_seq, rows, has_state, single_step, n_alias):
    refs = refs[n_alias:]
    qkv_ref, z_ref, tail_ref = refs[:3]
    p = 3
    if has_state:
        s0_ref, c0_ref = refs[p], refs[p + 1]
        p += 2
    cw_ref, gp_ref, na_ref = refs[p:p + 3]
    o_ref, s_ref, cn_ref = refs[p + 3:p + 6]
    xp_ref, act_ref = refs[p + 6:p + 8]
    n = n_seq * rows
    c = pl.program_id(1)

    @pl.when(c == 0)
    def _init():
        if has_state:
            if not single_step:
                s_ref[...] = s0_ref[...]
            for s in range(n_seq):
                xp_ref[s, HIST - (CONV_W - 1):HIST, :] = c0_ref[s]
        else:
            s_ref[...] = jnp.zeros(s_ref.shape, F32)
            xp_ref[:, 0:HIST, :] = jnp.zeros((n_seq, HIST, QKV_A), F32)

    def emit(s, cs, acc):
        act_ref[s * rows:(s + 1) * rows, cs] = _silu(acc)

    def seq_rows(s):
        slab, off = divmod(s * rows, ROWS)
        return qkv_ref[slab, off:off + rows, :]

    _causal_conv_block(seq_rows, xp_ref, cw_ref, n_seq, rows, QKV_A, emit)

    @pl.when(c == pl.num_programs(1) - 1)
    def _conv_out():
        for s in range(n_seq):
            cn_ref[s] = xp_ref[s, HIST - (CONV_W - 1):HIST, :]

    n_slab = n // ROWS
    head_group = DELTA_M // n
    _, _, same_n = _seq_masks(n, rows)
    rn = lax.broadcasted_iota(jnp.int32, (n, n), 0)
    cn = lax.broadcasted_iota(jnp.int32, (n, n), 1)
    tail = tail_ref[...].reshape(n, TAIL_W)
    btile = _sigmoid(tail)
    gtile = -jnp.exp(gp_ref[0:1, :]) * jax.nn.softplus(tail + gp_ref[1:2, :])
    g_cum = _masked_sum(same_n & (rn >= cn), gtile)
    g_tot = _masked_sum(same_n, gtile)
    e_cum = jnp.exp(g_cum)
    e_rest = jnp.exp(g_tot - g_cum)
    g_cum_t = g_cum.T

    m = DELTA_M
    ri, ci, same = _seq_masks(m, rows)
    incl = same & (ri >= ci)
    strict = same & (ri > ci)
    inv_masks = _inverse_masks(ri, ci, rows)
    stack = lambda xs: xs[0] if len(xs) == 1 else jnp.concatenate(xs, axis=0)
    groups = [range(g * head_group, (g + 1) * head_group) for g in range(N_HEAD_A // head_group)]
    qn, kn, v, beta, gc, eg, er, g_row = ([] for _ in range(8))
    for heads in groups:
        q_h, k_h, v_h = [], [], []
        for h in heads:
            q = act_ref[:, h * DK_A:(h + 1) * DK_A]
            k = act_ref[:, KA + h * DK_A:KA + (h + 1) * DK_A]
            q_h.append(q * lax.rsqrt(jnp.sum(q * q, axis=-1, keepdims=True) + EPS) * (DK_A ** -0.5))
            k_h.append(k * lax.rsqrt(jnp.sum(k * k, axis=-1, keepdims=True) + EPS))
            v_h.append(act_ref[:, 2 * KA + h * DV_A:2 * KA + (h + 1) * DV_A])
        qn.append(stack(q_h))
        kn.append(stack(k_h))
        v.append(stack(v_h))
        beta.append(stack([btile[:, LANE_BETA + h:LANE_BETA + h + 1] for h in heads]))
        gc.append(stack([g_cum[:, LANE_A + h:LANE_A + h + 1] for h in heads]))
        eg.append(stack([e_cum[:, LANE_A + h:LANE_A + h + 1] for h in heads]))
        er.append(stack([e_rest[:, LANE_A + h:LANE_A + h + 1] for h in heads]))
        g_row.append(jnp.concatenate([g_cum_t[LANE_A + h:LANE_A + h + 1, :] for h in heads], axis=1))
    decay = [jnp.where(incl, jnp.exp(jnp.where(incl, c_ - r_, 0.0)), 0.0) for c_, r_ in zip(gc, g_row)]
    kb = [k_ * b_ for k_, b_ in zip(kn, beta)]
    prod = _mm3_each([stack([kb_, q_]) for kb_, q_ in zip(kb, qn)], kn, NT)
    a = [jnp.where(strict, p_[:m] * d_, 0.0) for p_, d_ in zip(prod, decay)]
    qk = [p_[m:] * d_ for p_, d_ in zip(prod, decay)]
    n_inv = _unit_lower_inverse_each(a, inv_masks)
    rhs = [jnp.concatenate([v_ * b_, kb_ * e_], axis=1) for v_, b_, kb_, e_ in zip(v, beta, kb, eg)]
    sol = [r_ + d for r_, d in zip(rhs, _mm1_each(n_inv, rhs))]
    qg = [q_ * e_ for q_, e_ in zip(qn, eg)]
    kdec = [k_ * e_ for k_, e_ in zip(kn, er)]
    probs = [(g, i, h, s, slice(i * n + s * rows, i * n + (s + 1) * rows))
             for g, heads in enumerate(groups) for i, h in enumerate(heads) for s in range(n_seq)]
    state_ref = s0_ref if (has_state and single_step) else s_ref
    states = [state_ref[s, h] for (_, _, h, s, _) in probs]
    r = _mm1_each([stack([sol[g][sl, DV_A:], qg[g][sl]]) for (g, _, _, _, sl) in probs], states)
    v_new = [sol[g][sl, :DV_A] - r_[:rows] for (g, _, _, _, sl), r_ in zip(probs, r)]
    per_group = lambda xs: [stack([x for (g2, *_), x in zip(probs, xs) if g2 == g]) for g in range(len(groups))]
    v_new_g = per_group(v_new)
    o_inter_g = per_group([r_[rows:] for r_ in r])
    o = [oi + d for oi, d in zip(o_inter_g, _mm1_each(qk, v_new_g))]
    upd = _mm1_each([kdec[g][sl] for (g, _, _, _, sl) in probs], v_new, TN)
    for (g, i, h, s, sl), st, du in zip(probs, states, upd):
        g_last = g_tot[s * rows:s * rows + 1, LANE_A + h:LANE_A + h + 1]
        s_ref[s, h] = st * jnp.exp(g_last) + du
    for g, heads in enumerate(groups):
        for i, h in enumerate(heads):
            hl = slice(h * DV_A, (h + 1) * DV_A)
            oh = o[g][i * n:(i + 1) * n]
            oh = oh * lax.rsqrt(jnp.mean(oh * oh, axis=-1, keepdims=True) + EPS) * na_ref[...]
            oh = oh * _silu(z_ref[:, :, hl].reshape(n, DV_A))
            o_ref[:, :, hl] = oh.reshape(n_slab, ROWS, DV_A).astype(BF16)


def _layer_spec(block, layer):
    zeros = (0,) * (len(block) - 1)
    return pl.BlockSpec((None,) + tuple(block), lambda i, c: (layer, i) + zeros)


def _carry_outputs(prev, out_start):
    if prev is None:
        return [], [], {}
    specs = [pl.BlockSpec(memory_space=pl.ANY) for _ in prev]
    return list(prev), specs, {k: out_start + k for k in range(len(prev))}


def _delta(proj, layer, n_layers, s0, c0, prev, conv_w, gate_params, norm_a, n_batch, t_len):
    rows = min(ROWS, t_len)
    if t_len >= ROWS:
        n_seq, n_slab = DELTA_SEQS, DELTA_SEQS
        slabs = proj.reshape(n_batch, t_len // ROWS, ROWS, AB_N)
    else:
        n_seq, n_slab = ROWS // t_len, 1
        slabs = proj.reshape(n_batch * t_len // ROWS, 1, ROWS, AB_N)
    n = n_seq * rows
    nb = n_batch // n_seq
    nc = t_len // rows
    has_state = s0 is not None
    slab_spec = lambda width, col: pl.BlockSpec((n_slab, None, ROWS, width), lambda i, c: (i, c, 0, col))
    args, in_specs, aliases = _carry_outputs(prev, 1)
    n_alias = len(args)
    in_specs += [slab_spec(QKV_A, 0), slab_spec(VA, OFF_ZA // VA), slab_spec(TAIL_W, OFF_TAIL // TAIL_W)]
    args += [slabs, slabs, slabs]
    s_block = (n_seq, N_HEAD_A, DK_A, DV_A)
    c_block = (n_seq, CONV_W - 1, QKV_A)
    if has_state:
        in_specs += [_layer_spec(s_block, layer), _layer_spec(c_block, layer)]
        args += [s0, c0]
    in_specs += [
        pl.BlockSpec((CONV_W, QKV_A), lambda i, c: (0, 0)),
        pl.BlockSpec((2, TAIL_W), lambda i, c: (0, 0)),
        pl.BlockSpec((1, DV_A), lambda i, c: (0, 0)),
    ]
    args += [conv_w, gate_params, norm_a.reshape(1, DV_A)]
    o, s_all, conv_all = pl.pallas_call(
        functools.partial(_delta_kernel, n_seq=n_seq, rows=rows, has_state=has_state, single_step=nc == 1,
                          n_alias=n_alias),
        out_shape=(
            jax.ShapeDtypeStruct(slabs.shape[:3] + (VA,), BF16),
            jax.ShapeDtypeStruct((n_layers, n_batch, N_HEAD_A, DK_A, DV_A), F32),
            jax.ShapeDtypeStruct((n_layers, n_batch, CONV_W - 1, QKV_A), F32),
        ),
        grid=(nb, nc),
        in_specs=in_specs,
        out_specs=(
            slab_spec(VA, 0),
            _layer_spec(s_block, layer),
            _layer_spec(c_block, layer),
        ),
        scratch_shapes=[
            pltpu.VMEM((n_seq, HIST + rows, QKV_A), F32),
            pltpu.VMEM((n, QKV_A), F32),
        ],
        input_output_aliases=aliases,
        compiler_params=pltpu.CompilerParams(
            dimension_semantics=("parallel", "arbitrary"), vmem_limit_bytes=VMEM_LIMIT),
        name="delta",
    )(*args)
    return o.reshape(n_batch * t_len, VA), s_all, conv_all


def _gla_kernel(*refs, rows, chained, has_state, n_alias):
    refs = refs[n_alias:]
    q_ref, k_ref, v_ref, z_ref, tail_ref = refs[:5]
    p = 5
    if has_state:
        s0_ref = refs[p]
        p += 1
    wlr_ref, blr_ref, nb_ref = refs[p:p + 3]
    o_ref, s_ref = refs[p + 3:p + 5]
    n_slab = q_ref.shape[0]
    n = n_slab * ROWS
    n_sub = ROWS // rows
    c = pl.program_id(1)

    state_ref = s0_ref if (has_state and not chained) else s_ref

    @pl.when(c == 0)
    def _init():
        if not has_state:
            s_ref[...] = jnp.zeros(s_ref.shape, F32)
        elif chained:
            s_ref[...] = s0_ref[...]

    ri, ci, same = _seq_masks(ROWS, rows)
    incl = same & (ri >= ci)
    slabs = [slice(u * ROWS, (u + 1) * ROWS) for u in range(n_slab)]
    cumulate = lambda mask, x: jnp.concatenate(
        [_masked_sum(mask, x[sl]) for sl in slabs], axis=0)
    gk = jax.nn.log_sigmoid(_mm3(tail_ref[...].reshape(n, TAIL_W), wlr_ref[...]) + blr_ref[...]) / GLA_NORMALIZER
    b_loc = cumulate(incl, gk)
    q = q_ref[...].reshape(n, KB) * (DK_B ** -0.5)
    k = k_ref[...].reshape(n, KB)
    v_all = v_ref[...].reshape(n, VB)
    q_loc = q * jnp.exp(b_loc)
    heads = range(N_HEAD_B)
    hls = [slice(h * DK_B, (h + 1) * DK_B) for h in heads]
    vls = [slice(h * DV_B, (h + 1) * DV_B) for h in heads]

    def finish(h, o):
        o = o * lax.rsqrt(jnp.mean(o * o, axis=-1, keepdims=True) + EPS) * nb_ref[...]
        o = o * _silu(z_ref[:, :, vls[h]].reshape(n, DV_B))
        o_ref[:, :, vls[h]] = o.reshape(n_slab, ROWS, DV_B).astype(BF16)

    def key_column(row_vals):
        return jnp.broadcast_to(row_vals, (8, DK_B)).T[:, 0:1]

    if chained:
        tri = ri >= ci
        b_cum = cumulate(tri, gk)
        per_slab = lambda rws: jnp.concatenate([jnp.broadcast_to(r_, (ROWS, KB)) for r_ in rws], axis=0)
        b_last = [b_cum[sl.stop - 1:sl.stop] for sl in slabs]
        q_cum = q * jnp.exp(b_cum)
        kdec = k * jnp.exp(per_slab(b_last) - b_cum)
        row = lax.broadcasted_iota(jnp.int32, (n, KB), 0) & (ROWS - 1)
        k_rel = []
        for s in range(n_sub):
            start = [b_cum[sl.start + s * rows - 1:sl.start + s * rows] if s else jnp.zeros((1, KB), F32)
                     for sl in slabs]
            k_rel.append(k * jnp.exp(jnp.where(row < (s + 1) * rows, per_slab(start) - b_cum, 0.0)))
        probs = [(u, h) for u in range(n_slab) for h in heads]
        a_parts = _mm3_each(
            [q_loc[slabs[u].start + s * rows:slabs[u].start + (s + 1) * rows, hls[h]]
             for (u, h) in probs for s in range(n_sub)],
            [k_rel[s][slabs[u], hls[h]] for (u, h) in probs for s in range(n_sub)], NT)
        a = [jnp.where(tri, jnp.concatenate(a_parts[j * n_sub:(j + 1) * n_sub], axis=0), 0.0)
             for j in range(len(probs))]
        vs = [v_all[slabs[u], vls[h]] for (u, h) in probs]
        sts = [state_ref[u, h] for (u, h) in probs]
        o_state = _mm1_each([q_cum[slabs[u], hls[h]] for (u, h) in probs], sts)
        intra = _mm1_each(a, vs)
        upd = _mm1_each([kdec[slabs[u], hls[h]] for (u, h) in probs], vs, TN)
        for j, (u, h) in enumerate(probs):
            s_ref[u, h] = sts[j] * jnp.exp(key_column(b_last[u][:, hls[h]])) + upd[j]
        for h in heads:
            finish(h, jnp.concatenate([o_state[j] + intra[j] for j, (_, h2) in enumerate(probs) if h2 == h], axis=0))
    else:
        vs = [v_all[:, vl] for vl in vls]
        b_tot = _masked_sum(same, gk)
        kd = k * jnp.exp(-b_loc)
        kdec = k * jnp.exp(b_tot - b_loc)
        a = [jnp.where(incl, x, 0.0) for x in _mm3_each([q_loc[:, hl] for hl in hls], [kd[:, hl] for hl in hls], NT)]
        intra = _mm1_each(a, vs)
        probs = [(h, s, slice(s * rows, (s + 1) * rows)) for h in heads for s in range(n_sub)]
        sts = [state_ref[s, h] for (h, s, _) in probs]
        o_state = _mm1_each([q_loc[sl, hls[h]] for (h, _, sl) in probs], sts)
        upd = _mm1_each([kdec[sl, hls[h]] for (h, _, sl) in probs], [vs[h][sl] for (h, _, sl) in probs], TN)
        for (h, s, sl), st, du in zip(probs, sts, upd):
            s_ref[s, h] = st * jnp.exp(key_column(b_tot[s * rows:s * rows + 1, hls[h]])) + du
        for h in heads:
            finish(h, jnp.concatenate(o_state[h * n_sub:(h + 1) * n_sub], axis=0) + intra[h])


def _gla(proj, layer, n_layers, s0, prev, w_lr_pad, b_lr, norm_b, n_batch, t_len, chunk):
    rows = min(chunk, t_len)
    has_state = s0 is not None
    chained = t_len >= ROWS
    if chained:
        n_slab, n_state, nc = GLA_SEQS, GLA_SEQS, t_len // ROWS
        slabs = proj.reshape(n_batch, nc, ROWS, AB_N)
    else:
        n_slab, n_state, nc = 1, ROWS // t_len, 1
        slabs = proj.reshape(n_batch * t_len // ROWS, 1, ROWS, AB_N)
    nb = n_batch // n_state
    slab_spec = lambda width, col: pl.BlockSpec((n_slab, None, ROWS, width), lambda i, c: (i, c, 0, col))
    args, in_specs, aliases = _carry_outputs(prev, 1)
    n_alias = len(args)
    in_specs += [slab_spec(KB, OFF_QB // KB), slab_spec(KB, OFF_KB // KB), slab_spec(VB, OFF_VB // VB),
                 slab_spec(VB, OFF_ZB // VB), slab_spec(TAIL_W, OFF_TAIL // TAIL_W)]
    args += [slabs] * 5
    s_block = (n_state, N_HEAD_B, DK_B, DV_B)
    if has_state:
        in_specs.append(_layer_spec(s_block, layer))
        args.append(s0)
    in_specs += [
        pl.BlockSpec((TAIL_W, KB), lambda i, c: (0, 0)),
        pl.BlockSpec((1, KB), lambda i, c: (0, 0)),
        pl.BlockSpec((1, DV_B), lambda i, c: (0, 0)),
    ]
    args += [w_lr_pad, b_lr.reshape(1, KB), norm_b.reshape(1, DV_B)]
    o, s_all = pl.pallas_call(
        functools.partial(_gla_kernel, rows=rows, chained=chained, has_state=has_state, n_alias=n_alias),
        out_shape=(
            jax.ShapeDtypeStruct(slabs.shape[:3] + (VB,), BF16),
            jax.ShapeDtypeStruct((n_layers, n_batch, N_HEAD_B, DK_B, DV_B), F32),
        ),
        grid=(nb, nc),
        in_specs=in_specs,
        out_specs=(slab_spec(VB, 0), _layer_spec(s_block, layer)),
        input_output_aliases=aliases,
        compiler_params=pltpu.CompilerParams(
            dimension_semantics=("parallel", "arbitrary"), vmem_limit_bytes=VMEM_LIMIT),
        name="gla",
    )(*args)
    return o.reshape(n_batch * t_len, VB), s_all


def _lru_kernel(*refs, n_seq, rows, has_state, reset_first, n_alias):
    refs = refs[n_alias:]
    xb_ref, gate_ref = refs[:2]
    p = 2
    if has_state:
        h0_ref, c0_ref = refs[p], refs[p + 1]
        p += 2
    cw_ref, cb_ref, wg_ref, ba_ref, bx_ref, lam_ref = refs[p:p + 6]
    y_ref, h_ref, cn_ref = refs[p + 6:p + 9]
    xp_ref, xc_ref, a_ref, b_ref = refs[p + 9:p + 13]
    n = n_seq * rows
    c = pl.program_id(1)

    @pl.when(c == 0)
    def _init():
        if has_state:
            h_ref[...] = h0_ref[...]
            for s in range(n_seq):
                xp_ref[s, HIST - (CONV_W - 1):HIST, :] = c0_ref[s]
        else:
            h_ref[...] = jnp.zeros(h_ref.shape, F32)
            xp_ref[:, 0:HIST, :] = jnp.zeros((n_seq, HIST, W_LRU), F32)

    def emit(s, cs, acc):
        xc_ref[s * rows:(s + 1) * rows, cs] = acc

    _causal_conv_block(lambda s: xb_ref[s * rows:(s + 1) * rows, :], xp_ref, cw_ref, n_seq, rows, W_LRU,
                       emit, bias_ref=cb_ref)

    @pl.when(c == pl.num_programs(1) - 1)
    def _conv_out():
        for s in range(n_seq):
            cn_ref[s] = xp_ref[s, HIST - (CONV_W - 1):HIST, :]

    row = lax.broadcasted_iota(jnp.int32, (n, LRU_BW), 0)
    sub = lax.broadcasted_iota(jnp.int32, (n // 8, 8, LRU_BW), 1)
    scan_steps = [(d, sub >= d) for d in (1, 2, 4)]
    first_row = (row == 0) & (c == 0)
    for blk in range(LRU_BLOCKS):
        bl = slice(blk * LRU_BW, (blk + 1) * LRU_BW)
        xc = xc_ref[:, bl]
        pre = jnp.dot(xc.astype(BF16), wg_ref[blk], preferred_element_type=F32)
        t_r = jnp.tanh(pre[:, :LRU_BW] + ba_ref[:, bl])
        t_i = jnp.tanh(pre[:, LRU_BW:] + bx_ref[:, bl])
        gi = 0.5 * t_i + 0.5
        log_a = (t_r + 1.0) * (-0.5 * LRU_C * jax.nn.softplus(-lam_ref[:, bl]))
        a = jnp.exp(log_a)
        z = -jnp.tanh(log_a) * (a * a + 1.0)
        mult = z * lax.rsqrt(jnp.maximum(z, TINY))
        if reset_first:
            mult = jnp.where(first_row, 1.0, mult)
        b = mult * gi * xc
        a = a.reshape(n // 8, 8, LRU_BW)
        b = b.reshape(n // 8, 8, LRU_BW)
        for d, m in scan_steps:
            a_sh = pltpu.roll(a, d, axis=1)
            b_sh = pltpu.roll(b, d, axis=1)
            b = jnp.where(m, a * b_sh + b, b)
            a = jnp.where(m, a * a_sh, a)
        a_ref[:, bl] = a.reshape(n, LRU_BW)
        b_ref[:, bl] = b.reshape(n, LRU_BW)

    for s in range(n_seq):
        def body(g, hp, s=s):
            r0 = pl.multiple_of(s * rows + g * 8, 8)
            hs = a_ref[pl.ds(r0, 8), :] * hp + b_ref[pl.ds(r0, 8), :]
            b_ref[pl.ds(r0, 8), :] = hs
            return hs[7:8, :]

        h_ref[s] = lax.fori_loop(0, rows // 8, body, h_ref[s])

    lc = 512
    for c0 in range(0, W_LRU, lc):
        cs = slice(c0, c0 + lc)
        y_ref[:, cs] = (b_ref[:, cs] * _silu(gate_ref[:, cs])).astype(BF16)


def _lru(proj, layer, n_layers, h0, c0, prev, conv_w, conv_b, w_a, b_a, w_x, b_x, lam, n_batch, t_len,
         reset_first):
    has_state = h0 is not None
    if t_len >= 256:
        rows, n_seq = 256, 1
    else:
        rows, n_seq = t_len, 128 // t_len
    n = rows * n_seq
    nb = n_batch // n_seq
    nc = t_len // rows
    args, in_specs, aliases = _carry_outputs(prev, 1)
    n_alias = len(args)
    in_specs += [
        pl.BlockSpec((n, W_LRU), lambda i, c: (i * nc + c, 0)),
        pl.BlockSpec((n, W_LRU), lambda i, c: (i * nc + c, 1)),
    ]
    args += [proj, proj]
    h_block = (n_seq, 1, W_LRU)
    c_block = (n_seq, CONV_W - 1, W_LRU)
    if has_state:
        in_specs += [_layer_spec(h_block, layer), _layer_spec(c_block, layer)]
        args += [h0, c0]
    vec = pl.BlockSpec((1, W_LRU), lambda i, c: (0, 0))
    w_gates = (0.5 * jnp.concatenate([w_a, w_x], axis=-1)).astype(BF16)
    blk = pl.BlockSpec(w_gates.shape, lambda i, c: (0, 0, 0))
    in_specs += [pl.BlockSpec((CONV_W, W_LRU), lambda i, c: (0, 0)), vec, blk, vec, vec, vec]
    args += [conv_w, conv_b.reshape(1, W_LRU), w_gates, 0.5 * b_a.reshape(1, W_LRU),
             0.5 * b_x.reshape(1, W_LRU), lam.reshape(1, W_LRU)]
    return pl.pallas_call(
        functools.partial(_lru_kernel, n_seq=n_seq, rows=rows, has_state=has_state,
                          reset_first=reset_first, n_alias=n_alias),
        out_shape=(
            jax.ShapeDtypeStruct((n_batch * t_len, W_LRU), BF16),
            jax.ShapeDtypeStruct((n_layers, n_batch, 1, W_LRU), F32),
            jax.ShapeDtypeStruct((n_layers, n_batch, CONV_W - 1, W_LRU), F32),
        ),
        grid=(nb, nc),
        in_specs=in_specs,
        out_specs=(
            pl.BlockSpec((n, W_LRU), lambda i, c: (i * nc + c, 0)),
            _layer_spec(h_block, layer),
            _layer_spec(c_block, layer),
        ),
        input_output_aliases=aliases,
        scratch_shapes=[
            pltpu.VMEM((n_seq, HIST + rows, W_LRU), F32),
            pltpu.VMEM((n, W_LRU), F32),
            pltpu.VMEM((n, W_LRU), F32),
            pltpu.VMEM((n, W_LRU), F32),
        ],
        compiler_params=pltpu.CompilerParams(
            dimension_semantics=("parallel", "arbitrary"), vmem_limit_bytes=VMEM_LIMIT),
        name="lru",
    )(*args)


def _ab_w_in_kernel(w_ref, o_ref):
    sizes = [QKV_A, N_HEAD_A, N_HEAD_A, VA, KB, KB, VB, LOWRANK, VB]
    offs = [0]
    for s in sizes:
        offs.append(offs[-1] + s)
    w = w_ref[...]
    qkv, b_raw, a_raw, z_a, q_b, k_b, v_b, lr_b, z_b = [w[:, offs[i]:offs[i + 1]] for i in range(9)]
    used = OFF_TAIL + 2 * N_HEAD_A + LOWRANK
    pad = jnp.zeros((w.shape[0], AB_N - used), w.dtype)
    o_ref[...] = jnp.concatenate([qkv, z_a, q_b, k_b, v_b, z_b, b_raw, a_raw, lr_b, pad], axis=1)


def _ab_w_in_layout(w, layer):
    _, d, n = w.shape
    tr = 256
    return pl.pallas_call(
        _ab_w_in_kernel,
        out_shape=jax.ShapeDtypeStruct((d, AB_N), BF16),
        grid=(d // tr,),
        in_specs=[pl.BlockSpec((None, tr, n), lambda i: (layer, i, 0))],
        out_specs=pl.BlockSpec((tr, AB_N), lambda i: (i, 0)),
        compiler_params=pltpu.CompilerParams(dimension_semantics=("parallel",)),
        name="w_in_layout",
    )(w)


def _cast_kernel(w_ref, o_ref):
    o_ref[...] = w_ref[...].astype(BF16)


def _layer_bf16(w, layer):
    _, r, c = w.shape
    tr = 512
    return pl.pallas_call(
        _cast_kernel,
        out_shape=jax.ShapeDtypeStruct((r, c), BF16),
        grid=(r // tr,),
        in_specs=[pl.BlockSpec((None, tr, c), lambda i: (layer, i, 0))],
        out_specs=pl.BlockSpec((tr, c), lambda i: (i, 0)),
        compiler_params=pltpu.CompilerParams(dimension_semantics=("parallel",)),
        name="w_cast",
    )(w)


def _tail_row(vals, lane0):
    return jnp.zeros((TAIL_W,), F32).at[lane0:lane0 + vals.shape[0]].set(vals.astype(F32))


def kernel(x_prompt, x_sample, state_delta, state_delta_conv, state_gla, state_lru, state_lru_conv,
           ab_norm, ab_w_in, ab_conv_w, ab_a_log, ab_dt_bias, ab_norm_a, ab_gla_w_lr, ab_gla_b_lr,
           ab_norm_b, ab_w_out, lru_norm, lru_w_in, lru_conv_w, lru_conv_b, lru_w_a, lru_b_a,
           lru_w_x, lru_b_x, lru_lambda, lru_w_out, final_norm):
    n_ab, n_lru = ab_norm.shape[0], lru_norm.shape[0]
    depth = n_ab + n_lru
    lru_h0 = state_lru.reshape(n_lru, -1, 1, W_LRU)
    groups = []
    for x, carried in ((x_prompt, False), (x_sample, True)):
        groups.append(dict(x=x.reshape(-1, D_MODEL), nb=x.shape[0], t=x.shape[1], carried=carried,
                           delta=None, gla=None, lru=None))

    ab_w_in_bf16 = ab_w_in.astype(BF16)
    for layer in range(depth):
        j = layer // 2
        last = layer == depth - 1
        if layer % 2 == 0:
            w_in = _ab_w_in_layout(ab_w_in_bf16, j)
            w_out = _layer_bf16(ab_w_out, j)
            gate_params = jnp.stack([_tail_row(ab_a_log[j], LANE_A), _tail_row(ab_dt_bias[j], LANE_A)])
            w_lr_pad = jnp.zeros((TAIL_W, KB), F32).at[LANE_LR:LANE_LR + LOWRANK].set(ab_gla_w_lr[j])
            for g in groups:
                proj = _norm_mm(g["x"], ab_norm[j], w_in, AB_TN)
                s_a, c_a, s_b = (state_delta, state_delta_conv, state_gla) if g["carried"] else (None,) * 3
                o_a, *g["delta"] = _delta(proj, j, n_ab, s_a, c_a, g["delta"], ab_conv_w[j], gate_params,
                                          ab_norm_a[j], g["nb"], g["t"])
                o_b, *g["gla"] = _gla(proj, j, n_ab, s_b, g["gla"], w_lr_pad, ab_gla_b_lr[j], ab_norm_b[j],
                                      g["nb"], g["t"], 16)
                g["x"] = _out_mm([o_a, o_b], w_out, g["x"], final_norm if last else None)
        else:
            w_in = _layer_bf16(lru_w_in, j)
            w_out = _layer_bf16(lru_w_out, j)
            for g in groups:
                proj = _norm_mm(g["x"], lru_norm[j], w_in, 1024)
                h0, c0 = (lru_h0, state_lru_conv) if g["carried"] else (None, None)
                y, *g["lru"] = _lru(proj, j, n_lru, h0, c0, g["lru"], lru_conv_w[j], lru_conv_b[j],
                                    lru_w_a[j], lru_b_a[j], lru_w_x[j], lru_b_x[j], lru_lambda[j],
                                    g["nb"], g["t"], reset_first=not g["carried"])
                g["x"] = _out_mm([y], w_out, g["x"], final_norm if last else None)

    outs = []
    for g, x in zip(groups, (x_prompt, x_sample)):
        h_all, lconv = g["lru"]
        outs.append([g["x"].reshape(x.shape), g["delta"][0], g["delta"][1], g["gla"][0],
                     h_all.reshape(n_lru, g["nb"], W_LRU), lconv])
    p, s = outs
    return (p[0], s[0], p[1], p[2], p[3], p[4], p[5], s[1], s[2], s[3], s[4], s[5])
```

```python
import functools

import jax
import jax.numpy as jnp
from jax import lax
from jax.experimental import pallas as pl
from jax.experimental.pallas import tpu as pltpu

F32 = jnp.float32
BF16 = jnp.bfloat16

D_MODEL = 2048
N_HEAD_A, DK_A, DV_A = 8, 128, 128
N_HEAD_B, DK_B, DV_B = 4, 128, 256
KA = N_HEAD_A * DK_A
VA = N_HEAD_A * DV_A
KB = N_HEAD_B * DK_B
VB = N_HEAD_B * DV_B
QKV_A = 2 * KA + VA
LOWRANK = 16
GLA_NORMALIZER = 16.0
W_LRU = D_MODEL
LRU_BLOCKS = 16
LRU_BW = W_LRU // LRU_BLOCKS
LRU_C = 8.0
CONV_W = 4
EPS = 1e-6
LANES = 128
TINY = 1.1754944e-38

OFF_QKV = 0
OFF_ZA = OFF_QKV + QKV_A
OFF_QB = OFF_ZA + VA
OFF_KB = OFF_QB + KB
OFF_VB = OFF_KB + KB
OFF_ZB = OFF_VB + VB
OFF_TAIL = OFF_ZB + VB
TAIL_W = 128
AB_TN = 1536
AB_N = 7680
LANE_BETA, LANE_A, LANE_LR = 0, N_HEAD_A, 2 * N_HEAD_A

ROWS = 64
DELTA_M = 128
DELTA_SEQS = 2
GLA_SEQS = 4
LRU_ROWS = 512
LRU_SHORT_ROWS = 128
HIST = 8
VMEM_LIMIT = 56 * 1024 * 1024

NN = (((1,), (0,)), ((), ()))
NT = (((1,), (1,)), ((), ()))
TN = (((0,), (0,)), ((), ()))


def _masked_sum(mask, x):
    m = jnp.where(mask, 1.0, 0.0).astype(BF16)
    hi = x.astype(BF16)
    rest = x - hi.astype(F32)
    mid = rest.astype(BF16)
    lo = (rest - mid.astype(F32)).astype(BF16)
    if m.shape[1] % LANES:
        dot = functools.partial(jnp.dot, preferred_element_type=F32)
        return dot(m, hi) + (dot(m, mid) + dot(m, lo))
    return jnp.dot(jnp.concatenate([m, m, m], axis=1), jnp.concatenate([hi, mid, lo], axis=0),
                   preferred_element_type=F32)


def _split(a):
    hi = a.astype(BF16)
    return hi, (a - hi.astype(F32)).astype(BF16)


def _mm3(a, b, dims=NN):
    ah, al = a if isinstance(a, tuple) else _split(a)
    bh, bl = b if isinstance(b, tuple) else _split(b)
    (ca,), (cb,) = dims[0]
    dot = functools.partial(lax.dot_general, dimension_numbers=dims, preferred_element_type=F32)
    if ah.shape[ca] % LANES:
        return dot(ah, bh) + (dot(ah, bl) + dot(al, bh))
    return dot(jnp.concatenate([ah, ah, al], axis=ca), jnp.concatenate([bh, bl, bh], axis=cb))


def _split_each(xs):
    return [x if isinstance(x, tuple) else _split(x) for x in xs]


def _mm3_each(as_, bs, dims=NN):
    sa, sb = _split_each(as_), _split_each(bs)
    return [_mm3(a, b, dims) for a, b in zip(sa, sb)]


def _mm1_each(as_, bs, dims=NN):
    ca = [a.astype(BF16) for a in as_]
    cb = [b.astype(BF16) for b in bs]
    return [lax.dot_general(a, b, dims, preferred_element_type=F32) for a, b in zip(ca, cb)]


def _sigmoid(x):
    return 0.5 * jnp.tanh(0.5 * x) + 0.5


def _silu(x):
    h = 0.5 * x
    return h + h * jnp.tanh(h)


def _norm_mm_kernel(x_ref, g_ref, w_ref, o_ref, h_ref):
    @pl.when(pl.program_id(1) == 0)
    def _():
        x = x_ref[...]
        ms = jnp.mean(x * x, axis=-1, keepdims=True)
        h_ref[...] = (x * lax.rsqrt(ms + EPS) * g_ref[...]).astype(BF16)

    o_ref[...] = jnp.dot(h_ref[...], w_ref[...], preferred_element_type=F32)


def _norm_mm(x, g, w, tn):
    m, k = x.shape
    n = w.shape[1]
    tm = min(m, 1024)
    w_spec = pl.BlockSpec((k, tn), lambda i, j: (0, j))
    return pl.pallas_call(
        _norm_mm_kernel,
        out_shape=jax.ShapeDtypeStruct((m, n), F32),
        grid=(m // tm, n // tn),
        in_specs=[
            pl.BlockSpec((tm, k), lambda i, j: (i, 0)),
            pl.BlockSpec((1, k), lambda i, j: (0, 0)),
            w_spec,
        ],
        out_specs=pl.BlockSpec((tm, tn), lambda i, j: (i, j)),
        scratch_shapes=[pltpu.VMEM((tm, k), BF16)],
        compiler_params=pltpu.CompilerParams(
            dimension_semantics=("parallel", "arbitrary"), vmem_limit_bytes=VMEM_LIMIT),
        name="norm_mm",
    )(x, g.reshape(1, k), w)


def _out_mm_kernel(*refs, n_in, final):
    o_refs = refs[:n_in]
    w_ref, x_ref = refs[n_in], refs[n_in + 1]
    out_ref = refs[-1]
    acc = x_ref[...]
    k0 = 0
    for o_ref in o_refs:
        kk = o_ref.shape[1]
        acc = acc + jnp.dot(o_ref[...], w_ref[k0:k0 + kk, :], preferred_element_type=F32)
        k0 += kk
    if final:
        fg_ref = refs[n_in + 2]
        ms = jnp.mean(acc * acc, axis=-1, keepdims=True)
        acc = acc * lax.rsqrt(ms + EPS) * fg_ref[...]
    out_ref[...] = acc


def _out_mm(os_, w, x, final_g=None):
    m, d = x.shape
    tm = 512
    final = final_g is not None
    in_specs = [pl.BlockSpec((tm, o.shape[1]), lambda i: (i, 0)) for o in os_]
    in_specs += [pl.BlockSpec(w.shape, lambda i: (0, 0)), pl.BlockSpec((tm, d), lambda i: (i, 0))]
    args = list(os_) + [w, x]
    if final:
        in_specs.append(pl.BlockSpec((1, d), lambda i: (0, 0)))
        args.append(final_g.reshape(1, d))
    return pl.pallas_call(
        functools.partial(_out_mm_kernel, n_in=len(os_), final=final),
        out_shape=jax.ShapeDtypeStruct((m, d), F32),
        grid=(m // tm,),
        in_specs=in_specs,
        out_specs=pl.BlockSpec((tm, d), lambda i: (i, 0)),
        compiler_params=pltpu.CompilerParams(
            dimension_semantics=("parallel",), vmem_limit_bytes=VMEM_LIMIT),
        name="out_mm",
    )(*args)


def _causal_conv_block(seq_rows, xp_ref, cw_ref, n_seq, rows, width, emit, bias_ref=None):
    lc = 512
    groups = rows // 8
    sub = lax.broadcasted_iota(jnp.int32, (groups, 8, lc), 1)
    for s in range(n_seq):
        xp_ref[s, HIST:HIST + rows, :] = seq_rows(s)
        for c0 in range(0, width, lc):
            cs = slice(c0, c0 + lc)
            x3 = xp_ref[s, :, cs].reshape(groups + HIST // 8, 8, lc)
            acc = x3[1:] * cw_ref[CONV_W - 1:CONV_W, cs]
            for d in range(1, CONV_W):
                rolled = pltpu.roll(x3, d, axis=1)
                tap = jnp.where(sub >= d, rolled[1:], rolled[:-1])
                acc = acc + tap * cw_ref[CONV_W - 1 - d:CONV_W - d, cs]
            acc = acc.reshape(rows, lc)
            if bias_ref is not None:
                acc = acc + bias_ref[:, cs]
            emit(s, cs, acc)
        xp_ref[s, 0:HIST, :] = xp_ref[s, rows:rows + HIST, :]


def _seq_masks(n, rows):
    ri = lax.broadcasted_iota(jnp.int32, (n, n), 0)
    ci = lax.broadcasted_iota(jnp.int32, (n, n), 1)
    if rows == n:
        same = ri >= 0
    else:
        sh = rows.bit_length() - 1
        same = (ri >> sh) == (ci >> sh)
    return ri, ci, same


NEUMANN_BLOCK = 8


def _inverse_masks(ri, ci, rows):
    assert rows >= NEUMANN_BLOCK
    blk = lambda k: (ri >> (k.bit_length() - 1)) == (ci >> (k.bit_length() - 1))
    levels = []
    k = NEUMANN_BLOCK
    while k < rows:
        levels.append(blk(2 * k) & jnp.logical_not(blk(k)))
        k *= 2
    return blk(NEUMANN_BLOCK), levels


def _unit_lower_inverse_each(as_, masks):
    base, levels = masks
    a8 = [jnp.where(base, a, 0.0) for a in as_]
    p2 = _mm1_each(a8, a8)
    p4 = _mm1_each(p2, p2)
    a8p2 = _mm1_each(a8, p2)
    n = [(p - ap) - a for a, p, ap in zip(a8, p2, a8p2)]
    n = [ni + (p + d) for ni, p, d in zip(n, p4, _mm1_each(n, p4))]
    for lvl in levels:
        off = [jnp.where(lvl, a, 0.0) for a in as_]
        t_off = [o + d for o, d in zip(off, _mm1_each(n, off))]
        n = [ni - (to + d) for ni, to, d in zip(n, t_off, _mm1_each(t_off, n))]
    res = [-((a + ni) + d) for a, ni, d in zip(as_, n, _mm3_each(as_, n))]
    return [ni + (r_ + d) for ni, r_, d in zip(n, res, _mm1_each(n, res))]


def _delta_kernel(*refs, n_seq, rows, has_state, single_step, n_alias):
    refs = refs[n_alias:]
    qkv_ref, z_ref, tail_ref = refs[:3]
    p = 3
    if has_state:
        s0_ref, c0_ref = refs[p], refs[p + 1]
        p += 2
    cw_ref, gp_ref, na_ref = refs[p:p + 3]
    o_ref, s_ref, cn_ref = refs[p + 3:p + 6]
    xp_ref, act_ref = refs[p + 6:p + 8]
    n = n_seq * rows
    c = pl.program_id(1)

    @pl.when(c == 0)
    def _init():
        if has_state:
            if not single_step:
                s_ref[...] = s0_ref[...]
            for s in range(n_seq):
                xp_ref[s, HIST - (CONV_W - 1):HIST, :] = c0_ref[s]
        else:
            s_ref[...] = jnp.zeros(s_ref.shape, F32)
            xp_ref[:, 0:HIST, :] = jnp.zeros((n_seq, HIST, QKV_A), F32)

    def emit(s, cs, acc):
        act_ref[s * rows:(s + 1) * rows, cs] = _silu(acc)

    def seq_rows(s):
        slab, off = divmod(s * rows, ROWS)
        return qkv_ref[slab, off:off + rows, :]

    _causal_conv_block(seq_rows, xp_ref, cw_ref, n_seq, rows, QKV_A, emit)

    @pl.when(c == pl.num_programs(1) - 1)
    def _conv_out():
        for s in range(n_seq):
            cn_ref[s] = xp_ref[s, HIST - (CONV_W - 1):HIST, :]

    n_slab = n // ROWS
    head_group = DELTA_M // n
    _, _, same_n = _seq_masks(n, rows)
    rn = lax.broadcasted_iota(jnp.int32, (n, n), 0)
    cn = lax.broadcasted_iota(jnp.int32, (n, n), 1)
    tail = tail_ref[...].reshape(n, TAIL_W)
    btile = _sigmoid(tail)
    gtile = -jnp.exp(gp_ref[0:1, :]) * jax.nn.softplus(tail + gp_ref[1:2, :])
    g_cum = _masked_sum(same_n & (rn >= cn), gtile)
    g_tot = _masked_sum(same_n, gtile)
    e_cum = jnp.exp(g_cum)
    e_rest = jnp.exp(g_tot - g_cum)
    g_cum_t = g_cum.T

    m = DELTA_M
    ri, ci, same = _seq_masks(m, rows)
    incl = same & (ri >= ci)
    strict = same & (ri > ci)
    inv_masks = _inverse_masks(ri, ci, rows)
    stack = lambda xs: xs[0] if len(xs) == 1 else jnp.concatenate(xs, axis=0)
    groups = [range(g * head_group, (g + 1) * head_group) for g in range(N_HEAD_A // head_group)]
    qn, kn, v, beta, gc, eg, er, g_row = ([] for _ in range(8))
    for heads in groups:
        q_h, k_h, v_h = [], [], []
        for h in heads:
            q = act_ref[:, h * DK_A:(h + 1) * DK_A]
            k = act_ref[:, KA + h * DK_A:KA + (h + 1) * DK_A]
            q_h.append(q * lax.rsqrt(jnp.sum(q * q, axis=-1, keepdims=True) + EPS) * (DK_A ** -0.5))
            k_h.append(k * lax.rsqrt(jnp.sum(k * k, axis=-1, keepdims=True) + EPS))
            v_h.append(act_ref[:, 2 * KA + h * DV_A:2 * KA + (h + 1) * DV_A])
        qn.append(stack(q_h))
        kn.append(stack(k_h))
        v.append(stack(v_h))
        beta.append(stack([btile[:, LANE_BETA + h:LANE_BETA + h + 1] for h in heads]))
        gc.append(stack([g_cum[:, LANE_A + h:LANE_A + h + 1] for h in heads]))
        eg.append(stack([e_cum[:, LANE_A + h:LANE_A + h + 1] for h in heads]))
        er.append(stack([e_rest[:, LANE_A + h:LANE_A + h + 1] for h in heads]))
        g_row.append(jnp.concatenate([g_cum_t[LANE_A + h:LANE_A + h + 1, :] for h in heads], axis=1))
    decay = [jnp.where(incl, jnp.exp(jnp.where(incl, c_ - r_, 0.0)), 0.0) for c_, r_ in zip(gc, g_row)]
    kb = [k_ * b_ for k_, b_ in zip(kn, beta)]
    prod = _mm3_each([stack([kb_, q_]) for kb_, q_ in zip(kb, qn)], kn, NT)
    a = [jnp.where(strict, p_[:m] * d_, 0.0) for p_, d_ in zip(prod, decay)]
    qk = [p_[m:] * d_ for p_, d_ in zip(prod, decay)]
    n_inv = _unit_lower_inverse_each(a, inv_masks)
    rhs = [jnp.concatenate([v_ * b_, kb_ * e_], axis=1) for v_, b_, kb_, e_ in zip(v, beta, kb, eg)]
    sol = [r_ + d for r_, d in zip(rhs, _mm1_each(n_inv, rhs))]
    qg = [q_ * e_ for q_, e_ in zip(qn, eg)]
    kdec = [k_ * e_ for k_, e_ in zip(kn, er)]
    probs = [(g, i, h, s, slice(i * n + s * rows, i * n + (s + 1) * rows))
             for g, heads in enumerate(groups) for i, h in enumerate(heads) for s in range(n_seq)]
    state_ref = s0_ref if (has_state and single_step) else s_ref
    states = [state_ref[s, h] for (_, _, h, s, _) in probs]
    r = _mm1_each([stack([sol[g][sl, DV_A:], qg[g][sl]]) for (g, _, _, _, sl) in probs], states)
    v_new = [sol[g][sl, :DV_A] - r_[:rows] for (g, _, _, _, sl), r_ in zip(probs, r)]
    per_group = lambda xs: [stack([x for (g2, *_), x in zip(probs, xs) if g2 == g]) for g in range(len(groups))]
    v_new_g = per_group(v_new)
    o_inter_g = per_group([r_[rows:] for r_ in r])
    o = [oi + d for oi, d in zip(o_inter_g, _mm1_each(qk, v_new_g))]
    upd = _mm1_each([kdec[g][sl] for (g, _, _, _, sl) in probs], v_new, TN)
    for (g, i, h, s, sl), st, du in zip(probs, states, upd):
        g_last = g_tot[s * rows:s * rows + 1, LANE_A + h:LANE_A + h + 1]
        s_ref[s, h] = st * jnp.exp(g_last) + du
    for g, heads in enumerate(groups):
        for i, h in enumerate(heads):
            hl = slice(h * DV_A, (h + 1) * DV_A)
            oh = o[g][i * n:(i + 1) * n]
            oh = oh * lax.rsqrt(jnp.mean(oh * oh, axis=-1, keepdims=True) + EPS) * na_ref[...]
            oh = oh * _silu(z_ref[:, :, hl].reshape(n, DV_A))
            o_ref[:, :, hl] = oh.reshape(n_slab, ROWS, DV_A).astype(BF16)


def _layer_spec(block, layer):
    zeros = (0,) * (len(block) - 1)
    return pl.BlockSpec((None,) + tuple(block), lambda i, c: (layer, i) + zeros)


def _carry_outputs(prev, out_start):
    if prev is None:
        return [], [], {}
    specs = [pl.BlockSpec(memory_space=pl.ANY) for _ in prev]
    return list(prev), specs, {k: out_start + k for k in range(len(prev))}


def _delta(proj, layer, n_layers, s0, c0, prev, conv_w, gate_params, norm_a, n_batch, t_len):
    rows = min(ROWS, t_len)
    if t_len >= ROWS:
        n_seq, n_slab = DELTA_SEQS, DELTA_SEQS
        slabs = proj.reshape(n_batch, t_len // ROWS, ROWS, AB_N)
    else:
        n_seq, n_slab = ROWS // t_len, 1
        slabs = proj.reshape(n_batch * t_len // ROWS, 1, ROWS, AB_N)
    n = n_seq * rows
    nb = n_batch // n_seq
    nc = t_len // rows
    has_state = s0 is not None
    slab_spec = lambda width, col: pl.BlockSpec((n_slab, None, ROWS, width), lambda i, c: (i, c, 0, col))
    args, in_specs, aliases = _carry_outputs(prev, 1)
    n_alias = len(args)
    in_specs += [slab_spec(QKV_A, 0), slab_spec(VA, OFF_ZA // VA), slab_spec(TAIL_W, OFF_TAIL // TAIL_W)]
    args += [slabs, slabs, slabs]
    s_block = (n_seq, N_HEAD_A, DK_A, DV_A)
    c_block = (n_seq, CONV_W - 1, QKV_A)
    if has_state:
        in_specs += [_layer_spec(s_block, layer), _layer_spec(c_block, layer)]
        args += [s0, c0]
    in_specs += [
        pl.BlockSpec((CONV_W, QKV_A), lambda i, c: (0, 0)),
        pl.BlockSpec((2, TAIL_W), lambda i, c: (0, 0)),
        pl.BlockSpec((1, DV_A), lambda i, c: (0, 0)),
    ]
    args += [conv_w, gate_params, norm_a.reshape(1, DV_A)]
    o, s_all, conv_all = pl.pallas_call(
        functools.partial(_delta_kernel, n_seq=n_seq, rows=rows, has_state=has_state, single_step=nc == 1,
                          n_alias=n_alias),
        out_shape=(
            jax.ShapeDtypeStruct(slabs.shape[:3] + (VA,), BF16),
            jax.ShapeDtypeStruct((n_layers, n_batch, N_HEAD_A, DK_A, DV_A), F32),
            jax.ShapeDtypeStruct((n_layers, n_batch, CONV_W - 1, QKV_A), F32),
        ),
        grid=(nb, nc),
        in_specs=in_specs,
        out_specs=(
            slab_spec(VA, 0),
            _layer_spec(s_block, layer),
            _layer_spec(c_block, layer),
        ),
        scratch_shapes=[
            pltpu.VMEM((n_seq, HIST + rows, QKV_A), F32),
            pltpu.VMEM((n, QKV_A), F32),
        ],
        input_output_aliases=aliases,
        compiler_params=pltpu.CompilerParams(
            dimension_semantics=("parallel", "arbitrary"), vmem_limit_bytes=VMEM_LIMIT),
        name="delta",
    )(*args)
    return o.reshape(n_batch * t_len, VA), s_all, conv_all


def _gla_kernel(*refs, rows, chained, has_state, n_alias):
    refs = refs[n_alias:]
    q_ref, k_ref, v_ref, z_ref, tail_ref = refs[:5]
    p = 5
    if has_state:
        s0_ref = refs[p]
        p += 1
    wlr_ref, blr_ref, nb_ref = refs[p:p + 3]
    o_ref, s_ref = refs[p + 3:p + 5]
    n_slab = q_ref.shape[0]
    n = n_slab * ROWS
    n_sub = ROWS // rows
    c = pl.program_id(1)

    state_ref = s0_ref if (has_state and not chained) else s_ref

    @pl.when(c == 0)
    def _init():
        if not has_state:
            s_ref[...] = jnp.zeros(s_ref.shape, F32)
        elif chained:
            s_ref[...] = s0_ref[...]

    ri, ci, same = _seq_masks(ROWS, rows)
    incl = same & (ri >= ci)
    slabs = [slice(u * ROWS, (u + 1) * ROWS) for u in range(n_slab)]
    cumulate = lambda mask, x: jnp.concatenate(
        [_masked_sum(mask, x[sl]) for sl in slabs], axis=0)
    gk = jax.nn.log_sigmoid(_mm3(tail_ref[...].reshape(n, TAIL_W), wlr_ref[...]) + blr_ref[...]) / GLA_NORMALIZER
    b_loc = cumulate(incl, gk)
    q = q_ref[...].reshape(n, KB) * (DK_B ** -0.5)
    k = k_ref[...].reshape(n, KB)
    v_all = v_ref[...].reshape(n, VB)
    q_loc = q * jnp.exp(b_loc)
    heads = range(N_HEAD_B)
    hls = [slice(h * DK_B, (h + 1) * DK_B) for h in heads]
    vls = [slice(h * DV_B, (h + 1) * DV_B) for h in heads]

    def finish(h, o):
        o = o * lax.rsqrt(jnp.mean(o * o, axis=-1, keepdims=True) + EPS) * nb_ref[...]
        o = o * _silu(z_ref[:, :, vls[h]].reshape(n, DV_B))
        o_ref[:, :, vls[h]] = o.reshape(n_slab, ROWS, DV_B).astype(BF16)

    def key_column(row_vals):
        return jnp.broadcast_to(row_vals, (8, DK_B)).T[:, 0:1]

    if chained:
        tri = ri >= ci
        b_cum = cumulate(tri, gk)
        per_slab = lambda rws: jnp.concatenate([jnp.broadcast_to(r_, (ROWS, KB)) for r_ in rws], axis=0)
        b_last = [b_cum[sl.stop - 1:sl.stop] for sl in slabs]
        q_cum = q * jnp.exp(b_cum)
        kdec = k * jnp.exp(per_slab(b_last) - b_cum)
        row = lax.broadcasted_iota(jnp.int32, (n, KB), 0) & (ROWS - 1)
        k_rel = []
        for s in range(n_sub):
            start = [b_cum[sl.start + s * rows - 1:sl.start + s * rows] if s else jnp.zeros((1, KB), F32)
                     for sl in slabs]
            k_rel.append(k * jnp.exp(jnp.where(row < (s + 1) * rows, per_slab(start) - b_cum, 0.0)))
        probs = [(u, h) for u in range(n_slab) for h in heads]
        a_parts = _mm3_each(
            [q_loc[slabs[u].start + s * rows:slabs[u].start + (s + 1) * rows, hls[h]]
             for (u, h) in probs for s in range(n_sub)],
            [k_rel[s][slabs[u], hls[h]] for (u, h) in probs for s in range(n_sub)], NT)
        a = [jnp.where(tri, jnp.concatenate(a_parts[j * n_sub:(j + 1) * n_sub], axis=0), 0.0)
             for j in range(len(probs))]
        vs = [v_all[slabs[u], vls[h]] for (u, h) in probs]
        sts = [state_ref[u, h] for (u, h) in probs]
        o_state = _mm1_each([q_cum[slabs[u], hls[h]] for (u, h) in probs], sts)
        intra = _mm1_each(a, vs)
        upd = _mm1_each([kdec[slabs[u], hls[h]] for (u, h) in probs], vs, TN)
        for j, (u, h) in enumerate(probs):
            s_ref[u, h] = sts[j] * jnp.exp(key_column(b_last[u][:, hls[h]])) + upd[j]
        for h in heads:
            finish(h, jnp.concatenate([o_state[j] + intra[j] for j, (_, h2) in enumerate(probs) if h2 == h], axis=0))
    else:
        vs = [v_all[:, vl] for vl in vls]
        b_tot = _masked_sum(same, gk)
        kd = k * jnp.exp(-b_loc)
        kdec = k * jnp.exp(b_tot - b_loc)
        a = [jnp.where(incl, x, 0.0) for x in _mm3_each([q_loc[:, hl] for hl in hls], [kd[:, hl] for hl in hls], NT)]
        intra = _mm1_each(a, vs)
        probs = [(h, s, slice(s * rows, (s + 1) * rows)) for h in heads for s in range(n_sub)]
        sts = [state_ref[s, h] for (h, s, _) in probs]
        o_state = _mm1_each([q_loc[sl, hls[h]] for (h, _, sl) in probs], sts)
        upd = _mm1_each([kdec[sl, hls[h]] for (h, _, sl) in probs], [vs[h][sl] for (h, _, sl) in probs], TN)
        for (h, s, sl), st, du in zip(probs, sts, upd):
            s_ref[s, h] = st * jnp.exp(key_column(b_tot[s * rows:s * rows + 1, hls[h]])) + du
        for h in heads:
            finish(h, jnp.concatenate(o_state[h * n_sub:(h + 1) * n_sub], axis=0) + intra[h])


def _gla(proj, layer, n_layers, s0, prev, w_lr_pad, b_lr, norm_b, n_batch, t_len, chunk):
    rows = min(chunk, t_len)
    has_state = s0 is not None
    chained = t_len >= ROWS
    if chained:
        n_slab, n_state, nc = GLA_SEQS, GLA_SEQS, t_len // ROWS
        slabs = proj.reshape(n_batch, nc, ROWS, AB_N)
    else:
        n_slab, n_state, nc = 1, ROWS // t_len, 1
        slabs = proj.reshape(n_batch * t_len // ROWS, 1, ROWS, AB_N)
    nb = n_batch // n_state
    slab_spec = lambda width, col: pl.BlockSpec((n_slab, None, ROWS, width), lambda i, c: (i, c, 0, col))
    args, in_specs, aliases = _carry_outputs(prev, 1)
    n_alias = len(args)
    in_specs += [slab_spec(KB, OFF_QB // KB), slab_spec(KB, OFF_KB // KB), slab_spec(VB, OFF_VB // VB),
                 slab_spec(VB, OFF_ZB // VB), slab_spec(TAIL_W, OFF_TAIL // TAIL_W)]
    args += [slabs] * 5
    s_block = (n_state, N_HEAD_B, DK_B, DV_B)
    if has_state:
        in_specs.append(_layer_spec(s_block, layer))
        args.append(s0)
    in_specs += [
        pl.BlockSpec((TAIL_W, KB), lambda i, c: (0, 0)),
        pl.BlockSpec((1, KB), lambda i, c: (0, 0)),
        pl.BlockSpec((1, DV_B), lambda i, c: (0, 0)),
    ]
    args += [w_lr_pad, b_lr.reshape(1, KB), norm_b.reshape(1, DV_B)]
    o, s_all = pl.pallas_call(
        functools.partial(_gla_kernel, rows=rows, chained=chained, has_state=has_state, n_alias=n_alias),
        out_shape=(
            jax.ShapeDtypeStruct(slabs.shape[:3] + (VB,), BF16),
            jax.ShapeDtypeStruct((n_layers, n_batch, N_HEAD_B, DK_B, DV_B), F32),
        ),
        grid=(nb, nc),
        in_specs=in_specs,
        out_specs=(slab_spec(VB, 0), _layer_spec(s_block, layer)),
        input_output_aliases=aliases,
        compiler_params=pltpu.CompilerParams(
            dimension_semantics=("parallel", "arbitrary"), vmem_limit_bytes=VMEM_LIMIT),
        name="gla",
    )(*args)
    return o.reshape(n_batch * t_len, VB), s_all


def _lru_kernel(*refs, n_seq, rows, has_state, reset_first, n_alias):
    refs = refs[n_alias:]
    xb_ref, gate_ref = refs[:2]
    p = 2
    if has_state:
        h0_ref, c0_ref = refs[p], refs[p + 1]
        p += 2
    cw_ref, cb_ref, wg_ref, ba_ref, bx_ref, lam_ref = refs[p:p + 6]
    y_ref, h_ref, cn_ref = refs[p + 6:p + 9]
    xp_ref, xc_ref, a_ref, b_ref = refs[p + 9:p + 13]
    n = n_seq * rows
    c = pl.program_id(1)

    @pl.when(c == 0)
    def _init():
        if has_state:
            h_ref[...] = h0_ref[...]
            for s in range(n_seq):
                xp_ref[s, HIST - (CONV_W - 1):HIST, :] = c0_ref[s]
        else:
            h_ref[...] = jnp.zeros(h_ref.shape, F32)
            xp_ref[:, 0:HIST, :] = jnp.zeros((n_seq, HIST, W_LRU), F32)

    def emit(s, cs, acc):
        xc_ref[s * rows:(s + 1) * rows, cs] = acc

    _causal_conv_block(lambda s: xb_ref[s * rows:(s + 1) * rows, :], xp_ref, cw_ref, n_seq, rows, W_LRU,
                       emit, bias_ref=cb_ref)

    @pl.when(c == pl.num_programs(1) - 1)
    def _conv_out():
        for s in range(n_seq):
            cn_ref[s] = xp_ref[s, HIST - (CONV_W - 1):HIST, :]

    row = lax.broadcasted_iota(jnp.int32, (n, LRU_BW), 0)
    sub = lax.broadcasted_iota(jnp.int32, (n // 8, 8, LRU_BW), 1)
    scan_steps = [(d, sub >= d) for d in (1, 2, 4)]
    first_row = (row == 0) & (c == 0)
    for blk in range(LRU_BLOCKS):
        bl = slice(blk * LRU_BW, (blk + 1) * LRU_BW)
        xc = xc_ref[:, bl]
        pre = jnp.dot(xc.astype(BF16), wg_ref[blk], preferred_element_type=F32)
        t_r = jnp.tanh(pre[:, :LRU_BW] + ba_ref[:, bl])
        t_i = jnp.tanh(pre[:, LRU_BW:] + bx_ref[:, bl])
        gi = 0.5 * t_i + 0.5
        log_a = (t_r + 1.0) * (-0.5 * LRU_C * jax.nn.softplus(-lam_ref[:, bl]))
        a = jnp.exp(log_a)
        z = -jnp.tanh(log_a) * (a * a + 1.0)
        mult = z * lax.rsqrt(jnp.maximum(z, TINY))
        if reset_first:
            mult = jnp.where(first_row, 1.0, mult)
        b = mult * gi * xc
        a = a.reshape(n // 8, 8, LRU_BW)
        b = b.reshape(n // 8, 8, LRU_BW)
        for d, m in scan_steps:
            a_sh = pltpu.roll(a, d, axis=1)
            b_sh = pltpu.roll(b, d, axis=1)
            b = jnp.where(m, a * b_sh + b, b)
            a = jnp.where(m, a * a_sh, a)
        a_ref[:, bl] = a.reshape(n, LRU_BW)
        b_ref[:, bl] = b.reshape(n, LRU_BW)

    for s in range(n_seq):
        def body(g, hp, s=s):
            r0 = pl.multiple_of(s * rows + g * 8, 8)
            hs = a_ref[pl.ds(r0, 8), :] * hp + b_ref[pl.ds(r0, 8), :]
            b_ref[pl.ds(r0, 8), :] = hs
            return hs[7:8, :]

        h_ref[s] = lax.fori_loop(0, rows // 8, body, h_ref[s])

    lc = 512
    for c0 in range(0, W_LRU, lc):
        cs = slice(c0, c0 + lc)
        y_ref[:, cs] = (b_ref[:, cs] * _silu(gate_ref[:, cs])).astype(BF16)


def _lru(proj, layer, n_layers, h0, c0, prev, conv_w, conv_b, w_a, b_a, w_x, b_x, lam, n_batch, t_len,
         reset_first):
    has_state = h0 is not None
    if t_len >= LRU_ROWS:
        rows, n_seq = LRU_ROWS, 1
    else:
        rows, n_seq = t_len, LRU_SHORT_ROWS // t_len
    n = rows * n_seq
    nb = n_batch // n_seq
    nc = t_len // rows
    args, in_specs, aliases = _carry_outputs(prev, 1)
    n_alias = len(args)
    in_specs += [
        pl.BlockSpec((n, W_LRU), lambda i, c: (i * nc + c, 0)),
        pl.BlockSpec((n, W_LRU), lambda i, c: (i * nc + c, 1)),
    ]
    args += [proj, proj]
    h_block = (n_seq, 1, W_LRU)
    c_block = (n_seq, CONV_W - 1, W_LRU)
    if has_state:
        in_specs += [_layer_spec(h_block, layer), _layer_spec(c_block, layer)]
        args += [h0, c0]
    vec = pl.BlockSpec((1, W_LRU), lambda i, c: (0, 0))
    w_gates = (0.5 * jnp.concatenate([w_a, w_x], axis=-1)).astype(BF16)
    blk = pl.BlockSpec(w_gates.shape, lambda i, c: (0, 0, 0))
    in_specs += [pl.BlockSpec((CONV_W, W_LRU), lambda i, c: (0, 0)), vec, blk, vec, vec, vec]
    args += [conv_w, conv_b.reshape(1, W_LRU), w_gates, 0.5 * b_a.reshape(1, W_LRU),
             0.5 * b_x.reshape(1, W_LRU), lam.reshape(1, W_LRU)]
    return pl.pallas_call(
        functools.partial(_lru_kernel, n_seq=n_seq, rows=rows, has_state=has_state,
                          reset_first=reset_first, n_alias=n_alias),
        out_shape=(
            jax.ShapeDtypeStruct((n_batch * t_len, W_LRU), BF16),
            jax.ShapeDtypeStruct((n_layers, n_batch, 1, W_LRU), F32),
            jax.ShapeDtypeStruct((n_layers, n_batch, CONV_W - 1, W_LRU), F32),
        ),
        grid=(nb, nc),
        in_specs=in_specs,
        out_specs=(
            pl.BlockSpec((n, W_LRU), lambda i, c: (i * nc + c, 0)),
            _layer_spec(h_block, layer),
            _layer_spec(c_block, layer),
        ),
        input_output_aliases=aliases,
        scratch_shapes=[
            pltpu.VMEM((n_seq, HIST + rows, W_LRU), F32),
            pltpu.VMEM((n, W_LRU), F32),
            pltpu.VMEM((n, W_LRU), F32),
            pltpu.VMEM((n, W_LRU), F32),
        ],
        compiler_params=pltpu.CompilerParams(
            dimension_semantics=("parallel", "arbitrary"), vmem_limit_bytes=VMEM_LIMIT),
        name="lru",
    )(*args)


def _ab_w_in_kernel(w_ref, o_ref):
    sizes = [QKV_A, N_HEAD_A, N_HEAD_A, VA, KB, KB, VB, LOWRANK, VB]
    offs = [0]
    for s in sizes:
        offs.append(offs[-1] + s)
    w = w_ref[...]
    qkv, b_raw, a_raw, z_a, q_b, k_b, v_b, lr_b, z_b = [w[:, offs[i]:offs[i + 1]] for i in range(9)]
    used = OFF_TAIL + 2 * N_HEAD_A + LOWRANK
    pad = jnp.zeros((w.shape[0], AB_N - used), w.dtype)
    o_ref[...] = jnp.concatenate([qkv, z_a, q_b, k_b, v_b, z_b, b_raw, a_raw, lr_b, pad], axis=1)


def _ab_w_in_layout(w, layer):
    _, d, n = w.shape
    tr = 256
    return pl.pallas_call(
        _ab_w_in_kernel,
        out_shape=jax.ShapeDtypeStruct((d, AB_N), BF16),
        grid=(d // tr,),
        in_specs=[pl.BlockSpec((None, tr, n), lambda i: (layer, i, 0))],
        out_specs=pl.BlockSpec((tr, AB_N), lambda i: (i, 0)),
        compiler_params=pltpu.CompilerParams(dimension_semantics=("parallel",)),
        name="w_in_layout",
    )(w)


def _cast_kernel(w_ref, o_ref):
    o_ref[...] = w_ref[...].astype(BF16)


def _layer_bf16(w, layer):
    _, r, c = w.shape
    tr = 512
    return pl.pallas_call(
        _cast_kernel,
        out_shape=jax.ShapeDtypeStruct((r, c), BF16),
        grid=(r // tr,),
        in_specs=[pl.BlockSpec((None, tr, c), lambda i: (layer, i, 0))],
        out_specs=pl.BlockSpec((tr, c), lambda i: (i, 0)),
        compiler_params=pltpu.CompilerParams(dimension_semantics=("parallel",)),
        name="w_cast",
    )(w)


def _tail_row(vals, lane0):
    return jnp.zeros((TAIL_W,), F32).at[lane0:lane0 + vals.shape[0]].set(vals.astype(F32))


def kernel(x_prompt, x_sample, state_delta, state_delta_conv, state_gla, state_lru, state_lru_conv,
           ab_norm, ab_w_in, ab_conv_w, ab_a_log, ab_dt_bias, ab_norm_a, ab_gla_w_lr, ab_gla_b_lr,
           ab_norm_b, ab_w_out, lru_norm, lru_w_in, lru_conv_w, lru_conv_b, lru_w_a, lru_b_a,
           lru_w_x, lru_b_x, lru_lambda, lru_w_out, final_norm):
    n_ab, n_lru = ab_norm.shape[0], lru_norm.shape[0]
    depth = n_ab + n_lru
    lru_h0 = state_lru.reshape(n_lru, -1, 1, W_LRU)
    groups = []
    for x, carried in ((x_prompt, False), (x_sample, True)):
        groups.append(dict(x=x.reshape(-1, D_MODEL), nb=x.shape[0], t=x.shape[1], carried=carried,
                           delta=None, gla=None, lru=None))

    ab_w_in_bf16 = ab_w_in.astype(BF16)
    for layer in range(depth):
        j = layer // 2
        last = layer == depth - 1
        if layer % 2 == 0:
            w_in = _ab_w_in_layout(ab_w_in_bf16, j)
            w_out = _layer_bf16(ab_w_out, j)
            gate_params = jnp.stack([_tail_row(ab_a_log[j], LANE_A), _tail_row(ab_dt_bias[j], LANE_A)])
            w_lr_pad = jnp.zeros((TAIL_W, KB), F32).at[LANE_LR:LANE_LR + LOWRANK].set(ab_gla_w_lr[j])
            for g in groups:
                proj = _norm_mm(g["x"], ab_norm[j], w_in, AB_TN)
                s_a, c_a, s_b = (state_delta, state_delta_conv, state_gla) if g["carried"] else (None,) * 3
                o_a, *g["delta"] = _delta(proj, j, n_ab, s_a, c_a, g["delta"], ab_conv_w[j], gate_params,
                                          ab_norm_a[j], g["nb"], g["t"])
                o_b, *g["gla"] = _gla(proj, j, n_ab, s_b, g["gla"], w_lr_pad, ab_gla_b_lr[j], ab_norm_b[j],
                                      g["nb"], g["t"], 16)
                g["x"] = _out_mm([o_a, o_b], w_out, g["x"], final_norm if last else None)
        else:
            w_in = _layer_bf16(lru_w_in, j)
            w_out = _layer_bf16(lru_w_out, j)
            for g in groups:
                proj = _norm_mm(g["x"], lru_norm[j], w_in, 1024)
                h0, c0 = (lru_h0, state_lru_conv) if g["carried"] else (None, None)
                y, *g["lru"] = _lru(proj, j, n_lru, h0, c0, g["lru"], lru_conv_w[j], lru_conv_b[j],
                                    lru_w_a[j], lru_b_a[j], lru_w_x[j], lru_b_x[j], lru_lambda[j],
                                    g["nb"], g["t"], reset_first=not g["carried"])
                g["x"] = _out_mm([y], w_out, g["x"], final_norm if last else None)

    outs = []
    for g, x in zip(groups, (x_prompt, x_sample)):
        h_all, lconv = g["lru"]
        outs.append([g["x"].reshape(x.shape), g["delta"][0], g["delta"][1], g["gla"][0],
                     h_all.reshape(n_lru, g["nb"], W_LRU), lconv])
    p, s = outs
    return (p[0], s[0], p[1], p[2], p[3], p[4], p[5], s[1], s[2], s[3], s[4], s[5])
```

```python
import functools

import jax
import jax.numpy as jnp
from jax import lax
from jax.experimental import pallas as pl
from jax.experimental.pallas import tpu as pltpu

F32 = jnp.float32
BF16 = jnp.bfloat16

D_MODEL = 2048
N_HEAD_A, DK_A, DV_A = 8, 128, 128
N_HEAD_B, DK_B, DV_B = 4, 128, 256
KA = N_HEAD_A * DK_A
VA = N_HEAD_A * DV_A
KB = N_HEAD_B * DK_B
VB = N_HEAD_B * DV_B
QKV_A = 2 * KA + VA
LOWRANK = 16
GLA_NORMALIZER = 16.0
W_LRU = D_MODEL
LRU_BLOCKS = 16
LRU_BW = W_LRU // LRU_BLOCKS
LRU_C = 8.0
CONV_W = 4
EPS = 1e-6
LANES = 128
TINY = 1.1754944e-38

OFF_QKV = 0
OFF_ZA = OFF_QKV + QKV_A
OFF_QB = OFF_ZA + VA
OFF_KB = OFF_QB + KB
OFF_VB = OFF_KB + KB
OFF_ZB = OFF_VB + VB
OFF_TAIL = OFF_ZB + VB
TAIL_W = 128
AB_TN = 1536
AB_N = 7680
LANE_BETA, LANE_A, LANE_LR = 0, N_HEAD_A, 2 * N_HEAD_A

ROWS = 64
DELTA_M = 128
DELTA_SEQS = 2
GLA_SEQS = 4
LRU_ROWS = 512
LRU_SHORT_ROWS = 128
HIST = 8
VMEM_LIMIT = 56 * 1024 * 1024

NN = (((1,), (0,)), ((), ()))
NT = (((1,), (1,)), ((), ()))
TN = (((0,), (0,)), ((), ()))


def _masked_sum(mask, x):
    m = jnp.where(mask, 1.0, 0.0).astype(BF16)
    hi = x.astype(BF16)
    rest = x - hi.astype(F32)
    mid = rest.astype(BF16)
    lo = (rest - mid.astype(F32)).astype(BF16)
    if m.shape[1] % LANES:
        dot = functools.partial(jnp.dot, preferred_element_type=F32)
        return dot(m, hi) + (dot(m, mid) + dot(m, lo))
    return jnp.dot(jnp.concatenate([m, m, m], axis=1), jnp.concatenate([hi, mid, lo], axis=0),
                   preferred_element_type=F32)


def _split(a):
    hi = a.astype(BF16)
    return hi, (a - hi.astype(F32)).astype(BF16)


def _mm3(a, b, dims=NN):
    ah, al = a if isinstance(a, tuple) else _split(a)
    bh, bl = b if isinstance(b, tuple) else _split(b)
    (ca,), (cb,) = dims[0]
    dot = functools.partial(lax.dot_general, dimension_numbers=dims, preferred_element_type=F32)
    if ah.shape[ca] % LANES:
        return dot(ah, bh) + (dot(ah, bl) + dot(al, bh))
    return dot(jnp.concatenate([ah, ah, al], axis=ca), jnp.concatenate([bh, bl, bh], axis=cb))


def _split_each(xs):
    return [x if isinstance(x, tuple) else _split(x) for x in xs]


def _mm3_each(as_, bs, dims=NN):
    sa, sb = _split_each(as_), _split_each(bs)
    return [_mm3(a, b, dims) for a, b in zip(sa, sb)]


def _mm1_each(as_, bs, dims=NN):
    ca = [a.astype(BF16) for a in as_]
    cb = [b.astype(BF16) for b in bs]
    return [lax.dot_general(a, b, dims, preferred_element_type=F32) for a, b in zip(ca, cb)]


def _sigmoid(x):
    return 0.5 * jnp.tanh(0.5 * x) + 0.5


def _silu(x):
    h = 0.5 * x
    return h + h * jnp.tanh(h)


def _norm_mm_kernel(x_ref, g_ref, w_ref, o_ref, h_ref, *, last_cols):
    j = pl.program_id(1)

    @pl.when(j == 0)
    def _():
        x = x_ref[...]
        ms = jnp.mean(x * x, axis=-1, keepdims=True)
        h_ref[...] = (x * lax.rsqrt(ms + EPS) * g_ref[...]).astype(BF16)

    tn = o_ref.shape[1]
    if last_cols == tn:
        o_ref[...] = jnp.dot(h_ref[...], w_ref[...], preferred_element_type=F32)
    else:
        last = pl.num_programs(1) - 1

        @pl.when(j != last)
        def _():
            o_ref[...] = jnp.dot(h_ref[...], w_ref[...], preferred_element_type=F32)

        @pl.when(j == last)
        def _():
            o_ref[:, :last_cols] = jnp.dot(h_ref[...], w_ref[:, :last_cols], preferred_element_type=F32)
            o_ref[:, last_cols:] = jnp.zeros((o_ref.shape[0], tn - last_cols), F32)


def _norm_mm(x, g, w, tn, n_valid=None):
    m, k = x.shape
    n = w.shape[1]
    tm = min(m, 1024)
    last_cols = tn if n_valid is None else n_valid - (n // tn - 1) * tn
    w_spec = pl.BlockSpec((k, tn), lambda i, j: (0, j))
    return pl.pallas_call(
        functools.partial(_norm_mm_kernel, last_cols=last_cols),
        out_shape=jax.ShapeDtypeStruct((m, n), F32),
        grid=(m // tm, n // tn),
        in_specs=[
            pl.BlockSpec((tm, k), lambda i, j: (i, 0)),
            pl.BlockSpec((1, k), lambda i, j: (0, 0)),
            w_spec,
        ],
        out_specs=pl.BlockSpec((tm, tn), lambda i, j: (i, j)),
        scratch_shapes=[pltpu.VMEM((tm, k), BF16)],
        compiler_params=pltpu.CompilerParams(
            dimension_semantics=("parallel", "arbitrary"), vmem_limit_bytes=VMEM_LIMIT),
        name="norm_mm",
    )(x, g.reshape(1, k), w)


def _out_mm_kernel(*refs, n_in, final):
    o_refs = refs[:n_in]
    w_ref, x_ref = refs[n_in], refs[n_in + 1]
    out_ref = refs[-1]
    acc = x_ref[...]
    k0 = 0
    for o_ref in o_refs:
        kk = o_ref.shape[1]
        acc = acc + jnp.dot(o_ref[...], w_ref[k0:k0 + kk, :], preferred_element_type=F32)
        k0 += kk
    if final:
        fg_ref = refs[n_in + 2]
        ms = jnp.mean(acc * acc, axis=-1, keepdims=True)
        acc = acc * lax.rsqrt(ms + EPS) * fg_ref[...]
    out_ref[...] = acc


def _out_mm(os_, w, x, final_g=None):
    m, d = x.shape
    tm = 512
    final = final_g is not None
    in_specs = [pl.BlockSpec((tm, o.shape[1]), lambda i: (i, 0)) for o in os_]
    in_specs += [pl.BlockSpec(w.shape, lambda i: (0, 0)), pl.BlockSpec((tm, d), lambda i: (i, 0))]
    args = list(os_) + [w, x]
    if final:
        in_specs.append(pl.BlockSpec((1, d), lambda i: (0, 0)))
        args.append(final_g.reshape(1, d))
    return pl.pallas_call(
        functools.partial(_out_mm_kernel, n_in=len(os_), final=final),
        out_shape=jax.ShapeDtypeStruct((m, d), F32),
        grid=(m // tm,),
        in_specs=in_specs,
        out_specs=pl.BlockSpec((tm, d), lambda i: (i, 0)),
        compiler_params=pltpu.CompilerParams(
            dimension_semantics=("parallel",), vmem_limit_bytes=VMEM_LIMIT),
        name="out_mm",
    )(*args)


def _causal_conv_block(seq_rows, xp_ref, cw_ref, n_seq, rows, width, emit, bias_ref=None):
    lc = 512
    groups = rows // 8
    sub = lax.broadcasted_iota(jnp.int32, (groups, 8, lc), 1)
    for s in range(n_seq):
        xp_ref[s, HIST:HIST + rows, :] = seq_rows(s)
        for c0 in range(0, width, lc):
            cs = slice(c0, c0 + lc)
            x3 = xp_ref[s, :, cs].reshape(groups + HIST // 8, 8, lc)
            acc = x3[1:] * cw_ref[CONV_W - 1:CONV_W, cs]
            for d in range(1, CONV_W):
                rolled = pltpu.roll(x3, d, axis=1)
                tap = jnp.where(sub >= d, rolled[1:], rolled[:-1])
                acc = acc + tap * cw_ref[CONV_W - 1 - d:CONV_W - d, cs]
            acc = acc.reshape(rows, lc)
            if bias_ref is not None:
                acc = acc + bias_ref[:, cs]
            emit(s, cs, acc)
        xp_ref[s, 0:HIST, :] = xp_ref[s, rows:rows + HIST, :]


def _seq_masks(n, rows):
    ri = lax.broadcasted_iota(jnp.int32, (n, n), 0)
    ci = lax.broadcasted_iota(jnp.int32, (n, n), 1)
    if rows == n:
        same = ri >= 0
    else:
        sh = rows.bit_length() - 1
        same = (ri >> sh) == (ci >> sh)
    return ri, ci, same


NEUMANN_BLOCK = 8


def _inverse_masks(ri, ci, rows):
    assert rows >= NEUMANN_BLOCK
    blk = lambda k: (ri >> (k.bit_length() - 1)) == (ci >> (k.bit_length() - 1))
    levels = []
    k = NEUMANN_BLOCK
    while k < rows:
        levels.append(blk(2 * k) & jnp.logical_not(blk(k)))
        k *= 2
    return blk(NEUMANN_BLOCK), levels


def _unit_lower_inverse_each(as_, masks):
    base, levels = masks
    a8 = [jnp.where(base, a, 0.0) for a in as_]
    p2 = _mm1_each(a8, a8)
    p4 = _mm1_each(p2, p2)
    a8p2 = _mm1_each(a8, p2)
    n = [(p - ap) - a for a, p, ap in zip(a8, p2, a8p2)]
    n = [ni + (p + d) for ni, p, d in zip(n, p4, _mm1_each(n, p4))]
    for lvl in levels:
        off = [jnp.where(lvl, a, 0.0) for a in as_]
        t_off = [o + d for o, d in zip(off, _mm1_each(n, off))]
        n = [ni - (to + d) for ni, to, d in zip(n, t_off, _mm1_each(t_off, n))]
    res = [-((a + ni) + d) for a, ni, d in zip(as_, n, _mm3_each(as_, n))]
    return [ni + (r_ + d) for ni, r_, d in zip(n, res, _mm1_each(n, res))]


def _delta_kernel(*refs, n_seq, rows, has_state, single_step, n_alias):
    refs = refs[n_alias:]
    qkv_ref, z_ref, tail_ref = refs[:3]
    p = 3
    if has_state:
        s0_ref, c0_ref = refs[p], refs[p + 1]
        p += 2
    cw_ref, gp_ref, na_ref = refs[p:p + 3]
    o_ref, s_ref, cn_ref = refs[p + 3:p + 6]
    xp_ref, act_ref = refs[p + 6:p + 8]
    n = n_seq * rows
    c = pl.program_id(1)

    @pl.when(c == 0)
    def _init():
        if has_state:
            if not single_step:
                s_ref[...] = s0_ref[...]
            for s in range(n_seq):
                xp_ref[s, HIST - (CONV_W - 1):HIST, :] = c0_ref[s]
        else:
            s_ref[...] = jnp.zeros(s_ref.shape, F32)
            xp_ref[:, 0:HIST, :] = jnp.zeros((n_seq, HIST, QKV_A), F32)

    def emit(s, cs, acc):
        act_ref[s * rows:(s + 1) * rows, cs] = _silu(acc)

    def seq_rows(s):
        slab, off = divmod(s * rows, ROWS)
        return qkv_ref[slab, off:off + rows, :]

    _causal_conv_block(seq_rows, xp_ref, cw_ref, n_seq, rows, QKV_A, emit)

    @pl.when(c == pl.num_programs(1) - 1)
    def _conv_out():
        for s in range(n_seq):
            cn_ref[s] = xp_ref[s, HIST - (CONV_W - 1):HIST, :]

    n_slab = n // ROWS
    head_group = DELTA_M // n
    _, _, same_n = _seq_masks(n, rows)
    rn = lax.broadcasted_iota(jnp.int32, (n, n), 0)
    cn = lax.broadcasted_iota(jnp.int32, (n, n), 1)
    tail = tail_ref[...].reshape(n, TAIL_W)
    btile = _sigmoid(tail)
    gtile = -jnp.exp(gp_ref[0:1, :]) * jax.nn.softplus(tail + gp_ref[1:2, :])
    g_cum = _masked_sum(same_n & (rn >= cn), gtile)
    g_tot = _masked_sum(same_n, gtile)
    e_cum = jnp.exp(g_cum)
    e_rest = jnp.exp(g_tot - g_cum)
    g_cum_t = g_cum.T

    m = DELTA_M
    ri, ci, same = _seq_masks(m, rows)
    incl = same & (ri >= ci)
    strict = same & (ri > ci)
    inv_masks = _inverse_masks(ri, ci, rows)
    stack = lambda xs: xs[0] if len(xs) == 1 else jnp.concatenate(xs, axis=0)
    groups = [range(g * head_group, (g + 1) * head_group) for g in range(N_HEAD_A // head_group)]
    qn, kn, v, beta, gc, eg, er, g_row = ([] for _ in range(8))
    for heads in groups:
        q_h, k_h, v_h = [], [], []
        for h in heads:
            q = act_ref[:, h * DK_A:(h + 1) * DK_A]
            k = act_ref[:, KA + h * DK_A:KA + (h + 1) * DK_A]
            q_h.append(q * lax.rsqrt(jnp.sum(q * q, axis=-1, keepdims=True) + EPS) * (DK_A ** -0.5))
            k_h.append(k * lax.rsqrt(jnp.sum(k * k, axis=-1, keepdims=True) + EPS))
            v_h.append(act_ref[:, 2 * KA + h * DV_A:2 * KA + (h + 1) * DV_A])
        qn.append(stack(q_h))
        kn.append(stack(k_h))
        v.append(stack(v_h))
        beta.append(stack([btile[:, LANE_BETA + h:LANE_BETA + h + 1] for h in heads]))
        gc.append(stack([g_cum[:, LANE_A + h:LANE_A + h + 1] for h in heads]))
        eg.append(stack([e_cum[:, LANE_A + h:LANE_A + h + 1] for h in heads]))
        er.append(stack([e_rest[:, LANE_A + h:LANE_A + h + 1] for h in heads]))
        g_row.append(jnp.concatenate([g_cum_t[LANE_A + h:LANE_A + h + 1, :] for h in heads], axis=1))
    decay = [jnp.where(incl, jnp.exp(jnp.where(incl, c_ - r_, 0.0)), 0.0) for c_, r_ in zip(gc, g_row)]
    kb = [k_ * b_ for k_, b_ in zip(kn, beta)]
    prod = _mm3_each([stack([kb_, q_]) for kb_, q_ in zip(kb, qn)], kn, NT)
    a = [jnp.where(strict, p_[:m] * d_, 0.0) for p_, d_ in zip(prod, decay)]
    qk = [p_[m:] * d_ for p_, d_ in zip(prod, decay)]
    n_inv = _unit_lower_inverse_each(a, inv_masks)
    rhs = [jnp.concatenate([v_ * b_, kb_ * e_], axis=1) for v_, b_, kb_, e_ in zip(v, beta, kb, eg)]
    sol = [r_ + d for r_, d in zip(rhs, _mm1_each(n_inv, rhs))]
    qg = [q_ * e_ for q_, e_ in zip(qn, eg)]
    kdec = [k_ * e_ for k_, e_ in zip(kn, er)]
    probs = [(g, i, h, s, slice(i * n + s * rows, i * n + (s + 1) * rows))
             for g, heads in enumerate(groups) for i, h in enumerate(heads) for s in range(n_seq)]
    state_ref = s0_ref if (has_state and single_step) else s_ref
    states = [state_ref[s, h] for (_, _, h, s, _) in probs]
    r = _mm1_each([stack([sol[g][sl, DV_A:], qg[g][sl]]) for (g, _, _, _, sl) in probs], states)
    v_new = [sol[g][sl, :DV_A] - r_[:rows] for (g, _, _, _, sl), r_ in zip(probs, r)]
    per_group = lambda xs: [stack([x for (g2, *_), x in zip(probs, xs) if g2 == g]) for g in range(len(groups))]
    v_new_g = per_group(v_new)
    o_inter_g = per_group([r_[rows:] for r_ in r])
    o = [oi + d for oi, d in zip(o_inter_g, _mm1_each(qk, v_new_g))]
    upd = _mm1_each([kdec[g][sl] for (g, _, _, _, sl) in probs], v_new, TN)
    for (g, i, h, s, sl), st, du in zip(probs, states, upd):
        g_last = g_tot[s * rows:s * rows + 1, LANE_A + h:LANE_A + h + 1]
        s_ref[s, h] = st * jnp.exp(g_last) + du
    for g, heads in enumerate(groups):
        for i, h in enumerate(heads):
            hl = slice(h * DV_A, (h + 1) * DV_A)
            oh = o[g][i * n:(i + 1) * n]
            oh = oh * lax.rsqrt(jnp.mean(oh * oh, axis=-1, keepdims=True) + EPS) * na_ref[...]
            oh = oh * _silu(z_ref[:, :, hl].reshape(n, DV_A))
            o_ref[:, :, hl] = oh.reshape(n_slab, ROWS, DV_A).astype(BF16)


def _layer_spec(block, layer):
    zeros = (0,) * (len(block) - 1)
    return pl.BlockSpec((None,) + tuple(block), lambda i, c: (layer, i) + zeros)


def _carry_outputs(prev, out_start):
    if prev is None:
        return [], [], {}
    specs = [pl.BlockSpec(memory_space=pl.ANY) for _ in prev]
    return list(prev), specs, {k: out_start + k for k in range(len(prev))}


def _delta(proj, layer, n_layers, s0, c0, prev, conv_w, gate_params, norm_a, n_batch, t_len):
    rows = min(ROWS, t_len)
    if t_len >= ROWS:
        n_seq, n_slab = DELTA_SEQS, DELTA_SEQS
        slabs = proj.reshape(n_batch, t_len // ROWS, ROWS, AB_N)
    else:
        n_seq, n_slab = ROWS // t_len, 1
        slabs = proj.reshape(n_batch * t_len // ROWS, 1, ROWS, AB_N)
    n = n_seq * rows
    nb = n_batch // n_seq
    nc = t_len // rows
    has_state = s0 is not None
    slab_spec = lambda width, col: pl.BlockSpec((n_slab, None, ROWS, width), lambda i, c: (i, c, 0, col))
    args, in_specs, aliases = _carry_outputs(prev, 1)
    n_alias = len(args)
    in_specs += [slab_spec(QKV_A, 0), slab_spec(VA, OFF_ZA // VA), slab_spec(TAIL_W, OFF_TAIL // TAIL_W)]
    args += [slabs, slabs, slabs]
    s_block = (n_seq, N_HEAD_A, DK_A, DV_A)
    c_block = (n_seq, CONV_W - 1, QKV_A)
    if has_state:
        in_specs += [_layer_spec(s_block, layer), _layer_spec(c_block, layer)]
        args += [s0, c0]
    in_specs += [
        pl.BlockSpec((CONV_W, QKV_A), lambda i, c: (0, 0)),
        pl.BlockSpec((2, TAIL_W), lambda i, c: (0, 0)),
        pl.BlockSpec((1, DV_A), lambda i, c: (0, 0)),
    ]
    args += [conv_w, gate_params, norm_a.reshape(1, DV_A)]
    o, s_all, conv_all = pl.pallas_call(
        functools.partial(_delta_kernel, n_seq=n_seq, rows=rows, has_state=has_state, single_step=nc == 1,
                          n_alias=n_alias),
        out_shape=(
            jax.ShapeDtypeStruct(slabs.shape[:3] + (VA,), BF16),
            jax.ShapeDtypeStruct((n_layers, n_batch, N_HEAD_A, DK_A, DV_A), F32),
            jax.ShapeDtypeStruct((n_layers, n_batch, CONV_W - 1, QKV_A), F32),
        ),
        grid=(nb, nc),
        in_specs=in_specs,
        out_specs=(
            slab_spec(VA, 0),
            _layer_spec(s_block, layer),
            _layer_spec(c_block, layer),
        ),
        scratch_shapes=[
            pltpu.VMEM((n_seq, HIST + rows, QKV_A), F32),
            pltpu.VMEM((n, QKV_A), F32),
        ],
        input_output_aliases=aliases,
        compiler_params=pltpu.CompilerParams(
            dimension_semantics=("parallel", "arbitrary"), vmem_limit_bytes=VMEM_LIMIT),
        name="delta",
    )(*args)
    return o.reshape(n_batch * t_len, VA), s_all, conv_all


def _gla_kernel(*refs, rows, chained, has_state, n_alias):
    refs = refs[n_alias:]
    q_ref, k_ref, v_ref, z_ref, tail_ref = refs[:5]
    p = 5
    if has_state:
        s0_ref = refs[p]
        p += 1
    wlr_ref, blr_ref, nb_ref = refs[p:p + 3]
    o_ref, s_ref = refs[p + 3:p + 5]
    n_slab = q_ref.shape[0]
    n = n_slab * ROWS
    n_sub = ROWS // rows
    c = pl.program_id(1)

    state_ref = s0_ref if (has_state and not chained) else s_ref

    @pl.when(c == 0)
    def _init():
        if not has_state:
            s_ref[...] = jnp.zeros(s_ref.shape, F32)
        elif chained:
            s_ref[...] = s0_ref[...]

    ri, ci, same = _seq_masks(ROWS, rows)
    incl = same & (ri >= ci)
    slabs = [slice(u * ROWS, (u + 1) * ROWS) for u in range(n_slab)]
    cumulate = lambda mask, x: jnp.concatenate(
        [_masked_sum(mask, x[sl]) for sl in slabs], axis=0)
    gk = jax.nn.log_sigmoid(_mm3(tail_ref[...].reshape(n, TAIL_W), wlr_ref[...]) + blr_ref[...]) / GLA_NORMALIZER
    b_loc = cumulate(incl, gk)
    q = q_ref[...].reshape(n, KB) * (DK_B ** -0.5)
    k = k_ref[...].reshape(n, KB)
    v_all = v_ref[...].reshape(n, VB)
    q_loc = q * jnp.exp(b_loc)
    heads = range(N_HEAD_B)
    hls = [slice(h * DK_B, (h + 1) * DK_B) for h in heads]
    vls = [slice(h * DV_B, (h + 1) * DV_B) for h in heads]

    def finish(h, o):
        o = o * lax.rsqrt(jnp.mean(o * o, axis=-1, keepdims=True) + EPS) * nb_ref[...]
        o = o * _silu(z_ref[:, :, vls[h]].reshape(n, DV_B))
        o_ref[:, :, vls[h]] = o.reshape(n_slab, ROWS, DV_B).astype(BF16)

    def key_column(row_vals):
        return jnp.broadcast_to(row_vals, (8, DK_B)).T[:, 0:1]

    if chained:
        tri = ri >= ci
        b_cum = cumulate(tri, gk)
        per_slab = lambda rws: jnp.concatenate([jnp.broadcast_to(r_, (ROWS, KB)) for r_ in rws], axis=0)
        b_last = [b_cum[sl.stop - 1:sl.stop] for sl in slabs]
        q_cum = q * jnp.exp(b_cum)
        kdec = k * jnp.exp(per_slab(b_last) - b_cum)
        row = lax.broadcasted_iota(jnp.int32, (n, KB), 0) & (ROWS - 1)
        k_rel = []
        for s in range(n_sub):
            start = [b_cum[sl.start + s * rows - 1:sl.start + s * rows] if s else jnp.zeros((1, KB), F32)
                     for sl in slabs]
            k_rel.append(k * jnp.exp(jnp.where(row < (s + 1) * rows, per_slab(start) - b_cum, 0.0)))
        probs = [(u, h) for u in range(n_slab) for h in heads]
        a_parts = _mm3_each(
            [q_loc[slabs[u].start + s * rows:slabs[u].start + (s + 1) * rows, hls[h]]
             for (u, h) in probs for s in range(n_sub)],
            [k_rel[s][slabs[u], hls[h]] for (u, h) in probs for s in range(n_sub)], NT)
        a = [jnp.where(tri, jnp.concatenate(a_parts[j * n_sub:(j + 1) * n_sub], axis=0), 0.0)
             for j in range(len(probs))]
        vs = [v_all[slabs[u], vls[h]] for (u, h) in probs]
        sts = [state_ref[u, h] for (u, h) in probs]
        o_state = _mm1_each([q_cum[slabs[u], hls[h]] for (u, h) in probs], sts)
        intra = _mm1_each(a, vs)
        upd = _mm1_each([kdec[slabs[u], hls[h]] for (u, h) in probs], vs, TN)
        for j, (u, h) in enumerate(probs):
            s_ref[u, h] = sts[j] * jnp.exp(key_column(b_last[u][:, hls[h]])) + upd[j]
        for h in heads:
            finish(h, jnp.concatenate([o_state[j] + intra[j] for j, (_, h2) in enumerate(probs) if h2 == h], axis=0))
    else:
        vs = [v_all[:, vl] for vl in vls]
        b_tot = _masked_sum(same, gk)
        kd = k * jnp.exp(-b_loc)
        kdec = k * jnp.exp(b_tot - b_loc)
        a = [jnp.where(incl, x, 0.0) for x in _mm3_each([q_loc[:, hl] for hl in hls], [kd[:, hl] for hl in hls], NT)]
        intra = _mm1_each(a, vs)
        probs = [(h, s, slice(s * rows, (s + 1) * rows)) for h in heads for s in range(n_sub)]
        sts = [state_ref[s, h] for (h, s, _) in probs]
        o_state = _mm1_each([q_loc[sl, hls[h]] for (h, _, sl) in probs], sts)
        upd = _mm1_each([kdec[sl, hls[h]] for (h, _, sl) in probs], [vs[h][sl] for (h, _, sl) in probs], TN)
        for (h, s, sl), st, du in zip(probs, sts, upd):
            s_ref[s, h] = st * jnp.exp(key_column(b_tot[s * rows:s * rows + 1, hls[h]])) + du
        for h in heads:
            finish(h, jnp.concatenate(o_state[h * n_sub:(h + 1) * n_sub], axis=0) + intra[h])


def _gla(proj, layer, n_layers, s0, prev, w_lr_pad, b_lr, norm_b, n_batch, t_len, chunk):
    rows = min(chunk, t_len)
    has_state = s0 is not None
    chained = t_len >= ROWS
    if chained:
        n_slab, n_state, nc = GLA_SEQS, GLA_SEQS, t_len // ROWS
        slabs = proj.reshape(n_batch, nc, ROWS, AB_N)
    else:
        n_slab, n_state, nc = 1, ROWS // t_len, 1
        slabs = proj.reshape(n_batch * t_len // ROWS, 1, ROWS, AB_N)
    nb = n_batch // n_state
    slab_spec = lambda width, col: pl.BlockSpec((n_slab, None, ROWS, width), lambda i, c: (i, c, 0, col))
    args, in_specs, aliases = _carry_outputs(prev, 1)
    n_alias = len(args)
    in_specs += [slab_spec(KB, OFF_QB // KB), slab_spec(KB, OFF_KB // KB), slab_spec(VB, OFF_VB // VB),
                 slab_spec(VB, OFF_ZB // VB), slab_spec(TAIL_W, OFF_TAIL // TAIL_W)]
    args += [slabs] * 5
    s_block = (n_state, N_HEAD_B, DK_B, DV_B)
    if has_state:
        in_specs.append(_layer_spec(s_block, layer))
        args.append(s0)
    in_specs += [
        pl.BlockSpec((TAIL_W, KB), lambda i, c: (0, 0)),
        pl.BlockSpec((1, KB), lambda i, c: (0, 0)),
        pl.BlockSpec((1, DV_B), lambda i, c: (0, 0)),
    ]
    args += [w_lr_pad, b_lr.reshape(1, KB), norm_b.reshape(1, DV_B)]
    o, s_all = pl.pallas_call(
        functools.partial(_gla_kernel, rows=rows, chained=chained, has_state=has_state, n_alias=n_alias),
        out_shape=(
            jax.ShapeDtypeStruct(slabs.shape[:3] + (VB,), BF16),
            jax.ShapeDtypeStruct((n_layers, n_batch, N_HEAD_B, DK_B, DV_B), F32),
        ),
        grid=(nb, nc),
        in_specs=in_specs,
        out_specs=(slab_spec(VB, 0), _layer_spec(s_block, layer)),
        input_output_aliases=aliases,
        compiler_params=pltpu.CompilerParams(
            dimension_semantics=("parallel", "arbitrary"), vmem_limit_bytes=VMEM_LIMIT),
        name="gla",
    )(*args)
    return o.reshape(n_batch * t_len, VB), s_all


def _lru_kernel(*refs, n_seq, rows, has_state, reset_first, n_alias):
    refs = refs[n_alias:]
    xb_ref, gate_ref = refs[:2]
    p = 2
    if has_state:
        h0_ref, c0_ref = refs[p], refs[p + 1]
        p += 2
    cw_ref, cb_ref, wg_ref, ba_ref, bx_ref, lam_ref = refs[p:p + 6]
    y_ref, h_ref, cn_ref = refs[p + 6:p + 9]
    xp_ref, xc_ref, a_ref, b_ref = refs[p + 9:p + 13]
    n = n_seq * rows
    c = pl.program_id(1)

    @pl.when(c == 0)
    def _init():
        if has_state:
            h_ref[...] = h0_ref[...]
            for s in range(n_seq):
                xp_ref[s, HIST - (CONV_W - 1):HIST, :] = c0_ref[s]
        else:
            h_ref[...] = jnp.zeros(h_ref.shape, F32)
            xp_ref[:, 0:HIST, :] = jnp.zeros((n_seq, HIST, W_LRU), F32)

    def emit(s, cs, acc):
        xc_ref[s * rows:(s + 1) * rows, cs] = acc

    _causal_conv_block(lambda s: xb_ref[s * rows:(s + 1) * rows, :], xp_ref, cw_ref, n_seq, rows, W_LRU,
                       emit, bias_ref=cb_ref)

    @pl.when(c == pl.num_programs(1) - 1)
    def _conv_out():
        for s in range(n_seq):
            cn_ref[s] = xp_ref[s, HIST - (CONV_W - 1):HIST, :]

    row = lax.broadcasted_iota(jnp.int32, (n, LRU_BW), 0)
    sub = lax.broadcasted_iota(jnp.int32, (n // 8, 8, LRU_BW), 1)
    scan_steps = [(d, sub >= d) for d in (1, 2, 4)]
    first_row = (row == 0) & (c == 0)
    for blk in range(LRU_BLOCKS):
        bl = slice(blk * LRU_BW, (blk + 1) * LRU_BW)
        xc = xc_ref[:, bl]
        pre = jnp.dot(xc.astype(BF16), wg_ref[blk], preferred_element_type=F32)
        t_r = jnp.tanh(pre[:, :LRU_BW] + ba_ref[:, bl])
        t_i = jnp.tanh(pre[:, LRU_BW:] + bx_ref[:, bl])
        gi = 0.5 * t_i + 0.5
        log_a = (t_r + 1.0) * (-0.5 * LRU_C * jax.nn.softplus(-lam_ref[:, bl]))
        a = jnp.exp(log_a)
        z = -jnp.tanh(log_a) * (a * a + 1.0)
        mult = z * lax.rsqrt(jnp.maximum(z, TINY))
        if reset_first:
            mult = jnp.where(first_row, 1.0, mult)
        b = mult * gi * xc
        a = a.reshape(n // 8, 8, LRU_BW)
        b = b.reshape(n // 8, 8, LRU_BW)
        for d, m in scan_steps:
            a_sh = pltpu.roll(a, d, axis=1)
            b_sh = pltpu.roll(b, d, axis=1)
            b = jnp.where(m, a * b_sh + b, b)
            a = jnp.where(m, a * a_sh, a)
        a_ref[:, bl] = a.reshape(n, LRU_BW)
        b_ref[:, bl] = b.reshape(n, LRU_BW)

    for s in range(n_seq):
        def body(g, hp, s=s):
            r0 = pl.multiple_of(s * rows + g * 8, 8)
            hs = a_ref[pl.ds(r0, 8), :] * hp + b_ref[pl.ds(r0, 8), :]
            b_ref[pl.ds(r0, 8), :] = hs
            return hs[7:8, :]

        h_ref[s] = lax.fori_loop(0, rows // 8, body, h_ref[s])

    lc = 512
    for c0 in range(0, W_LRU, lc):
        cs = slice(c0, c0 + lc)
        y_ref[:, cs] = (b_ref[:, cs] * _silu(gate_ref[:, cs])).astype(BF16)


def _lru(proj, layer, n_layers, h0, c0, prev, conv_w, conv_b, w_a, b_a, w_x, b_x, lam, n_batch, t_len,
         reset_first):
    has_state = h0 is not None
    if t_len >= LRU_ROWS:
        rows, n_seq = LRU_ROWS, 1
    else:
        rows, n_seq = t_len, LRU_SHORT_ROWS // t_len
    n = rows * n_seq
    nb = n_batch // n_seq
    nc = t_len // rows
    args, in_specs, aliases = _carry_outputs(prev, 1)
    n_alias = len(args)
    in_specs += [
        pl.BlockSpec((n, W_LRU), lambda i, c: (i * nc + c, 0)),
        pl.BlockSpec((n, W_LRU), lambda i, c: (i * nc + c, 1)),
    ]
    args += [proj, proj]
    h_block = (n_seq, 1, W_LRU)
    c_block = (n_seq, CONV_W - 1, W_LRU)
    if has_state:
        in_specs += [_layer_spec(h_block, layer), _layer_spec(c_block, layer)]
        args += [h0, c0]
    vec = pl.BlockSpec((1, W_LRU), lambda i, c: (0, 0))
    w_gates = (0.5 * jnp.concatenate([w_a, w_x], axis=-1)).astype(BF16)
    blk = pl.BlockSpec(w_gates.shape, lambda i, c: (0, 0, 0))
    in_specs += [pl.BlockSpec((CONV_W, W_LRU), lambda i, c: (0, 0)), vec, blk, vec, vec, vec]
    args += [conv_w, conv_b.reshape(1, W_LRU), w_gates, 0.5 * b_a.reshape(1, W_LRU),
             0.5 * b_x.reshape(1, W_LRU), lam.reshape(1, W_LRU)]
    return pl.pallas_call(
        functools.partial(_lru_kernel, n_seq=n_seq, rows=rows, has_state=has_state,
                          reset_first=reset_first, n_alias=n_alias),
        out_shape=(
            jax.ShapeDtypeStruct((n_batch * t_len, W_LRU), BF16),
            jax.ShapeDtypeStruct((n_layers, n_batch, 1, W_LRU), F32),
            jax.ShapeDtypeStruct((n_layers, n_batch, CONV_W - 1, W_LRU), F32),
        ),
        grid=(nb, nc),
        in_specs=in_specs,
        out_specs=(
            pl.BlockSpec((n, W_LRU), lambda i, c: (i * nc + c, 0)),
            _layer_spec(h_block, layer),
            _layer_spec(c_block, layer),
        ),
        input_output_aliases=aliases,
        scratch_shapes=[
            pltpu.VMEM((n_seq, HIST + rows, W_LRU), F32),
            pltpu.VMEM((n, W_LRU), F32),
            pltpu.VMEM((n, W_LRU), F32),
            pltpu.VMEM((n, W_LRU), F32),
        ],
        compiler_params=pltpu.CompilerParams(
            dimension_semantics=("parallel", "arbitrary"), vmem_limit_bytes=VMEM_LIMIT),
        name="lru",
    )(*args)


def _ab_w_in_kernel(w_ref, o_ref):
    sizes = [QKV_A, N_HEAD_A, N_HEAD_A, VA, KB, KB, VB, LOWRANK, VB]
    offs = [0]
    for s in sizes:
        offs.append(offs[-1] + s)
    w = w_ref[...]
    qkv, b_raw, a_raw, z_a, q_b, k_b, v_b, lr_b, z_b = [w[:, offs[i]:offs[i + 1]] for i in range(9)]
    used = OFF_TAIL + 2 * N_HEAD_A + LOWRANK
    pad = jnp.zeros((w.shape[0], AB_N - used), w.dtype)
    o_ref[...] = jnp.concatenate([qkv, z_a, q_b, k_b, v_b, z_b, b_raw, a_raw, lr_b, pad], axis=1)


def _ab_w_in_layout(w, layer):
    _, d, n = w.shape
    tr = 256
    return pl.pallas_call(
        _ab_w_in_kernel,
        out_shape=jax.ShapeDtypeStruct((d, AB_N), BF16),
        grid=(d // tr,),
        in_specs=[pl.BlockSpec((None, tr, n), lambda i: (layer, i, 0))],
        out_specs=pl.BlockSpec((tr, AB_N), lambda i: (i, 0)),
        compiler_params=pltpu.CompilerParams(dimension_semantics=("parallel",)),
        name="w_in_layout",
    )(w)


def _cast_kernel(w_ref, o_ref):
    o_ref[...] = w_ref[...].astype(BF16)


def _layer_bf16(w, layer):
    _, r, c = w.shape
    tr = 512
    return pl.pallas_call(
        _cast_kernel,
        out_shape=jax.ShapeDtypeStruct((r, c), BF16),
        grid=(r // tr,),
        in_specs=[pl.BlockSpec((None, tr, c), lambda i: (layer, i, 0))],
        out_specs=pl.BlockSpec((tr, c), lambda i: (i, 0)),
        compiler_params=pltpu.CompilerParams(dimension_semantics=("parallel",)),
        name="w_cast",
    )(w)


def _tail_row(vals, lane0):
    return jnp.zeros((TAIL_W,), F32).at[lane0:lane0 + vals.shape[0]].set(vals.astype(F32))


def kernel(x_prompt, x_sample, state_delta, state_delta_conv, state_gla, state_lru, state_lru_conv,
           ab_norm, ab_w_in, ab_conv_w, ab_a_log, ab_dt_bias, ab_norm_a, ab_gla_w_lr, ab_gla_b_lr,
           ab_norm_b, ab_w_out, lru_norm, lru_w_in, lru_conv_w, lru_conv_b, lru_w_a, lru_b_a,
           lru_w_x, lru_b_x, lru_lambda, lru_w_out, final_norm):
    n_ab, n_lru = ab_norm.shape[0], lru_norm.shape[0]
    depth = n_ab + n_lru
    lru_h0 = state_lru.reshape(n_lru, -1, 1, W_LRU)
    groups = []
    for x, carried in ((x_prompt, False), (x_sample, True)):
        groups.append(dict(x=x.reshape(-1, D_MODEL), nb=x.shape[0], t=x.shape[1], carried=carried,
                           delta=None, gla=None, lru=None))

    ab_w_in_bf16 = ab_w_in.astype(BF16)
    for layer in range(depth):
        j = layer // 2
        last = layer == depth - 1
        if layer % 2 == 0:
            w_in = _ab_w_in_layout(ab_w_in_bf16, j)
            w_out = _layer_bf16(ab_w_out, j)
            gate_params = jnp.stack([_tail_row(ab_a_log[j], LANE_A), _tail_row(ab_dt_bias[j], LANE_A)])
            w_lr_pad = jnp.zeros((TAIL_W, KB), F32).at[LANE_LR:LANE_LR + LOWRANK].set(ab_gla_w_lr[j])
            for g in groups:
                proj = _norm_mm(g["x"], ab_norm[j], w_in, AB_TN, n_valid=OFF_TAIL + TAIL_W)
                s_a, c_a, s_b = (state_delta, state_delta_conv, state_gla) if g["carried"] else (None,) * 3
                o_a, *g["delta"] = _delta(proj, j, n_ab, s_a, c_a, g["delta"], ab_conv_w[j], gate_params,
                                          ab_norm_a[j], g["nb"], g["t"])
                o_b, *g["gla"] = _gla(proj, j, n_ab, s_b, g["gla"], w_lr_pad, ab_gla_b_lr[j], ab_norm_b[j],
                                      g["nb"], g["t"], 16)
                g["x"] = _out_mm([o_a, o_b], w_out, g["x"], final_norm if last else None)
        else:
            w_in = _layer_bf16(lru_w_in, j)
            w_out = _layer_bf16(lru_w_out, j)
            for g in groups:
                proj = _norm_mm(g["x"], lru_norm[j], w_in, 1024)
                h0, c0 = (lru_h0, state_lru_conv) if g["carried"] else (None, None)
                y, *g["lru"] = _lru(proj, j, n_lru, h0, c0, g["lru"], lru_conv_w[j], lru_conv_b[j],
                                    lru_w_a[j], lru_b_a[j], lru_w_x[j], lru_b_x[j], lru_lambda[j],
                                    g["nb"], g["t"], reset_first=not g["carried"])
                g["x"] = _out_mm([y], w_out, g["x"], final_norm if last else None)

    outs = []
    for g, x in zip(groups, (x_prompt, x_sample)):
        h_all, lconv = g["lru"]
        outs.append([g["x"].reshape(x.shape), g["delta"][0], g["delta"][1], g["gla"][0],
                     h_all.reshape(n_lru, g["nb"], W_LRU), lconv])
    p, s = outs
    return (p[0], s[0], p[1], p[2], p[3], p[4], p[5], s[1], s[2], s[3], s[4], s[5])
```

```python
import functools

import jax
import jax.numpy as jnp
from jax import lax
from jax.experimental import pallas as pl
from jax.experimental.pallas import tpu as pltpu

F32 = jnp.float32
BF16 = jnp.bfloat16

D_MODEL = 2048
N_HEAD_A, DK_A, DV_A = 8, 128, 128
N_HEAD_B, DK_B, DV_B = 4, 128, 256
KA = N_HEAD_A * DK_A
VA = N_HEAD_A * DV_A
KB = N_HEAD_B * DK_B
VB = N_HEAD_B * DV_B
QKV_A = 2 * KA + VA
LOWRANK = 16
GLA_NORMALIZER = 16.0
W_LRU = D_MODEL
LRU_BLOCKS = 16
LRU_BW = W_LRU // LRU_BLOCKS
LRU_C = 8.0
CONV_W = 4
EPS = 1e-6
LANES = 128
TINY = 1.1754944e-38

OFF_QKV = 0
OFF_ZA = OFF_QKV + QKV_A
OFF_QB = OFF_ZA + VA
OFF_KB = OFF_QB + KB
OFF_VB = OFF_KB + KB
OFF_ZB = OFF_VB + VB
OFF_TAIL = OFF_ZB + VB
TAIL_W = 128
AB_TN = 1536
AB_N = 7680
LANE_BETA, LANE_A, LANE_LR = 0, N_HEAD_A, 2 * N_HEAD_A

ROWS = 64
DELTA_M = 128
DELTA_SEQS = 2
GLA_SEQS = 4
LRU_ROWS = 512
LRU_SHORT_ROWS = 128
HIST = 8
VMEM_LIMIT = 56 * 1024 * 1024

NN = (((1,), (0,)), ((), ()))
NT = (((1,), (1,)), ((), ()))
TN = (((0,), (0,)), ((), ()))


def _masked_sum(mask, x):
    m = jnp.where(mask, 1.0, 0.0).astype(BF16)
    hi = x.astype(BF16)
    rest = x - hi.astype(F32)
    mid = rest.astype(BF16)
    lo = (rest - mid.astype(F32)).astype(BF16)
    if m.shape[1] % LANES:
        dot = functools.partial(jnp.dot, preferred_element_type=F32)
        return dot(m, hi) + (dot(m, mid) + dot(m, lo))
    return jnp.dot(jnp.concatenate([m, m, m], axis=1), jnp.concatenate([hi, mid, lo], axis=0),
                   preferred_element_type=F32)


def _split(a):
    hi = a.astype(BF16)
    return hi, (a - hi.astype(F32)).astype(BF16)


def _mm3(a, b, dims=NN):
    ah, al = a if isinstance(a, tuple) else _split(a)
    bh, bl = b if isinstance(b, tuple) else _split(b)
    (ca,), (cb,) = dims[0]
    dot = functools.partial(lax.dot_general, dimension_numbers=dims, preferred_element_type=F32)
    if ah.shape[ca] % LANES:
        return dot(ah, bh) + (dot(ah, bl) + dot(al, bh))
    return dot(jnp.concatenate([ah, ah, al], axis=ca), jnp.concatenate([bh, bl, bh], axis=cb))


def _split_each(xs):
    return [x if isinstance(x, tuple) else _split(x) for x in xs]


def _mm3_each(as_, bs, dims=NN):
    sa, sb = _split_each(as_), _split_each(bs)
    return [_mm3(a, b, dims) for a, b in zip(sa, sb)]


def _mm1_each(as_, bs, dims=NN):
    ca = [a.astype(BF16) for a in as_]
    cb = [b.astype(BF16) for b in bs]
    return [lax.dot_general(a, b, dims, preferred_element_type=F32) for a, b in zip(ca, cb)]


def _sigmoid(x):
    return 0.5 * jnp.tanh(0.5 * x) + 0.5


def _silu(x):
    h = 0.5 * x
    return h + h * jnp.tanh(h)


def _norm_mm_kernel(x_ref, g_ref, w_ref, o_ref, h_ref):
    @pl.when(pl.program_id(1) == 0)
    def _():
        x = x_ref[...]
        ms = jnp.mean(x * x, axis=-1, keepdims=True)
        h_ref[...] = (x * lax.rsqrt(ms + EPS) * g_ref[...]).astype(BF16)

    o_ref[...] = jnp.dot(h_ref[...], w_ref[...], preferred_element_type=F32)


def _norm_mm(x, g, w, tn):
    m, k = x.shape
    n = w.shape[1]
    tm = min(m, 1024)
    w_spec = pl.BlockSpec((k, tn), lambda i, j: (0, j))
    return pl.pallas_call(
        _norm_mm_kernel,
        out_shape=jax.ShapeDtypeStruct((m, n), F32),
        grid=(m // tm, n // tn),
        in_specs=[
            pl.BlockSpec((tm, k), lambda i, j: (i, 0)),
            pl.BlockSpec((1, k), lambda i, j: (0, 0)),
            w_spec,
        ],
        out_specs=pl.BlockSpec((tm, tn), lambda i, j: (i, j)),
        scratch_shapes=[pltpu.VMEM((tm, k), BF16)],
        compiler_params=pltpu.CompilerParams(
            dimension_semantics=("parallel", "arbitrary"), vmem_limit_bytes=VMEM_LIMIT),
        name="norm_mm",
    )(x, g.reshape(1, k), w)


def _out_mm_kernel(*refs, n_in, final):
    o_refs = refs[:n_in]
    w_ref, x_ref = refs[n_in], refs[n_in + 1]
    out_ref = refs[-1]
    acc = x_ref[...]
    k0 = 0
    for o_ref in o_refs:
        kk = o_ref.shape[1]
        acc = acc + jnp.dot(o_ref[...], w_ref[k0:k0 + kk, :], preferred_element_type=F32)
        k0 += kk
    if final:
        fg_ref = refs[n_in + 2]
        ms = jnp.mean(acc * acc, axis=-1, keepdims=True)
        acc = acc * lax.rsqrt(ms + EPS) * fg_ref[...]
    out_ref[...] = acc


def _out_mm(os_, w, x, final_g=None):
    m, d = x.shape
    tm = 512
    final = final_g is not None
    in_specs = [pl.BlockSpec((tm, o.shape[1]), lambda i: (i, 0)) for o in os_]
    in_specs += [pl.BlockSpec(w.shape, lambda i: (0, 0)), pl.BlockSpec((tm, d), lambda i: (i, 0))]
    args = list(os_) + [w, x]
    if final:
        in_specs.append(pl.BlockSpec((1, d), lambda i: (0, 0)))
        args.append(final_g.reshape(1, d))
    return pl.pallas_call(
        functools.partial(_out_mm_kernel, n_in=len(os_), final=final),
        out_shape=jax.ShapeDtypeStruct((m, d), F32),
        grid=(m // tm,),
        in_specs=in_specs,
        out_specs=pl.BlockSpec((tm, d), lambda i: (i, 0)),
        compiler_params=pltpu.CompilerParams(
            dimension_semantics=("parallel",), vmem_limit_bytes=VMEM_LIMIT),
        name="out_mm",
    )(*args)


def _causal_conv_block(seq_rows, xp_ref, cw_ref, n_seq, rows, width, emit, bias_ref=None):
    lc = LANES
    groups = rows // 8
    sub = lax.broadcasted_iota(jnp.int32, (groups, 8, lc), 1)
    for s in range(n_seq):
        xp_ref[s, HIST:HIST + rows, :] = seq_rows(s)
        for c0 in range(0, width, lc):
            cs = slice(c0, c0 + lc)
            x3 = xp_ref[s, :, cs].reshape(groups + HIST // 8, 8, lc)
            acc = x3[1:] * cw_ref[CONV_W - 1:CONV_W, cs]
            for d in range(1, CONV_W):
                rolled = pltpu.roll(x3, d, axis=1)
                tap = jnp.where(sub >= d, rolled[1:], rolled[:-1])
                acc = acc + tap * cw_ref[CONV_W - 1 - d:CONV_W - d, cs]
            acc = acc.reshape(rows, lc)
            if bias_ref is not None:
                acc = acc + bias_ref[:, cs]
            emit(s, cs, acc)
        xp_ref[s, 0:HIST, :] = xp_ref[s, rows:rows + HIST, :]


def _seq_masks(n, rows):
    ri = lax.broadcasted_iota(jnp.int32, (n, n), 0)
    ci = lax.broadcasted_iota(jnp.int32, (n, n), 1)
    if rows == n:
        same = ri >= 0
    else:
        sh = rows.bit_length() - 1
        same = (ri >> sh) == (ci >> sh)
    return ri, ci, same


NEUMANN_BLOCK = 8


def _inverse_masks(ri, ci, rows):
    assert rows >= NEUMANN_BLOCK
    blk = lambda k: (ri >> (k.bit_length() - 1)) == (ci >> (k.bit_length() - 1))
    levels = []
    k = NEUMANN_BLOCK
    while k < rows:
        levels.append(blk(2 * k) & jnp.logical_not(blk(k)))
        k *= 2
    return blk(NEUMANN_BLOCK), levels


def _unit_lower_inverse_each(as_, masks):
    base, levels = masks
    a8 = [jnp.where(base, a, 0.0) for a in as_]
    p2 = _mm1_each(a8, a8)
    p4 = _mm1_each(p2, p2)
    a8p2 = _mm1_each(a8, p2)
    n = [(p - ap) - a for a, p, ap in zip(a8, p2, a8p2)]
    n = [ni + (p + d) for ni, p, d in zip(n, p4, _mm1_each(n, p4))]
    for lvl in levels:
        off = [jnp.where(lvl, a, 0.0) for a in as_]
        t_off = [o + d for o, d in zip(off, _mm1_each(n, off))]
        n = [ni - (to + d) for ni, to, d in zip(n, t_off, _mm1_each(t_off, n))]
    res = [-((a + ni) + d) for a, ni, d in zip(as_, n, _mm3_each(as_, n))]
    return [ni + (r_ + d) for ni, r_, d in zip(n, res, _mm1_each(n, res))]


def _delta_kernel(*refs, n_seq, rows, has_state, single_step, n_alias):
    refs = refs[n_alias:]
    qkv_ref, z_ref, tail_ref = refs[:3]
    p = 3
    if has_state:
        s0_ref, c0_ref = refs[p], refs[p + 1]
        p += 2
    cw_ref, gp_ref, na_ref = refs[p:p + 3]
    o_ref, s_ref, cn_ref = refs[p + 3:p + 6]
    xp_ref, act_ref = refs[p + 6:p + 8]
    n = n_seq * rows
    c = pl.program_id(1)

    @pl.when(c == 0)
    def _init():
        if has_state:
            if not single_step:
                s_ref[...] = s0_ref[...]
            for s in range(n_seq):
                xp_ref[s, HIST - (CONV_W - 1):HIST, :] = c0_ref[s]
        else:
            s_ref[...] = jnp.zeros(s_ref.shape, F32)
            xp_ref[:, 0:HIST, :] = jnp.zeros((n_seq, HIST, QKV_A), F32)

    def emit(s, cs, acc):
        act_ref[s * rows:(s + 1) * rows, cs] = _silu(acc)

    def seq_rows(s):
        slab, off = divmod(s * rows, ROWS)
        return qkv_ref[slab, off:off + rows, :]

    _causal_conv_block(seq_rows, xp_ref, cw_ref, n_seq, rows, QKV_A, emit)

    @pl.when(c == pl.num_programs(1) - 1)
    def _conv_out():
        for s in range(n_seq):
            cn_ref[s] = xp_ref[s, HIST - (CONV_W - 1):HIST, :]

    n_slab = n // ROWS
    head_group = DELTA_M // n
    _, _, same_n = _seq_masks(n, rows)
    rn = lax.broadcasted_iota(jnp.int32, (n, n), 0)
    cn = lax.broadcasted_iota(jnp.int32, (n, n), 1)
    tail = tail_ref[...].reshape(n, TAIL_W)
    btile = _sigmoid(tail)
    gtile = -jnp.exp(gp_ref[0:1, :]) * jax.nn.softplus(tail + gp_ref[1:2, :])
    g_cum = _masked_sum(same_n & (rn >= cn), gtile)
    g_tot = _masked_sum(same_n, gtile)
    e_cum = jnp.exp(g_cum)
    e_rest = jnp.exp(g_tot - g_cum)
    g_cum_t = g_cum.T

    m = DELTA_M
    ri, ci, same = _seq_masks(m, rows)
    incl = same & (ri >= ci)
    strict = same & (ri > ci)
    inv_masks = _inverse_masks(ri, ci, rows)
    stack = lambda xs: xs[0] if len(xs) == 1 else jnp.concatenate(xs, axis=0)
    groups = [range(g * head_group, (g + 1) * head_group) for g in range(N_HEAD_A // head_group)]
    qn, kn, v, beta, gc, eg, er, g_row = ([] for _ in range(8))
    for heads in groups:
        q_h, k_h, v_h = [], [], []
        for h in heads:
            q = act_ref[:, h * DK_A:(h + 1) * DK_A]
            k = act_ref[:, KA + h * DK_A:KA + (h + 1) * DK_A]
            q_h.append(q * lax.rsqrt(jnp.sum(q * q, axis=-1, keepdims=True) + EPS) * (DK_A ** -0.5))
            k_h.append(k * lax.rsqrt(jnp.sum(k * k, axis=-1, keepdims=True) + EPS))
            v_h.append(act_ref[:, 2 * KA + h * DV_A:2 * KA + (h + 1) * DV_A])
        qn.append(stack(q_h))
        kn.append(stack(k_h))
        v.append(stack(v_h))
        beta.append(stack([btile[:, LANE_BETA + h:LANE_BETA + h + 1] for h in heads]))
        gc.append(stack([g_cum[:, LANE_A + h:LANE_A + h + 1] for h in heads]))
        eg.append(stack([e_cum[:, LANE_A + h:LANE_A + h + 1] for h in heads]))
        er.append(stack([e_rest[:, LANE_A + h:LANE_A + h + 1] for h in heads]))
        g_row.append(jnp.concatenate([g_cum_t[LANE_A + h:LANE_A + h + 1, :] for h in heads], axis=1))
    decay = [jnp.where(incl, jnp.exp(jnp.where(incl, c_ - r_, 0.0)), 0.0) for c_, r_ in zip(gc, g_row)]
    kb = [k_ * b_ for k_, b_ in zip(kn, beta)]
    prod = _mm3_each([stack([kb_, q_]) for kb_, q_ in zip(kb, qn)], kn, NT)
    a = [jnp.where(strict, p_[:m] * d_, 0.0) for p_, d_ in zip(prod, decay)]
    qk = [p_[m:] * d_ for p_, d_ in zip(prod, decay)]
    n_inv = _unit_lower_inverse_each(a, inv_masks)
    rhs = [jnp.concatenate([v_ * b_, kb_ * e_], axis=1) for v_, b_, kb_, e_ in zip(v, beta, kb, eg)]
    sol = [r_ + d for r_, d in zip(rhs, _mm1_each(n_inv, rhs))]
    qg = [q_ * e_ for q_, e_ in zip(qn, eg)]
    kdec = [k_ * e_ for k_, e_ in zip(kn, er)]
    probs = [(g, i, h, s, slice(i * n + s * rows, i * n + (s + 1) * rows))
             for g, heads in enumerate(groups) for i, h in enumerate(heads) for s in range(n_seq)]
    state_ref = s0_ref if (has_state and single_step) else s_ref
    states = [state_ref[s, h] for (_, _, h, s, _) in probs]
    r = _mm1_each([stack([sol[g][sl, DV_A:], qg[g][sl]]) for (g, _, _, _, sl) in probs], states)
    v_new = [sol[g][sl, :DV_A] - r_[:rows] for (g, _, _, _, sl), r_ in zip(probs, r)]
    per_group = lambda xs: [stack([x for (g2, *_), x in zip(probs, xs) if g2 == g]) for g in range(len(groups))]
    v_new_g = per_group(v_new)
    o_inter_g = per_group([r_[rows:] for r_ in r])
    o = [oi + d for oi, d in zip(o_inter_g, _mm1_each(qk, v_new_g))]
    upd = _mm1_each([kdec[g][sl] for (g, _, _, _, sl) in probs], v_new, TN)
    for (g, i, h, s, sl), st, du in zip(probs, states, upd):
        g_last = g_tot[s * rows:s * rows + 1, LANE_A + h:LANE_A + h + 1]
        s_ref[s, h] = st * jnp.exp(g_last) + du
    for g, heads in enumerate(groups):
        for i, h in enumerate(heads):
            hl = slice(h * DV_A, (h + 1) * DV_A)
            oh = o[g][i * n:(i + 1) * n]
            oh = oh * lax.rsqrt(jnp.mean(oh * oh, axis=-1, keepdims=True) + EPS) * na_ref[...]
            oh = oh * _silu(z_ref[:, :, hl].reshape(n, DV_A))
            o_ref[:, :, hl] = oh.reshape(n_slab, ROWS, DV_A).astype(BF16)


def _layer_spec(block, layer):
    zeros = (0,) * (len(block) - 1)
    return pl.BlockSpec((None,) + tuple(block), lambda i, c: (layer, i) + zeros)


def _carry_outputs(prev, out_start):
    if prev is None:
        return [], [], {}
    specs = [pl.BlockSpec(memory_space=pl.ANY) for _ in prev]
    return list(prev), specs, {k: out_start + k for k in range(len(prev))}


def _delta(proj, layer, n_layers, s0, c0, prev, conv_w, gate_params, norm_a, n_batch, t_len):
    rows = min(ROWS, t_len)
    if t_len >= ROWS:
        n_seq, n_slab = DELTA_SEQS, DELTA_SEQS
        slabs = proj.reshape(n_batch, t_len // ROWS, ROWS, AB_N)
    else:
        n_seq, n_slab = ROWS // t_len, 1
        slabs = proj.reshape(n_batch * t_len // ROWS, 1, ROWS, AB_N)
    n = n_seq * rows
    nb = n_batch // n_seq
    nc = t_len // rows
    has_state = s0 is not None
    slab_spec = lambda width, col: pl.BlockSpec((n_slab, None, ROWS, width), lambda i, c: (i, c, 0, col))
    args, in_specs, aliases = _carry_outputs(prev, 1)
    n_alias = len(args)
    in_specs += [slab_spec(QKV_A, 0), slab_spec(VA, OFF_ZA // VA), slab_spec(TAIL_W, OFF_TAIL // TAIL_W)]
    args += [slabs, slabs, slabs]
    s_block = (n_seq, N_HEAD_A, DK_A, DV_A)
    c_block = (n_seq, CONV_W - 1, QKV_A)
    if has_state:
        in_specs += [_layer_spec(s_block, layer), _layer_spec(c_block, layer)]
        args += [s0, c0]
    in_specs += [
        pl.BlockSpec((CONV_W, QKV_A), lambda i, c: (0, 0)),
        pl.BlockSpec((2, TAIL_W), lambda i, c: (0, 0)),
        pl.BlockSpec((1, DV_A), lambda i, c: (0, 0)),
    ]
    args += [conv_w, gate_params, norm_a.reshape(1, DV_A)]
    o, s_all, conv_all = pl.pallas_call(
        functools.partial(_delta_kernel, n_seq=n_seq, rows=rows, has_state=has_state, single_step=nc == 1,
                          n_alias=n_alias),
        out_shape=(
            jax.ShapeDtypeStruct(slabs.shape[:3] + (VA,), BF16),
            jax.ShapeDtypeStruct((n_layers, n_batch, N_HEAD_A, DK_A, DV_A), F32),
            jax.ShapeDtypeStruct((n_layers, n_batch, CONV_W - 1, QKV_A), F32),
        ),
        grid=(nb, nc),
        in_specs=in_specs,
        out_specs=(
            slab_spec(VA, 0),
            _layer_spec(s_block, layer),
            _layer_spec(c_block, layer),
        ),
        scratch_shapes=[
            pltpu.VMEM((n_seq, HIST + rows, QKV_A), F32),
            pltpu.VMEM((n, QKV_A), F32),
        ],
        input_output_aliases=aliases,
        compiler_params=pltpu.CompilerParams(
            dimension_semantics=("parallel", "arbitrary"), vmem_limit_bytes=VMEM_LIMIT),
        name="delta",
    )(*args)
    return o.reshape(n_batch * t_len, VA), s_all, conv_all


def _gla_kernel(*refs, rows, chained, has_state, n_alias):
    refs = refs[n_alias:]
    q_ref, k_ref, v_ref, z_ref, tail_ref = refs[:5]
    p = 5
    if has_state:
        s0_ref = refs[p]
        p += 1
    wlr_ref, blr_ref, nb_ref = refs[p:p + 3]
    o_ref, s_ref = refs[p + 3:p + 5]
    n_slab = q_ref.shape[0]
    n = n_slab * ROWS
    n_sub = ROWS // rows
    c = pl.program_id(1)

    state_ref = s0_ref if (has_state and not chained) else s_ref

    @pl.when(c == 0)
    def _init():
        if not has_state:
            s_ref[...] = jnp.zeros(s_ref.shape, F32)
        elif chained:
            s_ref[...] = s0_ref[...]

    ri, ci, same = _seq_masks(ROWS, rows)
    incl = same & (ri >= ci)
    slabs = [slice(u * ROWS, (u + 1) * ROWS) for u in range(n_slab)]
    cumulate = lambda mask, x: jnp.concatenate(
        [_masked_sum(mask, x[sl]) for sl in slabs], axis=0)
    gk = jax.nn.log_sigmoid(_mm3(tail_ref[...].reshape(n, TAIL_W), wlr_ref[...]) + blr_ref[...]) / GLA_NORMALIZER
    b_loc = cumulate(incl, gk)
    q = q_ref[...].reshape(n, KB) * (DK_B ** -0.5)
    k = k_ref[...].reshape(n, KB)
    v_all = v_ref[...].reshape(n, VB)
    q_loc = q * jnp.exp(b_loc)
    heads = range(N_HEAD_B)
    hls = [slice(h * DK_B, (h + 1) * DK_B) for h in heads]
    vls = [slice(h * DV_B, (h + 1) * DV_B) for h in heads]

    def finish(h, o):
        o = o * lax.rsqrt(jnp.mean(o * o, axis=-1, keepdims=True) + EPS) * nb_ref[...]
        o = o * _silu(z_ref[:, :, vls[h]].reshape(n, DV_B))
        o_ref[:, :, vls[h]] = o.reshape(n_slab, ROWS, DV_B).astype(BF16)

    def key_column(row_vals):
        return jnp.broadcast_to(row_vals, (8, DK_B)).T[:, 0:1]

    if chained:
        tri = ri >= ci
        b_cum = cumulate(tri, gk)
        per_slab = lambda rws: jnp.concatenate([jnp.broadcast_to(r_, (ROWS, KB)) for r_ in rws], axis=0)
        b_last = [b_cum[sl.stop - 1:sl.stop] for sl in slabs]
        q_cum = q * jnp.exp(b_cum)
        kdec = k * jnp.exp(per_slab(b_last) - b_cum)
        row = lax.broadcasted_iota(jnp.int32, (n, KB), 0) & (ROWS - 1)
        k_rel = []
        for s in range(n_sub):
            start = [b_cum[sl.start + s * rows - 1:sl.start + s * rows] if s else jnp.zeros((1, KB), F32)
                     for sl in slabs]
            k_rel.append(k * jnp.exp(jnp.where(row < (s + 1) * rows, per_slab(start) - b_cum, 0.0)))
        probs = [(u, h) for u in range(n_slab) for h in heads]
        a_parts = _mm3_each(
            [q_loc[slabs[u].start + s * rows:slabs[u].start + (s + 1) * rows, hls[h]]
             for (u, h) in probs for s in range(n_sub)],
            [k_rel[s][slabs[u], hls[h]] for (u, h) in probs for s in range(n_sub)], NT)
        a = [jnp.where(tri, jnp.concatenate(a_parts[j * n_sub:(j + 1) * n_sub], axis=0), 0.0)
             for j in range(len(probs))]
        vs = [v_all[slabs[u], vls[h]] for (u, h) in probs]
        sts = [state_ref[u, h] for (u, h) in probs]
        o_state = _mm1_each([q_cum[slabs[u], hls[h]] for (u, h) in probs], sts)
        intra = _mm1_each(a, vs)
        upd = _mm1_each([kdec[slabs[u], hls[h]] for (u, h) in probs], vs, TN)
        for j, (u, h) in enumerate(probs):
            s_ref[u, h] = sts[j] * jnp.exp(key_column(b_last[u][:, hls[h]])) + upd[j]
        for h in heads:
            finish(h, jnp.concatenate([o_state[j] + intra[j] for j, (_, h2) in enumerate(probs) if h2 == h], axis=0))
    else:
        vs = [v_all[:, vl] for vl in vls]
        b_tot = _masked_sum(same, gk)
        kd = k * jnp.exp(-b_loc)
        kdec = k * jnp.exp(b_tot - b_loc)
        a = [jnp.where(incl, x, 0.0) for x in _mm3_each([q_loc[:, hl] for hl in hls], [kd[:, hl] for hl in hls], NT)]
        intra = _mm1_each(a, vs)
        probs = [(h, s, slice(s * rows, (s + 1) * rows)) for h in heads for s in range(n_sub)]
        sts = [state_ref[s, h] for (h, s, _) in probs]
        o_state = _mm1_each([q_loc[sl, hls[h]] for (h, _, sl) in probs], sts)
        upd = _mm1_each([kdec[sl, hls[h]] for (h, _, sl) in probs], [vs[h][sl] for (h, _, sl) in probs], TN)
        for (h, s, sl), st, du in zip(probs, sts, upd):
            s_ref[s, h] = st * jnp.exp(key_column(b_tot[s * rows:s * rows + 1, hls[h]])) + du
        for h in heads:
            finish(h, jnp.concatenate(o_state[h * n_sub:(h + 1) * n_sub], axis=0) + intra[h])


def _gla(proj, layer, n_layers, s0, prev, w_lr_pad, b_lr, norm_b, n_batch, t_len, chunk):
    rows = min(chunk, t_len)
    has_state = s0 is not None
    chained = t_len >= ROWS
    if chained:
        n_slab, n_state, nc = GLA_SEQS, GLA_SEQS, t_len // ROWS
        slabs = proj.reshape(n_batch, nc, ROWS, AB_N)
    else:
        n_slab, n_state, nc = 1, ROWS // t_len, 1
        slabs = proj.reshape(n_batch * t_len // ROWS, 1, ROWS, AB_N)
    nb = n_batch // n_state
    slab_spec = lambda width, col: pl.BlockSpec((n_slab, None, ROWS, width), lambda i, c: (i, c, 0, col))
    args, in_specs, aliases = _carry_outputs(prev, 1)
    n_alias = len(args)
    in_specs += [slab_spec(KB, OFF_QB // KB), slab_spec(KB, OFF_KB // KB), slab_spec(VB, OFF_VB // VB),
                 slab_spec(VB, OFF_ZB // VB), slab_spec(TAIL_W, OFF_TAIL // TAIL_W)]
    args += [slabs] * 5
    s_block = (n_state, N_HEAD_B, DK_B, DV_B)
    if has_state:
        in_specs.append(_layer_spec(s_block, layer))
        args.append(s0)
    in_specs += [
        pl.BlockSpec((TAIL_W, KB), lambda i, c: (0, 0)),
        pl.BlockSpec((1, KB), lambda i, c: (0, 0)),
        pl.BlockSpec((1, DV_B), lambda i, c: (0, 0)),
    ]
    args += [w_lr_pad, b_lr.reshape(1, KB), norm_b.reshape(1, DV_B)]
    o, s_all = pl.pallas_call(
        functools.partial(_gla_kernel, rows=rows, chained=chained, has_state=has_state, n_alias=n_alias),
        out_shape=(
            jax.ShapeDtypeStruct(slabs.shape[:3] + (VB,), BF16),
            jax.ShapeDtypeStruct((n_layers, n_batch, N_HEAD_B, DK_B, DV_B), F32),
        ),
        grid=(nb, nc),
        in_specs=in_specs,
        out_specs=(slab_spec(VB, 0), _layer_spec(s_block, layer)),
        input_output_aliases=aliases,
        compiler_params=pltpu.CompilerParams(
            dimension_semantics=("parallel", "arbitrary"), vmem_limit_bytes=VMEM_LIMIT),
        name="gla",
    )(*args)
    return o.reshape(n_batch * t_len, VB), s_all


def _lru_kernel(*refs, n_seq, rows, has_state, reset_first, n_alias):
    refs = refs[n_alias:]
    xb_ref, gate_ref = refs[:2]
    p = 2
    if has_state:
        h0_ref, c0_ref = refs[p], refs[p + 1]
        p += 2
    cw_ref, cb_ref, wg_ref, ba_ref, bx_ref, lam_ref = refs[p:p + 6]
    y_ref, h_ref, cn_ref = refs[p + 6:p + 9]
    xp_ref, xc_ref, a_ref, b_ref = refs[p + 9:p + 13]
    n = n_seq * rows
    c = pl.program_id(1)

    @pl.when(c == 0)
    def _init():
        if has_state:
            h_ref[...] = h0_ref[...]
            for s in range(n_seq):
                xp_ref[s, HIST - (CONV_W - 1):HIST, :] = c0_ref[s]
        else:
            h_ref[...] = jnp.zeros(h_ref.shape, F32)
            xp_ref[:, 0:HIST, :] = jnp.zeros((n_seq, HIST, W_LRU), F32)

    def emit(s, cs, acc):
        xc_ref[s * rows:(s + 1) * rows, cs] = acc

    _causal_conv_block(lambda s: xb_ref[s * rows:(s + 1) * rows, :], xp_ref, cw_ref, n_seq, rows, W_LRU,
                       emit, bias_ref=cb_ref)

    @pl.when(c == pl.num_programs(1) - 1)
    def _conv_out():
        for s in range(n_seq):
            cn_ref[s] = xp_ref[s, HIST - (CONV_W - 1):HIST, :]

    row = lax.broadcasted_iota(jnp.int32, (n, LRU_BW), 0)
    sub = lax.broadcasted_iota(jnp.int32, (n // 8, 8, LRU_BW), 1)
    scan_steps = [(d, sub >= d) for d in (1, 2, 4)]
    first_row = (row == 0) & (c == 0)
    for blk in range(LRU_BLOCKS):
        bl = slice(blk * LRU_BW, (blk + 1) * LRU_BW)
        xc = xc_ref[:, bl]
        pre = jnp.dot(xc.astype(BF16), wg_ref[blk], preferred_element_type=F32)
        t_r = jnp.tanh(pre[:, :LRU_BW] + ba_ref[:, bl])
        t_i = jnp.tanh(pre[:, LRU_BW:] + bx_ref[:, bl])
        gi = 0.5 * t_i + 0.5
        log_a = (t_r + 1.0) * (-0.5 * LRU_C * jax.nn.softplus(-lam_ref[:, bl]))
        a = jnp.exp(log_a)
        z = -jnp.tanh(log_a) * (a * a + 1.0)
        mult = z * lax.rsqrt(jnp.maximum(z, TINY))
        if reset_first:
            mult = jnp.where(first_row, 1.0, mult)
        b = mult * gi * xc
        a = a.reshape(n // 8, 8, LRU_BW)
        b = b.reshape(n // 8, 8, LRU_BW)
        for d, m in scan_steps:
            a_sh = pltpu.roll(a, d, axis=1)
            b_sh = pltpu.roll(b, d, axis=1)
            b = jnp.where(m, a * b_sh + b, b)
            a = jnp.where(m, a * a_sh, a)
        a_ref[:, bl] = a.reshape(n, LRU_BW)
        b_ref[:, bl] = b.reshape(n, LRU_BW)

    for s in range(n_seq):
        def body(g, hp, s=s):
            r0 = pl.multiple_of(s * rows + g * 8, 8)
            hs = a_ref[pl.ds(r0, 8), :] * hp + b_ref[pl.ds(r0, 8), :]
            b_ref[pl.ds(r0, 8), :] = hs
            return hs[7:8, :]

        h_ref[s] = lax.fori_loop(0, rows // 8, body, h_ref[s])

    lc = 512
    for c0 in range(0, W_LRU, lc):
        cs = slice(c0, c0 + lc)
        y_ref[:, cs] = (b_ref[:, cs] * _silu(gate_ref[:, cs])).astype(BF16)


def _lru(proj, layer, n_layers, h0, c0, prev, conv_w, conv_b, w_a, b_a, w_x, b_x, lam, n_batch, t_len,
         reset_first):
    has_state = h0 is not None
    if t_len >= LRU_ROWS:
        rows, n_seq = LRU_ROWS, 1
    else:
        rows, n_seq = t_len, LRU_SHORT_ROWS // t_len
    n = rows * n_seq
    nb = n_batch // n_seq
    nc = t_len // rows
    args, in_specs, aliases = _carry_outputs(prev, 1)
    n_alias = len(args)
    in_specs += [
        pl.BlockSpec((n, W_LRU), lambda i, c: (i * nc + c, 0)),
        pl.BlockSpec((n, W_LRU), lambda i, c: (i * nc + c, 1)),
    ]
    args += [proj, proj]
    h_block = (n_seq, 1, W_LRU)
    c_block = (n_seq, CONV_W - 1, W_LRU)
    if has_state:
        in_specs += [_layer_spec(h_block, layer), _layer_spec(c_block, layer)]
        args += [h0, c0]
    vec = pl.BlockSpec((1, W_LRU), lambda i, c: (0, 0))
    w_gates = (0.5 * jnp.concatenate([w_a, w_x], axis=-1)).astype(BF16)
    blk = pl.BlockSpec(w_gates.shape, lambda i, c: (0, 0, 0))
    in_specs += [pl.BlockSpec((CONV_W, W_LRU), lambda i, c: (0, 0)), vec, blk, vec, vec, vec]
    args += [conv_w, conv_b.reshape(1, W_LRU), w_gates, 0.5 * b_a.reshape(1, W_LRU),
             0.5 * b_x.reshape(1, W_LRU), lam.reshape(1, W_LRU)]
    return pl.pallas_call(
        functools.partial(_lru_kernel, n_seq=n_seq, rows=rows, has_state=has_state,
                          reset_first=reset_first, n_alias=n_alias),
        out_shape=(
            jax.ShapeDtypeStruct((n_batch * t_len, W_LRU), BF16),
            jax.ShapeDtypeStruct((n_layers, n_batch, 1, W_LRU), F32),
            jax.ShapeDtypeStruct((n_layers, n_batch, CONV_W - 1, W_LRU), F32),
        ),
        grid=(nb, nc),
        in_specs=in_specs,
        out_specs=(
            pl.BlockSpec((n, W_LRU), lambda i, c: (i * nc + c, 0)),
            _layer_spec(h_block, layer),
            _layer_spec(c_block, layer),
        ),
        input_output_aliases=aliases,
        scratch_shapes=[
            pltpu.VMEM((n_seq, HIST + rows, W_LRU), F32),
            pltpu.VMEM((n, W_LRU), F32),
            pltpu.VMEM((n, W_LRU), F32),
            pltpu.VMEM((n, W_LRU), F32),
        ],
        compiler_params=pltpu.CompilerParams(
            dimension_semantics=("parallel", "arbitrary"), vmem_limit_bytes=VMEM_LIMIT),
        name="lru",
    )(*args)


def _ab_w_in_kernel(w_ref, o_ref):
    sizes = [QKV_A, N_HEAD_A, N_HEAD_A, VA, KB, KB, VB, LOWRANK, VB]
    offs = [0]
    for s in sizes:
        offs.append(offs[-1] + s)
    w = w_ref[...]
    qkv, b_raw, a_raw, z_a, q_b, k_b, v_b, lr_b, z_b = [w[:, offs[i]:offs[i + 1]] for i in range(9)]
    used = OFF_TAIL + 2 * N_HEAD_A + LOWRANK
    pad = jnp.zeros((w.shape[0], AB_N - used), w.dtype)
    o_ref[...] = jnp.concatenate([qkv, z_a, q_b, k_b, v_b, z_b, b_raw, a_raw, lr_b, pad], axis=1)


def _ab_w_in_layout(w, layer):
    _, d, n = w.shape
    tr = 256
    return pl.pallas_call(
        _ab_w_in_kernel,
        out_shape=jax.ShapeDtypeStruct((d, AB_N), BF16),
        grid=(d // tr,),
        in_specs=[pl.BlockSpec((None, tr, n), lambda i: (layer, i, 0))],
        out_specs=pl.BlockSpec((tr, AB_N), lambda i: (i, 0)),
        compiler_params=pltpu.CompilerParams(dimension_semantics=("parallel",)),
        name="w_in_layout",
    )(w)


def _cast_kernel(w_ref, o_ref):
    o_ref[...] = w_ref[...].astype(BF16)


def _layer_bf16(w, layer):
    _, r, c = w.shape
    tr = 512
    return pl.pallas_call(
        _cast_kernel,
        out_shape=jax.ShapeDtypeStruct((r, c), BF16),
        grid=(r // tr,),
        in_specs=[pl.BlockSpec((None, tr, c), lambda i: (layer, i, 0))],
        out_specs=pl.BlockSpec((tr, c), lambda i: (i, 0)),
        compiler_params=pltpu.CompilerParams(dimension_semantics=("parallel",)),
        name="w_cast",
    )(w)


def _tail_row(vals, lane0):
    return jnp.zeros((TAIL_W,), F32).at[lane0:lane0 + vals.shape[0]].set(vals.astype(F32))


def kernel(x_prompt, x_sample, state_delta, state_delta_conv, state_gla, state_lru, state_lru_conv,
           ab_norm, ab_w_in, ab_conv_w, ab_a_log, ab_dt_bias, ab_norm_a, ab_gla_w_lr, ab_gla_b_lr,
           ab_norm_b, ab_w_out, lru_norm, lru_w_in, lru_conv_w, lru_conv_b, lru_w_a, lru_b_a,
           lru_w_x, lru_b_x, lru_lambda, lru_w_out, final_norm):
    n_ab, n_lru = ab_norm.shape[0], lru_norm.shape[0]
    depth = n_ab + n_lru
    lru_h0 = state_lru.reshape(n_lru, -1, 1, W_LRU)
    groups = []
    for x, carried in ((x_prompt, False), (x_sample, True)):
        groups.append(dict(x=x.reshape(-1, D_MODEL), nb=x.shape[0], t=x.shape[1], carried=carried,
                           delta=None, gla=None, lru=None))

    ab_w_in_bf16 = ab_w_in.astype(BF16)
    for layer in range(depth):
        j = layer // 2
        last = layer == depth - 1
        if layer % 2 == 0:
            w_in = _ab_w_in_layout(ab_w_in_bf16, j)
            w_out = _layer_bf16(ab_w_out, j)
            gate_params = jnp.stack([_tail_row(ab_a_log[j], LANE_A), _tail_row(ab_dt_bias[j], LANE_A)])
            w_lr_pad = jnp.zeros((TAIL_W, KB), F32).at[LANE_LR:LANE_LR + LOWRANK].set(ab_gla_w_lr[j])
            for g in groups:
                proj = _norm_mm(g["x"], ab_norm[j], w_in, AB_TN)
                s_a, c_a, s_b = (state_delta, state_delta_conv, state_gla) if g["carried"] else (None,) * 3
                o_a, *g["delta"] = _delta(proj, j, n_ab, s_a, c_a, g["delta"], ab_conv_w[j], gate_params,
                                          ab_norm_a[j], g["nb"], g["t"])
                o_b, *g["gla"] = _gla(proj, j, n_ab, s_b, g["gla"], w_lr_pad, ab_gla_b_lr[j], ab_norm_b[j],
                                      g["nb"], g["t"], 16)
                g["x"] = _out_mm([o_a, o_b], w_out, g["x"], final_norm if last else None)
        else:
            w_in = _layer_bf16(lru_w_in, j)
            w_out = _layer_bf16(lru_w_out, j)
            for g in groups:
                proj = _norm_mm(g["x"], lru_norm[j], w_in, 1024)
                h0, c0 = (lru_h0, state_lru_conv) if g["carried"] else (None, None)
                y, *g["lru"] = _lru(proj, j, n_lru, h0, c0, g["lru"], lru_conv_w[j], lru_conv_b[j],
                                    lru_w_a[j], lru_b_a[j], lru_w_x[j], lru_b_x[j], lru_lambda[j],
                                    g["nb"], g["t"], reset_first=not g["carried"])
                g["x"] = _out_mm([y], w_out, g["x"], final_norm if last else None)

    outs = []
    for g, x in zip(groups, (x_prompt, x_sample)):
        h_all, lconv = g["lru"]
        outs.append([g["x"].reshape(x.shape), g["delta"][0], g["delta"][1], g["gla"][0],
                     h_all.reshape(n_lru, g["nb"], W_LRU), lconv])
    p, s = outs
    return (p[0], s[0], p[1], p[2], p[3], p[4], p[5], s[1], s[2], s[3], s[4], s[5])
```

```python
import functools

import jax
import jax.numpy as jnp
from jax import lax
from jax.experimental import pallas as pl
from jax.experimental.pallas import tpu as pltpu

F32 = jnp.float32
BF16 = jnp.bfloat16

D_MODEL = 2048
N_HEAD_A, DK_A, DV_A = 8, 128, 128
N_HEAD_B, DK_B, DV_B = 4, 128, 256
KA = N_HEAD_A * DK_A
VA = N_HEAD_A * DV_A
KB = N_HEAD_B * DK_B
VB = N_HEAD_B * DV_B
QKV_A = 2 * KA + VA
LOWRANK = 16
GLA_NORMALIZER = 16.0
W_LRU = D_MODEL
LRU_BLOCKS = 16
LRU_BW = W_LRU // LRU_BLOCKS
LRU_C = 8.0
CONV_W = 4
EPS = 1e-6
LANES = 128
TINY = 1.1754944e-38

OFF_QKV = 0
OFF_ZA = OFF_QKV + QKV_A
OFF_QB = OFF_ZA + VA
OFF_KB = OFF_QB + KB
OFF_VB = OFF_KB + KB
OFF_ZB = OFF_VB + VB
OFF_TAIL = OFF_ZB + VB
TAIL_W = 128
AB_TN = 1536
AB_N = 7680
LANE_BETA, LANE_A, LANE_LR = 0, N_HEAD_A, 2 * N_HEAD_A

ROWS = 64
DELTA_M = 128
DELTA_SEQS = 2
GLA_SEQS = 4
LRU_ROWS = 512
LRU_SHORT_ROWS = 128
HIST = 8
VMEM_LIMIT = 56 * 1024 * 1024

NN = (((1,), (0,)), ((), ()))
NT = (((1,), (1,)), ((), ()))
TN = (((0,), (0,)), ((), ()))


def _masked_sum(mask, x):
    m = jnp.where(mask, 1.0, 0.0).astype(BF16)
    hi = x.astype(BF16)
    rest = x - hi.astype(F32)
    mid = rest.astype(BF16)
    lo = (rest - mid.astype(F32)).astype(BF16)
    if m.shape[1] % LANES:
        dot = functools.partial(jnp.dot, preferred_element_type=F32)
        return dot(m, hi) + (dot(m, mid) + dot(m, lo))
    return jnp.dot(jnp.concatenate([m, m, m], axis=1), jnp.concatenate([hi, mid, lo], axis=0),
                   preferred_element_type=F32)


def _split(a):
    hi = a.astype(BF16)
    return hi, (a - hi.astype(F32)).astype(BF16)


def _mm3(a, b, dims=NN):
    ah, al = a if isinstance(a, tuple) else _split(a)
    bh, bl = b if isinstance(b, tuple) else _split(b)
    (ca,), (cb,) = dims[0]
    dot = functools.partial(lax.dot_general, dimension_numbers=dims, preferred_element_type=F32)
    if ah.shape[ca] % LANES:
        return dot(ah, bh) + (dot(ah, bl) + dot(al, bh))
    return dot(jnp.concatenate([ah, ah, al], axis=ca), jnp.concatenate([bh, bl, bh], axis=cb))


def _split_each(xs):
    return [x if isinstance(x, tuple) else _split(x) for x in xs]


def _mm3_each(as_, bs, dims=NN):
    sa, sb = _split_each(as_), _split_each(bs)
    return [_mm3(a, b, dims) for a, b in zip(sa, sb)]


def _mm1_each(as_, bs, dims=NN):
    ca = [a.astype(BF16) for a in as_]
    cb = [b.astype(BF16) for b in bs]
    return [lax.dot_general(a, b, dims, preferred_element_type=F32) for a, b in zip(ca, cb)]


def _sigmoid(x):
    return 0.5 * jnp.tanh(0.5 * x) + 0.5


def _silu(x):
    h = 0.5 * x
    return h + h * jnp.tanh(h)


def _norm_mm_kernel(x_ref, g_ref, w_ref, o_ref, h_ref):
    @pl.when(pl.program_id(1) == 0)
    def _():
        x = x_ref[...]
        ms = jnp.mean(x * x, axis=-1, keepdims=True)
        h_ref[...] = (x * lax.rsqrt(ms + EPS) * g_ref[...]).astype(BF16)

    o_ref[...] = jnp.dot(h_ref[...], w_ref[...], preferred_element_type=F32)


def _norm_mm(x, g, w, tn):
    m, k = x.shape
    n = w.shape[1]
    tm = min(m, 1024)
    w_spec = pl.BlockSpec((k, tn), lambda i, j: (0, j))
    return pl.pallas_call(
        _norm_mm_kernel,
        out_shape=jax.ShapeDtypeStruct((m, n), F32),
        grid=(m // tm, n // tn),
        in_specs=[
            pl.BlockSpec((tm, k), lambda i, j: (i, 0)),
            pl.BlockSpec((1, k), lambda i, j: (0, 0)),
            w_spec,
        ],
        out_specs=pl.BlockSpec((tm, tn), lambda i, j: (i, j)),
        scratch_shapes=[pltpu.VMEM((tm, k), BF16)],
        compiler_params=pltpu.CompilerParams(
            dimension_semantics=("parallel", "arbitrary"), vmem_limit_bytes=VMEM_LIMIT),
        name="norm_mm",
    )(x, g.reshape(1, k), w)


def _out_mm_kernel(*refs, n_in, final):
    o_refs = refs[:n_in]
    w_ref, x_ref = refs[n_in], refs[n_in + 1]
    out_ref = refs[-1]
    acc = x_ref[...]
    k0 = 0
    for o_ref in o_refs:
        kk = o_ref.shape[1]
        acc = acc + jnp.dot(o_ref[...], w_ref[k0:k0 + kk, :], preferred_element_type=F32)
        k0 += kk
    if final:
        fg_ref = refs[n_in + 2]
        ms = jnp.mean(acc * acc, axis=-1, keepdims=True)
        acc = acc * lax.rsqrt(ms + EPS) * fg_ref[...]
    out_ref[...] = acc


def _out_mm(os_, w, x, final_g=None):
    m, d = x.shape
    tm = 512
    final = final_g is not None
    in_specs = [pl.BlockSpec((tm, o.shape[1]), lambda i: (i, 0)) for o in os_]
    in_specs += [pl.BlockSpec(w.shape, lambda i: (0, 0)), pl.BlockSpec((tm, d), lambda i: (i, 0))]
    args = list(os_) + [w, x]
    if final:
        in_specs.append(pl.BlockSpec((1, d), lambda i: (0, 0)))
        args.append(final_g.reshape(1, d))
    return pl.pallas_call(
        functools.partial(_out_mm_kernel, n_in=len(os_), final=final),
        out_shape=jax.ShapeDtypeStruct((m, d), F32),
        grid=(m // tm,),
        in_specs=in_specs,
        out_specs=pl.BlockSpec((tm, d), lambda i: (i, 0)),
        compiler_params=pltpu.CompilerParams(
            dimension_semantics=("parallel",), vmem_limit_bytes=VMEM_LIMIT),
        name="out_mm",
    )(*args)


def _causal_conv_block(seq_rows, xp_ref, cw_ref, n_seq, rows, width, emit, bias_ref=None):
    lc = LANES
    groups = rows // 8
    sub = lax.broadcasted_iota(jnp.int32, (groups, 8, lc), 1)
    for s in range(n_seq):
        xp_ref[s, HIST:HIST + rows, :] = seq_rows(s)
        for c0 in range(0, width, lc):
            cs = slice(c0, c0 + lc)
            x3 = xp_ref[s, :, cs].reshape(groups + HIST // 8, 8, lc)
            acc = x3[1:] * cw_ref[CONV_W - 1:CONV_W, cs]
            for d in range(1, CONV_W):
                rolled = pltpu.roll(x3, d, axis=1)
                tap = jnp.where(sub >= d, rolled[1:], rolled[:-1])
                acc = acc + tap * cw_ref[CONV_W - 1 - d:CONV_W - d, cs]
            acc = acc.reshape(rows, lc)
            if bias_ref is not None:
                acc = acc + bias_ref[:, cs]
            emit(s, cs, acc)
        xp_ref[s, 0:HIST, :] = xp_ref[s, rows:rows + HIST, :]


def _seq_masks(n, rows):
    ri = lax.broadcasted_iota(jnp.int32, (n, n), 0)
    ci = lax.broadcasted_iota(jnp.int32, (n, n), 1)
    if rows == n:
        same = ri >= 0
    else:
        sh = rows.bit_length() - 1
        same = (ri >> sh) == (ci >> sh)
    return ri, ci, same


NEUMANN_BLOCK = 8


def _inverse_masks(ri, ci, rows):
    assert rows >= NEUMANN_BLOCK
    blk = lambda k: (ri >> (k.bit_length() - 1)) == (ci >> (k.bit_length() - 1))
    levels = []
    k = NEUMANN_BLOCK
    while k < rows:
        levels.append(blk(2 * k) & jnp.logical_not(blk(k)))
        k *= 2
    return blk(NEUMANN_BLOCK), levels


def _unit_lower_inverse_each(as_, masks):
    base, levels = masks
    a8 = [jnp.where(base, a, 0.0) for a in as_]
    p2 = _mm1_each(a8, a8)
    p4 = _mm1_each(p2, p2)
    a8p2 = _mm1_each(a8, p2)
    n = [(p - ap) - a for a, p, ap in zip(a8, p2, a8p2)]
    n = [ni + (p + d) for ni, p, d in zip(n, p4, _mm1_each(n, p4))]
    for lvl in levels:
        off = [jnp.where(lvl, a, 0.0) for a in as_]
        t_off = [o + d for o, d in zip(off, _mm1_each(n, off))]
        n = [ni - (to + d) for ni, to, d in zip(n, t_off, _mm1_each(t_off, n))]
    res = [-((a + ni) + d) for a, ni, d in zip(as_, n, _mm3_each(as_, n))]
    return [ni + (r_ + d) for ni, r_, d in zip(n, res, _mm1_each(n, res))]


def _delta_kernel(*refs, n_seq, rows, has_state, single_step, n_alias):
    refs = refs[n_alias:]
    qkv_ref, z_ref, tail_ref = refs[:3]
    p = 3
    if has_state:
        s0_ref, c0_ref = refs[p], refs[p + 1]
        p += 2
    cw_ref, gp_ref, na_ref = refs[p:p + 3]
    o_ref, s_ref, cn_ref = refs[p + 3:p + 6]
    xp_ref, act_ref = refs[p + 6:p + 8]
    n = n_seq * rows
    c = pl.program_id(1)

    @pl.when(c == 0)
    def _init():
        if has_state:
            if not single_step:
                s_ref[...] = s0_ref[...]
            for s in range(n_seq):
                xp_ref[s, HIST - (CONV_W - 1):HIST, :] = c0_ref[s]
        else:
            s_ref[...] = jnp.zeros(s_ref.shape, F32)
            xp_ref[:, 0:HIST, :] = jnp.zeros((n_seq, HIST, QKV_A), F32)

    def emit(s, cs, acc):
        act_ref[s * rows:(s + 1) * rows, cs] = _silu(acc)

    def seq_rows(s):
        slab, off = divmod(s * rows, ROWS)
        return qkv_ref[slab, off:off + rows, :]

    _causal_conv_block(seq_rows, xp_ref, cw_ref, n_seq, rows, QKV_A, emit)

    @pl.when(c == pl.num_programs(1) - 1)
    def _conv_out():
        for s in range(n_seq):
            cn_ref[s] = xp_ref[s, HIST - (CONV_W - 1):HIST, :]

    n_slab = n // ROWS
    head_group = DELTA_M // n
    _, _, same_n = _seq_masks(n, rows)
    rn = lax.broadcasted_iota(jnp.int32, (n, n), 0)
    cn = lax.broadcasted_iota(jnp.int32, (n, n), 1)
    tail = tail_ref[...].reshape(n, TAIL_W)
    btile = _sigmoid(tail)
    gtile = -jnp.exp(gp_ref[0:1, :]) * jax.nn.softplus(tail + gp_ref[1:2, :])
    g_cum = _masked_sum(same_n & (rn >= cn), gtile)
    g_tot = _masked_sum(same_n, gtile)
    e_cum = jnp.exp(g_cum)
    e_rest = jnp.exp(g_tot - g_cum)
    g_cum_t = g_cum.T

    m = DELTA_M
    ri, ci, same = _seq_masks(m, rows)
    incl = same & (ri >= ci)
    strict = same & (ri > ci)
    inv_masks = _inverse_masks(ri, ci, rows)
    stack = lambda xs: xs[0] if len(xs) == 1 else jnp.concatenate(xs, axis=0)
    groups = [range(g * head_group, (g + 1) * head_group) for g in range(N_HEAD_A // head_group)]
    qn, kn, v, beta, gc, eg, er, g_row = ([] for _ in range(8))
    for heads in groups:
        q_h, k_h, v_h = [], [], []
        for h in heads:
            q = act_ref[:, h * DK_A:(h + 1) * DK_A]
            k = act_ref[:, KA + h * DK_A:KA + (h + 1) * DK_A]
            q_h.append(q * lax.rsqrt(jnp.sum(q * q, axis=-1, keepdims=True) + EPS) * (DK_A ** -0.5))
            k_h.append(k * lax.rsqrt(jnp.sum(k * k, axis=-1, keepdims=True) + EPS))
            v_h.append(act_ref[:, 2 * KA + h * DV_A:2 * KA + (h + 1) * DV_A])
        qn.append(stack(q_h))
        kn.append(stack(k_h))
        v.append(stack(v_h))
        beta.append(stack([btile[:, LANE_BETA + h:LANE_BETA + h + 1] for h in heads]))
        gc.append(stack([g_cum[:, LANE_A + h:LANE_A + h + 1] for h in heads]))
        eg.append(stack([e_cum[:, LANE_A + h:LANE_A + h + 1] for h in heads]))
        er.append(stack([e_rest[:, LANE_A + h:LANE_A + h + 1] for h in heads]))
        g_row.append(jnp.concatenate([g_cum_t[LANE_A + h:LANE_A + h + 1, :] for h in heads], axis=1))
    decay = [jnp.where(incl, jnp.exp(jnp.where(incl, c_ - r_, 0.0)), 0.0) for c_, r_ in zip(gc, g_row)]
    kb = [k_ * b_ for k_, b_ in zip(kn, beta)]
    a = [jnp.where(strict, p_ * d_, 0.0) for p_, d_ in zip(_mm1_each(kb, kn, NT), decay)]
    qk = [p_ * d_ for p_, d_ in zip(_mm3_each(qn, kn, NT), decay)]
    n_inv = _unit_lower_inverse_each(a, inv_masks)
    rhs = [jnp.concatenate([v_ * b_, kb_ * e_], axis=1) for v_, b_, kb_, e_ in zip(v, beta, kb, eg)]
    sol = [r_ + d for r_, d in zip(rhs, _mm1_each(n_inv, rhs))]
    qg = [q_ * e_ for q_, e_ in zip(qn, eg)]
    kdec = [k_ * e_ for k_, e_ in zip(kn, er)]
    probs = [(g, i, h, s, slice(i * n + s * rows, i * n + (s + 1) * rows))
             for g, heads in enumerate(groups) for i, h in enumerate(heads) for s in range(n_seq)]
    state_ref = s0_ref if (has_state and single_step) else s_ref
    states = [state_ref[s, h] for (_, _, h, s, _) in probs]
    r = _mm1_each([stack([sol[g][sl, DV_A:], qg[g][sl]]) for (g, _, _, _, sl) in probs], states)
    v_new = [sol[g][sl, :DV_A] - r_[:rows] for (g, _, _, _, sl), r_ in zip(probs, r)]
    per_group = lambda xs: [stack([x for (g2, *_), x in zip(probs, xs) if g2 == g]) for g in range(len(groups))]
    v_new_g = per_group(v_new)
    o_inter_g = per_group([r_[rows:] for r_ in r])
    o = [oi + d for oi, d in zip(o_inter_g, _mm1_each(qk, v_new_g))]
    upd = _mm1_each([kdec[g][sl] for (g, _, _, _, sl) in probs], v_new, TN)
    for (g, i, h, s, sl), st, du in zip(probs, states, upd):
        g_last = g_tot[s * rows:s * rows + 1, LANE_A + h:LANE_A + h + 1]
        s_ref[s, h] = st * jnp.exp(g_last) + du
    for g, heads in enumerate(groups):
        for i, h in enumerate(heads):
            hl = slice(h * DV_A, (h + 1) * DV_A)
            oh = o[g][i * n:(i + 1) * n]
            oh = oh * lax.rsqrt(jnp.mean(oh * oh, axis=-1, keepdims=True) + EPS) * na_ref[...]
            oh = oh * _silu(z_ref[:, :, hl].reshape(n, DV_A))
            o_ref[:, :, hl] = oh.reshape(n_slab, ROWS, DV_A).astype(BF16)


def _layer_spec(block, layer):
    zeros = (0,) * (len(block) - 1)
    return pl.BlockSpec((None,) + tuple(block), lambda i, c: (layer, i) + zeros)


def _carry_outputs(prev, out_start):
    if prev is None:
        return [], [], {}
    specs = [pl.BlockSpec(memory_space=pl.ANY) for _ in prev]
    return list(prev), specs, {k: out_start + k for k in range(len(prev))}


def _delta(proj, layer, n_layers, s0, c0, prev, conv_w, gate_params, norm_a, n_batch, t_len):
    rows = min(ROWS, t_len)
    if t_len >= ROWS:
        n_seq, n_slab = DELTA_SEQS, DELTA_SEQS
        slabs = proj.reshape(n_batch, t_len // ROWS, ROWS, AB_N)
    else:
        n_seq, n_slab = ROWS // t_len, 1
        slabs = proj.reshape(n_batch * t_len // ROWS, 1, ROWS, AB_N)
    n = n_seq * rows
    nb = n_batch // n_seq
    nc = t_len // rows
    has_state = s0 is not None
    slab_spec = lambda width, col: pl.BlockSpec((n_slab, None, ROWS, width), lambda i, c: (i, c, 0, col))
    args, in_specs, aliases = _carry_outputs(prev, 1)
    n_alias = len(args)
    in_specs += [slab_spec(QKV_A, 0), slab_spec(VA, OFF_ZA // VA), slab_spec(TAIL_W, OFF_TAIL // TAIL_W)]
    args += [slabs, slabs, slabs]
    s_block = (n_seq, N_HEAD_A, DK_A, DV_A)
    c_block = (n_seq, CONV_W - 1, QKV_A)
    if has_state:
        in_specs += [_layer_spec(s_block, layer), _layer_spec(c_block, layer)]
        args += [s0, c0]
    in_specs += [
        pl.BlockSpec((CONV_W, QKV_A), lambda i, c: (0, 0)),
        pl.BlockSpec((2, TAIL_W), lambda i, c: (0, 0)),
        pl.BlockSpec((1, DV_A), lambda i, c: (0, 0)),
    ]
    args += [conv_w, gate_params, norm_a.reshape(1, DV_A)]
    o, s_all, conv_all = pl.pallas_call(
        functools.partial(_delta_kernel, n_seq=n_seq, rows=rows, has_state=has_state, single_step=nc == 1,
                          n_alias=n_alias),
        out_shape=(
            jax.ShapeDtypeStruct(slabs.shape[:3] + (VA,), BF16),
            jax.ShapeDtypeStruct((n_layers, n_batch, N_HEAD_A, DK_A, DV_A), F32),
            jax.ShapeDtypeStruct((n_layers, n_batch, CONV_W - 1, QKV_A), F32),
        ),
        grid=(nb, nc),
        in_specs=in_specs,
        out_specs=(
            slab_spec(VA, 0),
            _layer_spec(s_block, layer),
            _layer_spec(c_block, layer),
        ),
        scratch_shapes=[
            pltpu.VMEM((n_seq, HIST + rows, QKV_A), F32),
            pltpu.VMEM((n, QKV_A), F32),
        ],
        input_output_aliases=aliases,
        compiler_params=pltpu.CompilerParams(
            dimension_semantics=("parallel", "arbitrary"), vmem_limit_bytes=VMEM_LIMIT),
        name="delta",
    )(*args)
    return o.reshape(n_batch * t_len, VA), s_all, conv_all


def _gla_kernel(*refs, rows, chained, has_state, n_alias):
    refs = refs[n_alias:]
    q_ref, k_ref, v_ref, z_ref, tail_ref = refs[:5]
    p = 5
    if has_state:
        s0_ref = refs[p]
        p += 1
    wlr_ref, blr_ref, nb_ref = refs[p:p + 3]
    o_ref, s_ref = refs[p + 3:p + 5]
    n_slab = q_ref.shape[0]
    n = n_slab * ROWS
    n_sub = ROWS // rows
    c = pl.program_id(1)

    state_ref = s0_ref if (has_state and not chained) else s_ref

    @pl.when(c == 0)
    def _init():
        if not has_state:
            s_ref[...] = jnp.zeros(s_ref.shape, F32)
        elif chained:
            s_ref[...] = s0_ref[...]

    ri, ci, same = _seq_masks(ROWS, rows)
    incl = same & (ri >= ci)
    slabs = [slice(u * ROWS, (u + 1) * ROWS) for u in range(n_slab)]
    cumulate = lambda mask, x: jnp.concatenate(
        [_masked_sum(mask, x[sl]) for sl in slabs], axis=0)
    gk = jax.nn.log_sigmoid(_mm3(tail_ref[...].reshape(n, TAIL_W), wlr_ref[...]) + blr_ref[...]) / GLA_NORMALIZER
    b_loc = cumulate(incl, gk)
    q = q_ref[...].reshape(n, KB) * (DK_B ** -0.5)
    k = k_ref[...].reshape(n, KB)
    v_all = v_ref[...].reshape(n, VB)
    q_loc = q * jnp.exp(b_loc)
    heads = range(N_HEAD_B)
    hls = [slice(h * DK_B, (h + 1) * DK_B) for h in heads]
    vls = [slice(h * DV_B, (h + 1) * DV_B) for h in heads]

    def finish(h, o):
        o = o * lax.rsqrt(jnp.mean(o * o, axis=-1, keepdims=True) + EPS) * nb_ref[...]
        o = o * _silu(z_ref[:, :, vls[h]].reshape(n, DV_B))
        o_ref[:, :, vls[h]] = o.reshape(n_slab, ROWS, DV_B).astype(BF16)

    def key_column(row_vals):
        return jnp.broadcast_to(row_vals, (8, DK_B)).T[:, 0:1]

    if chained:
        tri = ri >= ci
        b_cum = cumulate(tri, gk)
        per_slab = lambda rws: jnp.concatenate([jnp.broadcast_to(r_, (ROWS, KB)) for r_ in rws], axis=0)
        b_last = [b_cum[sl.stop - 1:sl.stop] for sl in slabs]
        q_cum = q * jnp.exp(b_cum)
        kdec = k * jnp.exp(per_slab(b_last) - b_cum)
        row = lax.broadcasted_iota(jnp.int32, (n, KB), 0) & (ROWS - 1)
        k_rel = []
        for s in range(n_sub):
            start = [b_cum[sl.start + s * rows - 1:sl.start + s * rows] if s else jnp.zeros((1, KB), F32)
                     for sl in slabs]
            k_rel.append(k * jnp.exp(jnp.where(row < (s + 1) * rows, per_slab(start) - b_cum, 0.0)))
        probs = [(u, h) for u in range(n_slab) for h in heads]
        a_parts = _mm3_each(
            [q_loc[slabs[u].start + s * rows:slabs[u].start + (s + 1) * rows, hls[h]]
             for (u, h) in probs for s in range(n_sub)],
            [k_rel[s][slabs[u], hls[h]] for (u, h) in probs for s in range(n_sub)], NT)
        a = [jnp.where(tri, jnp.concatenate(a_parts[j * n_sub:(j + 1) * n_sub], axis=0), 0.0)
             for j in range(len(probs))]
        vs = [v_all[slabs[u], vls[h]] for (u, h) in probs]
        sts = [state_ref[u, h] for (u, h) in probs]
        o_state = _mm1_each([q_cum[slabs[u], hls[h]] for (u, h) in probs], sts)
        intra = _mm1_each(a, vs)
        upd = _mm1_each([kdec[slabs[u], hls[h]] for (u, h) in probs], vs, TN)
        for j, (u, h) in enumerate(probs):
            s_ref[u, h] = sts[j] * jnp.exp(key_column(b_last[u][:, hls[h]])) + upd[j]
        for h in heads:
            finish(h, jnp.concatenate([o_state[j] + intra[j] for j, (_, h2) in enumerate(probs) if h2 == h], axis=0))
    else:
        vs = [v_all[:, vl] for vl in vls]
        b_tot = _masked_sum(same, gk)
        kd = k * jnp.exp(-b_loc)
        kdec = k * jnp.exp(b_tot - b_loc)
        a = [jnp.where(incl, x, 0.0) for x in _mm3_each([q_loc[:, hl] for hl in hls], [kd[:, hl] for hl in hls], NT)]
        intra = _mm1_each(a, vs)
        probs = [(h, s, slice(s * rows, (s + 1) * rows)) for h in heads for s in range(n_sub)]
        sts = [state_ref[s, h] for (h, s, _) in probs]
        o_state = _mm1_each([q_loc[sl, hls[h]] for (h, _, sl) in probs], sts)
        upd = _mm1_each([kdec[sl, hls[h]] for (h, _, sl) in probs], [vs[h][sl] for (h, _, sl) in probs], TN)
        for (h, s, sl), st, du in zip(probs, sts, upd):
            s_ref[s, h] = st * jnp.exp(key_column(b_tot[s * rows:s * rows + 1, hls[h]])) + du
        for h in heads:
            finish(h, jnp.concatenate(o_state[h * n_sub:(h + 1) * n_sub], axis=0) + intra[h])


def _gla(proj, layer, n_layers, s0, prev, w_lr_pad, b_lr, norm_b, n_batch, t_len, chunk):
    rows = min(chunk, t_len)
    has_state = s0 is not None
    chained = t_len >= ROWS
    if chained:
        n_slab, n_state, nc = GLA_SEQS, GLA_SEQS, t_len // ROWS
        slabs = proj.reshape(n_batch, nc, ROWS, AB_N)
    else:
        n_slab, n_state, nc = 1, ROWS // t_len, 1
        slabs = proj.reshape(n_batch * t_len // ROWS, 1, ROWS, AB_N)
    nb = n_batch // n_state
    slab_spec = lambda width, col: pl.BlockSpec((n_slab, None, ROWS, width), lambda i, c: (i, c, 0, col))
    args, in_specs, aliases = _carry_outputs(prev, 1)
    n_alias = len(args)
    in_specs += [slab_spec(KB, OFF_QB // KB), slab_spec(KB, OFF_KB // KB), slab_spec(VB, OFF_VB // VB),
                 slab_spec(VB, OFF_ZB // VB), slab_spec(TAIL_W, OFF_TAIL // TAIL_W)]
    args += [slabs] * 5
    s_block = (n_state, N_HEAD_B, DK_B, DV_B)
    if has_state:
        in_specs.append(_layer_spec(s_block, layer))
        args.append(s0)
    in_specs += [
        pl.BlockSpec((TAIL_W, KB), lambda i, c: (0, 0)),
        pl.BlockSpec((1, KB), lambda i, c: (0, 0)),
        pl.BlockSpec((1, DV_B), lambda i, c: (0, 0)),
    ]
    args += [w_lr_pad, b_lr.reshape(1, KB), norm_b.reshape(1, DV_B)]
    o, s_all = pl.pallas_call(
        functools.partial(_gla_kernel, rows=rows, chained=chained, has_state=has_state, n_alias=n_alias),
        out_shape=(
            jax.ShapeDtypeStruct(slabs.shape[:3] + (VB,), BF16),
            jax.ShapeDtypeStruct((n_layers, n_batch, N_HEAD_B, DK_B, DV_B), F32),
        ),
        grid=(nb, nc),
        in_specs=in_specs,
        out_specs=(slab_spec(VB, 0), _layer_spec(s_block, layer)),
        input_output_aliases=aliases,
        compiler_params=pltpu.CompilerParams(
            dimension_semantics=("parallel", "arbitrary"), vmem_limit_bytes=VMEM_LIMIT),
        name="gla",
    )(*args)
    return o.reshape(n_batch * t_len, VB), s_all


def _lru_kernel(*refs, n_seq, rows, has_state, reset_first, n_alias):
    refs = refs[n_alias:]
    xb_ref, gate_ref = refs[:2]
    p = 2
    if has_state:
        h0_ref, c0_ref = refs[p], refs[p + 1]
        p += 2
    cw_ref, cb_ref, wg_ref, ba_ref, bx_ref, lam_ref = refs[p:p + 6]
    y_ref, h_ref, cn_ref = refs[p + 6:p + 9]
    xp_ref, xc_ref, a_ref, b_ref = refs[p + 9:p + 13]
    n = n_seq * rows
    c = pl.program_id(1)

    @pl.when(c == 0)
    def _init():
        if has_state:
            h_ref[...] = h0_ref[...]
            for s in range(n_seq):
                xp_ref[s, HIST - (CONV_W - 1):HIST, :] = c0_ref[s]
        else:
            h_ref[...] = jnp.zeros(h_ref.shape, F32)
            xp_ref[:, 0:HIST, :] = jnp.zeros((n_seq, HIST, W_LRU), F32)

    def emit(s, cs, acc):
        xc_ref[s * rows:(s + 1) * rows, cs] = acc

    _causal_conv_block(lambda s: xb_ref[s * rows:(s + 1) * rows, :], xp_ref, cw_ref, n_seq, rows, W_LRU,
                       emit, bias_ref=cb_ref)

    @pl.when(c == pl.num_programs(1) - 1)
    def _conv_out():
        for s in range(n_seq):
            cn_ref[s] = xp_ref[s, HIST - (CONV_W - 1):HIST, :]

    row = lax.broadcasted_iota(jnp.int32, (n, LRU_BW), 0)
    sub = lax.broadcasted_iota(jnp.int32, (n // 8, 8, LRU_BW), 1)
    scan_steps = [(d, sub >= d) for d in (1, 2, 4)]
    first_row = (row == 0) & (c == 0)
    for blk in range(LRU_BLOCKS):
        bl = slice(blk * LRU_BW, (blk + 1) * LRU_BW)
        xc = xc_ref[:, bl]
        pre = jnp.dot(xc.astype(BF16), wg_ref[blk], preferred_element_type=F32)
        t_r = jnp.tanh(pre[:, :LRU_BW] + ba_ref[:, bl])
        t_i = jnp.tanh(pre[:, LRU_BW:] + bx_ref[:, bl])
        gi = 0.5 * t_i + 0.5
        log_a = (t_r + 1.0) * (-0.5 * LRU_C * jax.nn.softplus(-lam_ref[:, bl]))
        a = jnp.exp(log_a)
        z = -jnp.tanh(log_a) * (a * a + 1.0)
        mult = z * lax.rsqrt(jnp.maximum(z, TINY))
        if reset_first:
            mult = jnp.where(first_row, 1.0, mult)
        b = mult * gi * xc
        a = a.reshape(n // 8, 8, LRU_BW)
        b = b.reshape(n // 8, 8, LRU_BW)
        for d, m in scan_steps:
            a_sh = pltpu.roll(a, d, axis=1)
            b_sh = pltpu.roll(b, d, axis=1)
            b = jnp.where(m, a * b_sh + b, b)
            a = jnp.where(m, a * a_sh, a)
        a_ref[:, bl] = a.reshape(n, LRU_BW)
        b_ref[:, bl] = b.reshape(n, LRU_BW)

    for s in range(n_seq):
        def body(g, hp, s=s):
            r0 = pl.multiple_of(s * rows + g * 8, 8)
            hs = a_ref[pl.ds(r0, 8), :] * hp + b_ref[pl.ds(r0, 8), :]
            b_ref[pl.ds(r0, 8), :] = hs
            return hs[7:8, :]

        h_ref[s] = lax.fori_loop(0, rows // 8, body, h_ref[s])

    lc = 512
    for c0 in range(0, W_LRU, lc):
        cs = slice(c0, c0 + lc)
        y_ref[:, cs] = (b_ref[:, cs] * _silu(gate_ref[:, cs])).astype(BF16)


def _lru(proj, layer, n_layers, h0, c0, prev, conv_w, conv_b, w_a, b_a, w_x, b_x, lam, n_batch, t_len,
         reset_first):
    has_state = h0 is not None
    if t_len >= LRU_ROWS:
        rows, n_seq = LRU_ROWS, 1
    else:
        rows, n_seq = t_len, LRU_SHORT_ROWS // t_len
    n = rows * n_seq
    nb = n_batch // n_seq
    nc = t_len // rows
    args, in_specs, aliases = _carry_outputs(prev, 1)
    n_alias = len(args)
    in_specs += [
        pl.BlockSpec((n, W_LRU), lambda i, c: (i * nc + c, 0)),
        pl.BlockSpec((n, W_LRU), lambda i, c: (i * nc + c, 1)),
    ]
    args += [proj, proj]
    h_block = (n_seq, 1, W_LRU)
    c_block = (n_seq, CONV_W - 1, W_LRU)
    if has_state:
        in_specs += [_layer_spec(h_block, layer), _layer_spec(c_block, layer)]
        args += [h0, c0]
    vec = pl.BlockSpec((1, W_LRU), lambda i, c: (0, 0))
    w_gates = (0.5 * jnp.concatenate([w_a, w_x], axis=-1)).astype(BF16)
    blk = pl.BlockSpec(w_gates.shape, lambda i, c: (0, 0, 0))
    in_specs += [pl.BlockSpec((CONV_W, W_LRU), lambda i, c: (0, 0)), vec, blk, vec, vec, vec]
    args += [conv_w, conv_b.reshape(1, W_LRU), w_gates, 0.5 * b_a.reshape(1, W_LRU),
             0.5 * b_x.reshape(1, W_LRU), lam.reshape(1, W_LRU)]
    return pl.pallas_call(
        functools.partial(_lru_kernel, n_seq=n_seq, rows=rows, has_state=has_state,
                          reset_first=reset_first, n_alias=n_alias),
        out_shape=(
            jax.ShapeDtypeStruct((n_batch * t_len, W_LRU), BF16),
            jax.ShapeDtypeStruct((n_layers, n_batch, 1, W_LRU), F32),
            jax.ShapeDtypeStruct((n_layers, n_batch, CONV_W - 1, W_LRU), F32),
        ),
        grid=(nb, nc),
        in_specs=in_specs,
        out_specs=(
            pl.BlockSpec((n, W_LRU), lambda i, c: (i * nc + c, 0)),
            _layer_spec(h_block, layer),
            _layer_spec(c_block, layer),
        ),
        input_output_aliases=aliases,
        scratch_shapes=[
            pltpu.VMEM((n_seq, HIST + rows, W_LRU), F32),
            pltpu.VMEM((n, W_LRU), F32),
            pltpu.VMEM((n, W_LRU), F32),
            pltpu.VMEM((n, W_LRU), F32),
        ],
        compiler_params=pltpu.CompilerParams(
            dimension_semantics=("parallel", "arbitrary"), vmem_limit_bytes=VMEM_LIMIT),
        name="lru",
    )(*args)


def _ab_w_in_kernel(w_ref, o_ref):
    sizes = [QKV_A, N_HEAD_A, N_HEAD_A, VA, KB, KB, VB, LOWRANK, VB]
    offs = [0]
    for s in sizes:
        offs.append(offs[-1] + s)
    w = w_ref[...]
    qkv, b_raw, a_raw, z_a, q_b, k_b, v_b, lr_b, z_b = [w[:, offs[i]:offs[i + 1]] for i in range(9)]
    used = OFF_TAIL + 2 * N_HEAD_A + LOWRANK
    pad = jnp.zeros((w.shape[0], AB_N - used), w.dtype)
    o_ref[...] = jnp.concatenate([qkv, z_a, q_b, k_b, v_b, z_b, b_raw, a_raw, lr_b, pad], axis=1)


def _ab_w_in_layout(w, layer):
    _, d, n = w.shape
    tr = 256
    return pl.pallas_call(
        _ab_w_in_kernel,
        out_shape=jax.ShapeDtypeStruct((d, AB_N), BF16),
        grid=(d // tr,),
        in_specs=[pl.BlockSpec((None, tr, n), lambda i: (layer, i, 0))],
        out_specs=pl.BlockSpec((tr, AB_N), lambda i: (i, 0)),
        compiler_params=pltpu.CompilerParams(dimension_semantics=("parallel",)),
        name="w_in_layout",
    )(w)


def _cast_kernel(w_ref, o_ref):
    o_ref[...] = w_ref[...].astype(BF16)


def _layer_bf16(w, layer):
    _, r, c = w.shape
    tr = 512
    return pl.pallas_call(
        _cast_kernel,
        out_shape=jax.ShapeDtypeStruct((r, c), BF16),
        grid=(r // tr,),
        in_specs=[pl.BlockSpec((None, tr, c), lambda i: (layer, i, 0))],
        out_specs=pl.BlockSpec((tr, c), lambda i: (i, 0)),
        compiler_params=pltpu.CompilerParams(dimension_semantics=("parallel",)),
        name="w_cast",
    )(w)


def _tail_row(vals, lane0):
    return jnp.zeros((TAIL_W,), F32).at[lane0:lane0 + vals.shape[0]].set(vals.astype(F32))


def kernel(x_prompt, x_sample, state_delta, state_delta_conv, state_gla, state_lru, state_lru_conv,
           ab_norm, ab_w_in, ab_conv_w, ab_a_log, ab_dt_bias, ab_norm_a, ab_gla_w_lr, ab_gla_b_lr,
           ab_norm_b, ab_w_out, lru_norm, lru_w_in, lru_conv_w, lru_conv_b, lru_w_a, lru_b_a,
           lru_w_x, lru_b_x, lru_lambda, lru_w_out, final_norm):
    n_ab, n_lru = ab_norm.shape[0], lru_norm.shape[0]
    depth = n_ab + n_lru
    lru_h0 = state_lru.reshape(n_lru, -1, 1, W_LRU)
    groups = []
    for x, carried in ((x_prompt, False), (x_sample, True)):
        groups.append(dict(x=x.reshape(-1, D_MODEL), nb=x.shape[0], t=x.shape[1], carried=carried,
                           delta=None, gla=None, lru=None))

    ab_w_in_bf16 = ab_w_in.astype(BF16)
    for layer in range(depth):
        j = layer // 2
        last = layer == depth - 1
        if layer % 2 == 0:
            w_in = _ab_w_in_layout(ab_w_in_bf16, j)
            w_out = _layer_bf16(ab_w_out, j)
            gate_params = jnp.stack([_tail_row(ab_a_log[j], LANE_A), _tail_row(ab_dt_bias[j], LANE_A)])
            w_lr_pad = jnp.zeros((TAIL_W, KB), F32).at[LANE_LR:LANE_LR + LOWRANK].set(ab_gla_w_lr[j])
            for g in groups:
                proj = _norm_mm(g["x"], ab_norm[j], w_in, AB_TN)
                s_a, c_a, s_b = (state_delta, state_delta_conv, state_gla) if g["carried"] else (None,) * 3
                o_a, *g["delta"] = _delta(proj, j, n_ab, s_a, c_a, g["delta"], ab_conv_w[j], gate_params,
                                          ab_norm_a[j], g["nb"], g["t"])
                o_b, *g["gla"] = _gla(proj, j, n_ab, s_b, g["gla"], w_lr_pad, ab_gla_b_lr[j], ab_norm_b[j],
                                      g["nb"], g["t"], 16)
                g["x"] = _out_mm([o_a, o_b], w_out, g["x"], final_norm if last else None)
        else:
            w_in = _layer_bf16(lru_w_in, j)
            w_out = _layer_bf16(lru_w_out, j)
            for g in groups:
                proj = _norm_mm(g["x"], lru_norm[j], w_in, 1024)
                h0, c0 = (lru_h0, state_lru_conv) if g["carried"] else (None, None)
                y, *g["lru"] = _lru(proj, j, n_lru, h0, c0, g["lru"], lru_conv_w[j], lru_conv_b[j],
                                    lru_w_a[j], lru_b_a[j], lru_w_x[j], lru_b_x[j], lru_lambda[j],
                                    g["nb"], g["t"], reset_first=not g["carried"])
                g["x"] = _out_mm([y], w_out, g["x"], final_norm if last else None)

    outs = []
    for g, x in zip(groups, (x_prompt, x_sample)):
        h_all, lconv = g["lru"]
        outs.append([g["x"].reshape(x.shape), g["delta"][0], g["delta"][1], g["gla"][0],
                     h_all.reshape(n_lru, g["nb"], W_LRU), lconv])
    p, s = outs
    return (p[0], s[0], p[1], p[2], p[3], p[4], p[5], s[1], s[2], s[3], s[4], s[5])
```

```python
import functools

import jax
import jax.numpy as jnp
from jax import lax
from jax.experimental import pallas as pl
from jax.experimental.pallas import tpu as pltpu

F32 = jnp.float32
BF16 = jnp.bfloat16

D_MODEL = 2048
N_HEAD_A, DK_A, DV_A = 8, 128, 128
N_HEAD_B, DK_B, DV_B = 4, 128, 256
KA = N_HEAD_A * DK_A
VA = N_HEAD_A * DV_A
KB = N_HEAD_B * DK_B
VB = N_HEAD_B * DV_B
QKV_A = 2 * KA + VA
LOWRANK = 16
GLA_NORMALIZER = 16.0
W_LRU = D_MODEL
LRU_BLOCKS = 16
LRU_BW = W_LRU // LRU_BLOCKS
LRU_C = 8.0
CONV_W = 4
EPS = 1e-6
LANES = 128
TINY = 1.1754944e-38

OFF_QKV = 0
OFF_ZA = OFF_QKV + QKV_A
OFF_QB = OFF_ZA + VA
OFF_KB = OFF_QB + KB
OFF_VB = OFF_KB + KB
OFF_ZB = OFF_VB + VB
OFF_TAIL = OFF_ZB + VB
TAIL_W = 128
AB_TN = 1536
AB_N = 7680
LANE_BETA, LANE_A, LANE_LR = 0, N_HEAD_A, 2 * N_HEAD_A

ROWS = 64
DELTA_M = 128
DELTA_SEQS = 2
GLA_SEQS = 4
LRU_ROWS = 512
LRU_SHORT_ROWS = 128
HIST = 8
VMEM_LIMIT = 56 * 1024 * 1024

NN = (((1,), (0,)), ((), ()))
NT = (((1,), (1,)), ((), ()))
TN = (((0,), (0,)), ((), ()))


def _masked_sum(mask, x):
    m = jnp.where(mask, 1.0, 0.0).astype(BF16)
    hi = x.astype(BF16)
    rest = x - hi.astype(F32)
    mid = rest.astype(BF16)
    lo = (rest - mid.astype(F32)).astype(BF16)
    if m.shape[1] % LANES:
        dot = functools.partial(jnp.dot, preferred_element_type=F32)
        return dot(m, hi) + (dot(m, mid) + dot(m, lo))
    return jnp.dot(jnp.concatenate([m, m, m], axis=1), jnp.concatenate([hi, mid, lo], axis=0),
                   preferred_element_type=F32)


def _split(a):
    hi = a.astype(BF16)
    return hi, (a - hi.astype(F32)).astype(BF16)


def _mm3(a, b, dims=NN):
    ah, al = a if isinstance(a, tuple) else _split(a)
    bh, bl = b if isinstance(b, tuple) else _split(b)
    (ca,), (cb,) = dims[0]
    dot = functools.partial(lax.dot_general, dimension_numbers=dims, preferred_element_type=F32)
    if ah.shape[ca] % LANES:
        return dot(ah, bh) + (dot(ah, bl) + dot(al, bh))
    return dot(jnp.concatenate([ah, ah, al], axis=ca), jnp.concatenate([bh, bl, bh], axis=cb))


def _split_each(xs):
    return [x if isinstance(x, tuple) else _split(x) for x in xs]


def _mm3_each(as_, bs, dims=NN):
    sa, sb = _split_each(as_), _split_each(bs)
    return [_mm3(a, b, dims) for a, b in zip(sa, sb)]


def _mm1_each(as_, bs, dims=NN):
    ca = [a.astype(BF16) for a in as_]
    cb = [b.astype(BF16) for b in bs]
    return [lax.dot_general(a, b, dims, preferred_element_type=F32) for a, b in zip(ca, cb)]


def _sigmoid(x):
    return 0.5 * jnp.tanh(0.5 * x) + 0.5


def _silu(x):
    h = 0.5 * x
    return h + h * jnp.tanh(h)


def _norm_mm_kernel(x_ref, g_ref, w_ref, o_ref, h_ref):
    @pl.when(pl.program_id(1) == 0)
    def _():
        x = x_ref[...]
        ms = jnp.mean(x * x, axis=-1, keepdims=True)
        h_ref[...] = (x * lax.rsqrt(ms + EPS) * g_ref[...]).astype(BF16)

    o_ref[...] = jnp.dot(h_ref[...], w_ref[...], preferred_element_type=F32)


def _norm_mm(x, g, w, tn):
    m, k = x.shape
    n = w.shape[1]
    tm = min(m, 1024)
    w_spec = pl.BlockSpec((k, tn), lambda i, j: (0, j))
    return pl.pallas_call(
        _norm_mm_kernel,
        out_shape=jax.ShapeDtypeStruct((m, n), F32),
        grid=(m // tm, n // tn),
        in_specs=[
            pl.BlockSpec((tm, k), lambda i, j: (i, 0)),
            pl.BlockSpec((1, k), lambda i, j: (0, 0)),
            w_spec,
        ],
        out_specs=pl.BlockSpec((tm, tn), lambda i, j: (i, j)),
        scratch_shapes=[pltpu.VMEM((tm, k), BF16)],
        compiler_params=pltpu.CompilerParams(
            dimension_semantics=("parallel", "arbitrary"), vmem_limit_bytes=VMEM_LIMIT),
        name="norm_mm",
    )(x, g.reshape(1, k), w)


def _out_mm_kernel(*refs, n_in, final):
    o_refs = refs[:n_in]
    w_ref, x_ref = refs[n_in], refs[n_in + 1]
    out_ref = refs[-1]
    acc = x_ref[...]
    k0 = 0
    for o_ref in o_refs:
        kk = o_ref.shape[1]
        acc = acc + jnp.dot(o_ref[...], w_ref[k0:k0 + kk, :], preferred_element_type=F32)
        k0 += kk
    if final:
        fg_ref = refs[n_in + 2]
        ms = jnp.mean(acc * acc, axis=-1, keepdims=True)
        acc = acc * lax.rsqrt(ms + EPS) * fg_ref[...]
    out_ref[...] = acc


def _out_mm(os_, w, x, final_g=None):
    m, d = x.shape
    tm = 512
    final = final_g is not None
    in_specs = [pl.BlockSpec((tm, o.shape[1]), lambda i: (i, 0)) for o in os_]
    in_specs += [pl.BlockSpec(w.shape, lambda i: (0, 0)), pl.BlockSpec((tm, d), lambda i: (i, 0))]
    args = list(os_) + [w, x]
    if final:
        in_specs.append(pl.BlockSpec((1, d), lambda i: (0, 0)))
        args.append(final_g.reshape(1, d))
    return pl.pallas_call(
        functools.partial(_out_mm_kernel, n_in=len(os_), final=final),
        out_shape=jax.ShapeDtypeStruct((m, d), F32),
        grid=(m // tm,),
        in_specs=in_specs,
        out_specs=pl.BlockSpec((tm, d), lambda i: (i, 0)),
        compiler_params=pltpu.CompilerParams(
            dimension_semantics=("parallel",), vmem_limit_bytes=VMEM_LIMIT),
        name="out_mm",
    )(*args)


def _causal_conv_block(seq_rows, xp_ref, cw_ref, n_seq, rows, width, emit, bias_ref=None):
    lc = LANES
    groups = rows // 8
    sub = lax.broadcasted_iota(jnp.int32, (groups, 8, lc), 1)
    for s in range(n_seq):
        xp_ref[s, HIST:HIST + rows, :] = seq_rows(s)
        for c0 in range(0, width, lc):
            cs = slice(c0, c0 + lc)
            x3 = xp_ref[s, :, cs].reshape(groups + HIST // 8, 8, lc)
            acc = x3[1:] * cw_ref[CONV_W - 1:CONV_W, cs]
            for d in range(1, CONV_W):
                rolled = pltpu.roll(x3, d, axis=1)
                tap = jnp.where(sub >= d, rolled[1:], rolled[:-1])
                acc = acc + tap * cw_ref[CONV_W - 1 - d:CONV_W - d, cs]
            acc = acc.reshape(rows, lc)
            if bias_ref is not None:
                acc = acc + bias_ref[:, cs]
            emit(s, cs, acc)
        xp_ref[s, 0:HIST, :] = xp_ref[s, rows:rows + HIST, :]


def _seq_masks(n, rows):
    ri = lax.broadcasted_iota(jnp.int32, (n, n), 0)
    ci = lax.broadcasted_iota(jnp.int32, (n, n), 1)
    if rows == n:
        same = ri >= 0
    else:
        sh = rows.bit_length() - 1
        same = (ri >> sh) == (ci >> sh)
    return ri, ci, same


NEUMANN_BLOCK = 8


def _inverse_masks(ri, ci, rows):
    assert rows >= NEUMANN_BLOCK
    blk = lambda k: (ri >> (k.bit_length() - 1)) == (ci >> (k.bit_length() - 1))
    levels = []
    k = NEUMANN_BLOCK
    while k < rows:
        levels.append(blk(2 * k) & jnp.logical_not(blk(k)))
        k *= 2
    return blk(NEUMANN_BLOCK), levels


def _unit_lower_inverse_each(as_, masks):
    base, levels = masks
    a8 = [jnp.where(base, a, 0.0) for a in as_]
    p2 = _mm1_each(a8, a8)
    p4 = _mm1_each(p2, p2)
    a8p2 = _mm1_each(a8, p2)
    n = [(p - ap) - a for a, p, ap in zip(a8, p2, a8p2)]
    n = [ni + (p + d) for ni, p, d in zip(n, p4, _mm1_each(n, p4))]
    for lvl in levels:
        off = [jnp.where(lvl, a, 0.0) for a in as_]
        t_off = [o + d for o, d in zip(off, _mm1_each(n, off))]
        n = [ni - (to + d) for ni, to, d in zip(n, t_off, _mm1_each(t_off, n))]
    res = [-((a + ni) + d) for a, ni, d in zip(as_, n, _mm3_each(as_, n))]
    return [ni + (r_ + d) for ni, r_, d in zip(n, res, _mm1_each(n, res))]


def _delta_kernel(*refs, n_seq, rows, has_state, single_step, n_alias):
    refs = refs[n_alias:]
    qkv_ref, z_ref, tail_ref = refs[:3]
    p = 3
    if has_state:
        s0_ref, c0_ref = refs[p], refs[p + 1]
        p += 2
    cw_ref, gp_ref, na_ref = refs[p:p + 3]
    o_ref, s_ref, cn_ref = refs[p + 3:p + 6]
    xp_ref, act_ref = refs[p + 6:p + 8]
    n = n_seq * rows
    c = pl.program_id(1)

    @pl.when(c == 0)
    def _init():
        if has_state:
            if not single_step:
                s_ref[...] = s0_ref[...]
            for s in range(n_seq):
                xp_ref[s, HIST - (CONV_W - 1):HIST, :] = c0_ref[s]
        else:
            s_ref[...] = jnp.zeros(s_ref.shape, F32)
            xp_ref[:, 0:HIST, :] = jnp.zeros((n_seq, HIST, QKV_A), F32)

    def emit(s, cs, acc):
        act_ref[s * rows:(s + 1) * rows, cs] = _silu(acc)

    def seq_rows(s):
        slab, off = divmod(s * rows, ROWS)
        return qkv_ref[slab, off:off + rows, :]

    _causal_conv_block(seq_rows, xp_ref, cw_ref, n_seq, rows, QKV_A, emit)

    @pl.when(c == pl.num_programs(1) - 1)
    def _conv_out():
        for s in range(n_seq):
            cn_ref[s] = xp_ref[s, HIST - (CONV_W - 1):HIST, :]

    n_slab = n // ROWS
    head_group = DELTA_M // n
    _, _, same_n = _seq_masks(n, rows)
    rn = lax.broadcasted_iota(jnp.int32, (n, n), 0)
    cn = lax.broadcasted_iota(jnp.int32, (n, n), 1)
    tail = tail_ref[...].reshape(n, TAIL_W)
    btile = _sigmoid(tail)
    gtile = -jnp.exp(gp_ref[0:1, :]) * jax.nn.softplus(tail + gp_ref[1:2, :])
    g_cum = _masked_sum(same_n & (rn >= cn), gtile)
    g_tot = _masked_sum(same_n, gtile)
    e_cum = jnp.exp(g_cum)
    e_rest = jnp.exp(g_tot - g_cum)
    g_cum_t = g_cum.T

    m = DELTA_M
    ri, ci, same = _seq_masks(m, rows)
    incl = same & (ri >= ci)
    strict = same & (ri > ci)
    inv_masks = _inverse_masks(ri, ci, rows)
    stack = lambda xs: xs[0] if len(xs) == 1 else jnp.concatenate(xs, axis=0)
    groups = [range(g * head_group, (g + 1) * head_group) for g in range(N_HEAD_A // head_group)]
    qn, kn, v, beta, gc, eg, er, g_row = ([] for _ in range(8))
    for heads in groups:
        q_h, k_h, v_h = [], [], []
        for h in heads:
            q = act_ref[:, h * DK_A:(h + 1) * DK_A]
            k = act_ref[:, KA + h * DK_A:KA + (h + 1) * DK_A]
            q_h.append(q * lax.rsqrt(jnp.sum(q * q, axis=-1, keepdims=True) + EPS) * (DK_A ** -0.5))
            k_h.append(k * lax.rsqrt(jnp.sum(k * k, axis=-1, keepdims=True) + EPS))
            v_h.append(act_ref[:, 2 * KA + h * DV_A:2 * KA + (h + 1) * DV_A])
        qn.append(stack(q_h))
        kn.append(stack(k_h))
        v.append(stack(v_h))
        beta.append(stack([btile[:, LANE_BETA + h:LANE_BETA + h + 1] for h in heads]))
        gc.append(stack([g_cum[:, LANE_A + h:LANE_A + h + 1] for h in heads]))
        eg.append(stack([e_cum[:, LANE_A + h:LANE_A + h + 1] for h in heads]))
        er.append(stack([e_rest[:, LANE_A + h:LANE_A + h + 1] for h in heads]))
        g_row.append(jnp.concatenate([g_cum_t[LANE_A + h:LANE_A + h + 1, :] for h in heads], axis=1))
    decay = [jnp.where(incl, jnp.exp(jnp.where(incl, c_ - r_, 0.0)), 0.0) for c_, r_ in zip(gc, g_row)]
    kb = [k_ * b_ for k_, b_ in zip(kn, beta)]
    a = [jnp.where(strict, p_ * d_, 0.0) for p_, d_ in zip(_mm1_each(kb, kn, NT), decay)]
    qk = [p_ * d_ for p_, d_ in zip(_mm3_each(qn, kn, NT), decay)]
    n_inv = _unit_lower_inverse_each(a, inv_masks)
    rhs = [jnp.concatenate([v_ * b_, kb_ * e_], axis=1) for v_, b_, kb_, e_ in zip(v, beta, kb, eg)]
    sol = [r_ + d for r_, d in zip(rhs, _mm1_each(n_inv, rhs))]
    qg = [q_ * e_ for q_, e_ in zip(qn, eg)]
    kdec = [k_ * e_ for k_, e_ in zip(kn, er)]
    probs = [(g, i, h, s, slice(i * n + s * rows, i * n + (s + 1) * rows))
             for g, heads in enumerate(groups) for i, h in enumerate(heads) for s in range(n_seq)]
    state_ref = s0_ref if (has_state and single_step) else s_ref
    states = [state_ref[s, h] for (_, _, h, s, _) in probs]
    r = _mm1_each([stack([sol[g][sl, DV_A:], qg[g][sl]]) for (g, _, _, _, sl) in probs], states)
    v_new = [sol[g][sl, :DV_A] - r_[:rows] for (g, _, _, _, sl), r_ in zip(probs, r)]
    per_group = lambda xs: [stack([x for (g2, *_), x in zip(probs, xs) if g2 == g]) for g in range(len(groups))]
    v_new_g = per_group(v_new)
    o_inter_g = per_group([r_[rows:] for r_ in r])
    o = [oi + d for oi, d in zip(o_inter_g, _mm1_each(qk, v_new_g))]
    upd = _mm1_each([kdec[g][sl] for (g, _, _, _, sl) in probs], v_new, TN)
    for (g, i, h, s, sl), st, du in zip(probs, states, upd):
        g_last = g_tot[s * rows:s * rows + 1, LANE_A + h:LANE_A + h + 1]
        s_ref[s, h] = st * jnp.exp(g_last) + du
    for g, heads in enumerate(groups):
        for i, h in enumerate(heads):
            hl = slice(h * DV_A, (h + 1) * DV_A)
            oh = o[g][i * n:(i + 1) * n]
            oh = oh * lax.rsqrt(jnp.mean(oh * oh, axis=-1, keepdims=True) + EPS) * na_ref[...]
            oh = oh * _silu(z_ref[:, :, hl].reshape(n, DV_A))
            o_ref[:, :, hl] = oh.reshape(n_slab, ROWS, DV_A).astype(BF16)


def _layer_spec(block, layer):
    zeros = (0,) * (len(block) - 1)
    return pl.BlockSpec((None,) + tuple(block), lambda i, c: (layer, i) + zeros)


def _carry_outputs(prev, out_start):
    if prev is None:
        return [], [], {}
    specs = [pl.BlockSpec(memory_space=pl.ANY) for _ in prev]
    return list(prev), specs, {k: out_start + k for k in range(len(prev))}


def _delta(proj, layer, n_layers, s0, c0, prev, conv_w, gate_params, norm_a, n_batch, t_len):
    rows = min(ROWS, t_len)
    if t_len >= ROWS:
        n_seq, n_slab = DELTA_SEQS, DELTA_SEQS
        slabs = proj.reshape(n_batch, t_len // ROWS, ROWS, AB_N)
    else:
        n_seq, n_slab = ROWS // t_len, 1
        slabs = proj.reshape(n_batch * t_len // ROWS, 1, ROWS, AB_N)
    n = n_seq * rows
    nb = n_batch // n_seq
    nc = t_len // rows
    has_state = s0 is not None
    slab_spec = lambda width, col: pl.BlockSpec((n_slab, None, ROWS, width), lambda i, c: (i, c, 0, col))
    args, in_specs, aliases = _carry_outputs(prev, 1)
    n_alias = len(args)
    in_specs += [slab_spec(QKV_A, 0), slab_spec(VA, OFF_ZA // VA), slab_spec(TAIL_W, OFF_TAIL // TAIL_W)]
    args += [slabs, slabs, slabs]
    s_block = (n_seq, N_HEAD_A, DK_A, DV_A)
    c_block = (n_seq, CONV_W - 1, QKV_A)
    if has_state:
        in_specs += [_layer_spec(s_block, layer), _layer_spec(c_block, layer)]
        args += [s0, c0]
    in_specs += [
        pl.BlockSpec((CONV_W, QKV_A), lambda i, c: (0, 0)),
        pl.BlockSpec((2, TAIL_W), lambda i, c: (0, 0)),
        pl.BlockSpec((1, DV_A), lambda i, c: (0, 0)),
    ]
    args += [conv_w, gate_params, norm_a.reshape(1, DV_A)]
    o, s_all, conv_all = pl.pallas_call(
        functools.partial(_delta_kernel, n_seq=n_seq, rows=rows, has_state=has_state, single_step=nc == 1,
                          n_alias=n_alias),
        out_shape=(
            jax.ShapeDtypeStruct(slabs.shape[:3] + (VA,), BF16),
            jax.ShapeDtypeStruct((n_layers, n_batch, N_HEAD_A, DK_A, DV_A), F32),
            jax.ShapeDtypeStruct((n_layers, n_batch, CONV_W - 1, QKV_A), F32),
        ),
        grid=(nb, nc),
        in_specs=in_specs,
        out_specs=(
            slab_spec(VA, 0),
            _layer_spec(s_block, layer),
            _layer_spec(c_block, layer),
        ),
        scratch_shapes=[
            pltpu.VMEM((n_seq, HIST + rows, QKV_A), F32),
            pltpu.VMEM((n, QKV_A), F32),
        ],
        input_output_aliases=aliases,
        compiler_params=pltpu.CompilerParams(
            dimension_semantics=("parallel", "arbitrary"), vmem_limit_bytes=VMEM_LIMIT),
        name="delta",
    )(*args)
    return o.reshape(n_batch * t_len, VA), s_all, conv_all


def _gla_kernel(*refs, rows, chained, has_state, n_alias):
    refs = refs[n_alias:]
    q_ref, k_ref, v_ref, z_ref, tail_ref = refs[:5]
    p = 5
    if has_state:
        s0_ref = refs[p]
        p += 1
    wlr_ref, blr_ref, nb_ref = refs[p:p + 3]
    o_ref, s_ref = refs[p + 3:p + 5]
    n_slab = q_ref.shape[0]
    n = n_slab * ROWS
    n_sub = ROWS // rows
    c = pl.program_id(1)

    state_ref = s0_ref if (has_state and not chained) else s_ref

    @pl.when(c == 0)
    def _init():
        if not has_state:
            s_ref[...] = jnp.zeros(s_ref.shape, F32)
        elif chained:
            s_ref[...] = s0_ref[...]

    ri, ci, same = _seq_masks(ROWS, rows)
    incl = same & (ri >= ci)
    slabs = [slice(u * ROWS, (u + 1) * ROWS) for u in range(n_slab)]
    cumulate = lambda mask, x: jnp.concatenate(
        [_masked_sum(mask, x[sl]) for sl in slabs], axis=0)
    gk = jax.nn.log_sigmoid(_mm3(tail_ref[...].reshape(n, TAIL_W), wlr_ref[...]) + blr_ref[...]) / GLA_NORMALIZER
    b_loc = cumulate(incl, gk)
    q = q_ref[...].reshape(n, KB) * (DK_B ** -0.5)
    k = k_ref[...].reshape(n, KB)
    v_all = v_ref[...].reshape(n, VB)
    q_loc = q * jnp.exp(b_loc)
    heads = range(N_HEAD_B)
    hls = [slice(h * DK_B, (h + 1) * DK_B) for h in heads]
    vls = [slice(h * DV_B, (h + 1) * DV_B) for h in heads]

    def finish(h, o):
        o = o * lax.rsqrt(jnp.mean(o * o, axis=-1, keepdims=True) + EPS) * nb_ref[...]
        o = o * _silu(z_ref[:, :, vls[h]].reshape(n, DV_B))
        o_ref[:, :, vls[h]] = o.reshape(n_slab, ROWS, DV_B).astype(BF16)

    def key_column(row_vals):
        return jnp.broadcast_to(row_vals, (8, DK_B)).T[:, 0:1]

    if chained:
        tri = ri >= ci
        b_cum = cumulate(tri, gk)
        per_slab = lambda rws: jnp.concatenate([jnp.broadcast_to(r_, (ROWS, KB)) for r_ in rws], axis=0)
        b_last = [b_cum[sl.stop - 1:sl.stop] for sl in slabs]
        q_cum = q * jnp.exp(b_cum)
        kdec = k * jnp.exp(per_slab(b_last) - b_cum)
        row = lax.broadcasted_iota(jnp.int32, (n, KB), 0) & (ROWS - 1)
        k_rel = []
        for s in range(n_sub):
            start = [b_cum[sl.start + s * rows - 1:sl.start + s * rows] if s else jnp.zeros((1, KB), F32)
                     for sl in slabs]
            k_rel.append(k * jnp.exp(jnp.where(row < (s + 1) * rows, per_slab(start) - b_cum, 0.0)))
        probs = [(u, h) for u in range(n_slab) for h in heads]
        a_parts = _mm3_each(
            [q_loc[slabs[u].start + s * rows:slabs[u].start + (s + 1) * rows, hls[h]]
             for (u, h) in probs for s in range(n_sub)],
            [k_rel[s][slabs[u], hls[h]] for (u, h) in probs for s in range(n_sub)], NT)
        a = [jnp.where(tri, jnp.concatenate(a_parts[j * n_sub:(j + 1) * n_sub], axis=0), 0.0)
             for j in range(len(probs))]
        vs = [v_all[slabs[u], vls[h]] for (u, h) in probs]
        sts = [state_ref[u, h] for (u, h) in probs]
        o_all = _mm1_each(
            [jnp.concatenate([q_cum[slabs[u], hls[h]], a[j]], axis=1) for j, (u, h) in enumerate(probs)],
            [jnp.concatenate([st_, v_], axis=0) for st_, v_ in zip(sts, vs)])
        upd = _mm1_each([kdec[slabs[u], hls[h]] for (u, h) in probs], vs, TN)
        for j, (u, h) in enumerate(probs):
            s_ref[u, h] = sts[j] * jnp.exp(key_column(b_last[u][:, hls[h]])) + upd[j]
        for h in heads:
            finish(h, jnp.concatenate([o_all[j] for j, (_, h2) in enumerate(probs) if h2 == h], axis=0))
    else:
        vs = [v_all[:, vl] for vl in vls]
        b_tot = _masked_sum(same, gk)
        kd = k * jnp.exp(-b_loc)
        kdec = k * jnp.exp(b_tot - b_loc)
        a = [jnp.where(incl, x, 0.0) for x in _mm3_each([q_loc[:, hl] for hl in hls], [kd[:, hl] for hl in hls], NT)]
        intra = _mm1_each(a, vs)
        probs = [(h, s, slice(s * rows, (s + 1) * rows)) for h in heads for s in range(n_sub)]
        sts = [state_ref[s, h] for (h, s, _) in probs]
        o_state = _mm1_each([q_loc[sl, hls[h]] for (h, _, sl) in probs], sts)
        upd = _mm1_each([kdec[sl, hls[h]] for (h, _, sl) in probs], [vs[h][sl] for (h, _, sl) in probs], TN)
        for (h, s, sl), st, du in zip(probs, sts, upd):
            s_ref[s, h] = st * jnp.exp(key_column(b_tot[s * rows:s * rows + 1, hls[h]])) + du
        for h in heads:
            finish(h, jnp.concatenate(o_state[h * n_sub:(h + 1) * n_sub], axis=0) + intra[h])


def _gla(proj, layer, n_layers, s0, prev, w_lr_pad, b_lr, norm_b, n_batch, t_len, chunk):
    rows = min(chunk, t_len)
    has_state = s0 is not None
    chained = t_len >= ROWS
    if chained:
        n_slab, n_state, nc = GLA_SEQS, GLA_SEQS, t_len // ROWS
        slabs = proj.reshape(n_batch, nc, ROWS, AB_N)
    else:
        n_slab, n_state, nc = 1, ROWS // t_len, 1
        slabs = proj.reshape(n_batch * t_len // ROWS, 1, ROWS, AB_N)
    nb = n_batch // n_state
    slab_spec = lambda width, col: pl.BlockSpec((n_slab, None, ROWS, width), lambda i, c: (i, c, 0, col))
    args, in_specs, aliases = _carry_outputs(prev, 1)
    n_alias = len(args)
    in_specs += [slab_spec(KB, OFF_QB // KB), slab_spec(KB, OFF_KB // KB), slab_spec(VB, OFF_VB // VB),
                 slab_spec(VB, OFF_ZB // VB), slab_spec(TAIL_W, OFF_TAIL // TAIL_W)]
    args += [slabs] * 5
    s_block = (n_state, N_HEAD_B, DK_B, DV_B)
    if has_state:
        in_specs.append(_layer_spec(s_block, layer))
        args.append(s0)
    in_specs += [
        pl.BlockSpec((TAIL_W, KB), lambda i, c: (0, 0)),
        pl.BlockSpec((1, KB), lambda i, c: (0, 0)),
        pl.BlockSpec((1, DV_B), lambda i, c: (0, 0)),
    ]
    args += [w_lr_pad, b_lr.reshape(1, KB), norm_b.reshape(1, DV_B)]
    o, s_all = pl.pallas_call(
        functools.partial(_gla_kernel, rows=rows, chained=chained, has_state=has_state, n_alias=n_alias),
        out_shape=(
            jax.ShapeDtypeStruct(slabs.shape[:3] + (VB,), BF16),
            jax.ShapeDtypeStruct((n_layers, n_batch, N_HEAD_B, DK_B, DV_B), F32),
        ),
        grid=(nb, nc),
        in_specs=in_specs,
        out_specs=(slab_spec(VB, 0), _layer_spec(s_block, layer)),
        input_output_aliases=aliases,
        compiler_params=pltpu.CompilerParams(
            dimension_semantics=("parallel", "arbitrary"), vmem_limit_bytes=VMEM_LIMIT),
        name="gla",
    )(*args)
    return o.reshape(n_batch * t_len, VB), s_all


def _lru_kernel(*refs, n_seq, rows, has_state, reset_first, n_alias):
    refs = refs[n_alias:]
    xb_ref, gate_ref = refs[:2]
    p = 2
    if has_state:
        h0_ref, c0_ref = refs[p], refs[p + 1]
        p += 2
    cw_ref, cb_ref, wg_ref, ba_ref, bx_ref, lam_ref = refs[p:p + 6]
    y_ref, h_ref, cn_ref = refs[p + 6:p + 9]
    xp_ref, xc_ref, a_ref, b_ref = refs[p + 9:p + 13]
    n = n_seq * rows
    c = pl.program_id(1)

    @pl.when(c == 0)
    def _init():
        if has_state:
            h_ref[...] = h0_ref[...]
            for s in range(n_seq):
                xp_ref[s, HIST - (CONV_W - 1):HIST, :] = c0_ref[s]
        else:
            h_ref[...] = jnp.zeros(h_ref.shape, F32)
            xp_ref[:, 0:HIST, :] = jnp.zeros((n_seq, HIST, W_LRU), F32)

    def emit(s, cs, acc):
        xc_ref[s * rows:(s + 1) * rows, cs] = acc

    _causal_conv_block(lambda s: xb_ref[s * rows:(s + 1) * rows, :], xp_ref, cw_ref, n_seq, rows, W_LRU,
                       emit, bias_ref=cb_ref)

    @pl.when(c == pl.num_programs(1) - 1)
    def _conv_out():
        for s in range(n_seq):
            cn_ref[s] = xp_ref[s, HIST - (CONV_W - 1):HIST, :]

    row = lax.broadcasted_iota(jnp.int32, (n, LRU_BW), 0)
    sub = lax.broadcasted_iota(jnp.int32, (n // 8, 8, LRU_BW), 1)
    scan_steps = [(d, sub >= d) for d in (1, 2, 4)]
    first_row = (row == 0) & (c == 0)
    for blk in range(LRU_BLOCKS):
        bl = slice(blk * LRU_BW, (blk + 1) * LRU_BW)
        xc = xc_ref[:, bl]
        pre = jnp.dot(xc.astype(BF16), wg_ref[blk], preferred_element_type=F32)
        t_r = jnp.tanh(pre[:, :LRU_BW] + ba_ref[:, bl])
        t_i = jnp.tanh(pre[:, LRU_BW:] + bx_ref[:, bl])
        gi = 0.5 * t_i + 0.5
        log_a = (t_r + 1.0) * (-0.5 * LRU_C * jax.nn.softplus(-lam_ref[:, bl]))
        a = jnp.exp(log_a)
        z = -jnp.tanh(log_a) * (a * a + 1.0)
        mult = z * lax.rsqrt(jnp.maximum(z, TINY))
        if reset_first:
            mult = jnp.where(first_row, 1.0, mult)
        b = mult * gi * xc
        a = a.reshape(n // 8, 8, LRU_BW)
        b = b.reshape(n // 8, 8, LRU_BW)
        for d, m in scan_steps:
            a_sh = pltpu.roll(a, d, axis=1)
            b_sh = pltpu.roll(b, d, axis=1)
            b = jnp.where(m, a * b_sh + b, b)
            a = jnp.where(m, a * a_sh, a)
        a_ref[:, bl] = a.reshape(n, LRU_BW)
        b_ref[:, bl] = b.reshape(n, LRU_BW)

    for s in range(n_seq):
        def body(g, hp, s=s):
            r0 = pl.multiple_of(s * rows + g * 8, 8)
            hs = a_ref[pl.ds(r0, 8), :] * hp + b_ref[pl.ds(r0, 8), :]
            b_ref[pl.ds(r0, 8), :] = hs
            return hs[7:8, :]

        h_ref[s] = lax.fori_loop(0, rows // 8, body, h_ref[s])

    lc = 512
    for c0 in range(0, W_LRU, lc):
        cs = slice(c0, c0 + lc)
        y_ref[:, cs] = (b_ref[:, cs] * _silu(gate_ref[:, cs])).astype(BF16)


def _lru(proj, layer, n_layers, h0, c0, prev, conv_w, conv_b, w_a, b_a, w_x, b_x, lam, n_batch, t_len,
         reset_first):
    has_state = h0 is not None
    if t_len >= LRU_ROWS:
        rows, n_seq = LRU_ROWS, 1
    else:
        rows, n_seq = t_len, LRU_SHORT_ROWS // t_len
    n = rows * n_seq
    nb = n_batch // n_seq
    nc = t_len // rows
    args, in_specs, aliases = _carry_outputs(prev, 1)
    n_alias = len(args)
    in_specs += [
        pl.BlockSpec((n, W_LRU), lambda i, c: (i * nc + c, 0)),
        pl.BlockSpec((n, W_LRU), lambda i, c: (i * nc + c, 1)),
    ]
    args += [proj, proj]
    h_block = (n_seq, 1, W_LRU)
    c_block = (n_seq, CONV_W - 1, W_LRU)
    if has_state:
        in_specs += [_layer_spec(h_block, layer), _layer_spec(c_block, layer)]
        args += [h0, c0]
    vec = pl.BlockSpec((1, W_LRU), lambda i, c: (0, 0))
    w_gates = (0.5 * jnp.concatenate([w_a, w_x], axis=-1)).astype(BF16)
    blk = pl.BlockSpec(w_gates.shape, lambda i, c: (0, 0, 0))
    in_specs += [pl.BlockSpec((CONV_W, W_LRU), lambda i, c: (0, 0)), vec, blk, vec, vec, vec]
    args += [conv_w, conv_b.reshape(1, W_LRU), w_gates, 0.5 * b_a.reshape(1, W_LRU),
             0.5 * b_x.reshape(1, W_LRU), lam.reshape(1, W_LRU)]
    return pl.pallas_call(
        functools.partial(_lru_kernel, n_seq=n_seq, rows=rows, has_state=has_state,
                          reset_first=reset_first, n_alias=n_alias),
        out_shape=(
            jax.ShapeDtypeStruct((n_batch * t_len, W_LRU), BF16),
            jax.ShapeDtypeStruct((n_layers, n_batch, 1, W_LRU), F32),
            jax.ShapeDtypeStruct((n_layers, n_batch, CONV_W - 1, W_LRU), F32),
        ),
        grid=(nb, nc),
        in_specs=in_specs,
        out_specs=(
            pl.BlockSpec((n, W_LRU), lambda i, c: (i * nc + c, 0)),
            _layer_spec(h_block, layer),
            _layer_spec(c_block, layer),
        ),
        input_output_aliases=aliases,
        scratch_shapes=[
            pltpu.VMEM((n_seq, HIST + rows, W_LRU), F32),
            pltpu.VMEM((n, W_LRU), F32),
            pltpu.VMEM((n, W_LRU), F32),
            pltpu.VMEM((n, W_LRU), F32),
        ],
        compiler_params=pltpu.CompilerParams(
            dimension_semantics=("parallel", "arbitrary"), vmem_limit_bytes=VMEM_LIMIT),
        name="lru",
    )(*args)


def _ab_w_in_kernel(w_ref, o_ref):
    sizes = [QKV_A, N_HEAD_A, N_HEAD_A, VA, KB, KB, VB, LOWRANK, VB]
    offs = [0]
    for s in sizes:
        offs.append(offs[-1] + s)
    w = w_ref[...]
    qkv, b_raw, a_raw, z_a, q_b, k_b, v_b, lr_b, z_b = [w[:, offs[i]:offs[i + 1]] for i in range(9)]
    used = OFF_TAIL + 2 * N_HEAD_A + LOWRANK
    pad = jnp.zeros((w.shape[0], AB_N - used), w.dtype)
    o_ref[...] = jnp.concatenate([qkv, z_a, q_b, k_b, v_b, z_b, b_raw, a_raw, lr_b, pad], axis=1)


def _ab_w_in_layout(w, layer):
    _, d, n = w.shape
    tr = 256
    return pl.pallas_call(
        _ab_w_in_kernel,
        out_shape=jax.ShapeDtypeStruct((d, AB_N), BF16),
        grid=(d // tr,),
        in_specs=[pl.BlockSpec((None, tr, n), lambda i: (layer, i, 0))],
        out_specs=pl.BlockSpec((tr, AB_N), lambda i: (i, 0)),
        compiler_params=pltpu.CompilerParams(dimension_semantics=("parallel",)),
        name="w_in_layout",
    )(w)


def _cast_kernel(w_ref, o_ref):
    o_ref[...] = w_ref[...].astype(BF16)


def _layer_bf16(w, layer):
    _, r, c = w.shape
    tr = 512
    return pl.pallas_call(
        _cast_kernel,
        out_shape=jax.ShapeDtypeStruct((r, c), BF16),
        grid=(r // tr,),
        in_specs=[pl.BlockSpec((None, tr, c), lambda i: (layer, i, 0))],
        out_specs=pl.BlockSpec((tr, c), lambda i: (i, 0)),
        compiler_params=pltpu.CompilerParams(dimension_semantics=("parallel",)),
        name="w_cast",
    )(w)


def _tail_row(vals, lane0):
    return jnp.zeros((TAIL_W,), F32).at[lane0:lane0 + vals.shape[0]].set(vals.astype(F32))


def kernel(x_prompt, x_sample, state_delta, state_delta_conv, state_gla, state_lru, state_lru_conv,
           ab_norm, ab_w_in, ab_conv_w, ab_a_log, ab_dt_bias, ab_norm_a, ab_gla_w_lr, ab_gla_b_lr,
           ab_norm_b, ab_w_out, lru_norm, lru_w_in, lru_conv_w, lru_conv_b, lru_w_a, lru_b_a,
           lru_w_x, lru_b_x, lru_lambda, lru_w_out, final_norm):
    n_ab, n_lru = ab_norm.shape[0], lru_norm.shape[0]
    depth = n_ab + n_lru
    lru_h0 = state_lru.reshape(n_lru, -1, 1, W_LRU)
    groups = []
    for x, carried in ((x_prompt, False), (x_sample, True)):
        groups.append(dict(x=x.reshape(-1, D_MODEL), nb=x.shape[0], t=x.shape[1], carried=carried,
                           delta=None, gla=None, lru=None))

    ab_w_in_bf16 = ab_w_in.astype(BF16)
    for layer in range(depth):
        j = layer // 2
        last = layer == depth - 1
        if layer % 2 == 0:
            w_in = _ab_w_in_layout(ab_w_in_bf16, j)
            w_out = _layer_bf16(ab_w_out, j)
            gate_params = jnp.stack([_tail_row(ab_a_log[j], LANE_A), _tail_row(ab_dt_bias[j], LANE_A)])
            w_lr_pad = jnp.zeros((TAIL_W, KB), F32).at[LANE_LR:LANE_LR + LOWRANK].set(ab_gla_w_lr[j])
            for g in groups:
                proj = _norm_mm(g["x"], ab_norm[j], w_in, AB_TN)
                s_a, c_a, s_b = (state_delta, state_delta_conv, state_gla) if g["carried"] else (None,) * 3
                o_a, *g["delta"] = _delta(proj, j, n_ab, s_a, c_a, g["delta"], ab_conv_w[j], gate_params,
                                          ab_norm_a[j], g["nb"], g["t"])
                o_b, *g["gla"] = _gla(proj, j, n_ab, s_b, g["gla"], w_lr_pad, ab_gla_b_lr[j], ab_norm_b[j],
                                      g["nb"], g["t"], 16)
                g["x"] = _out_mm([o_a, o_b], w_out, g["x"], final_norm if last else None)
        else:
            w_in = _layer_bf16(lru_w_in, j)
            w_out = _layer_bf16(lru_w_out, j)
            for g in groups:
                proj = _norm_mm(g["x"], lru_norm[j], w_in, 1024)
                h0, c0 = (lru_h0, state_lru_conv) if g["carried"] else (None, None)
                y, *g["lru"] = _lru(proj, j, n_lru, h0, c0, g["lru"], lru_conv_w[j], lru_conv_b[j],
                                    lru_w_a[j], lru_b_a[j], lru_w_x[j], lru_b_x[j], lru_lambda[j],
                                    g["nb"], g["t"], reset_first=not g["carried"])
                g["x"] = _out_mm([y], w_out, g["x"], final_norm if last else None)

    outs = []
    for g, x in zip(groups, (x_prompt, x_sample)):
        h_all, lconv = g["lru"]
        outs.append([g["x"].reshape(x.shape), g["delta"][0], g["delta"][1], g["gla"][0],
                     h_all.reshape(n_lru, g["nb"], W_LRU), lconv])
    p, s = outs
    return (p[0], s[0], p[1], p[2], p[3], p[4], p[5], s[1], s[2], s[3], s[4], s[5])
```

```python
import functools

import jax
import jax.numpy as jnp
from jax import lax
from jax.experimental import pallas as pl
from jax.experimental.pallas import tpu as pltpu

F32 = jnp.float32
BF16 = jnp.bfloat16

D_MODEL = 2048
N_HEAD_A, DK_A, DV_A = 8, 128, 128
N_HEAD_B, DK_B, DV_B = 4, 128, 256
KA = N_HEAD_A * DK_A
VA = N_HEAD_A * DV_A
KB = N_HEAD_B * DK_B
VB = N_HEAD_B * DV_B
QKV_A = 2 * KA + VA
LOWRANK = 16
GLA_NORMALIZER = 16.0
W_LRU = D_MODEL
LRU_BLOCKS = 16
LRU_BW = W_LRU // LRU_BLOCKS
LRU_C = 8.0
CONV_W = 4
EPS = 1e-6
LANES = 128
TINY = 1.1754944e-38

OFF_QKV = 0
OFF_ZA = OFF_QKV + QKV_A
OFF_QB = OFF_ZA + VA
OFF_KB = OFF_QB + KB
OFF_VB = OFF_KB + KB
OFF_ZB = OFF_VB + VB
OFF_TAIL = OFF_ZB + VB
TAIL_W = 128
AB_TN = 1536
AB_N = 7680
LANE_BETA, LANE_A, LANE_LR = 0, N_HEAD_A, 2 * N_HEAD_A

ROWS = 64
DELTA_M = 128
DELTA_SEQS = 2
GLA_SEQS = 4
LRU_ROWS = 512
LRU_SHORT_ROWS = 256
HIST = 8
VMEM_LIMIT = 56 * 1024 * 1024

NN = (((1,), (0,)), ((), ()))
NT = (((1,), (1,)), ((), ()))
TN = (((0,), (0,)), ((), ()))


def _masked_sum(mask, x):
    m = jnp.where(mask, 1.0, 0.0).astype(BF16)
    hi = x.astype(BF16)
    rest = x - hi.astype(F32)
    mid = rest.astype(BF16)
    lo = (rest - mid.astype(F32)).astype(BF16)
    if m.shape[1] % LANES:
        dot = functools.partial(jnp.dot, preferred_element_type=F32)
        return dot(m, hi) + (dot(m, mid) + dot(m, lo))
    return jnp.dot(jnp.concatenate([m, m, m], axis=1), jnp.concatenate([hi, mid, lo], axis=0),
                   preferred_element_type=F32)


def _split(a):
    hi = a.astype(BF16)
    return hi, (a - hi.astype(F32)).astype(BF16)


def _mm3(a, b, dims=NN):
    ah, al = a if isinstance(a, tuple) else _split(a)
    bh, bl = b if isinstance(b, tuple) else _split(b)
    (ca,), (cb,) = dims[0]
    dot = functools.partial(lax.dot_general, dimension_numbers=dims, preferred_element_type=F32)
    if ah.shape[ca] % LANES:
        return dot(ah, bh) + (dot(ah, bl) + dot(al, bh))
    return dot(jnp.concatenate([ah, ah, al], axis=ca), jnp.concatenate([bh, bl, bh], axis=cb))


def _split_each(xs):
    return [x if isinstance(x, tuple) else _split(x) for x in xs]


def _mm3_each(as_, bs, dims=NN):
    sa, sb = _split_each(as_), _split_each(bs)
    return [_mm3(a, b, dims) for a, b in zip(sa, sb)]


def _mm1_each(as_, bs, dims=NN):
    ca = [a.astype(BF16) for a in as_]
    cb = [b.astype(BF16) for b in bs]
    return [lax.dot_general(a, b, dims, preferred_element_type=F32) for a, b in zip(ca, cb)]


def _sigmoid(x):
    return 0.5 * jnp.tanh(0.5 * x) + 0.5


def _silu(x):
    h = 0.5 * x
    return h + h * jnp.tanh(h)


def _norm_mm_kernel(x_ref, g_ref, w_ref, o_ref, h_ref):
    @pl.when(pl.program_id(1) == 0)
    def _():
        x = x_ref[...]
        ms = jnp.mean(x * x, axis=-1, keepdims=True)
        h_ref[...] = (x * lax.rsqrt(ms + EPS) * g_ref[...]).astype(BF16)

    o_ref[...] = jnp.dot(h_ref[...], w_ref[...], preferred_element_type=F32)


def _norm_mm(x, g, w, tn):
    m, k = x.shape
    n = w.shape[1]
    tm = min(m, 1024)
    w_spec = pl.BlockSpec((k, tn), lambda i, j: (0, j))
    return pl.pallas_call(
        _norm_mm_kernel,
        out_shape=jax.ShapeDtypeStruct((m, n), F32),
        grid=(m // tm, n // tn),
        in_specs=[
            pl.BlockSpec((tm, k), lambda i, j: (i, 0)),
            pl.BlockSpec((1, k), lambda i, j: (0, 0)),
            w_spec,
        ],
        out_specs=pl.BlockSpec((tm, tn), lambda i, j: (i, j)),
        scratch_shapes=[pltpu.VMEM((tm, k), BF16)],
        compiler_params=pltpu.CompilerParams(
            dimension_semantics=("parallel", "arbitrary"), vmem_limit_bytes=VMEM_LIMIT),
        name="norm_mm",
    )(x, g.reshape(1, k), w)


def _out_mm_kernel(*refs, n_in, final):
    o_refs = refs[:n_in]
    w_ref, x_ref = refs[n_in], refs[n_in + 1]
    out_ref = refs[-1]
    acc = x_ref[...]
    k0 = 0
    for o_ref in o_refs:
        kk = o_ref.shape[1]
        acc = acc + jnp.dot(o_ref[...], w_ref[k0:k0 + kk, :], preferred_element_type=F32)
        k0 += kk
    if final:
        fg_ref = refs[n_in + 2]
        ms = jnp.mean(acc * acc, axis=-1, keepdims=True)
        acc = acc * lax.rsqrt(ms + EPS) * fg_ref[...]
    out_ref[...] = acc


def _out_mm(os_, w, x, final_g=None):
    m, d = x.shape
    tm = 512
    final = final_g is not None
    in_specs = [pl.BlockSpec((tm, o.shape[1]), lambda i: (i, 0)) for o in os_]
    in_specs += [pl.BlockSpec(w.shape, lambda i: (0, 0)), pl.BlockSpec((tm, d), lambda i: (i, 0))]
    args = list(os_) + [w, x]
    if final:
        in_specs.append(pl.BlockSpec((1, d), lambda i: (0, 0)))
        args.append(final_g.reshape(1, d))
    return pl.pallas_call(
        functools.partial(_out_mm_kernel, n_in=len(os_), final=final),
        out_shape=jax.ShapeDtypeStruct((m, d), F32),
        grid=(m // tm,),
        in_specs=in_specs,
        out_specs=pl.BlockSpec((tm, d), lambda i: (i, 0)),
        compiler_params=pltpu.CompilerParams(
            dimension_semantics=("parallel",), vmem_limit_bytes=VMEM_LIMIT),
        name="out_mm",
    )(*args)


def _causal_conv_block(seq_rows, xp_ref, cw_ref, n_seq, rows, width, emit, bias_ref=None):
    lc = LANES
    groups = rows // 8
    sub = lax.broadcasted_iota(jnp.int32, (groups, 8, lc), 1)
    for s in range(n_seq):
        xp_ref[s, HIST:HIST + rows, :] = seq_rows(s)
        for c0 in range(0, width, lc):
            cs = slice(c0, c0 + lc)
            x3 = xp_ref[s, :, cs].reshape(groups + HIST // 8, 8, lc)
            acc = x3[1:] * cw_ref[CONV_W - 1:CONV_W, cs]
            for d in range(1, CONV_W):
                rolled = pltpu.roll(x3, d, axis=1)
                tap = jnp.where(sub >= d, rolled[1:], rolled[:-1])
                acc = acc + tap * cw_ref[CONV_W - 1 - d:CONV_W - d, cs]
            acc = acc.reshape(rows, lc)
            if bias_ref is not None:
                acc = acc + bias_ref[:, cs]
            emit(s, cs, acc)
        xp_ref[s, 0:HIST, :] = xp_ref[s, rows:rows + HIST, :]


def _seq_masks(n, rows):
    ri = lax.broadcasted_iota(jnp.int32, (n, n), 0)
    ci = lax.broadcasted_iota(jnp.int32, (n, n), 1)
    if rows == n:
        same = ri >= 0
    else:
        sh = rows.bit_length() - 1
        same = (ri >> sh) == (ci >> sh)
    return ri, ci, same


NEUMANN_BLOCK = 8


def _inverse_masks(ri, ci, rows):
    assert rows >= NEUMANN_BLOCK
    blk = lambda k: (ri >> (k.bit_length() - 1)) == (ci >> (k.bit_length() - 1))
    levels = []
    k = NEUMANN_BLOCK
    while k < rows:
        levels.append(blk(2 * k) & jnp.logical_not(blk(k)))
        k *= 2
    return blk(NEUMANN_BLOCK), levels


def _unit_lower_inverse_each(as_, masks):
    base, levels = masks
    a8 = [jnp.where(base, a, 0.0) for a in as_]
    p2 = _mm1_each(a8, a8)
    p4 = _mm1_each(p2, p2)
    a8p2 = _mm1_each(a8, p2)
    n = [(p - ap) - a for a, p, ap in zip(a8, p2, a8p2)]
    n = [ni + (p + d) for ni, p, d in zip(n, p4, _mm1_each(n, p4))]
    for lvl in levels:
        off = [jnp.where(lvl, a, 0.0) for a in as_]
        t_off = [o + d for o, d in zip(off, _mm1_each(n, off))]
        n = [ni - (to + d) for ni, to, d in zip(n, t_off, _mm1_each(t_off, n))]
    res = [-((a + ni) + d) for a, ni, d in zip(as_, n, _mm3_each(as_, n))]
    return [ni + (r_ + d) for ni, r_, d in zip(n, res, _mm1_each(n, res))]


def _delta_kernel(*refs, n_seq, rows, has_state, single_step, n_alias):
    refs = refs[n_alias:]
    qkv_ref, z_ref, tail_ref = refs[:3]
    p = 3
    if has_state:
        s0_ref, c0_ref = refs[p], refs[p + 1]
        p += 2
    cw_ref, gp_ref, na_ref = refs[p:p + 3]
    o_ref, s_ref, cn_ref = refs[p + 3:p + 6]
    xp_ref, act_ref = refs[p + 6:p + 8]
    n = n_seq * rows
    c = pl.program_id(1)

    @pl.when(c == 0)
    def _init():
        if has_state:
            if not single_step:
                s_ref[...] = s0_ref[...]
            for s in range(n_seq):
                xp_ref[s, HIST - (CONV_W - 1):HIST, :] = c0_ref[s]
        else:
            s_ref[...] = jnp.zeros(s_ref.shape, F32)
            xp_ref[:, 0:HIST, :] = jnp.zeros((n_seq, HIST, QKV_A), F32)

    def emit(s, cs, acc):
        act_ref[s * rows:(s + 1) * rows, cs] = _silu(acc)

    def seq_rows(s):
        slab, off = divmod(s * rows, ROWS)
        return qkv_ref[slab, off:off + rows, :]

    _causal_conv_block(seq_rows, xp_ref, cw_ref, n_seq, rows, QKV_A, emit)

    @pl.when(c == pl.num_programs(1) - 1)
    def _conv_out():
        for s in range(n_seq):
            cn_ref[s] = xp_ref[s, HIST - (CONV_W - 1):HIST, :]

    n_slab = n // ROWS
    head_group = DELTA_M // n
    _, _, same_n = _seq_masks(n, rows)
    rn = lax.broadcasted_iota(jnp.int32, (n, n), 0)
    cn = lax.broadcasted_iota(jnp.int32, (n, n), 1)
    tail = tail_ref[...].reshape(n, TAIL_W)
    btile = _sigmoid(tail)
    gtile = -jnp.exp(gp_ref[0:1, :]) * jax.nn.softplus(tail + gp_ref[1:2, :])
    g_cum = _masked_sum(same_n & (rn >= cn), gtile)
    g_tot = _masked_sum(same_n, gtile)
    e_cum = jnp.exp(g_cum)
    e_rest = jnp.exp(g_tot - g_cum)
    g_cum_t = g_cum.T

    m = DELTA_M
    ri, ci, same = _seq_masks(m, rows)
    incl = same & (ri >= ci)
    strict = same & (ri > ci)
    inv_masks = _inverse_masks(ri, ci, rows)
    stack = lambda xs: xs[0] if len(xs) == 1 else jnp.concatenate(xs, axis=0)
    groups = [range(g * head_group, (g + 1) * head_group) for g in range(N_HEAD_A // head_group)]
    qn, kn, v, beta, gc, eg, er, g_row = ([] for _ in range(8))
    for heads in groups:
        q_h, k_h, v_h = [], [], []
        for h in heads:
            q = act_ref[:, h * DK_A:(h + 1) * DK_A]
            k = act_ref[:, KA + h * DK_A:KA + (h + 1) * DK_A]
            q_h.append(q * lax.rsqrt(jnp.sum(q * q, axis=-1, keepdims=True) + EPS) * (DK_A ** -0.5))
            k_h.append(k * lax.rsqrt(jnp.sum(k * k, axis=-1, keepdims=True) + EPS))
            v_h.append(act_ref[:, 2 * KA + h * DV_A:2 * KA + (h + 1) * DV_A])
        qn.append(stack(q_h))
        kn.append(stack(k_h))
        v.append(stack(v_h))
        beta.append(stack([btile[:, LANE_BETA + h:LANE_BETA + h + 1] for h in heads]))
        gc.append(stack([g_cum[:, LANE_A + h:LANE_A + h + 1] for h in heads]))
        eg.append(stack([e_cum[:, LANE_A + h:LANE_A + h + 1] for h in heads]))
        er.append(stack([e_rest[:, LANE_A + h:LANE_A + h + 1] for h in heads]))
        g_row.append(jnp.concatenate([g_cum_t[LANE_A + h:LANE_A + h + 1, :] for h in heads], axis=1))
    decay = [jnp.where(incl, jnp.exp(jnp.where(incl, c_ - r_, 0.0)), 0.0) for c_, r_ in zip(gc, g_row)]
    kb = [k_ * b_ for k_, b_ in zip(kn, beta)]
    a = [jnp.where(strict, p_ * d_, 0.0) for p_, d_ in zip(_mm1_each(kb, kn, NT), decay)]
    qk = [p_ * d_ for p_, d_ in zip(_mm3_each(qn, kn, NT), decay)]
    n_inv = _unit_lower_inverse_each(a, inv_masks)
    rhs = [jnp.concatenate([v_ * b_, kb_ * e_], axis=1) for v_, b_, kb_, e_ in zip(v, beta, kb, eg)]
    sol = [r_ + d for r_, d in zip(rhs, _mm1_each(n_inv, rhs))]
    qg = [q_ * e_ for q_, e_ in zip(qn, eg)]
    kdec = [k_ * e_ for k_, e_ in zip(kn, er)]
    probs = [(g, i, h, s, slice(i * n + s * rows, i * n + (s + 1) * rows))
             for g, heads in enumerate(groups) for i, h in enumerate(heads) for s in range(n_seq)]
    state_ref = s0_ref if (has_state and single_step) else s_ref
    states = [state_ref[s, h] for (_, _, h, s, _) in probs]
    r = _mm1_each([stack([sol[g][sl, DV_A:], qg[g][sl]]) for (g, _, _, _, sl) in probs], states)
    v_new = [sol[g][sl, :DV_A] - r_[:rows] for (g, _, _, _, sl), r_ in zip(probs, r)]
    per_group = lambda xs: [stack([x for (g2, *_), x in zip(probs, xs) if g2 == g]) for g in range(len(groups))]
    v_new_g = per_group(v_new)
    o_inter_g = per_group([r_[rows:] for r_ in r])
    o = [oi + d for oi, d in zip(o_inter_g, _mm1_each(qk, v_new_g))]
    upd = _mm1_each([kdec[g][sl] for (g, _, _, _, sl) in probs], v_new, TN)
    for (g, i, h, s, sl), st, du in zip(probs, states, upd):
        g_last = g_tot[s * rows:s * rows + 1, LANE_A + h:LANE_A + h + 1]
        s_ref[s, h] = st * jnp.exp(g_last) + du
    for g, heads in enumerate(groups):
        for i, h in enumerate(heads):
            hl = slice(h * DV_A, (h + 1) * DV_A)
            oh = o[g][i * n:(i + 1) * n]
            oh = oh * lax.rsqrt(jnp.mean(oh * oh, axis=-1, keepdims=True) + EPS) * na_ref[...]
            oh = oh * _silu(z_ref[:, :, hl].reshape(n, DV_A))
            o_ref[:, :, hl] = oh.reshape(n_slab, ROWS, DV_A).astype(BF16)


def _layer_spec(block, layer):
    zeros = (0,) * (len(block) - 1)
    return pl.BlockSpec((None,) + tuple(block), lambda i, c: (layer, i) + zeros)


def _carry_outputs(prev, out_start):
    if prev is None:
        return [], [], {}
    specs = [pl.BlockSpec(memory_space=pl.ANY) for _ in prev]
    return list(prev), specs, {k: out_start + k for k in range(len(prev))}


def _delta(proj, layer, n_layers, s0, c0, prev, conv_w, gate_params, norm_a, n_batch, t_len):
    rows = min(ROWS, t_len)
    if t_len >= ROWS:
        n_seq, n_slab = DELTA_SEQS, DELTA_SEQS
        slabs = proj.reshape(n_batch, t_len // ROWS, ROWS, AB_N)
    else:
        n_seq, n_slab = ROWS // t_len, 1
        slabs = proj.reshape(n_batch * t_len // ROWS, 1, ROWS, AB_N)
    n = n_seq * rows
    nb = n_batch // n_seq
    nc = t_len // rows
    has_state = s0 is not None
    slab_spec = lambda width, col: pl.BlockSpec((n_slab, None, ROWS, width), lambda i, c: (i, c, 0, col))
    args, in_specs, aliases = _carry_outputs(prev, 1)
    n_alias = len(args)
    in_specs += [slab_spec(QKV_A, 0), slab_spec(VA, OFF_ZA // VA), slab_spec(TAIL_W, OFF_TAIL // TAIL_W)]
    args += [slabs, slabs, slabs]
    s_block = (n_seq, N_HEAD_A, DK_A, DV_A)
    c_block = (n_seq, CONV_W - 1, QKV_A)
    if has_state:
        in_specs += [_layer_spec(s_block, layer), _layer_spec(c_block, layer)]
        args += [s0, c0]
    in_specs += [
        pl.BlockSpec((CONV_W, QKV_A), lambda i, c: (0, 0)),
        pl.BlockSpec((2, TAIL_W), lambda i, c: (0, 0)),
        pl.BlockSpec((1, DV_A), lambda i, c: (0, 0)),
    ]
    args += [conv_w, gate_params, norm_a.reshape(1, DV_A)]
    o, s_all, conv_all = pl.pallas_call(
        functools.partial(_delta_kernel, n_seq=n_seq, rows=rows, has_state=has_state, single_step=nc == 1,
                          n_alias=n_alias),
        out_shape=(
            jax.ShapeDtypeStruct(slabs.shape[:3] + (VA,), BF16),
            jax.ShapeDtypeStruct((n_layers, n_batch, N_HEAD_A, DK_A, DV_A), F32),
            jax.ShapeDtypeStruct((n_layers, n_batch, CONV_W - 1, QKV_A), F32),
        ),
        grid=(nb, nc),
        in_specs=in_specs,
        out_specs=(
            slab_spec(VA, 0),
            _layer_spec(s_block, layer),
            _layer_spec(c_block, layer),
        ),
        scratch_shapes=[
            pltpu.VMEM((n_seq, HIST + rows, QKV_A), F32),
            pltpu.VMEM((n, QKV_A), F32),
        ],
        input_output_aliases=aliases,
        compiler_params=pltpu.CompilerParams(
            dimension_semantics=("parallel", "arbitrary"), vmem_limit_bytes=VMEM_LIMIT),
        name="delta",
    )(*args)
    return o.reshape(n_batch * t_len, VA), s_all, conv_all


def _gla_kernel(*refs, rows, chained, has_state, n_alias):
    refs = refs[n_alias:]
    q_ref, k_ref, v_ref, z_ref, tail_ref = refs[:5]
    p = 5
    if has_state:
        s0_ref = refs[p]
        p += 1
    wlr_ref, blr_ref, nb_ref = refs[p:p + 3]
    o_ref, s_ref = refs[p + 3:p + 5]
    n_slab = q_ref.shape[0]
    n = n_slab * ROWS
    n_sub = ROWS // rows
    c = pl.program_id(1)

    state_ref = s0_ref if (has_state and not chained) else s_ref

    @pl.when(c == 0)
    def _init():
        if not has_state:
            s_ref[...] = jnp.zeros(s_ref.shape, F32)
        elif chained:
            s_ref[...] = s0_ref[...]

    ri, ci, same = _seq_masks(ROWS, rows)
    incl = same & (ri >= ci)
    slabs = [slice(u * ROWS, (u + 1) * ROWS) for u in range(n_slab)]
    cumulate = lambda mask, x: jnp.concatenate(
        [_masked_sum(mask, x[sl]) for sl in slabs], axis=0)
    gk = jax.nn.log_sigmoid(_mm3(tail_ref[...].reshape(n, TAIL_W), wlr_ref[...]) + blr_ref[...]) / GLA_NORMALIZER
    b_loc = cumulate(incl, gk)
    q = q_ref[...].reshape(n, KB) * (DK_B ** -0.5)
    k = k_ref[...].reshape(n, KB)
    v_all = v_ref[...].reshape(n, VB)
    q_loc = q * jnp.exp(b_loc)
    heads = range(N_HEAD_B)
    hls = [slice(h * DK_B, (h + 1) * DK_B) for h in heads]
    vls = [slice(h * DV_B, (h + 1) * DV_B) for h in heads]

    def finish(h, o):
        o = o * lax.rsqrt(jnp.mean(o * o, axis=-1, keepdims=True) + EPS) * nb_ref[...]
        o = o * _silu(z_ref[:, :, vls[h]].reshape(n, DV_B))
        o_ref[:, :, vls[h]] = o.reshape(n_slab, ROWS, DV_B).astype(BF16)

    def key_column(row_vals):
        return jnp.broadcast_to(row_vals, (8, DK_B)).T[:, 0:1]

    if chained:
        tri = ri >= ci
        b_cum = cumulate(tri, gk)
        per_slab = lambda rws: jnp.concatenate([jnp.broadcast_to(r_, (ROWS, KB)) for r_ in rws], axis=0)
        b_last = [b_cum[sl.stop - 1:sl.stop] for sl in slabs]
        q_cum = q * jnp.exp(b_cum)
        kdec = k * jnp.exp(per_slab(b_last) - b_cum)
        row = lax.broadcasted_iota(jnp.int32, (n, KB), 0) & (ROWS - 1)
        k_rel = []
        for s in range(n_sub):
            start = [b_cum[sl.start + s * rows - 1:sl.start + s * rows] if s else jnp.zeros((1, KB), F32)
                     for sl in slabs]
            k_rel.append(k * jnp.exp(jnp.where(row < (s + 1) * rows, per_slab(start) - b_cum, 0.0)))
        probs = [(u, h) for u in range(n_slab) for h in heads]
        a_parts = _mm3_each(
            [q_loc[slabs[u].start + s * rows:slabs[u].start + (s + 1) * rows, hls[h]]
             for (u, h) in probs for s in range(n_sub)],
            [k_rel[s][slabs[u], hls[h]] for (u, h) in probs for s in range(n_sub)], NT)
        a = [jnp.where(tri, jnp.concatenate(a_parts[j * n_sub:(j + 1) * n_sub], axis=0), 0.0)
             for j in range(len(probs))]
        vs = [v_all[slabs[u], vls[h]] for (u, h) in probs]
        sts = [state_ref[u, h] for (u, h) in probs]
        o_state = _mm1_each([q_cum[slabs[u], hls[h]] for (u, h) in probs], sts)
        intra = _mm1_each(a, vs)
        upd = _mm1_each([kdec[slabs[u], hls[h]] for (u, h) in probs], vs, TN)
        for j, (u, h) in enumerate(probs):
            s_ref[u, h] = sts[j] * jnp.exp(key_column(b_last[u][:, hls[h]])) + upd[j]
        for h in heads:
            finish(h, jnp.concatenate([o_state[j] + intra[j] for j, (_, h2) in enumerate(probs) if h2 == h], axis=0))
    else:
        vs = [v_all[:, vl] for vl in vls]
        b_tot = _masked_sum(same, gk)
        kd = k * jnp.exp(-b_loc)
        kdec = k * jnp.exp(b_tot - b_loc)
        a = [jnp.where(incl, x, 0.0) for x in _mm3_each([q_loc[:, hl] for hl in hls], [kd[:, hl] for hl in hls], NT)]
        intra = _mm1_each(a, vs)
        probs = [(h, s, slice(s * rows, (s + 1) * rows)) for h in heads for s in range(n_sub)]
        sts = [state_ref[s, h] for (h, s, _) in probs]
        o_state = _mm1_each([q_loc[sl, hls[h]] for (h, _, sl) in probs], sts)
        upd = _mm1_each([kdec[sl, hls[h]] for (h, _, sl) in probs], [vs[h][sl] for (h, _, sl) in probs], TN)
        for (h, s, sl), st, du in zip(probs, sts, upd):
            s_ref[s, h] = st * jnp.exp(key_column(b_tot[s * rows:s * rows + 1, hls[h]])) + du
        for h in heads:
            finish(h, jnp.concatenate(o_state[h * n_sub:(h + 1) * n_sub], axis=0) + intra[h])


def _gla(proj, layer, n_layers, s0, prev, w_lr_pad, b_lr, norm_b, n_batch, t_len, chunk):
    rows = min(chunk, t_len)
    has_state = s0 is not None
    chained = t_len >= ROWS
    if chained:
        n_slab, n_state, nc = GLA_SEQS, GLA_SEQS, t_len // ROWS
        slabs = proj.reshape(n_batch, nc, ROWS, AB_N)
    else:
        n_slab, n_state, nc = 1, ROWS // t_len, 1
        slabs = proj.reshape(n_batch * t_len // ROWS, 1, ROWS, AB_N)
    nb = n_batch // n_state
    slab_spec = lambda width, col: pl.BlockSpec((n_slab, None, ROWS, width), lambda i, c: (i, c, 0, col))
    args, in_specs, aliases = _carry_outputs(prev, 1)
    n_alias = len(args)
    in_specs += [slab_spec(KB, OFF_QB // KB), slab_spec(KB, OFF_KB // KB), slab_spec(VB, OFF_VB // VB),
                 slab_spec(VB, OFF_ZB // VB), slab_spec(TAIL_W, OFF_TAIL // TAIL_W)]
    args += [slabs] * 5
    s_block = (n_state, N_HEAD_B, DK_B, DV_B)
    if has_state:
        in_specs.append(_layer_spec(s_block, layer))
        args.append(s0)
    in_specs += [
        pl.BlockSpec((TAIL_W, KB), lambda i, c: (0, 0)),
        pl.BlockSpec((1, KB), lambda i, c: (0, 0)),
        pl.BlockSpec((1, DV_B), lambda i, c: (0, 0)),
    ]
    args += [w_lr_pad, b_lr.reshape(1, KB), norm_b.reshape(1, DV_B)]
    o, s_all = pl.pallas_call(
        functools.partial(_gla_kernel, rows=rows, chained=chained, has_state=has_state, n_alias=n_alias),
        out_shape=(
            jax.ShapeDtypeStruct(slabs.shape[:3] + (VB,), BF16),
            jax.ShapeDtypeStruct((n_layers, n_batch, N_HEAD_B, DK_B, DV_B), F32),
        ),
        grid=(nb, nc),
        in_specs=in_specs,
        out_specs=(slab_spec(VB, 0), _layer_spec(s_block, layer)),
        input_output_aliases=aliases,
        compiler_params=pltpu.CompilerParams(
            dimension_semantics=("parallel", "arbitrary"), vmem_limit_bytes=VMEM_LIMIT),
        name="gla",
    )(*args)
    return o.reshape(n_batch * t_len, VB), s_all


def _lru_kernel(*refs, n_seq, rows, has_state, reset_first, n_alias):
    refs = refs[n_alias:]
    xb_ref, gate_ref = refs[:2]
    p = 2
    if has_state:
        h0_ref, c0_ref = refs[p], refs[p + 1]
        p += 2
    cw_ref, cb_ref, wg_ref, ba_ref, bx_ref, lam_ref = refs[p:p + 6]
    y_ref, h_ref, cn_ref = refs[p + 6:p + 9]
    xp_ref, xc_ref, a_ref, b_ref = refs[p + 9:p + 13]
    n = n_seq * rows
    c = pl.program_id(1)

    @pl.when(c == 0)
    def _init():
        if has_state:
            h_ref[...] = h0_ref[...]
            for s in range(n_seq):
                xp_ref[s, HIST - (CONV_W - 1):HIST, :] = c0_ref[s]
        else:
            h_ref[...] = jnp.zeros(h_ref.shape, F32)
            xp_ref[:, 0:HIST, :] = jnp.zeros((n_seq, HIST, W_LRU), F32)

    def emit(s, cs, acc):
        xc_ref[s * rows:(s + 1) * rows, cs] = acc

    _causal_conv_block(lambda s: xb_ref[s * rows:(s + 1) * rows, :], xp_ref, cw_ref, n_seq, rows, W_LRU,
                       emit, bias_ref=cb_ref)

    @pl.when(c == pl.num_programs(1) - 1)
    def _conv_out():
        for s in range(n_seq):
            cn_ref[s] = xp_ref[s, HIST - (CONV_W - 1):HIST, :]

    row = lax.broadcasted_iota(jnp.int32, (n, LRU_BW), 0)
    sub = lax.broadcasted_iota(jnp.int32, (n // 8, 8, LRU_BW), 1)
    scan_steps = [(d, sub >= d) for d in (1, 2, 4)]
    first_row = (row == 0) & (c == 0)
    for blk in range(LRU_BLOCKS):
        bl = slice(blk * LRU_BW, (blk + 1) * LRU_BW)
        xc = xc_ref[:, bl]
        pre = jnp.dot(xc.astype(BF16), wg_ref[blk], preferred_element_type=F32)
        t_r = jnp.tanh(pre[:, :LRU_BW] + ba_ref[:, bl])
        t_i = jnp.tanh(pre[:, LRU_BW:] + bx_ref[:, bl])
        gi = 0.5 * t_i + 0.5
        log_a = (t_r + 1.0) * (-0.5 * LRU_C * jax.nn.softplus(-lam_ref[:, bl]))
        a = jnp.exp(log_a)
        z = -jnp.tanh(log_a) * (a * a + 1.0)
        mult = z * lax.rsqrt(jnp.maximum(z, TINY))
        if reset_first:
            mult = jnp.where(first_row, 1.0, mult)
        b = mult * gi * xc
        a = a.reshape(n // 8, 8, LRU_BW)
        b = b.reshape(n // 8, 8, LRU_BW)
        for d, m in scan_steps:
            a_sh = pltpu.roll(a, d, axis=1)
            b_sh = pltpu.roll(b, d, axis=1)
            b = jnp.where(m, a * b_sh + b, b)
            a = jnp.where(m, a * a_sh, a)
        a_ref[:, bl] = a.reshape(n, LRU_BW)
        b_ref[:, bl] = b.reshape(n, LRU_BW)

    for s in range(n_seq):
        def body(g, hp, s=s):
            r0 = pl.multiple_of(s * rows + g * 8, 8)
            hs = a_ref[pl.ds(r0, 8), :] * hp + b_ref[pl.ds(r0, 8), :]
            b_ref[pl.ds(r0, 8), :] = hs
            return hs[7:8, :]

        h_ref[s] = lax.fori_loop(0, rows // 8, body, h_ref[s])

    lc = 512
    for c0 in range(0, W_LRU, lc):
        cs = slice(c0, c0 + lc)
        y_ref[:, cs] = (b_ref[:, cs] * _silu(gate_ref[:, cs])).astype(BF16)


def _lru(proj, layer, n_layers, h0, c0, prev, conv_w, conv_b, w_a, b_a, w_x, b_x, lam, n_batch, t_len,
         reset_first):
    has_state = h0 is not None
    if t_len >= LRU_ROWS:
        rows, n_seq = LRU_ROWS, 1
    else:
        rows, n_seq = t_len, LRU_SHORT_ROWS // t_len
    n = rows * n_seq
    nb = n_batch // n_seq
    nc = t_len // rows
    args, in_specs, aliases = _carry_outputs(prev, 1)
    n_alias = len(args)
    in_specs += [
        pl.BlockSpec((n, W_LRU), lambda i, c: (i * nc + c, 0)),
        pl.BlockSpec((n, W_LRU), lambda i, c: (i * nc + c, 1)),
    ]
    args += [proj, proj]
    h_block = (n_seq, 1, W_LRU)
    c_block = (n_seq, CONV_W - 1, W_LRU)
    if has_state:
        in_specs += [_layer_spec(h_block, layer), _layer_spec(c_block, layer)]
        args += [h0, c0]
    vec = pl.BlockSpec((1, W_LRU), lambda i, c: (0, 0))
    w_gates = (0.5 * jnp.concatenate([w_a, w_x], axis=-1)).astype(BF16)
    blk = pl.BlockSpec(w_gates.shape, lambda i, c: (0, 0, 0))
    in_specs += [pl.BlockSpec((CONV_W, W_LRU), lambda i, c: (0, 0)), vec, blk, vec, vec, vec]
    args += [conv_w, conv_b.reshape(1, W_LRU), w_gates, 0.5 * b_a.reshape(1, W_LRU),
             0.5 * b_x.reshape(1, W_LRU), lam.reshape(1, W_LRU)]
    return pl.pallas_call(
        functools.partial(_lru_kernel, n_seq=n_seq, rows=rows, has_state=has_state,
                          reset_first=reset_first, n_alias=n_alias),
        out_shape=(
            jax.ShapeDtypeStruct((n_batch * t_len, W_LRU), BF16),
            jax.ShapeDtypeStruct((n_layers, n_batch, 1, W_LRU), F32),
            jax.ShapeDtypeStruct((n_layers, n_batch, CONV_W - 1, W_LRU), F32),
        ),
        grid=(nb, nc),
        in_specs=in_specs,
        out_specs=(
            pl.BlockSpec((n, W_LRU), lambda i, c: (i * nc + c, 0)),
            _layer_spec(h_block, layer),
            _layer_spec(c_block, layer),
        ),
        input_output_aliases=aliases,
        scratch_shapes=[
            pltpu.VMEM((n_seq, HIST + rows, W_LRU), F32),
            pltpu.VMEM((n, W_LRU), F32),
            pltpu.VMEM((n, W_LRU), F32),
            pltpu.VMEM((n, W_LRU), F32),
        ],
        compiler_params=pltpu.CompilerParams(
            dimension_semantics=("parallel", "arbitrary"), vmem_limit_bytes=VMEM_LIMIT),
        name="lru",
    )(*args)


def _ab_w_in_kernel(w_ref, o_ref):
    sizes = [QKV_A, N_HEAD_A, N_HEAD_A, VA, KB, KB, VB, LOWRANK, VB]
    offs = [0]
    for s in sizes:
        offs.append(offs[-1] + s)
    w = w_ref[...]
    qkv, b_raw, a_raw, z_a, q_b, k_b, v_b, lr_b, z_b = [w[:, offs[i]:offs[i + 1]] for i in range(9)]
    used = OFF_TAIL + 2 * N_HEAD_A + LOWRANK
    pad = jnp.zeros((w.shape[0], AB_N - used), w.dtype)
    o_ref[...] = jnp.concatenate([qkv, z_a, q_b, k_b, v_b, z_b, b_raw, a_raw, lr_b, pad], axis=1)


def _ab_w_in_layout(w, layer):
    _, d, n = w.shape
    tr = 256
    return pl.pallas_call(
        _ab_w_in_kernel,
        out_shape=jax.ShapeDtypeStruct((d, AB_N), BF16),
        grid=(d // tr,),
        in_specs=[pl.BlockSpec((None, tr, n), lambda i: (layer, i, 0))],
        out_specs=pl.BlockSpec((tr, AB_N), lambda i: (i, 0)),
        compiler_params=pltpu.CompilerParams(dimension_semantics=("parallel",)),
        name="w_in_layout",
    )(w)


def _cast_kernel(w_ref, o_ref):
    o_ref[...] = w_ref[...].astype(BF16)


def _layer_bf16(w, layer):
    _, r, c = w.shape
    tr = 512
    return pl.pallas_call(
        _cast_kernel,
        out_shape=jax.ShapeDtypeStruct((r, c), BF16),
        grid=(r // tr,),
        in_specs=[pl.BlockSpec((None, tr, c), lambda i: (layer, i, 0))],
        out_specs=pl.BlockSpec((tr, c), lambda i: (i, 0)),
        compiler_params=pltpu.CompilerParams(dimension_semantics=("parallel",)),
        name="w_cast",
    )(w)


def _tail_row(vals, lane0):
    return jnp.zeros((TAIL_W,), F32).at[lane0:lane0 + vals.shape[0]].set(vals.astype(F32))


def kernel(x_prompt, x_sample, state_delta, state_delta_conv, state_gla, state_lru, state_lru_conv,
           ab_norm, ab_w_in, ab_conv_w, ab_a_log, ab_dt_bias, ab_norm_a, ab_gla_w_lr, ab_gla_b_lr,
           ab_norm_b, ab_w_out, lru_norm, lru_w_in, lru_conv_w, lru_conv_b, lru_w_a, lru_b_a,
           lru_w_x, lru_b_x, lru_lambda, lru_w_out, final_norm):
    n_ab, n_lru = ab_norm.shape[0], lru_norm.shape[0]
    depth = n_ab + n_lru
    lru_h0 = state_lru.reshape(n_lru, -1, 1, W_LRU)
    groups = []
    for x, carried in ((x_prompt, False), (x_sample, True)):
        groups.append(dict(x=x.reshape(-1, D_MODEL), nb=x.shape[0], t=x.shape[1], carried=carried,
                           delta=None, gla=None, lru=None))

    ab_w_in_bf16 = ab_w_in.astype(BF16)
    for layer in range(depth):
        j = layer // 2
        last = layer == depth - 1
        if layer % 2 == 0:
            w_in = _ab_w_in_layout(ab_w_in_bf16, j)
            w_out = _layer_bf16(ab_w_out, j)
            gate_params = jnp.stack([_tail_row(ab_a_log[j], LANE_A), _tail_row(ab_dt_bias[j], LANE_A)])
            w_lr_pad = jnp.zeros((TAIL_W, KB), F32).at[LANE_LR:LANE_LR + LOWRANK].set(ab_gla_w_lr[j])
            for g in groups:
                proj = _norm_mm(g["x"], ab_norm[j], w_in, AB_TN)
                s_a, c_a, s_b = (state_delta, state_delta_conv, state_gla) if g["carried"] else (None,) * 3
                o_a, *g["delta"] = _delta(proj, j, n_ab, s_a, c_a, g["delta"], ab_conv_w[j], gate_params,
                                          ab_norm_a[j], g["nb"], g["t"])
                o_b, *g["gla"] = _gla(proj, j, n_ab, s_b, g["gla"], w_lr_pad, ab_gla_b_lr[j], ab_norm_b[j],
                                      g["nb"], g["t"], 16)
                g["x"] = _out_mm([o_a, o_b], w_out, g["x"], final_norm if last else None)
        else:
            w_in = _layer_bf16(lru_w_in, j)
            w_out = _layer_bf16(lru_w_out, j)
            for g in groups:
                proj = _norm_mm(g["x"], lru_norm[j], w_in, 1024)
                h0, c0 = (lru_h0, state_lru_conv) if g["carried"] else (None, None)
                y, *g["lru"] = _lru(proj, j, n_lru, h0, c0, g["lru"], lru_conv_w[j], lru_conv_b[j],
                                    lru_w_a[j], lru_b_a[j], lru_w_x[j], lru_b_x[j], lru_lambda[j],
                                    g["nb"], g["t"], reset_first=not g["carried"])
                g["x"] = _out_mm([y], w_out, g["x"], final_norm if last else None)

    outs = []
    for g, x in zip(groups, (x_prompt, x_sample)):
        h_all, lconv = g["lru"]
        outs.append([g["x"].reshape(x.shape), g["delta"][0], g["delta"][1], g["gla"][0],
                     h_all.reshape(n_lru, g["nb"], W_LRU), lconv])
    p, s = outs
    return (p[0], s[0], p[1], p[2], p[3], p[4], p[5], s[1], s[2], s[3], s[4], s[5])
```
